```python
import functools
import jax, jax.numpy as jnp
from jax import lax
import numpy as np

D_MODEL = 1024
BATCH = 16
SEQ = 256
DEPTH = 4
DEC_BATCH = 8
DEC_SEQ = 1024
PAST_LEN = 512

GRID_W = 64
MIX_WIDTH = D_MODEL
ATTN_WIDTH = MIX_WIDTH // 2
N_HEADS = 8
HEAD_DIM = ATTN_WIDTH // N_HEADS
SGU_WIDTH = MIX_WIDTH - ATTN_WIDTH
SGU_GROUPS = 4
SGU_GROUP_DIM = SGU_WIDTH // SGU_GROUPS
CHUNK = 128
WIN_H_MAX = 8
WIN_W = 16
Q_BLOCK = 128
IN_WIDTH = 3 * ATTN_WIDTH + 2 * SGU_WIDTH
N_EXPERTS = 16
N_EXPERT_GROUPS = 4
EXPERTS_PER_GROUP = N_EXPERTS // N_EXPERT_GROUPS
TOP_K = 2
D_EXPERT = 256
EPS = 1e-6
NEG = -1e30

kernel_name = "hybrid_natten_sgu_moe_diffusion_step"


def _rms_norm(x, w):
    xf = x.astype(jnp.float32)
    y = xf * lax.rsqrt(jnp.mean(xf * xf, axis=-1, keepdims=True) + EPS)
    return (y * w.astype(jnp.float32)).astype(x.dtype)


def _layer_norm(x, w, b):
    xf = x.astype(jnp.float32)
    mu = jnp.mean(xf, axis=-1, keepdims=True)
    var = jnp.mean(jnp.square(xf - mu), axis=-1, keepdims=True)
    y = (xf - mu) * lax.rsqrt(var + EPS)
    return (y * w.astype(jnp.float32) + b.astype(jnp.float32)).astype(x.dtype)


def _adaln(cond, w_ada, b_ada):
    m = jax.nn.silu(cond) @ w_ada + b_ada
    return jnp.split(m, 6, axis=-1)


def _context_attention(q, k, v):
    B, L, H, Dh = q.shape
    nb = L // Q_BLOCK
    qb = q.reshape(B, nb, Q_BLOCK, H, Dh).transpose(1, 0, 2, 3, 4)
    scale = HEAD_DIM ** -0.5

    def block(qi):
        s = jnp.einsum('bqhd,bkhd->bhqk', qi, k).astype(jnp.float32) * scale
        p = jax.nn.softmax(s, axis=-1).astype(v.dtype)
        return jnp.einsum('bhqk,bkhd->bqhd', p, v)

    o = lax.map(block, qb)
    return o.transpose(1, 0, 2, 3, 4).reshape(B, L, H * Dh)


def _neighbourhood_attention(q, k, v, k_ctx, v_ctx, rpb):
    B, L, H, Dh = q.shape
    rows = L // GRID_W
    win_h = min(WIN_H_MAX, rows)
    scale = HEAD_DIM ** -0.5
    qg = q.reshape(B, rows, GRID_W, H, Dh).transpose(1, 0, 2, 3, 4)
    kg = k.reshape(B, rows, GRID_W, H, Dh)
    vg = v.reshape(B, rows, GRID_W, H, Dh)
    cols = jnp.arange(GRID_W)
    col_start = jnp.clip(cols - WIN_W // 2, 0, GRID_W - WIN_W)
    col_mask = (cols[None, :] >= col_start[:, None]) & (cols[None, :] < col_start[:, None] + WIN_W)
    col_idx = jnp.clip(cols[None, :] - cols[:, None] + WIN_W - 1, 0, 2 * WIN_W - 2)
    rpb_c = rpb.astype(jnp.float32)[:, :, col_idx]
    row_start = jnp.clip(jnp.arange(rows) - win_h // 2, 0, rows - win_h)

    def row_block(args):
        q_r, r = args
        rs = row_start[r]
        k_band = lax.dynamic_slice_in_dim(kg, rs, win_h, axis=1)
        v_band = lax.dynamic_slice_in_dim(vg, rs, win_h, axis=1)
        row_idx = rs + jnp.arange(win_h) - r + WIN_H_MAX - 1
        bias = rpb_c[:, row_idx].transpose(0, 2, 1, 3)
        s_win = jnp.einsum('bqhd,bjkhd->bhqjk', q_r, k_band).astype(jnp.float32) * scale + bias[None]
        s_win = jnp.where(col_mask[:, None, :], s_win, NEG)
        s_ctx = jnp.einsum('bqhd,bchd->bhqc', q_r, k_ctx).astype(jnp.float32) * scale
        n_win = win_h * GRID_W
        s = jnp.concatenate([s_win.reshape(B, H, GRID_W, n_win), s_ctx], axis=-1)
        p = jax.nn.softmax(s, axis=-1).astype(v.dtype)
        p_win = p[..., :n_win].reshape(B, H, GRID_W, win_h, GRID_W)
        p_ctx = p[..., n_win:]
        return (jnp.einsum('bhqjk,bjkhd->bqhd', p_win, v_band)
                + jnp.einsum('bhqc,bchd->bqhd', p_ctx, v_ctx))

    o = lax.map(row_block, (qg, jnp.arange(rows)))
    return o.transpose(1, 0, 2, 3, 4).reshape(B, L, H * Dh)


def _spatial_gating(gu, gv, w_s, b_s, ln_w, ln_b):
    B, L, _ = gu.shape
    nc = L // CHUNK
    u = jax.nn.gelu(gu).reshape(B, nc, CHUNK, SGU_GROUPS, SGU_GROUP_DIM)
    vv = _layer_norm(jax.nn.gelu(gv).reshape(B, nc, CHUNK, SGU_GROUPS, SGU_GROUP_DIM), ln_w, ln_b)
    sv = jnp.einsum('gpq,bnqgc->bnpgc', w_s, vv) + b_s.T[:, :, None]
    return (u * sv).reshape(B, L, SGU_WIDTH)


def _moe(h, w_router, b_router, w1, w3, w2):
    B, L, D = h.shape
    t = h.reshape(B * L, D)
    s = jax.nn.sigmoid((t @ w_router).astype(jnp.float32))
    sel = s + b_router.astype(jnp.float32)
    g_score = lax.top_k(sel.reshape(-1, N_EXPERT_GROUPS, EXPERTS_PER_GROUP), 2)[0].sum(-1)
    best = jnp.argmax(g_score, axis=-1)
    e_mask = (jnp.arange(N_EXPERTS)[None, :] // EXPERTS_PER_GROUP) == best[:, None]
    _, idx = lax.top_k(jnp.where(e_mask, sel, NEG), TOP_K)
    w_sel = jnp.take_along_axis(s, idx, axis=-1)
    w_sel = w_sel / jnp.sum(w_sel, axis=-1, keepdims=True)
    gates = jnp.sum(jax.nn.one_hot(idx, N_EXPERTS, dtype=jnp.float32) * w_sel[..., None], axis=1)
    hid = jax.nn.silu(jnp.einsum('td,edf->tef', t, w1)) * jnp.einsum('td,edf->tef', t, w3)
    hid = hid * gates.astype(hid.dtype)[..., None]
    return jnp.einsum('tef,efd->td', hid, w2).reshape(B, L, D)


def _trunk_layer(x, cond, attend, lp, w_router, b_router):
    (w_ada, b_ada, norm1_w, norm2_w, w_in, q_norm_w, k_norm_w,
     w_sgu, b_sgu, sgu_ln_w, sgu_ln_b, w_out, w1, w3, w2) = lp
    sh1, sc1, g1, sh2, sc2, g2 = _adaln(cond, w_ada, b_ada)
    B, L, _ = x.shape
    h = _rms_norm(x, norm1_w) * (1 + sc1) + sh1
    p = h @ w_in
    q, k, v, gu, gv = jnp.split(
        p, [ATTN_WIDTH, 2 * ATTN_WIDTH, 3 * ATTN_WIDTH, 3 * ATTN_WIDTH + SGU_WIDTH], axis=-1)
    q = _rms_norm(q.reshape(B, L, N_HEADS, HEAD_DIM), q_norm_w)
    k = _rms_norm(k.reshape(B, L, N_HEADS, HEAD_DIM), k_norm_w)
    v = v.reshape(B, L, N_HEADS, HEAD_DIM)
    a = attend(q, k, v)
    sg = _spatial_gating(gu, gv, w_sgu, b_sgu, sgu_ln_w, sgu_ln_b)
    x = x + g1 * (jnp.concatenate([a, sg], axis=-1) @ w_out)
    h2 = _rms_norm(x, norm2_w) * (1 + sc2) + sh2
    x = x + g2 * _moe(h2, w_router, b_router, w1, w3, w2)
    return x, k, v


def setup_inputs(seed: int = 0) -> dict:
    key = jax.random.key(seed)
    ks = jax.random.split(key, 26)
    f32 = jnp.float32

    def nrm(k, shape, scale):
        return jax.random.normal(k, shape, f32) * scale

    D = D_MODEL
    kv_shape = (DEC_BATCH, DEPTH, PAST_LEN, N_HEADS, HEAD_DIM)
    return {
        "x_prompt": nrm(ks[0], (BATCH, SEQ, D), 1.0),
        "x_sample": nrm(ks[1], (DEC_BATCH, DEC_SEQ, D), 1.0),
        "cache_k": nrm(ks[2], kv_shape, 1.0),
        "cache_v": nrm(ks[3], kv_shape, 1.0),
        "c": nrm(ks[4], (DEC_BATCH, D), 1.0),
        "c_ctx": nrm(ks[5], (D,), 1.0),
        "w_ada": nrm(ks[6], (DEPTH, D, 6 * D), 0.5 * D ** -0.5),
        "b_ada": nrm(ks[7], (DEPTH, 6 * D), 0.02),
        "norm1_w": 1.0 + nrm(ks[8], (DEPTH, D), 0.02),
        "norm2_w": 1.0 + nrm(ks[9], (DEPTH, D), 0.02),
        "w_in": nrm(ks[10], (DEPTH, D, IN_WIDTH), D ** -0.5),
        "q_norm_w": 1.0 + nrm(ks[11], (DEPTH, HEAD_DIM), 0.02),
        "k_norm_w": 1.0 + nrm(ks[12], (DEPTH, HEAD_DIM), 0.02),
        "rpb": nrm(ks[13], (DEPTH, N_HEADS, 2 * WIN_H_MAX - 1, 2 * WIN_W - 1), 0.1),
        "w_sgu": nrm(ks[14], (DEPTH, SGU_GROUPS, CHUNK, CHUNK), CHUNK ** -0.5),
        "b_sgu": 1.0 + nrm(ks[15], (DEPTH, SGU_GROUPS, CHUNK), 0.1),
        "sgu_ln_w": 1.0 + nrm(ks[16], (DEPTH, SGU_GROUPS, SGU_GROUP_DIM), 0.02),
        "sgu_ln_b": nrm(ks[17], (DEPTH, SGU_GROUPS, SGU_GROUP_DIM), 0.02),
        "w_out": nrm(ks[18], (DEPTH, MIX_WIDTH, D), MIX_WIDTH ** -0.5),
        "w_router": nrm(ks[19], (D, N_EXPERTS), D ** -0.5),
        "b_router": nrm(ks[20], (N_EXPERTS,), 0.01),
        "w1": nrm(ks[21], (DEPTH, N_EXPERTS, D, D_EXPERT), D ** -0.5),
        "w3": nrm(ks[22], (DEPTH, N_EXPERTS, D, D_EXPERT), D ** -0.5),
        "w2": nrm(ks[23], (DEPTH, N_EXPERTS, D_EXPERT, D), D_EXPERT ** -0.5),
    }


def reference(x_prompt, x_sample, cache_k, cache_v, c, c_ctx, w_ada, b_ada, norm1_w, norm2_w,
              w_in, q_norm_w, k_norm_w, rpb, w_sgu, b_sgu, sgu_ln_w, sgu_ln_b, w_out,
              w_router, b_router, w1, w3, w2):
    xp = x_prompt
    xs = x_sample
    c_lat = c[:, None, :]
    ks, vs = [], []
    for l in range(DEPTH):
        lp = (w_ada[l], b_ada[l], norm1_w[l], norm2_w[l], w_in[l], q_norm_w[l], k_norm_w[l],
              w_sgu[l], b_sgu[l], sgu_ln_w[l], sgu_ln_b[l], w_out[l], w1[l], w3[l], w2[l])
        xp, k_l, v_l = _trunk_layer(xp, c_ctx, _context_attention, lp, w_router, b_router)
        ks.append(k_l)
        vs.append(v_l)
        attend = functools.partial(_neighbourhood_attention, k_ctx=cache_k[:, l],
                                   v_ctx=cache_v[:, l], rpb=rpb[l])
        xs, _, _ = _trunk_layer(xs, c_lat, attend, lp, w_router, b_router)
    new_k = jnp.stack(ks, axis=1)
    new_v = jnp.stack(vs, axis=1)
    return (xp, xs, new_k, new_v)
```

```python
import functools

import jax
import jax.numpy as jnp
from jax import lax
from jax.experimental import pallas as pl
from jax.experimental.pallas import tpu as pltpu

F32 = jnp.float32
BF16 = jnp.bfloat16

D_MODEL = 1024
DEPTH = 4
N_HEADS = 8
HEAD_DIM = 64
ATTN_WIDTH = N_HEADS * HEAD_DIM
SGU_GROUPS = 4
SGU_GROUP_DIM = 128
SGU_WIDTH = SGU_GROUPS * SGU_GROUP_DIM
CHUNK = 128
IN_WIDTH = 3 * ATTN_WIDTH + 2 * SGU_WIDTH
GRID_W = 64
WIN_H = 8
WIN_W = 16
N_EXPERTS = 16
N_EXPERT_GROUPS = 4
EXPERTS_PER_GROUP = 4
D_EXPERT = 256
GROUP_HIDDEN = EXPERTS_PER_GROUP * D_EXPERT
EPS = 1e-6
NEG = -1e30

LANES = 128
HEADS_PER_VREG = LANES // HEAD_DIM
N_HEAD_PAIRS = N_HEADS // HEADS_PER_VREG
COND_ROWS = 16
Q_ROWS = 4
WIN_ROWS = Q_ROWS + WIN_H
VMEM_LIMIT = 48 * 1024 * 1024


def _cparams(*sem):
    return pltpu.CompilerParams(dimension_semantics=sem, vmem_limit_bytes=VMEM_LIMIT)


def _adaln_kernel(cond_ref, w_ref, b_ref, o_ref):
    c = cond_ref[...]
    a = (c * jax.nn.sigmoid(c)).astype(BF16)
    o_ref[0] = jnp.dot(a, w_ref[0].astype(BF16), preferred_element_type=F32) + b_ref[0]


def _adaln(cond, w_ada, b_ada):
    tn = 1536
    n = 6 * D_MODEL
    return pl.pallas_call(
        _adaln_kernel,
        grid=(DEPTH, n // tn),
        in_specs=[
            pl.BlockSpec((COND_ROWS, D_MODEL), lambda l, j: (0, 0)),
            pl.BlockSpec((1, D_MODEL, tn), lambda l, j: (l, 0, j)),
            pl.BlockSpec((1, 1, tn), lambda l, j: (l, 0, j)),
        ],
        out_specs=pl.BlockSpec((1, COND_ROWS, tn), lambda l, j: (l, 0, j)),
        out_shape=jax.ShapeDtypeStruct((DEPTH, COND_ROWS, n), F32),
        compiler_params=_cparams("arbitrary", "arbitrary"),
        name="adaln",
    )(cond, w_ada, b_ada.reshape(DEPTH, 1, n))


def _gelu_tanh(x):
    return 0.5 * x * (1.0 + jnp.tanh(0.7978845608028654 * (x + 0.044715 * (x * x * x))))


def _head_rms(t, bd, w):
    t2 = t * t
    hi = t2.astype(BF16)
    lo = (t2 - hi.astype(F32)).astype(BF16)
    outs = []
    for c in range(ATTN_WIDTH // 256):
        sl = slice(256 * c, 256 * c + 256)
        hl = jnp.concatenate([hi[:, sl], lo[:, sl]], axis=1)
        ss = jnp.dot(hl, bd, preferred_element_type=F32)
        outs.append(t[:, sl] * lax.rsqrt(ss * (1.0 / HEAD_DIM) + EPS))
    return jnp.concatenate(outs, axis=1) * w


def _inproj_kernel(x_ref, mod_ref, n1w_ref, win_ref, qw_ref, kw_ref, bd_ref, ws_ref, bs_ref,
                   lnw_ref, lnb_ref, q_ref, k_ref, v_ref, sg_ref, *, tm):
    x = x_ref[...]
    m = mod_ref[0]
    sh1, sc1 = m[0:1], m[1:2]
    ms = jnp.mean(x * x, axis=-1, keepdims=True)
    h = x * lax.rsqrt(ms + EPS) * n1w_ref[...]
    h = (h * (1.0 + sc1) + sh1).astype(BF16)
    p = jnp.dot(h, win_ref[...], preferred_element_type=F32)

    bd = bd_ref[...]
    q = _head_rms(p[:, 0:ATTN_WIDTH], bd, qw_ref[...])
    k = _head_rms(p[:, ATTN_WIDTH:2 * ATTN_WIDTH], bd, kw_ref[...])
    q_ref[...] = (q * (HEAD_DIM ** -0.5)).astype(q_ref.dtype)
    k_ref[...] = k.astype(k_ref.dtype)
    v_ref[...] = p[:, 2 * ATTN_WIDTH:3 * ATTN_WIDTH].astype(v_ref.dtype)

    off_u = 3 * ATTN_WIDTH
    off_v = off_u + SGU_WIDTH
    for g in range(SGU_GROUPS):
        gl = slice(g * SGU_GROUP_DIM, (g + 1) * SGU_GROUP_DIM)
        u = _gelu_tanh(p[:, off_u + g * SGU_GROUP_DIM: off_u + (g + 1) * SGU_GROUP_DIM])
        t = _gelu_tanh(p[:, off_v + g * SGU_GROUP_DIM: off_v + (g + 1) * SGU_GROUP_DIM])
        mu = jnp.mean(t, axis=-1, keepdims=True)
        d = t - mu
        var = jnp.mean(d * d, axis=-1, keepdims=True)
        y = (d * lax.rsqrt(var + EPS) * lnw_ref[g:g + 1, :] + lnb_ref[g:g + 1, :]).astype(BF16)
        for c in range(tm // CHUNK):
            rows = slice(c * CHUNK, (c + 1) * CHUNK)
            sv = jnp.dot(ws_ref[g], y[rows], preferred_element_type=F32) + bs_ref[g]
            sg_ref[rows, gl] = (u[rows] * sv).astype(sg_ref.dtype)


def _inproj(x, mod, row_of_tile, n1w, win, qw, kw, bd, ws, bs, lnw, lnb, *, tm, kv_dtype):
    t_tokens = x.shape[0]
    tok = lambda i: (i, 0)
    full2 = lambda i: (0, 0)
    full3 = lambda i: (0, 0, 0)
    out_sds = lambda dt: jax.ShapeDtypeStruct((t_tokens, ATTN_WIDTH), dt)
    return pl.pallas_call(
        functools.partial(_inproj_kernel, tm=tm),
        grid=(t_tokens // tm,),
        in_specs=[
            pl.BlockSpec((tm, D_MODEL), tok),
            pl.BlockSpec((1, 6, D_MODEL), lambda i: (row_of_tile(i), 0, 0)),
            pl.BlockSpec((1, D_MODEL), full2),
            pl.BlockSpec((D_MODEL, IN_WIDTH), full2),
            pl.BlockSpec((1, ATTN_WIDTH), full2),
            pl.BlockSpec((1, ATTN_WIDTH), full2),
            pl.BlockSpec((512, 256), full2),
            pl.BlockSpec((SGU_GROUPS, CHUNK, CHUNK), full3),
            pl.BlockSpec((SGU_GROUPS, CHUNK, SGU_GROUP_DIM), full3),
            pl.BlockSpec((SGU_GROUPS, SGU_GROUP_DIM), full2),
            pl.BlockSpec((SGU_GROUPS, SGU_GROUP_DIM), full2),
        ],
        out_specs=[pl.BlockSpec((tm, ATTN_WIDTH), tok)] * 4,
        out_shape=[out_sds(BF16), out_sds(kv_dtype), out_sds(kv_dtype), out_sds(BF16)],
        compiler_params=_cparams("arbitrary"),
        name="inproj",
    )(x, mod, n1w, win, qw, kw, bd, ws, bs, lnw, lnb)


def _route(lg_t, br):
    s = jax.nn.sigmoid(lg_t)
    sel = s + br
    row = lambda a, i: a[i:i + 1]
    g_score = []
    for g in range(N_EXPERT_GROUPS):
        v = [row(sel, 4 * g + i) for i in range(4)]
        best_pair = None
        for i in range(4):
            for j in range(i + 1, 4):
                pair = v[i] + v[j]
                best_pair = pair if best_pair is None else jnp.maximum(best_pair, pair)
        g_score.append(best_pair)
    best = jnp.zeros_like(g_score[0], dtype=jnp.int32)
    top = g_score[0]
    for g in range(1, N_EXPERT_GROUPS):
        upd = g_score[g] > top
        best = jnp.where(upd, g, best)
        top = jnp.where(upd, g_score[g], top)
    cand, aff = [], []
    for i in range(4):
        ci, si = row(sel, i), row(s, i)
        for g in range(1, N_EXPERT_GROUPS):
            ci = jnp.where(best == g, row(sel, 4 * g + i), ci)
            si = jnp.where(best == g, row(s, 4 * g + i), si)
        cand.append(ci)
        aff.append(si)

    def first_argmax(vals):
        idx = jnp.zeros_like(best)
        top_v = vals[0]
        for i in range(1, 4):
            upd = vals[i] > top_v
            idx = jnp.where(upd, i, idx)
            top_v = jnp.where(upd, vals[i], top_v)
        return idx

    i1 = first_argmax(cand)
    i2 = first_argmax([jnp.where(i1 == i, -jnp.inf, cand[i]) for i in range(4)])
    pick = lambda idx: sum(jnp.where(idx == i, aff[i], 0.0) for i in range(4))
    den = pick(i1) + pick(i2)
    gate = [jnp.where((i1 == i) | (i2 == i), aff[i] / den, 0.0) for i in range(4)]
    rows = []
    for g in range(N_EXPERT_GROUPS):
        gg = [jnp.where(best == g, gate[i], 0.0) for i in range(4)]
        hi = [x.astype(BF16).astype(F32) for x in gg]
        lo = [(x - h).astype(BF16).astype(F32) for x, h in zip(gg, hi)]
        rows += hi + lo
    return jnp.concatenate(rows, axis=0)


def _post_attention(a, sg, x, m, wout_ref, n2w_ref, wr_ref, br_ref, xm_ref, h2_ref, gt_ref):
    g1, sh2, sc2 = m[2:3], m[3:4], m[4:5]
    cat = jnp.concatenate([a, sg], axis=1)
    y = jnp.dot(cat, wout_ref[...], preferred_element_type=F32)
    xm = x + g1 * y
    xm_ref[...] = xm
    ms = jnp.mean(xm * xm, axis=-1, keepdims=True)
    h2 = xm * lax.rsqrt(ms + EPS) * n2w_ref[...]
    h2 = (h2 * (1.0 + sc2) + sh2).astype(BF16)
    h2_ref[...] = h2
    lg_t = lax.dot_general(wr_ref[...], h2, (((1,), (1,)), ((), ())), preferred_element_type=F32)
    gt_ref[...] = _route(lg_t, br_ref[...])


def _head_lane_mask(hh):
    lane = lax.broadcasted_iota(jnp.int32, (1, LANES), 1)
    return (lane >= hh * HEAD_DIM) & (lane < (hh + 1) * HEAD_DIM)


def _ctx_attn_kernel(q_ref, k_ref, v_ref, sg_ref, x_ref, mod_ref, wout_ref, n2w_ref, wr_ref, br_ref,
                     xm_ref, h2_ref, gt_ref):
    outs = []
    for j in range(N_HEAD_PAIRS):
        cols = slice(j * LANES, (j + 1) * LANES)
        q2 = q_ref[:, cols]
        k2 = k_ref[:, cols].astype(BF16)
        v2 = v_ref[:, cols].astype(BF16)
        o_pair = None
        for hh in range(HEADS_PER_VREG):
            msk = _head_lane_mask(hh)
            km = jnp.where(msk, k2, jnp.zeros_like(k2))
            vm = jnp.where(msk, v2, jnp.zeros_like(v2))
            s = lax.dot_general(q2, km, (((1,), (1,)), ((), ())), preferred_element_type=F32)
            mx = jnp.max(s, axis=-1, keepdims=True)
            e = jnp.exp(s - mx)
            l = jnp.sum(e, axis=-1, keepdims=True)
            o = jnp.dot(e.astype(BF16), vm, preferred_element_type=F32) * (1.0 / l)
            o_pair = o if o_pair is None else o_pair + o
        outs.append(o_pair.astype(BF16))
    a = jnp.concatenate(outs, axis=1)
    _post_attention(a, sg_ref[...], x_ref[...], mod_ref[0], wout_ref, n2w_ref, wr_ref, br_ref,
                    xm_ref, h2_ref, gt_ref)


def _ctx_attention(q, k, v, sg, x, mod, wout, n2w, wr, br, *, seq):
    t_tokens = x.shape[0]
    tok = lambda b: (b, 0)
    full2 = lambda b: (0, 0)
    return pl.pallas_call(
        _ctx_attn_kernel,
        grid=(t_tokens // seq,),
        in_specs=[
            pl.BlockSpec((seq, ATTN_WIDTH), tok),
            pl.BlockSpec((seq, ATTN_WIDTH), tok),
            pl.BlockSpec((seq, ATTN_WIDTH), tok),
            pl.BlockSpec((seq, SGU_WIDTH), tok),
            pl.BlockSpec((seq, D_MODEL), tok),
            pl.BlockSpec((1, 6, D_MODEL), lambda b: (0, 0, 0)),
            pl.BlockSpec((D_MODEL, D_MODEL), full2),
            pl.BlockSpec((1, D_MODEL), full2),
            pl.BlockSpec((N_EXPERTS, D_MODEL), full2),
            pl.BlockSpec((N_EXPERTS, 1), full2),
        ],
        out_specs=[
            pl.BlockSpec((seq, D_MODEL), tok),
            pl.BlockSpec((seq, D_MODEL), tok),
            pl.BlockSpec((2 * N_EXPERTS, seq), lambda b: (0, b)),
        ],
        out_shape=[
            jax.ShapeDtypeStruct((t_tokens, D_MODEL), F32),
            jax.ShapeDtypeStruct((t_tokens, D_MODEL), BF16),
            jax.ShapeDtypeStruct((2 * N_EXPERTS, t_tokens), F32),
        ],
        compiler_params=_cparams("arbitrary"),
        name="ctx_attn",
    )(q, k, v, sg, x, mod, wout, n2w, wr, br)


def _row_start(r, rows):
    return jnp.clip(r - WIN_H // 2, 0, rows - WIN_H)


def _nbr_attn_kernel(q_ref, k_ref, v_ref, ck_ref, cv_ref, tb_ref, sg_ref, x_ref, mod_ref, wout_ref,
                     n2w_ref, wr_ref, br_ref, xm_ref, h2_ref, gt_ref, ckb_ref, cvb_ref, *, rows):
    t = pl.program_id(1)

    @pl.when(t == 0)
    def _():
        ckb_ref[...] = ck_ref[0, 0].astype(BF16)
        cvb_ref[...] = cv_ref[0, 0].astype(BF16)

    r0 = t * Q_ROWS
    ws = jnp.minimum(_row_start(r0, rows), rows - WIN_ROWS)
    tok0 = pl.multiple_of(ws * GRID_W, GRID_W)
    kwin = k_ref[pl.ds(tok0, WIN_ROWS * GRID_W), :]
    vwin = v_ref[pl.ds(tok0, WIN_ROWS * GRID_W), :]

    lane = lax.broadcasted_iota(jnp.int32, (1, LANES), 1)
    blk_idx, row_mask = [], []
    for a in range(Q_ROWS):
        r = r0 + a
        rs = _row_start(r, rows)
        idx_a, mask_a = [], []
        for jp in range(WIN_ROWS // 2):
            kr = ws + 2 * jp
            idx_a.append(jnp.clip(kr - r + WIN_H, 0, 2 * WIN_H - 1))
            ok0 = (kr >= rs) & (kr < rs + WIN_H)
            ok1 = (kr + 1 >= rs) & (kr + 1 < rs + WIN_H)
            m0 = jnp.where(ok0, 0.0, NEG).astype(F32)
            m1 = jnp.where(ok1, 0.0, NEG).astype(F32)
            mask_a.append(jnp.where(lane < GRID_W, m0, m1))
        blk_idx.append(idx_a)
        row_mask.append(mask_a)

    outs = []
    for j in range(N_HEAD_PAIRS):
        cols = slice(j * LANES, (j + 1) * LANES)
        q2 = q_ref[:, cols]
        kw2, vw2 = kwin[:, cols], vwin[:, cols]
        kc2, vc2 = ckb_ref[:, cols], cvb_ref[:, cols]
        o_pair = None
        for hh in range(HEADS_PER_VREG):
            h = j * HEADS_PER_VREG + hh
            msk = _head_lane_mask(hh)
            zero = jnp.zeros((), BF16)
            kwm, vwm = jnp.where(msk, kw2, zero), jnp.where(msk, vw2, zero)
            kcm, vcm = jnp.where(msk, kc2, zero), jnp.where(msk, vc2, zero)
            bias = jnp.concatenate([
                jnp.concatenate([tb_ref[h, blk_idx[a][jp]] + row_mask[a][jp]
                                 for jp in range(WIN_ROWS // 2)], axis=1)
                for a in range(Q_ROWS)], axis=0)
            nt = (((1,), (1,)), ((), ()))
            s_w = lax.dot_general(q2, kwm, nt, preferred_element_type=F32) + bias
            s_c = lax.dot_general(q2, kcm, nt, preferred_element_type=F32)
            mx = jnp.maximum(jnp.max(s_w, axis=-1, keepdims=True), jnp.max(s_c, axis=-1, keepdims=True))
            e_w = jnp.exp(s_w - mx)
            e_c = jnp.exp(s_c - mx)
            l = jnp.sum(e_w, axis=-1, keepdims=True) + jnp.sum(e_c, axis=-1, keepdims=True)
            o = (jnp.dot(e_w.astype(BF16), vwm, preferred_element_type=F32)
                 + jnp.dot(e_c.astype(BF16), vcm, preferred_element_type=F32)) * (1.0 / l)
            o_pair = o if o_pair is None else o_pair + o
        outs.append(o_pair.astype(BF16))
    a_out = jnp.concatenate(outs, axis=1)
    _post_attention(a_out, sg_ref[...], x_ref[...], mod_ref[0], wout_ref, n2w_ref, wr_ref, br_ref,
                    xm_ref, h2_ref, gt_ref)


def _nbr_attention(q, k, v, cache_k, cache_v, layer, tb, sg, x, mod, wout, n2w, wr, br, *, batch, seq):
    t_tokens = x.shape[0]
    rows = seq // GRID_W
    tq = Q_ROWS * GRID_W
    nt = seq // tq
    tok = lambda b, t: (b * nt + t, 0)
    per_b = lambda b, t: (b, 0)
    full2 = lambda b, t: (0, 0)
    past = cache_k.shape[2]
    return pl.pallas_call(
        functools.partial(_nbr_attn_kernel, rows=rows),
        grid=(batch, nt),
        in_specs=[
            pl.BlockSpec((tq, ATTN_WIDTH), tok),
            pl.BlockSpec((seq, ATTN_WIDTH), per_b),
            pl.BlockSpec((seq, ATTN_WIDTH), per_b),
            pl.BlockSpec((1, 1, past, ATTN_WIDTH), lambda b, t: (b, layer, 0, 0)),
            pl.BlockSpec((1, 1, past, ATTN_WIDTH), lambda b, t: (b, layer, 0, 0)),
            pl.BlockSpec((N_HEADS, 2 * WIN_H, GRID_W, LANES), lambda b, t: (0, 0, 0, 0)),
            pl.BlockSpec((tq, SGU_WIDTH), tok),
            pl.BlockSpec((tq, D_MODEL), tok),
            pl.BlockSpec((1, 6, D_MODEL), lambda b, t: (1 + b, 0, 0)),
            pl.BlockSpec((D_MODEL, D_MODEL), full2),
            pl.BlockSpec((1, D_MODEL), full2),
            pl.BlockSpec((N_EXPERTS, D_MODEL), full2),
            pl.BlockSpec((N_EXPERTS, 1), full2),
        ],
        out_specs=[
            pl.BlockSpec((tq, D_MODEL), tok),
            pl.BlockSpec((tq, D_MODEL), tok),
            pl.BlockSpec((2 * N_EXPERTS, tq), lambda b, t: (0, b * nt + t)),
        ],
        out_shape=[
            jax.ShapeDtypeStruct((t_tokens, D_MODEL), F32),
            jax.ShapeDtypeStruct((t_tokens, D_MODEL), BF16),
            jax.ShapeDtypeStruct((2 * N_EXPERTS, t_tokens), F32),
        ],
        scratch_shapes=[pltpu.VMEM((past, ATTN_WIDTH), BF16), pltpu.VMEM((past, ATTN_WIDTH), BF16)],
        compiler_params=_cparams("arbitrary", "arbitrary"),
        name="nbr_attn",
    )(q, k, v, cache_k, cache_v, tb, sg, x, mod, wout, n2w, wr, br)


def _bias_table(rpb_l):
    cols = jnp.arange(GRID_W)
    col_start = jnp.clip(cols - WIN_W // 2, 0, GRID_W - WIN_W)
    col_ok = (cols[None, :] >= col_start[:, None]) & (cols[None, :] < col_start[:, None] + WIN_W)
    col_idx = jnp.clip(cols[None, :] - cols[:, None] + WIN_W - 1, 0, 2 * WIN_W - 2)
    t = jnp.where(col_ok[None, None], rpb_l[:, :, col_idx], NEG)
    t = jnp.pad(t, ((0, 0), (1, 1), (0, 0), (0, 0)))
    return jnp.concatenate([t[:, :-1], t[:, 1:]], axis=-1)


def _moe_kernel(h_ref, x_ref, gt_ref, mod_ref, ex_ref, w1_ref, w3_ref, w2_ref, o_ref, acc_ref):
    g = pl.program_id(1)

    @pl.when(g == 0)
    def _():
        acc_ref[...] = jnp.zeros_like(acc_ref)

    h = h_ref[...]
    gexp = lax.dot_general(gt_ref[...], ex_ref[...], (((0,), (0,)), ((), ())),
                           preferred_element_type=F32)
    hid = []
    for e in range(EXPERTS_PER_GROUP):
        h1 = jnp.dot(h, w1_ref[e], preferred_element_type=F32)
        h3 = jnp.dot(h, w3_ref[e], preferred_element_type=F32)
        act = (h1 * jax.nn.sigmoid(h1)) * h3
        hid.append((act * gexp[:, e * D_EXPERT:(e + 1) * D_EXPERT]).astype(BF16))
    acc_ref[...] += jnp.dot(jnp.concatenate(hid, axis=1), w2_ref[0], preferred_element_type=F32)

    @pl.when(g == N_EXPERT_GROUPS - 1)
    def _():
        g2 = mod_ref[0][5:6]
        o_ref[...] = x_ref[...] + g2 * acc_ref[...]


def _moe(h2, xm, gt, mod, row_of_tile, expand, w1, w3, w2, *, tm):
    t_tokens = xm.shape[0]
    tok = lambda i, g: (i, 0)
    return pl.pallas_call(
        _moe_kernel,
        grid=(t_tokens // tm, N_EXPERT_GROUPS),
        in_specs=[
            pl.BlockSpec((tm, D_MODEL), tok),
            pl.BlockSpec((tm, D_MODEL), tok),
            pl.BlockSpec((2 * EXPERTS_PER_GROUP, tm), lambda i, g: (g, i)),
            pl.BlockSpec((1, 6, D_MODEL), lambda i, g: (row_of_tile(i), 0, 0)),
            pl.BlockSpec((2 * EXPERTS_PER_GROUP, GROUP_HIDDEN), lambda i, g: (0, 0)),
            pl.BlockSpec((EXPERTS_PER_GROUP, D_MODEL, D_EXPERT), lambda i, g: (g, 0, 0)),
            pl.BlockSpec((EXPERTS_PER_GROUP, D_MODEL, D_EXPERT), lambda i, g: (g, 0, 0)),
            pl.BlockSpec((1, GROUP_HIDDEN, D_MODEL), lambda i, g: (g, 0, 0)),
        ],
        out_specs=pl.BlockSpec((tm, D_MODEL), tok),
        out_shape=jax.ShapeDtypeStruct((t_tokens, D_MODEL), F32),
        scratch_shapes=[pltpu.VMEM((tm, D_MODEL), F32)],
        compiler_params=_cparams("arbitrary", "arbitrary"),
        name="moe",
    )(h2, xm, gt, mod, expand, w1, w3, w2)


def kernel(x_prompt, x_sample, cache_k, cache_v, c, c_ctx, w_ada, b_ada, norm1_w, norm2_w, w_in,
           q_norm_w, k_norm_w, rpb, w_sgu, b_sgu, sgu_ln_w, sgu_ln_b, w_out, w_router, b_router,
           w1, w3, w2):
    batch, seq, _ = x_prompt.shape
    dec_batch, dec_seq, _ = x_sample.shape
    past = cache_k.shape[2]
    assert 1 + dec_batch <= COND_ROWS and dec_seq % (Q_ROWS * GRID_W) == 0 and seq % CHUNK == 0

    cond = jnp.zeros((COND_ROWS, D_MODEL), F32).at[0].set(c_ctx).at[1:1 + dec_batch].set(c)
    mod_all = _adaln(cond, w_ada, b_ada).reshape(DEPTH, COND_ROWS, 6, D_MODEL)

    xp = x_prompt.reshape(batch * seq, D_MODEL)
    xs = x_sample.reshape(dec_batch * dec_seq, D_MODEL)
    ck = cache_k.reshape(dec_batch, DEPTH, past, ATTN_WIDTH)
    cv = cache_v.reshape(dec_batch, DEPTH, past, ATTN_WIDTH)

    head_of = jnp.arange(256) // HEAD_DIM
    bd = (head_of[:, None] == head_of[None, :]).astype(BF16)
    bd = jnp.concatenate([bd, bd], axis=0)
    col_expert = jnp.arange(GROUP_HIDDEN) // D_EXPERT
    expand = (jnp.arange(EXPERTS_PER_GROUP)[:, None] == col_expert[None, :]).astype(F32)
    expand = jnp.concatenate([expand, expand], axis=0)

    wr_t = w_router.T.astype(BF16)
    br = b_router.reshape(N_EXPERTS, 1)
    w1b = w1.astype(BF16)
    w3b = w3.astype(BF16)
    w2b = w2.astype(BF16).reshape(DEPTH, N_EXPERT_GROUPS, GROUP_HIDDEN, D_MODEL)

    tm_ctx = seq
    tm_lat = 256
    tm_moe = 512
    lat_tiles_per_seq = dec_seq // tm_lat
    moe_tiles_per_seq = dec_seq // tm_moe
    ctx_row = lambda i: 0
    lat_row = lambda i: 1 + i // lat_tiles_per_seq
    lat_row_moe = lambda i: 1 + i // moe_tiles_per_seq

    ks, vs = [], []
    for l in range(DEPTH):
        mod = mod_all[l]
        n1w = norm1_w[l].reshape(1, D_MODEL)
        n2w = norm2_w[l].reshape(1, D_MODEL)
        win = w_in[l].astype(BF16)
        wout = w_out[l].astype(BF16)
        qw = jnp.tile(q_norm_w[l], N_HEADS).reshape(1, ATTN_WIDTH)
        kw = jnp.tile(k_norm_w[l], N_HEADS).reshape(1, ATTN_WIDTH)
        ws = w_sgu[l].astype(BF16)
        bs = jnp.broadcast_to(b_sgu[l][:, :, None], (SGU_GROUPS, CHUNK, SGU_GROUP_DIM))
        inproj = functools.partial(_inproj, n1w=n1w, win=win, qw=qw, kw=kw, bd=bd, ws=ws, bs=bs,
                                   lnw=sgu_ln_w[l], lnb=sgu_ln_b[l])
        moe = functools.partial(_moe, expand=expand, w1=w1b[l], w3=w3b[l], w2=w2b[l], tm=tm_moe)

        q, k, v, sg = inproj(xp, mod, ctx_row, tm=tm_ctx, kv_dtype=F32)
        ks.append(k)
        vs.append(v)
        xm, h2, gt = _ctx_attention(q, k, v, sg, xp, mod, wout, n2w, wr_t, br, seq=seq)
        xp = moe(h2, xm, gt, mod, ctx_row)

        q, k, v, sg = inproj(xs, mod, lat_row, tm=tm_lat, kv_dtype=BF16)
        xm, h2, gt = _nbr_attention(q, k, v, ck, cv, l, _bias_table(rpb[l]), sg, xs, mod, wout, n2w,
                                    wr_t, br, batch=dec_batch, seq=dec_seq)
        xs = moe(h2, xm, gt, mod, lat_row_moe)

    new_k = jnp.stack(ks, axis=0).reshape(DEPTH, batch, seq, N_HEADS, HEAD_DIM).transpose(1, 0, 2, 3, 4)
    new_v = jnp.stack(vs, axis=0).reshape(DEPTH, batch, seq, N_HEADS, HEAD_DIM).transpose(1, 0, 2, 3, 4)
    return (xp.reshape(batch, seq, D_MODEL), xs.reshape(dec_batch, dec_seq, D_MODEL), new_k, new_v)
```

```python
import functools

import jax
import jax.numpy as jnp
from jax import lax
from jax.experimental import pallas as pl
from jax.experimental.pallas import tpu as pltpu

F32 = jnp.float32
BF16 = jnp.bfloat16

D_MODEL = 1024
DEPTH = 4
N_HEADS = 8
HEAD_DIM = 64
ATTN_WIDTH = N_HEADS * HEAD_DIM
SGU_GROUPS = 4
SGU_GROUP_DIM = 128
SGU_WIDTH = SGU_GROUPS * SGU_GROUP_DIM
CHUNK = 128
IN_WIDTH = 3 * ATTN_WIDTH + 2 * SGU_WIDTH
GRID_W = 64
WIN_H = 8
WIN_W = 16
N_EXPERTS = 16
N_EXPERT_GROUPS = 4
EXPERTS_PER_GROUP = 4
D_EXPERT = 256
GROUP_HIDDEN = EXPERTS_PER_GROUP * D_EXPERT
EPS = 1e-6
NEG = -1e30

LANES = 128
HEADS_PER_VREG = LANES // HEAD_DIM
N_HEAD_PAIRS = N_HEADS // HEADS_PER_VREG
COND_ROWS = 16
Q_ROWS = 4
WIN_ROWS = Q_ROWS + WIN_H
VMEM_LIMIT = 48 * 1024 * 1024


def _cparams(*sem):
    return pltpu.CompilerParams(dimension_semantics=sem, vmem_limit_bytes=VMEM_LIMIT)


def _adaln_kernel(cond_ref, w_ref, b_ref, o_ref):
    c = cond_ref[...]
    a = (c * jax.nn.sigmoid(c)).astype(BF16)
    o_ref[0] = jnp.dot(a, w_ref[0].astype(BF16), preferred_element_type=F32) + b_ref[0]


def _adaln(cond, w_ada, b_ada):
    tn = 1536
    n = 6 * D_MODEL
    return pl.pallas_call(
        _adaln_kernel,
        grid=(DEPTH, n // tn),
        in_specs=[
            pl.BlockSpec((COND_ROWS, D_MODEL), lambda l, j: (0, 0)),
            pl.BlockSpec((1, D_MODEL, tn), lambda l, j: (l, 0, j)),
            pl.BlockSpec((1, 1, tn), lambda l, j: (l, 0, j)),
        ],
        out_specs=pl.BlockSpec((1, COND_ROWS, tn), lambda l, j: (l, 0, j)),
        out_shape=jax.ShapeDtypeStruct((DEPTH, COND_ROWS, n), F32),
        compiler_params=_cparams("arbitrary", "arbitrary"),
        name="adaln",
    )(cond, w_ada, b_ada.reshape(DEPTH, 1, n))


def _gelu_tanh(x):
    return 0.5 * x * (1.0 + jnp.tanh(0.7978845608028654 * (x + 0.044715 * (x * x * x))))


def _head_rms(t, bd, w):
    t2 = t * t
    hi = t2.astype(BF16)
    lo = (t2 - hi.astype(F32)).astype(BF16)
    outs = []
    for c in range(ATTN_WIDTH // 256):
        sl = slice(256 * c, 256 * c + 256)
        hl = jnp.concatenate([hi[:, sl], lo[:, sl]], axis=1)
        ss = jnp.dot(hl, bd, preferred_element_type=F32)
        outs.append(t[:, sl] * lax.rsqrt(ss * (1.0 / HEAD_DIM) + EPS))
    return jnp.concatenate(outs, axis=1) * w


def _inproj_kernel(*refs, tm, transposed_kv):
    (x_ref, mod_ref, n1w_ref, win_ref, qw_ref, kw_ref, bd_ref, ws_ref, bs_ref, lnw_ref,
     lnb_ref) = refs[:11]
    if transposed_kv:
        q_ref, kt_ref, vt_ref, v_ref, sg_ref = refs[13:]
    else:
        q_ref, k_ref, v_ref, sg_ref = refs[11:]
    x = x_ref[...]
    m = mod_ref[0, 0]
    sh1, sc1 = m[0:1], m[1:2]
    ms = jnp.mean(x * x, axis=-1, keepdims=True)
    h = x * lax.rsqrt(ms + EPS) * n1w_ref[0]
    h = (h * (1.0 + sc1) + sh1).astype(BF16)
    p = jnp.dot(h, win_ref[0], preferred_element_type=F32)

    bd = bd_ref[...]
    q = _head_rms(p[:, 0:ATTN_WIDTH], bd, qw_ref[0])
    k = _head_rms(p[:, ATTN_WIDTH:2 * ATTN_WIDTH], bd, kw_ref[0])
    v = p[:, 2 * ATTN_WIDTH:3 * ATTN_WIDTH]
    q_ref[...] = (q * (HEAD_DIM ** -0.5)).astype(q_ref.dtype)
    if transposed_kv:
        kt_ref[0, 0] = k.T
        vt_ref[0, 0] = v.T
    else:
        k_ref[...] = k.astype(k_ref.dtype)
    v_ref[...] = v.astype(v_ref.dtype)

    off_u = 3 * ATTN_WIDTH
    off_v = off_u + SGU_WIDTH
    for g in range(SGU_GROUPS):
        gl = slice(g * SGU_GROUP_DIM, (g + 1) * SGU_GROUP_DIM)
        u = _gelu_tanh(p[:, off_u + g * SGU_GROUP_DIM: off_u + (g + 1) * SGU_GROUP_DIM])
        t = _gelu_tanh(p[:, off_v + g * SGU_GROUP_DIM: off_v + (g + 1) * SGU_GROUP_DIM])
        mu = jnp.mean(t, axis=-1, keepdims=True)
        d = t - mu
        var = jnp.mean(d * d, axis=-1, keepdims=True)
        y = (d * lax.rsqrt(var + EPS) * lnw_ref[0, g:g + 1, :] + lnb_ref[0, g:g + 1, :]).astype(BF16)
        for c in range(tm // CHUNK):
            rows = slice(c * CHUNK, (c + 1) * CHUNK)
            sv = jnp.dot(ws_ref[0, g], y[rows], preferred_element_type=F32) + bs_ref[0, g]
            sg_ref[rows, gl] = (u[rows] * sv).astype(sg_ref.dtype)


def _inproj(x, layer, mod, row_of_tile, params, *, tm, kv_buffers=None):
    t_tokens = x.shape[0]
    tok = lambda i: (i, 0)
    lyr3 = lambda i: (layer, 0, 0)
    lyr4 = lambda i: (layer, 0, 0, 0)
    act = pl.BlockSpec((tm, ATTN_WIDTH), tok)
    act_sds = jax.ShapeDtypeStruct((t_tokens, ATTN_WIDTH), BF16)
    in_specs = [
        pl.BlockSpec((tm, D_MODEL), tok),
        pl.BlockSpec((1, 1, 6, D_MODEL), lambda i: (layer, row_of_tile(i), 0, 0)),
        pl.BlockSpec((1, 1, D_MODEL), lyr3),
        pl.BlockSpec((1, D_MODEL, IN_WIDTH), lyr3),
        pl.BlockSpec((1, 1, ATTN_WIDTH), lyr3),
        pl.BlockSpec((1, 1, ATTN_WIDTH), lyr3),
        pl.BlockSpec((512, 256), lambda i: (0, 0)),
        pl.BlockSpec((1, SGU_GROUPS, CHUNK, CHUNK), lyr4),
        pl.BlockSpec((1, SGU_GROUPS, CHUNK, SGU_GROUP_DIM), lyr4),
        pl.BlockSpec((1, SGU_GROUPS, SGU_GROUP_DIM), lyr3),
        pl.BlockSpec((1, SGU_GROUPS, SGU_GROUP_DIM), lyr3),
    ]
    args = [x, mod, params["n1w"], params["win"], params["qw"], params["kw"], params["bd"],
            params["ws"], params["bs"], params["lnw"], params["lnb"]]
    if kv_buffers is None:
        out_specs = [act] * 4
        out_shape = [act_sds] * 4
        aliases = {}
    else:
        kbuf, vbuf = kv_buffers
        kv_spec = pl.BlockSpec((1, 1, ATTN_WIDTH, tm), lambda i: (i, layer, 0, 0))
        in_specs += [pl.BlockSpec(memory_space=pl.ANY)] * 2
        args += [kbuf, vbuf]
        out_specs = [act, kv_spec, kv_spec, act, act]
        out_shape = [act_sds, jax.ShapeDtypeStruct(kbuf.shape, F32),
                     jax.ShapeDtypeStruct(vbuf.shape, F32), act_sds, act_sds]
        aliases = {11: 1, 12: 2}
    return pl.pallas_call(
        functools.partial(_inproj_kernel, tm=tm, transposed_kv=kv_buffers is not None),
        grid=(t_tokens // tm,),
        in_specs=in_specs,
        out_specs=out_specs,
        out_shape=out_shape,
        input_output_aliases=aliases,
        compiler_params=_cparams("arbitrary"),
        name="inproj",
    )(*args)


def _route(lg_t, br):
    s = jax.nn.sigmoid(lg_t)
    sel = s + br
    row = lambda a, i: a[i:i + 1]
    g_score = []
    for g in range(N_EXPERT_GROUPS):
        v = [row(sel, 4 * g + i) for i in range(4)]
        best_pair = None
        for i in range(4):
            for j in range(i + 1, 4):
                pair = v[i] + v[j]
                best_pair = pair if best_pair is None else jnp.maximum(best_pair, pair)
        g_score.append(best_pair)
    best = jnp.zeros_like(g_score[0], dtype=jnp.int32)
    top = g_score[0]
    for g in range(1, N_EXPERT_GROUPS):
        upd = g_score[g] > top
        best = jnp.where(upd, g, best)
        top = jnp.where(upd, g_score[g], top)
    cand, aff = [], []
    for i in range(4):
        ci, si = row(sel, i), row(s, i)
        for g in range(1, N_EXPERT_GROUPS):
            ci = jnp.where(best == g, row(sel, 4 * g + i), ci)
            si = jnp.where(best == g, row(s, 4 * g + i), si)
        cand.append(ci)
        aff.append(si)

    def first_argmax(vals):
        idx = jnp.zeros_like(best)
        top_v = vals[0]
        for i in range(1, 4):
            upd = vals[i] > top_v
            idx = jnp.where(upd, i, idx)
            top_v = jnp.where(upd, vals[i], top_v)
        return idx

    i1 = first_argmax(cand)
    i2 = first_argmax([jnp.where(i1 == i, -jnp.inf, cand[i]) for i in range(4)])
    pick = lambda idx: sum(jnp.where(idx == i, aff[i], 0.0) for i in range(4))
    den = pick(i1) + pick(i2)
    gate = [jnp.where((i1 == i) | (i2 == i), aff[i] / den, 0.0) for i in range(4)]
    rows = []
    for g in range(N_EXPERT_GROUPS):
        gg = [jnp.where(best == g, gate[i], 0.0) for i in range(4)]
        hi = [x.astype(BF16).astype(F32) for x in gg]
        lo = [(x - h).astype(BF16).astype(F32) for x, h in zip(gg, hi)]
        rows += hi + lo
    return jnp.concatenate(rows, axis=0)


def _post_attention(a, sg, x, m, wout_ref, n2w_ref, wr_ref, br_ref, xm_ref, h2_ref, gt_ref):
    g1, sh2, sc2 = m[2:3], m[3:4], m[4:5]
    cat = jnp.concatenate([a, sg], axis=1)
    y = jnp.dot(cat, wout_ref[0], preferred_element_type=F32)
    xm = x + g1 * y
    xm_ref[...] = xm
    ms = jnp.mean(xm * xm, axis=-1, keepdims=True)
    h2 = xm * lax.rsqrt(ms + EPS) * n2w_ref[0]
    h2 = (h2 * (1.0 + sc2) + sh2).astype(BF16)
    h2_ref[...] = h2
    lg_t = lax.dot_general(wr_ref[...], h2, (((1,), (1,)), ((), ())), preferred_element_type=F32)
    gt_ref[...] = _route(lg_t, br_ref[...])


def _head_mask(hh, axis):
    shape = (1, LANES) if axis == 1 else (LANES, 1)
    pos = lax.broadcasted_iota(jnp.int32, shape, axis)
    return (pos >= hh * HEAD_DIM) & (pos < (hh + 1) * HEAD_DIM)


def _ctx_attn_kernel(q_ref, kt_ref, v_ref, sg_ref, x_ref, mod_ref, wout_ref, n2w_ref, wr_ref, br_ref,
                     xm_ref, h2_ref, gt_ref):
    outs = []
    for j in range(N_HEAD_PAIRS):
        cols = slice(j * LANES, (j + 1) * LANES)
        q2 = q_ref[:, cols]
        kt2 = kt_ref[0, 0, cols, :].astype(BF16)
        v2 = v_ref[:, cols]
        o_pair = None
        for hh in range(HEADS_PER_VREG):
            ktm = jnp.where(_head_mask(hh, 0), kt2, jnp.zeros_like(kt2))
            vm = jnp.where(_head_mask(hh, 1), v2, jnp.zeros_like(v2))
            s = jnp.dot(q2, ktm, preferred_element_type=F32)
            mx = jnp.max(s, axis=-1, keepdims=True)
            e = jnp.exp(s - mx)
            l = jnp.sum(e, axis=-1, keepdims=True)
            o = jnp.dot(e.astype(BF16), vm, preferred_element_type=F32) * (1.0 / l)
            o_pair = o if o_pair is None else o_pair + o
        outs.append(o_pair.astype(BF16))
    a = jnp.concatenate(outs, axis=1)
    _post_attention(a, sg_ref[...], x_ref[...], mod_ref[0, 0], wout_ref, n2w_ref, wr_ref, br_ref,
                    xm_ref, h2_ref, gt_ref)


def _ctx_attention(q, kbuf, v, sg, x, layer, mod, params, *, seq):
    t_tokens = x.shape[0]
    tok = lambda b: (b, 0)
    lyr3 = lambda b: (layer, 0, 0)
    full2 = lambda b: (0, 0)
    return pl.pallas_call(
        _ctx_attn_kernel,
        grid=(t_tokens // seq,),
        in_specs=[
            pl.BlockSpec((seq, ATTN_WIDTH), tok),
            pl.BlockSpec((1, 1, ATTN_WIDTH, seq), lambda b: (b, layer, 0, 0)),
            pl.BlockSpec((seq, ATTN_WIDTH), tok),
            pl.BlockSpec((seq, SGU_WIDTH), tok),
            pl.BlockSpec((seq, D_MODEL), tok),
            pl.BlockSpec((1, 1, 6, D_MODEL), lambda b: (layer, 0, 0, 0)),
            pl.BlockSpec((1, D_MODEL, D_MODEL), lyr3),
            pl.BlockSpec((1, 1, D_MODEL), lyr3),
            pl.BlockSpec((N_EXPERTS, D_MODEL), full2),
            pl.BlockSpec((N_EXPERTS, 1), full2),
        ],
        out_specs=[
            pl.BlockSpec((seq, D_MODEL), tok),
            pl.BlockSpec((seq, D_MODEL), tok),
            pl.BlockSpec((2 * N_EXPERTS, seq), lambda b: (0, b)),
        ],
        out_shape=[
            jax.ShapeDtypeStruct((t_tokens, D_MODEL), F32),
            jax.ShapeDtypeStruct((t_tokens, D_MODEL), BF16),
            jax.ShapeDtypeStruct((2 * N_EXPERTS, t_tokens), F32),
        ],
        compiler_params=_cparams("arbitrary"),
        name="ctx_attn",
    )(q, kbuf, v, sg, x, mod, params["wout"], params["n2w"], params["wr"], params["br"])


def _row_start(r, rows):
    return jnp.clip(r - WIN_H // 2, 0, rows - WIN_H)


def _nbr_attn_kernel(q_ref, k_ref, v_ref, ck_ref, cv_ref, tb_ref, sg_ref, x_ref, mod_ref, wout_ref,
                     n2w_ref, wr_ref, br_ref, xm_ref, h2_ref, gt_ref,
                     kwm_ref, vwm_ref, kcm_ref, vcm_ref, *, rows):
    t = pl.program_id(1)

    @pl.when(t == 0)
    def _():
        for j in range(N_HEAD_PAIRS):
            cols = slice(j * LANES, (j + 1) * LANES)
            k2, v2 = k_ref[:, cols], v_ref[:, cols]
            for hh in range(HEADS_PER_VREG):
                msk = _head_mask(hh, 1)
                kwm_ref[HEADS_PER_VREG * j + hh] = jnp.where(msk, k2, jnp.zeros_like(k2))
                vwm_ref[HEADS_PER_VREG * j + hh] = jnp.where(msk, v2, jnp.zeros_like(v2))
            for src, dst in ((ck_ref, kcm_ref), (cv_ref, vcm_ref)):
                c2 = src[0, 0, cols, :].astype(BF16)
                for hh in range(HEADS_PER_VREG):
                    dst[HEADS_PER_VREG * j + hh] = jnp.where(_head_mask(hh, 0), c2, jnp.zeros_like(c2))

    r0 = t * Q_ROWS
    ws = jnp.minimum(_row_start(r0, rows), rows - WIN_ROWS)
    tok0 = pl.multiple_of(ws * GRID_W, GRID_W)

    lane = lax.broadcasted_iota(jnp.int32, (1, LANES), 1)
    blk_idx, row_mask = [], []
    for a in range(Q_ROWS):
        r = r0 + a
        rs = _row_start(r, rows)
        idx_a, mask_a = [], []
        for jp in range(WIN_ROWS // 2):
            kr = ws + 2 * jp
            idx_a.append(jnp.clip(kr - r + WIN_H, 0, 2 * WIN_H - 1))
            ok0 = (kr >= rs) & (kr < rs + WIN_H)
            ok1 = (kr + 1 >= rs) & (kr + 1 < rs + WIN_H)
            m0 = jnp.where(ok0, 0.0, NEG).astype(F32)
            m1 = jnp.where(ok1, 0.0, NEG).astype(F32)
            mask_a.append(jnp.where(lane < GRID_W, m0, m1))
        blk_idx.append(idx_a)
        row_mask.append(mask_a)

    nt = (((1,), (1,)), ((), ()))
    outs = []
    for j in range(N_HEAD_PAIRS):
        q2 = q_ref[:, j * LANES:(j + 1) * LANES]
        o_pair = None
        for hh in range(HEADS_PER_VREG):
            h = j * HEADS_PER_VREG + hh
            kwm = kwm_ref[h, pl.ds(tok0, WIN_ROWS * GRID_W), :]
            vwm = vwm_ref[h, pl.ds(tok0, WIN_ROWS * GRID_W), :]
            bias = jnp.concatenate([
                jnp.concatenate([tb_ref[0, h, blk_idx[a][jp]] + row_mask[a][jp]
                                 for jp in range(WIN_ROWS // 2)], axis=1)
                for a in range(Q_ROWS)], axis=0)
            s_w = lax.dot_general(q2, kwm, nt, preferred_element_type=F32) + bias
            s_c = jnp.dot(q2, kcm_ref[h], preferred_element_type=F32)
            mx = jnp.maximum(jnp.max(s_w, axis=-1, keepdims=True), jnp.max(s_c, axis=-1, keepdims=True))
            e_w = jnp.exp(s_w - mx)
            e_c = jnp.exp(s_c - mx)
            l = jnp.sum(e_w, axis=-1, keepdims=True) + jnp.sum(e_c, axis=-1, keepdims=True)
            o = (jnp.dot(e_w.astype(BF16), vwm, preferred_element_type=F32)
                 + lax.dot_general(e_c.astype(BF16), vcm_ref[h], nt,
                                   preferred_element_type=F32)) * (1.0 / l)
            o_pair = o if o_pair is None else o_pair + o
        outs.append(o_pair.astype(BF16))
    a_out = jnp.concatenate(outs, axis=1)
    _post_attention(a_out, sg_ref[...], x_ref[...], mod_ref[0, 0], wout_ref, n2w_ref, wr_ref, br_ref,
                    xm_ref, h2_ref, gt_ref)


def _nbr_attention(q, k, v, cache_k, cache_v, tb, sg, x, layer, mod, params, *, batch, seq):
    t_tokens = x.shape[0]
    rows = seq // GRID_W
    tq = Q_ROWS * GRID_W
    nt = seq // tq
    past = cache_k.shape[3]
    tok = lambda b, t: (b * nt + t, 0)
    per_b = lambda b, t: (b, 0)
    lyr3 = lambda b, t: (layer, 0, 0)
    full2 = lambda b, t: (0, 0)
    cache_spec = pl.BlockSpec((1, 1, ATTN_WIDTH, past), lambda b, t: (b, layer, 0, 0))
    masked = lambda n: pltpu.VMEM((N_HEADS, n, LANES), BF16)
    masked_t = pltpu.VMEM((N_HEADS, LANES, past), BF16)
    return pl.pallas_call(
        functools.partial(_nbr_attn_kernel, rows=rows),
        grid=(batch, nt),
        in_specs=[
            pl.BlockSpec((tq, ATTN_WIDTH), tok),
            pl.BlockSpec((seq, ATTN_WIDTH), per_b),
            pl.BlockSpec((seq, ATTN_WIDTH), per_b),
            cache_spec,
            cache_spec,
            pl.BlockSpec((1, N_HEADS, 2 * WIN_H, GRID_W, LANES), lambda b, t: (layer, 0, 0, 0, 0)),
            pl.BlockSpec((tq, SGU_WIDTH), tok),
            pl.BlockSpec((tq, D_MODEL), tok),
            pl.BlockSpec((1, 1, 6, D_MODEL), lambda b, t: (layer, 1 + b, 0, 0)),
            pl.BlockSpec((1, D_MODEL, D_MODEL), lyr3),
            pl.BlockSpec((1, 1, D_MODEL), lyr3),
            pl.BlockSpec((N_EXPERTS, D_MODEL), full2),
            pl.BlockSpec((N_EXPERTS, 1), full2),
        ],
        out_specs=[
            pl.BlockSpec((tq, D_MODEL), tok),
            pl.BlockSpec((tq, D_MODEL), tok),
            pl.BlockSpec((2 * N_EXPERTS, tq), lambda b, t: (0, b * nt + t)),
        ],
        out_shape=[
            jax.ShapeDtypeStruct((t_tokens, D_MODEL), F32),
            jax.ShapeDtypeStruct((t_tokens, D_MODEL), BF16),
            jax.ShapeDtypeStruct((2 * N_EXPERTS, t_tokens), F32),
        ],
        scratch_shapes=[masked(seq), masked(seq), masked_t, masked_t],
        compiler_params=_cparams("arbitrary", "arbitrary"),
        name="nbr_attn",
    )(q, k, v, cache_k, cache_v, tb, sg, x, mod, params["wout"], params["n2w"], params["wr"],
      params["br"])


def _bias_tables(rpb):
    cols = jnp.arange(GRID_W)
    col_start = jnp.clip(cols - WIN_W // 2, 0, GRID_W - WIN_W)
    col_ok = (cols[None, :] >= col_start[:, None]) & (cols[None, :] < col_start[:, None] + WIN_W)
    rel = cols[None, :] - cols[:, None] + WIN_W - 1
    pick = ((rel[None] == jnp.arange(2 * WIN_W - 1)[:, None, None]) & col_ok[None]).astype(F32)
    t = jnp.einsum("lhxm,mqk->lhxqk", rpb, pick, precision=lax.Precision.HIGHEST)
    t = jnp.where(col_ok, t, NEG)
    t = jnp.pad(t, ((0, 0), (0, 0), (1, 1), (0, 0), (0, 0)))
    return jnp.concatenate([t[:, :, :-1], t[:, :, 1:]], axis=-1)


def _moe_kernel(h_ref, x_ref, gt_ref, mod_ref, ex_ref, w1_ref, w3_ref, w2_ref, o_ref, acc_ref):
    g = pl.program_id(1)

    @pl.when(g == 0)
    def _():
        acc_ref[...] = jnp.zeros_like(acc_ref)

    h = h_ref[...]
    gexp = lax.dot_general(gt_ref[...], ex_ref[...], (((0,), (0,)), ((), ())),
                           preferred_element_type=F32)
    hid = []
    for e in range(EXPERTS_PER_GROUP):
        h1 = jnp.dot(h, w1_ref[0, e], preferred_element_type=F32)
        h3 = jnp.dot(h, w3_ref[0, e], preferred_element_type=F32)
        act = (h1 * jax.nn.sigmoid(h1)) * h3
        hid.append((act * gexp[:, e * D_EXPERT:(e + 1) * D_EXPERT]).astype(BF16))
    acc_ref[...] += jnp.dot(jnp.concatenate(hid, axis=1), w2_ref[0, 0], preferred_element_type=F32)

    @pl.when(g == N_EXPERT_GROUPS - 1)
    def _():
        g2 = mod_ref[0, 0][5:6]
        o_ref[...] = x_ref[...] + g2 * acc_ref[...]


def _moe(h2, xm, gt, layer, mod, row_of_tile, params, *, tm):
    t_tokens = xm.shape[0]
    tok = lambda i, g: (i, 0)
    expert_w = pl.BlockSpec((1, EXPERTS_PER_GROUP, D_MODEL, D_EXPERT), lambda i, g: (layer, g, 0, 0))
    return pl.pallas_call(
        _moe_kernel,
        grid=(t_tokens // tm, N_EXPERT_GROUPS),
        in_specs=[
            pl.BlockSpec((tm, D_MODEL), tok),
            pl.BlockSpec((tm, D_MODEL), tok),
            pl.BlockSpec((2 * EXPERTS_PER_GROUP, tm), lambda i, g: (g, i)),
            pl.BlockSpec((1, 1, 6, D_MODEL), lambda i, g: (layer, row_of_tile(i), 0, 0)),
            pl.BlockSpec((2 * EXPERTS_PER_GROUP, GROUP_HIDDEN), lambda i, g: (0, 0)),
            expert_w,
            expert_w,
            pl.BlockSpec((1, 1, GROUP_HIDDEN, D_MODEL), lambda i, g: (layer, g, 0, 0)),
        ],
        out_specs=pl.BlockSpec((tm, D_MODEL), tok),
        out_shape=jax.ShapeDtypeStruct((t_tokens, D_MODEL), F32),
        scratch_shapes=[pltpu.VMEM((tm, D_MODEL), F32)],
        compiler_params=_cparams("arbitrary", "arbitrary"),
        name="moe",
    )(h2, xm, gt, mod, params["expand"], params["w1"], params["w3"], params["w2"])


def kernel(x_prompt, x_sample, cache_k, cache_v, c, c_ctx, w_ada, b_ada, norm1_w, norm2_w, w_in,
           q_norm_w, k_norm_w, rpb, w_sgu, b_sgu, sgu_ln_w, sgu_ln_b, w_out, w_router, b_router,
           w1, w3, w2):
    batch, seq, _ = x_prompt.shape
    dec_batch, dec_seq, _ = x_sample.shape
    past = cache_k.shape[2]
    assert 1 + dec_batch <= COND_ROWS and dec_seq % (Q_ROWS * GRID_W) == 0 and seq % CHUNK == 0

    cond = jnp.zeros((COND_ROWS, D_MODEL), F32).at[0].set(c_ctx).at[1:1 + dec_batch].set(c)
    mod = _adaln(cond, w_ada, b_ada).reshape(DEPTH, COND_ROWS, 6, D_MODEL)

    xp = x_prompt.reshape(batch * seq, D_MODEL)
    xs = x_sample.reshape(dec_batch * dec_seq, D_MODEL)
    from_cache_layout = lambda a: a.transpose(0, 1, 3, 4, 2).reshape(dec_batch, DEPTH, ATTN_WIDTH, past)
    ck = from_cache_layout(cache_k)
    cv = from_cache_layout(cache_v)

    head_of = jnp.arange(256) // HEAD_DIM
    bd = (head_of[:, None] == head_of[None, :]).astype(BF16)
    col_expert = jnp.arange(GROUP_HIDDEN) // D_EXPERT
    expand = (jnp.arange(EXPERTS_PER_GROUP)[:, None] == col_expert[None, :]).astype(F32)

    params = dict(
        n1w=norm1_w.reshape(DEPTH, 1, D_MODEL),
        n2w=norm2_w.reshape(DEPTH, 1, D_MODEL),
        win=w_in.astype(BF16),
        wout=w_out.astype(BF16),
        qw=jnp.tile(q_norm_w, (1, N_HEADS)).reshape(DEPTH, 1, ATTN_WIDTH),
        kw=jnp.tile(k_norm_w, (1, N_HEADS)).reshape(DEPTH, 1, ATTN_WIDTH),
        bd=jnp.concatenate([bd, bd], axis=0),
        ws=w_sgu.astype(BF16),
        bs=jnp.broadcast_to(b_sgu[..., None], (DEPTH, SGU_GROUPS, CHUNK, SGU_GROUP_DIM)),
        lnw=sgu_ln_w,
        lnb=sgu_ln_b,
        wr=w_router.T.astype(BF16),
        br=b_router.reshape(N_EXPERTS, 1),
        expand=jnp.concatenate([expand, expand], axis=0),
        w1=w1.astype(BF16),
        w3=w3.astype(BF16),
        w2=w2.astype(BF16).reshape(DEPTH, N_EXPERT_GROUPS, GROUP_HIDDEN, D_MODEL),
    )
    tb = _bias_tables(rpb)

    tm_lat = 256
    tm_moe = 512
    lat_tiles_per_seq = dec_seq // tm_lat
    moe_tiles_per_seq = dec_seq // tm_moe
    ctx_row = lambda i: 0
    lat_row = lambda i: 1 + i // lat_tiles_per_seq
    lat_row_moe = lambda i: 1 + i // moe_tiles_per_seq

    kbuf = jnp.zeros((batch, DEPTH, ATTN_WIDTH, seq), F32)
    vbuf = jnp.zeros((batch, DEPTH, ATTN_WIDTH, seq), F32)

    for l in range(DEPTH):
        q, kbuf, vbuf, v, sg = _inproj(xp, l, mod, ctx_row, params, tm=seq, kv_buffers=(kbuf, vbuf))
        xm, h2, gt = _ctx_attention(q, kbuf, v, sg, xp, l, mod, params, seq=seq)
        xp = _moe(h2, xm, gt, l, mod, ctx_row, params, tm=tm_moe)

        q, k, v, sg = _inproj(xs, l, mod, lat_row, params, tm=tm_lat)
        xm, h2, gt = _nbr_attention(q, k, v, ck, cv, tb, sg, xs, l, mod, params,
                                    batch=dec_batch, seq=dec_seq)
        xs = _moe(h2, xm, gt, l, mod, lat_row_moe, params, tm=tm_moe)

    to_cache_layout = lambda buf: buf.reshape(batch, DEPTH, N_HEADS, HEAD_DIM, seq).transpose(0, 1, 4, 2, 3)
    return (xp.reshape(batch, seq, D_MODEL), xs.reshape(dec_batch, dec_seq, D_MODEL),
            to_cache_layout(kbuf), to_cache_layout(vbuf))
```

```python
import functools

import jax
import jax.numpy as jnp
from jax import lax
from jax.experimental import pallas as pl
from jax.experimental.pallas import tpu as pltpu

F32 = jnp.float32
BF16 = jnp.bfloat16

D_MODEL = 1024
DEPTH = 4
N_HEADS = 8
HEAD_DIM = 64
ATTN_WIDTH = N_HEADS * HEAD_DIM
SGU_GROUPS = 4
SGU_GROUP_DIM = 128
SGU_WIDTH = SGU_GROUPS * SGU_GROUP_DIM
CHUNK = 128
IN_WIDTH = 3 * ATTN_WIDTH + 2 * SGU_WIDTH
GRID_W = 64
WIN_H = 8
WIN_W = 16
N_EXPERTS = 16
N_EXPERT_GROUPS = 4
EXPERTS_PER_GROUP = 4
D_EXPERT = 256
GROUP_HIDDEN = EXPERTS_PER_GROUP * D_EXPERT
EPS = 1e-6
NEG = -1e30

LANES = 128
HEADS_PER_VREG = LANES // HEAD_DIM
N_HEAD_PAIRS = N_HEADS // HEADS_PER_VREG
COND_ROWS = 16
Q_ROWS = 4
WIN_ROWS = Q_ROWS + WIN_H
VMEM_LIMIT = 48 * 1024 * 1024

SRC_TILE = 256
CHUNK_ROWS = 16
SLAB_ROWS = SRC_TILE + N_EXPERT_GROUPS * CHUNK_ROWS
SLAB_CHUNKS = SLAB_ROWS // CHUNK_ROWS
SLAB_WIDTH = D_MODEL + LANES
MOE_TILE = 256
TILE_CHUNKS = MOE_TILE // CHUNK_ROWS


def _cparams(*sem):
    return pltpu.CompilerParams(dimension_semantics=sem, vmem_limit_bytes=VMEM_LIMIT)


def _adaln_kernel(cond_ref, w_ref, b_ref, o_ref):
    c = cond_ref[...]
    a = (c * jax.nn.sigmoid(c)).astype(BF16)
    o_ref[0] = jnp.dot(a, w_ref[0].astype(BF16), preferred_element_type=F32) + b_ref[0]


def _adaln(cond, w_ada, b_ada):
    tn = 1536
    n = 6 * D_MODEL
    return pl.pallas_call(
        _adaln_kernel,
        grid=(DEPTH, n // tn),
        in_specs=[
            pl.BlockSpec((COND_ROWS, D_MODEL), lambda l, j: (0, 0)),
            pl.BlockSpec((1, D_MODEL, tn), lambda l, j: (l, 0, j)),
            pl.BlockSpec((1, 1, tn), lambda l, j: (l, 0, j)),
        ],
        out_specs=pl.BlockSpec((1, COND_ROWS, tn), lambda l, j: (l, 0, j)),
        out_shape=jax.ShapeDtypeStruct((DEPTH, COND_ROWS, n), F32),
        compiler_params=_cparams("arbitrary", "arbitrary"),
        name="adaln",
    )(cond, w_ada, b_ada.reshape(DEPTH, 1, n))


def _gelu_tanh(x):
    return 0.5 * x * (1.0 + jnp.tanh(0.7978845608028654 * (x + 0.044715 * (x * x * x))))


def _head_rms(t, bd, w):
    t2 = t * t
    hi = t2.astype(BF16)
    lo = (t2 - hi.astype(F32)).astype(BF16)
    outs = []
    for c in range(ATTN_WIDTH // 256):
        sl = slice(256 * c, 256 * c + 256)
        hl = jnp.concatenate([hi[:, sl], lo[:, sl]], axis=1)
        ss = jnp.dot(hl, bd, preferred_element_type=F32)
        outs.append(t[:, sl] * lax.rsqrt(ss * (1.0 / HEAD_DIM) + EPS))
    return jnp.concatenate(outs, axis=1) * w


def _inproj_kernel(*refs, tm, transposed_kv):
    (x_ref, mod_ref, n1w_ref, win_ref, qw_ref, kw_ref, bd_ref, ws_ref, bs_ref, lnw_ref,
     lnb_ref) = refs[:11]
    if transposed_kv:
        q_ref, kt_ref, vt_ref, v_ref, sg_ref = refs[13:]
    else:
        q_ref, k_ref, v_ref, sg_ref = refs[11:]
    x = x_ref[...]
    m = mod_ref[0, 0]
    sh1, sc1 = m[0:1], m[1:2]
    ms = jnp.mean(x * x, axis=-1, keepdims=True)
    h = x * lax.rsqrt(ms + EPS) * n1w_ref[0]
    h = (h * (1.0 + sc1) + sh1).astype(BF16)
    p = jnp.dot(h, win_ref[0], preferred_element_type=F32)

    bd = bd_ref[...]
    q = _head_rms(p[:, 0:ATTN_WIDTH], bd, qw_ref[0])
    k = _head_rms(p[:, ATTN_WIDTH:2 * ATTN_WIDTH], bd, kw_ref[0])
    v = p[:, 2 * ATTN_WIDTH:3 * ATTN_WIDTH]
    q_ref[...] = (q * (HEAD_DIM ** -0.5)).astype(q_ref.dtype)
    if transposed_kv:
        kt_ref[0, 0] = k.T
        vt_ref[0, 0] = v.T
    else:
        k_ref[...] = k.astype(k_ref.dtype)
    v_ref[...] = v.astype(v_ref.dtype)

    off_u = 3 * ATTN_WIDTH
    off_v = off_u + SGU_WIDTH
    for g in range(SGU_GROUPS):
        gl = slice(g * SGU_GROUP_DIM, (g + 1) * SGU_GROUP_DIM)
        u = _gelu_tanh(p[:, off_u + g * SGU_GROUP_DIM: off_u + (g + 1) * SGU_GROUP_DIM])
        t = _gelu_tanh(p[:, off_v + g * SGU_GROUP_DIM: off_v + (g + 1) * SGU_GROUP_DIM])
        mu = jnp.mean(t, axis=-1, keepdims=True)
        d = t - mu
        var = jnp.mean(d * d, axis=-1, keepdims=True)
        y = (d * lax.rsqrt(var + EPS) * lnw_ref[0, g:g + 1, :] + lnb_ref[0, g:g + 1, :]).astype(BF16)
        for c in range(tm // CHUNK):
            rows = slice(c * CHUNK, (c + 1) * CHUNK)
            sv = jnp.dot(ws_ref[0, g], y[rows], preferred_element_type=F32) + bs_ref[0, g]
            sg_ref[rows, gl] = (u[rows] * sv).astype(sg_ref.dtype)


def _inproj(x, layer, mod, row_of_tile, params, *, tm, kv_buffers=None):
    t_tokens = x.shape[0]
    tok = lambda i: (i, 0)
    lyr3 = lambda i: (layer, 0, 0)
    lyr4 = lambda i: (layer, 0, 0, 0)
    act = pl.BlockSpec((tm, ATTN_WIDTH), tok)
    act_sds = jax.ShapeDtypeStruct((t_tokens, ATTN_WIDTH), BF16)
    in_specs = [
        pl.BlockSpec((tm, D_MODEL), tok),
        pl.BlockSpec((1, 1, 6, D_MODEL), lambda i: (layer, row_of_tile(i), 0, 0)),
        pl.BlockSpec((1, 1, D_MODEL), lyr3),
        pl.BlockSpec((1, D_MODEL, IN_WIDTH), lyr3),
        pl.BlockSpec((1, 1, ATTN_WIDTH), lyr3),
        pl.BlockSpec((1, 1, ATTN_WIDTH), lyr3),
        pl.BlockSpec((512, 256), lambda i: (0, 0)),
        pl.BlockSpec((1, SGU_GROUPS, CHUNK, CHUNK), lyr4),
        pl.BlockSpec((1, SGU_GROUPS, CHUNK, SGU_GROUP_DIM), lyr4),
        pl.BlockSpec((1, SGU_GROUPS, SGU_GROUP_DIM), lyr3),
        pl.BlockSpec((1, SGU_GROUPS, SGU_GROUP_DIM), lyr3),
    ]
    args = [x, mod, params["n1w"], params["win"], params["qw"], params["kw"], params["bd"],
            params["ws"], params["bs"], params["lnw"], params["lnb"]]
    if kv_buffers is None:
        out_specs = [act] * 4
        out_shape = [act_sds] * 4
        aliases = {}
    else:
        kbuf, vbuf = kv_buffers
        kv_spec = pl.BlockSpec((1, 1, ATTN_WIDTH, tm), lambda i: (i, layer, 0, 0))
        in_specs += [pl.BlockSpec(memory_space=pl.ANY)] * 2
        args += [kbuf, vbuf]
        out_specs = [act, kv_spec, kv_spec, act, act]
        out_shape = [act_sds, jax.ShapeDtypeStruct(kbuf.shape, F32),
                     jax.ShapeDtypeStruct(vbuf.shape, F32), act_sds, act_sds]
        aliases = {11: 1, 12: 2}
    return pl.pallas_call(
        functools.partial(_inproj_kernel, tm=tm, transposed_kv=kv_buffers is not None),
        grid=(t_tokens // tm,),
        in_specs=in_specs,
        out_specs=out_specs,
        out_shape=out_shape,
        input_output_aliases=aliases,
        compiler_params=_cparams("arbitrary"),
        name="inproj",
    )(*args)


def _route(lg_t, br):
    s = jax.nn.sigmoid(lg_t)
    sel = s + br
    row = lambda a, i: a[i:i + 1]
    g_score = []
    for g in range(N_EXPERT_GROUPS):
        v = [row(sel, 4 * g + i) for i in range(4)]
        best_pair = None
        for i in range(4):
            for j in range(i + 1, 4):
                pair = v[i] + v[j]
                best_pair = pair if best_pair is None else jnp.maximum(best_pair, pair)
        g_score.append(best_pair)
    best = jnp.zeros_like(g_score[0], dtype=jnp.int32)
    top = g_score[0]
    for g in range(1, N_EXPERT_GROUPS):
        upd = g_score[g] > top
        best = jnp.where(upd, g, best)
        top = jnp.where(upd, g_score[g], top)
    cand, aff = [], []
    for i in range(4):
        ci, si = row(sel, i), row(s, i)
        for g in range(1, N_EXPERT_GROUPS):
            ci = jnp.where(best == g, row(sel, 4 * g + i), ci)
            si = jnp.where(best == g, row(s, 4 * g + i), si)
        cand.append(ci)
        aff.append(si)

    def first_argmax(vals):
        idx = jnp.zeros_like(best)
        top_v = vals[0]
        for i in range(1, 4):
            upd = vals[i] > top_v
            idx = jnp.where(upd, i, idx)
            top_v = jnp.where(upd, vals[i], top_v)
        return idx

    i1 = first_argmax(cand)
    i2 = first_argmax([jnp.where(i1 == i, -jnp.inf, cand[i]) for i in range(4)])
    pick = lambda idx: sum(jnp.where(idx == i, aff[i], 0.0) for i in range(4))
    den = pick(i1) + pick(i2)
    gate = [jnp.where((i1 == i) | (i2 == i), aff[i] / den, 0.0) for i in range(4)]
    hi = [x.astype(BF16) for x in gate]
    lo = [(x - h.astype(F32)).astype(BF16) for x, h in zip(gate, hi)]
    return best, hi + lo


def _slab_positions(best, tri):
    n = best.shape[1]
    onehot = [jnp.where(best == g, 1.0, 0.0) for g in range(N_EXPERT_GROUPS)]
    pad = [jnp.zeros_like(onehot[0])] * (8 - N_EXPERT_GROUPS)
    oh = jnp.concatenate(onehot + pad, axis=0).astype(BF16)
    counts = jnp.dot(oh, tri, preferred_element_type=F32)
    dest = jnp.zeros((1, n), F32)
    seg_start = jnp.zeros((1, 1), F32)
    chunks = []
    for g in range(N_EXPERT_GROUPS):
        cg = counts[g:g + 1]
        n_g = jnp.max(cg, axis=1, keepdims=True)
        c_g = jnp.floor((n_g + (CHUNK_ROWS - 1)) * (1.0 / CHUNK_ROWS))
        dest = dest + onehot[g] * (seg_start + cg - 1.0)
        seg_start = seg_start + c_g * CHUNK_ROWS
        chunks.append(c_g)
    return dest.astype(jnp.int32), chunks


def _permutation(dest):
    n = dest.shape[1]
    hit = lax.broadcasted_iota(jnp.int32, (SLAB_ROWS, n), 0) == dest
    return jnp.where(hit, 1.0, 0.0).astype(BF16)


def _post_attention(a, sg, x, m, wout_ref, n2w_ref, wr_ref, br_ref, tri_ref,
                    xm_ref, slab_ref, dest_ref, meta_ref):
    g1, sh2, sc2 = m[2:3], m[3:4], m[4:5]
    cat = jnp.concatenate([a, sg], axis=1)
    y = jnp.dot(cat, wout_ref[0], preferred_element_type=F32)
    xm = x + g1 * y
    xm_ref[...] = xm
    ms = jnp.mean(xm * xm, axis=-1, keepdims=True)
    h2 = xm * lax.rsqrt(ms + EPS) * n2w_ref[0]
    h2 = (h2 * (1.0 + sc2) + sh2).astype(BF16)
    lg_t = lax.dot_general(wr_ref[...], h2, (((1,), (1,)), ((), ())), preferred_element_type=F32)
    best, gates = _route(lg_t, br_ref[...])

    dest, chunks = _slab_positions(best, tri_ref[...])
    perm = _permutation(dest)
    slab_ref[:, 0:D_MODEL] = jnp.dot(perm, h2, preferred_element_type=F32).astype(BF16)
    gmat = jnp.concatenate(gates + [jnp.zeros((LANES - len(gates), dest.shape[1]), BF16)], axis=0)
    slab_ref[:, D_MODEL:] = lax.dot_general(perm, gmat, (((1,), (1,)), ((), ())),
                                            preferred_element_type=F32).astype(BF16)
    dest_ref[0] = dest
    meta = [jnp.broadcast_to(c, (1, LANES)) for c in chunks]
    meta += [jnp.zeros((8 - len(chunks), LANES), F32)]
    meta_ref[0] = jnp.concatenate(meta, axis=0).astype(jnp.int32)


def _head_mask(hh, axis):
    shape = (1, LANES) if axis == 1 else (LANES, 1)
    pos = lax.broadcasted_iota(jnp.int32, shape, axis)
    return (pos >= hh * HEAD_DIM) & (pos < (hh + 1) * HEAD_DIM)


def _ctx_attn_kernel(q_ref, kt_ref, v_ref, sg_ref, x_ref, mod_ref, wout_ref, n2w_ref, wr_ref, br_ref,
                     tri_ref, xm_ref, slab_ref, dest_ref, meta_ref):
    outs = []
    for j in range(N_HEAD_PAIRS):
        cols = slice(j * LANES, (j + 1) * LANES)
        q2 = q_ref[:, cols]
        kt2 = kt_ref[0, 0, cols, :].astype(BF16)
        v2 = v_ref[:, cols]
        o_pair = None
        for hh in range(HEADS_PER_VREG):
            ktm = jnp.where(_head_mask(hh, 0), kt2, jnp.zeros_like(kt2))
            vm = jnp.where(_head_mask(hh, 1), v2, jnp.zeros_like(v2))
            s = jnp.dot(q2, ktm, preferred_element_type=F32)
            mx = jnp.max(s, axis=-1, keepdims=True)
            e = jnp.exp(s - mx)
            l = jnp.sum(e, axis=-1, keepdims=True)
            o = jnp.dot(e.astype(BF16), vm, preferred_element_type=F32) * (1.0 / l)
            o_pair = o if o_pair is None else o_pair + o
        outs.append(o_pair.astype(BF16))
    a = jnp.concatenate(outs, axis=1)
    _post_attention(a, sg_ref[...], x_ref[...], mod_ref[0, 0], wout_ref, n2w_ref, wr_ref, br_ref,
                    tri_ref, xm_ref, slab_ref, dest_ref, meta_ref)


def _post_attention_out(t_tokens, tile_index):
    n_src = t_tokens // SRC_TILE
    specs = [
        pl.BlockSpec((SRC_TILE, D_MODEL), lambda *g: (tile_index(*g), 0)),
        pl.BlockSpec((SLAB_ROWS, SLAB_WIDTH), lambda *g: (tile_index(*g), 0)),
        pl.BlockSpec((1, 1, SRC_TILE), lambda *g: (tile_index(*g), 0, 0)),
        pl.BlockSpec((1, 8, LANES), lambda *g: (tile_index(*g), 0, 0)),
    ]
    shapes = [
        jax.ShapeDtypeStruct((t_tokens, D_MODEL), F32),
        jax.ShapeDtypeStruct((n_src * SLAB_ROWS, SLAB_WIDTH), BF16),
        jax.ShapeDtypeStruct((n_src, 1, SRC_TILE), jnp.int32),
        jax.ShapeDtypeStruct((n_src, 8, LANES), jnp.int32),
    ]
    return specs, shapes


def _ctx_attention(q, kbuf, v, sg, x, layer, mod, params, *, seq):
    assert seq == SRC_TILE
    t_tokens = x.shape[0]
    tok = lambda b: (b, 0)
    lyr3 = lambda b: (layer, 0, 0)
    full2 = lambda b: (0, 0)
    out_specs, out_shape = _post_attention_out(t_tokens, lambda b: b)
    return pl.pallas_call(
        _ctx_attn_kernel,
        grid=(t_tokens // seq,),
        in_specs=[
            pl.BlockSpec((seq, ATTN_WIDTH), tok),
            pl.BlockSpec((1, 1, ATTN_WIDTH, seq), lambda b: (b, layer, 0, 0)),
            pl.BlockSpec((seq, ATTN_WIDTH), tok),
            pl.BlockSpec((seq, SGU_WIDTH), tok),
            pl.BlockSpec((seq, D_MODEL), tok),
            pl.BlockSpec((1, 1, 6, D_MODEL), lambda b: (layer, 0, 0, 0)),
            pl.BlockSpec((1, D_MODEL, D_MODEL), lyr3),
            pl.BlockSpec((1, 1, D_MODEL), lyr3),
            pl.BlockSpec((N_EXPERTS, D_MODEL), full2),
            pl.BlockSpec((N_EXPERTS, 1), full2),
            pl.BlockSpec((SRC_TILE, SRC_TILE), full2),
        ],
        out_specs=out_specs,
        out_shape=out_shape,
        compiler_params=_cparams("arbitrary"),
        name="ctx_attn",
    )(q, kbuf, v, sg, x, mod, params["wout"], params["n2w"], params["wr"], params["br"], params["tri"])


def _row_start(r, rows):
    return jnp.clip(r - WIN_H // 2, 0, rows - WIN_H)


def _nbr_attn_kernel(q_ref, k_ref, v_ref, ck_ref, cv_ref, tb_ref, sg_ref, x_ref, mod_ref, wout_ref,
                     n2w_ref, wr_ref, br_ref, tri_ref, xm_ref, slab_ref, dest_ref, meta_ref,
                     kwm_ref, vwm_ref, kcm_ref, vcm_ref, *, rows):
    t = pl.program_id(1)

    @pl.when(t == 0)
    def _():
        for j in range(N_HEAD_PAIRS):
            cols = slice(j * LANES, (j + 1) * LANES)
            k2, v2 = k_ref[:, cols], v_ref[:, cols]
            for hh in range(HEADS_PER_VREG):
                msk = _head_mask(hh, 1)
                kwm_ref[HEADS_PER_VREG * j + hh] = jnp.where(msk, k2, jnp.zeros_like(k2))
                vwm_ref[HEADS_PER_VREG * j + hh] = jnp.where(msk, v2, jnp.zeros_like(v2))
            for src, dst in ((ck_ref, kcm_ref), (cv_ref, vcm_ref)):
                c2 = src[0, 0, cols, :].astype(BF16)
                for hh in range(HEADS_PER_VREG):
                    dst[HEADS_PER_VREG * j + hh] = jnp.where(_head_mask(hh, 0), c2, jnp.zeros_like(c2))

    r0 = t * Q_ROWS
    ws = jnp.minimum(_row_start(r0, rows), rows - WIN_ROWS)
    tok0 = pl.multiple_of(ws * GRID_W, GRID_W)

    lane = lax.broadcasted_iota(jnp.int32, (1, LANES), 1)
    blk_idx, row_mask = [], []
    for a in range(Q_ROWS):
        r = r0 + a
        rs = _row_start(r, rows)
        idx_a, mask_a = [], []
        for jp in range(WIN_ROWS // 2):
            kr = ws + 2 * jp
            idx_a.append(jnp.clip(kr - r + WIN_H, 0, 2 * WIN_H - 1))
            ok0 = (kr >= rs) & (kr < rs + WIN_H)
            ok1 = (kr + 1 >= rs) & (kr + 1 < rs + WIN_H)
            m0 = jnp.where(ok0, 0.0, NEG).astype(F32)
            m1 = jnp.where(ok1, 0.0, NEG).astype(F32)
            mask_a.append(jnp.where(lane < GRID_W, m0, m1))
        blk_idx.append(idx_a)
        row_mask.append(mask_a)

    nt = (((1,), (1,)), ((), ()))
    outs = []
    for j in range(N_HEAD_PAIRS):
        q2 = q_ref[:, j * LANES:(j + 1) * LANES]
        o_pair = None
        for hh in range(HEADS_PER_VREG):
            h = j * HEADS_PER_VREG + hh
            kwm = kwm_ref[h, pl.ds(tok0, WIN_ROWS * GRID_W), :]
            vwm = vwm_ref[h, pl.ds(tok0, WIN_ROWS * GRID_W), :]
            bias = jnp.concatenate([
                jnp.concatenate([tb_ref[0, h, blk_idx[a][jp]] + row_mask[a][jp]
                                 for jp in range(WIN_ROWS // 2)], axis=1)
                for a in range(Q_ROWS)], axis=0)
            s_w = lax.dot_general(q2, kwm, nt, preferred_element_type=F32) + bias
            s_c = jnp.dot(q2, kcm_ref[h], preferred_element_type=F32)
            mx = jnp.maximum(jnp.max(s_w, axis=-1, keepdims=True), jnp.max(s_c, axis=-1, keepdims=True))
            e_w = jnp.exp(s_w - mx)
            e_c = jnp.exp(s_c - mx)
            l = jnp.sum(e_w, axis=-1, keepdims=True) + jnp.sum(e_c, axis=-1, keepdims=True)
            o = (jnp.dot(e_w.astype(BF16), vwm, preferred_element_type=F32)
                 + lax.dot_general(e_c.astype(BF16), vcm_ref[h], nt,
                                   preferred_element_type=F32)) * (1.0 / l)
            o_pair = o if o_pair is None else o_pair + o
        outs.append(o_pair.astype(BF16))
    a_out = jnp.concatenate(outs, axis=1)
    _post_attention(a_out, sg_ref[...], x_ref[...], mod_ref[0, 0], wout_ref, n2w_ref, wr_ref, br_ref,
                    tri_ref, xm_ref, slab_ref, dest_ref, meta_ref)


def _nbr_attention(q, k, v, cache_k, cache_v, tb, sg, x, layer, mod, params, *, batch, seq):
    t_tokens = x.shape[0]
    rows = seq // GRID_W
    tq = Q_ROWS * GRID_W
    assert tq == SRC_TILE
    nt = seq // tq
    past = cache_k.shape[3]
    out_specs, out_shape = _post_attention_out(t_tokens, lambda b, t: b * nt + t)
    tok = lambda b, t: (b * nt + t, 0)
    per_b = lambda b, t: (b, 0)
    lyr3 = lambda b, t: (layer, 0, 0)
    full2 = lambda b, t: (0, 0)
    cache_spec = pl.BlockSpec((1, 1, ATTN_WIDTH, past), lambda b, t: (b, layer, 0, 0))
    masked = lambda n: pltpu.VMEM((N_HEADS, n, LANES), BF16)
    masked_t = pltpu.VMEM((N_HEADS, LANES, past), BF16)
    return pl.pallas_call(
        functools.partial(_nbr_attn_kernel, rows=rows),
        grid=(batch, nt),
        in_specs=[
            pl.BlockSpec((tq, ATTN_WIDTH), tok),
            pl.BlockSpec((seq, ATTN_WIDTH), per_b),
            pl.BlockSpec((seq, ATTN_WIDTH), per_b),
            cache_spec,
            cache_spec,
            pl.BlockSpec((1, N_HEADS, 2 * WIN_H, GRID_W, LANES), lambda b, t: (layer, 0, 0, 0, 0)),
            pl.BlockSpec((tq, SGU_WIDTH), tok),
            pl.BlockSpec((tq, D_MODEL), tok),
            pl.BlockSpec((1, 1, 6, D_MODEL), lambda b, t: (layer, 1 + b, 0, 0)),
            pl.BlockSpec((1, D_MODEL, D_MODEL), lyr3),
            pl.BlockSpec((1, 1, D_MODEL), lyr3),
            pl.BlockSpec((N_EXPERTS, D_MODEL), full2),
            pl.BlockSpec((N_EXPERTS, 1), full2),
            pl.BlockSpec((SRC_TILE, SRC_TILE), full2),
        ],
        out_specs=out_specs,
        out_shape=out_shape,
        scratch_shapes=[masked(seq), masked(seq), masked_t, masked_t],
        compiler_params=_cparams("arbitrary", "arbitrary"),
        name="nbr_attn",
    )(q, k, v, cache_k, cache_v, tb, sg, x, mod, params["wout"], params["n2w"], params["wr"],
      params["br"], params["tri"])


def _bias_tables(rpb):
    cols = jnp.arange(GRID_W)
    col_start = jnp.clip(cols - WIN_W // 2, 0, GRID_W - WIN_W)
    col_ok = (cols[None, :] >= col_start[:, None]) & (cols[None, :] < col_start[:, None] + WIN_W)
    rel = cols[None, :] - cols[:, None] + WIN_W - 1
    pick = ((rel[None] == jnp.arange(2 * WIN_W - 1)[:, None, None]) & col_ok[None]).astype(F32)
    t = jnp.einsum("lhxm,mqk->lhxqk", rpb, pick, precision=lax.Precision.HIGHEST)
    t = jnp.where(col_ok, t, NEG)
    t = jnp.pad(t, ((0, 0), (0, 0), (1, 1), (0, 0), (0, 0)))
    return jnp.concatenate([t[:, :, :-1], t[:, :, 1:]], axis=-1)


def _max_moe_tiles(n_src):
    max_chunks = n_src * SRC_TILE // CHUNK_ROWS + n_src * N_EXPERT_GROUPS
    return -(-max_chunks // TILE_CHUNKS) + N_EXPERT_GROUPS


def _chunk_tables(chunks, n_tiles):
    n_src = chunks.shape[0]
    groups = jnp.arange(N_EXPERT_GROUPS)
    seg_end = jnp.cumsum(chunks, axis=1)
    seg_start = seg_end - chunks
    src_end = jnp.cumsum(chunks, axis=0)
    src_start = src_end - chunks
    total = src_end[-1]
    tiles = (total + TILE_CHUNKS - 1) // TILE_CHUNKS
    tile_end = jnp.cumsum(tiles)
    tile_start = tile_end - tiles
    n_used = tile_end[-1:]

    u = jnp.arange(n_tiles)
    gid = jnp.minimum(jnp.sum(u[:, None] >= tile_end[None, :], axis=1), N_EXPERT_GROUPS - 1)

    p = jnp.arange(n_tiles * TILE_CHUNKS)
    g_hot = jnp.repeat(gid, TILE_CHUNKS)[:, None] == groups[None, :]
    by_group = lambda vec: jnp.sum(jnp.where(g_hot, vec[None, :], 0), axis=1)
    q = p - TILE_CHUNKS * by_group(tile_start)
    live = q < by_group(total)
    src_end_p = jnp.sum(jnp.where(g_hot[:, None, :], src_end[None], 0), axis=2)
    t = jnp.minimum(jnp.sum(q[:, None] >= src_end_p, axis=1), n_src - 1)
    t_hot = t[:, None] == jnp.arange(n_src)[None, :]
    by_seg = lambda tab: jnp.sum(jnp.where(t_hot[:, :, None] & g_hot[:, None, :], tab[None], 0),
                                 axis=(1, 2))
    slab_chunk = by_seg(seg_start) + q - by_seg(src_start)
    src_row = jnp.where(live, t * SLAB_ROWS + CHUNK_ROWS * slab_chunk, 0)

    s = jnp.arange(SLAB_CHUNKS)
    gs = jnp.sum(s[None, :, None] >= seg_end[:, None, :], axis=2)
    used = gs < N_EXPERT_GROUPS
    s_hot = jnp.minimum(gs, N_EXPERT_GROUPS - 1)[:, :, None] == groups
    pick = lambda tab: jnp.sum(jnp.where(s_hot, tab, 0), axis=2)
    pos = (TILE_CHUNKS * pick(tile_start[None, None, :]) + pick(src_start[:, None, :])
           + s[None, :] - pick(seg_start[:, None, :]))
    out_row = jnp.where(used, CHUNK_ROWS * pos, 0).reshape(-1)
    i32 = lambda a: a.astype(jnp.int32)
    return i32(gid), i32(src_row), i32(out_row), i32(n_used)


def _chunk_copies(src_hbm, row_ref, first, n_chunks, buf, sem, slot):
    return [
        pltpu.make_async_copy(
            src_hbm.at[pl.ds(pl.multiple_of(row_ref[first + k], CHUNK_ROWS), CHUNK_ROWS), :],
            buf.at[slot, pl.ds(k * CHUNK_ROWS, CHUNK_ROWS), :],
            sem.at[slot])
        for k in range(n_chunks)
    ]


def _gather_step(src_hbm, row_ref, n_chunks, buf, sem):
    step = pl.program_id(0)
    slot = step % 2

    @pl.when(step == 0)
    def _():
        for cp in _chunk_copies(src_hbm, row_ref, 0, n_chunks, buf, sem, 0):
            cp.start()

    @pl.when(step + 1 < pl.num_programs(0))
    def _():
        for cp in _chunk_copies(src_hbm, row_ref, (step + 1) * n_chunks, n_chunks, buf, sem, 1 - slot):
            cp.start()

    for cp in _chunk_copies(src_hbm, row_ref, step * n_chunks, n_chunks, buf, sem, slot):
        cp.wait()
    return slot


def _moe_kernel(gid_ref, src_ref, nused_ref, slab_hbm, ex_ref, w1_ref, w3_ref, w2_ref, o_ref,
                buf, sem, w1b, w3b, w2b):
    u = pl.program_id(0)
    slot = _gather_step(slab_hbm, src_ref, TILE_CHUNKS, buf, sem)

    @pl.when((u == 0) | (gid_ref[u] != gid_ref[jnp.maximum(u - 1, 0)]))
    def _():
        w1b[...] = w1_ref[0].astype(BF16)
        w3b[...] = w3_ref[0].astype(BF16)
        w2b[...] = w2_ref[0, 0].astype(BF16)

    @pl.when(u < nused_ref[0])
    def _():
        h = buf[slot, :, 0:D_MODEL]
        gexp = jnp.dot(buf[slot, :, D_MODEL:], ex_ref[...], preferred_element_type=F32)
        hid = []
        for e in range(EXPERTS_PER_GROUP):
            h1 = jnp.dot(h, w1b[e], preferred_element_type=F32)
            h3 = jnp.dot(h, w3b[e], preferred_element_type=F32)
            act = (h1 * jax.nn.sigmoid(h1)) * h3
            hid.append((act * gexp[:, e * D_EXPERT:(e + 1) * D_EXPERT]).astype(BF16))
        o_ref[...] = jnp.dot(jnp.concatenate(hid, axis=1), w2b[...], preferred_element_type=F32)

    @pl.when(u >= nused_ref[0])
    def _():
        o_ref[...] = jnp.zeros_like(o_ref)


def _moe(slab, tables, layer, params):
    gid, src_row, _, n_used = tables
    n_tiles = gid.shape[0]
    expert_w = pl.BlockSpec((1, EXPERTS_PER_GROUP, D_MODEL, D_EXPERT),
                            lambda u, gid, src, nu: (layer, gid[u], 0, 0))
    return pl.pallas_call(
        _moe_kernel,
        grid_spec=pltpu.PrefetchScalarGridSpec(
            num_scalar_prefetch=3,
            grid=(n_tiles,),
            in_specs=[
                pl.BlockSpec(memory_space=pl.ANY),
                pl.BlockSpec((LANES, GROUP_HIDDEN), lambda u, gid, src, nu: (0, 0)),
                expert_w,
                expert_w,
                pl.BlockSpec((1, 1, GROUP_HIDDEN, D_MODEL), lambda u, gid, src, nu: (layer, gid[u], 0, 0)),
            ],
            out_specs=pl.BlockSpec((MOE_TILE, D_MODEL), lambda u, gid, src, nu: (u, 0)),
            scratch_shapes=[
                pltpu.VMEM((2, MOE_TILE, SLAB_WIDTH), BF16),
                pltpu.SemaphoreType.DMA((2,)),
                pltpu.VMEM((EXPERTS_PER_GROUP, D_MODEL, D_EXPERT), BF16),
                pltpu.VMEM((EXPERTS_PER_GROUP, D_MODEL, D_EXPERT), BF16),
                pltpu.VMEM((GROUP_HIDDEN, D_MODEL), BF16),
            ],
        ),
        out_shape=jax.ShapeDtypeStruct((n_tiles * MOE_TILE, D_MODEL), F32),
        compiler_params=_cparams("arbitrary"),
        name="moe",
    )(gid, src_row, n_used, slab, params["expand"], params["w1"], params["w3"], params["w2"])


def _unsort_kernel(row_ref, xm_ref, dest_ref, mod_ref, y_hbm, o_ref, buf, sem):
    slot = _gather_step(y_hbm, row_ref, SLAB_CHUNKS, buf, sem)
    y = buf[slot]
    hi = y.astype(BF16)
    lo = (y - hi.astype(F32)).astype(BF16)
    perm = _permutation(dest_ref[0])
    tn = (((0,), (0,)), ((), ()))
    moe = (lax.dot_general(perm, hi, tn, preferred_element_type=F32)
           + lax.dot_general(perm, lo, tn, preferred_element_type=F32))
    g2 = mod_ref[0, 0][5:6]
    o_ref[...] = xm_ref[...] + g2 * moe


def _unsort(y, xm, dest, tables, layer, mod, row_of_tile):
    t_tokens = xm.shape[0]
    out_row = tables[2]
    return pl.pallas_call(
        _unsort_kernel,
        grid_spec=pltpu.PrefetchScalarGridSpec(
            num_scalar_prefetch=1,
            grid=(t_tokens // SRC_TILE,),
            in_specs=[
                pl.BlockSpec((SRC_TILE, D_MODEL), lambda t, rows: (t, 0)),
                pl.BlockSpec((1, 1, SRC_TILE), lambda t, rows: (t, 0, 0)),
                pl.BlockSpec((1, 1, 6, D_MODEL), lambda t, rows: (layer, row_of_tile(t), 0, 0)),
                pl.BlockSpec(memory_space=pl.ANY),
            ],
            out_specs=pl.BlockSpec((SRC_TILE, D_MODEL), lambda t, rows: (t, 0)),
            scratch_shapes=[pltpu.VMEM((2, SLAB_ROWS, D_MODEL), F32), pltpu.SemaphoreType.DMA((2,))],
        ),
        out_shape=jax.ShapeDtypeStruct((t_tokens, D_MODEL), F32),
        compiler_params=_cparams("arbitrary"),
        name="unsort",
    )(out_row, xm, dest, mod, y)


def kernel(x_prompt, x_sample, cache_k, cache_v, c, c_ctx, w_ada, b_ada, norm1_w, norm2_w, w_in,
           q_norm_w, k_norm_w, rpb, w_sgu, b_sgu, sgu_ln_w, sgu_ln_b, w_out, w_router, b_router,
           w1, w3, w2):
    batch, seq, _ = x_prompt.shape
    dec_batch, dec_seq, _ = x_sample.shape
    past = cache_k.shape[2]
    assert 1 + dec_batch <= COND_ROWS and dec_seq % (Q_ROWS * GRID_W) == 0 and seq % CHUNK == 0

    cond = jnp.zeros((COND_ROWS, D_MODEL), F32).at[0].set(c_ctx).at[1:1 + dec_batch].set(c)
    mod = _adaln(cond, w_ada, b_ada).reshape(DEPTH, COND_ROWS, 6, D_MODEL)

    xp = x_prompt.reshape(batch * seq, D_MODEL)
    xs = x_sample.reshape(dec_batch * dec_seq, D_MODEL)
    from_cache_layout = lambda a: a.transpose(0, 1, 3, 4, 2).reshape(dec_batch, DEPTH, ATTN_WIDTH, past)
    ck = from_cache_layout(cache_k)
    cv = from_cache_layout(cache_v)

    head_of = jnp.arange(256) // HEAD_DIM
    bd = (head_of[:, None] == head_of[None, :]).astype(BF16)
    col_expert = jnp.arange(GROUP_HIDDEN) // D_EXPERT
    gate_lane = jnp.arange(LANES)
    expand = ((gate_lane[:, None] % EXPERTS_PER_GROUP == col_expert[None, :])
              & (gate_lane[:, None] < 2 * EXPERTS_PER_GROUP)).astype(BF16)
    tok_id = jnp.arange(SRC_TILE)
    tri = (tok_id[:, None] <= tok_id[None, :]).astype(BF16)

    params = dict(
        n1w=norm1_w.reshape(DEPTH, 1, D_MODEL),
        n2w=norm2_w.reshape(DEPTH, 1, D_MODEL),
        win=w_in.astype(BF16),
        wout=w_out.astype(BF16),
        qw=jnp.tile(q_norm_w, (1, N_HEADS)).reshape(DEPTH, 1, ATTN_WIDTH),
        kw=jnp.tile(k_norm_w, (1, N_HEADS)).reshape(DEPTH, 1, ATTN_WIDTH),
        bd=jnp.concatenate([bd, bd], axis=0),
        ws=w_sgu.astype(BF16),
        bs=jnp.broadcast_to(b_sgu[..., None], (DEPTH, SGU_GROUPS, CHUNK, SGU_GROUP_DIM)),
        lnw=sgu_ln_w,
        lnb=sgu_ln_b,
        wr=w_router.T.astype(BF16),
        br=b_router.reshape(N_EXPERTS, 1),
        expand=expand,
        tri=tri,
        w1=w1,
        w3=w3,
        w2=w2.reshape(DEPTH, N_EXPERT_GROUPS, GROUP_HIDDEN, D_MODEL),
    )
    tb = _bias_tables(rpb)

    lat_tiles_per_seq = dec_seq // SRC_TILE
    ctx_row = lambda i: 0
    lat_row = lambda i: 1 + i // lat_tiles_per_seq

    def moe_block(xm, slab, dest, meta, layer, row_of_tile):
        n_src = meta.shape[0]
        tables = _chunk_tables(meta[:, :N_EXPERT_GROUPS, 0], _max_moe_tiles(n_src))
        y = _moe(slab, tables, layer, params)
        return _unsort(y, xm, dest, tables, layer, mod, row_of_tile)

    kbuf = jnp.zeros((batch, DEPTH, ATTN_WIDTH, seq), F32)
    vbuf = jnp.zeros((batch, DEPTH, ATTN_WIDTH, seq), F32)

    for l in range(DEPTH):
        q, kbuf, vbuf, v, sg = _inproj(xp, l, mod, ctx_row, params, tm=seq, kv_buffers=(kbuf, vbuf))
        xp = moe_block(*_ctx_attention(q, kbuf, v, sg, xp, l, mod, params, seq=seq), l, ctx_row)

        q, k, v, sg = _inproj(xs, l, mod, lat_row, params, tm=SRC_TILE)
        xs = moe_block(*_nbr_attention(q, k, v, ck, cv, tb, sg, xs, l, mod, params,
                                       batch=dec_batch, seq=dec_seq), l, lat_row)

    to_cache_layout = lambda buf: buf.reshape(batch, DEPTH, N_HEADS, HEAD_DIM, seq).transpose(0, 1, 4, 2, 3)
    return (xp.reshape(batch, seq, D_MODEL), xs.reshape(dec_batch, dec_seq, D_MODEL),
            to_cache_layout(kbuf), to_cache_layout(vbuf))
```

```python
import functools

import jax
import jax.numpy as jnp
from jax import lax
from jax.experimental import pallas as pl
from jax.experimental.pallas import tpu as pltpu

F32 = jnp.float32
BF16 = jnp.bfloat16

D_MODEL = 1024
DEPTH = 4
N_HEADS = 8
HEAD_DIM = 64
ATTN_WIDTH = N_HEADS * HEAD_DIM
SGU_GROUPS = 4
SGU_GROUP_DIM = 128
SGU_WIDTH = SGU_GROUPS * SGU_GROUP_DIM
CHUNK = 128
IN_WIDTH = 3 * ATTN_WIDTH + 2 * SGU_WIDTH
GRID_W = 64
WIN_H = 8
WIN_W = 16
N_EXPERTS = 16
N_EXPERT_GROUPS = 4
EXPERTS_PER_GROUP = 4
D_EXPERT = 256
GROUP_HIDDEN = EXPERTS_PER_GROUP * D_EXPERT
EPS = 1e-6
NEG = -1e30
LOG2E = 1.4426950408889634

LANES = 128
HEADS_PER_VREG = LANES // HEAD_DIM
N_HEAD_PAIRS = N_HEADS // HEADS_PER_VREG
COND_ROWS = 16
Q_ROWS = 4
WIN_ROWS = Q_ROWS + WIN_H
VMEM_LIMIT = 48 * 1024 * 1024

SRC_TILE = 256
CHUNK_ROWS = 16
SLAB_ROWS = SRC_TILE + N_EXPERT_GROUPS * CHUNK_ROWS
SLAB_CHUNKS = SLAB_ROWS // CHUNK_ROWS
SLAB_WIDTH = D_MODEL + LANES
MOE_TILE = 256
TILE_CHUNKS = MOE_TILE // CHUNK_ROWS


def _cparams(*sem):
    return pltpu.CompilerParams(dimension_semantics=sem, vmem_limit_bytes=VMEM_LIMIT)


def _adaln_kernel(cond_ref, w_ref, b_ref, o_ref):
    c = cond_ref[...]
    a = (c * jax.nn.sigmoid(c)).astype(BF16)
    o_ref[0] = jnp.dot(a, w_ref[0].astype(BF16), preferred_element_type=F32) + b_ref[0]


def _adaln(cond, w_ada, b_ada):
    tn = 1536
    n = 6 * D_MODEL
    return pl.pallas_call(
        _adaln_kernel,
        grid=(DEPTH, n // tn),
        in_specs=[
            pl.BlockSpec((COND_ROWS, D_MODEL), lambda l, j: (0, 0)),
            pl.BlockSpec((1, D_MODEL, tn), lambda l, j: (l, 0, j)),
            pl.BlockSpec((1, 1, tn), lambda l, j: (l, 0, j)),
        ],
        out_specs=pl.BlockSpec((1, COND_ROWS, tn), lambda l, j: (l, 0, j)),
        out_shape=jax.ShapeDtypeStruct((DEPTH, COND_ROWS, n), F32),
        compiler_params=_cparams("arbitrary", "arbitrary"),
        name="adaln",
    )(cond, w_ada, b_ada.reshape(DEPTH, 1, n))


def _gelu_tanh(x):
    return 0.5 * x * (1.0 + jnp.tanh(0.7978845608028654 * (x + 0.044715 * (x * x * x))))


def _head_rms(t, bd, w):
    t2 = t * t
    hi = t2.astype(BF16)
    lo = (t2 - hi.astype(F32)).astype(BF16)
    outs = []
    for c in range(ATTN_WIDTH // 256):
        sl = slice(256 * c, 256 * c + 256)
        hl = jnp.concatenate([hi[:, sl], lo[:, sl]], axis=1)
        ss = jnp.dot(hl, bd, preferred_element_type=F32)
        outs.append(t[:, sl] * lax.rsqrt(ss * (1.0 / HEAD_DIM) + EPS))
    return jnp.concatenate(outs, axis=1) * w


def _inproj_kernel(*refs, tm, transposed_kv, fused_moe):
    refs = list(refs)
    if fused_moe:
        row_ref, xm_ref, dest_ref, prev_mod_ref, y_hbm = refs[:5]
        refs = refs[5:]
    else:
        x_ref = refs.pop(0)
    (mod_ref, n1w_ref, win_ref, qw_ref, kw_ref, bd_ref, ws_ref, bs_ref, lnw_ref, lnb_ref) = refs[:10]
    refs = refs[10:]
    if transposed_kv:
        refs = refs[2:]
    if fused_moe:
        x_out_ref = refs.pop(0)
    if transposed_kv:
        q_ref, kt_ref, vt_ref, v_ref, sg_ref = refs[:5]
        refs = refs[5:]
    else:
        q_ref, k_ref, v_ref, sg_ref = refs[:4]
        refs = refs[4:]
    if fused_moe:
        buf, sem = refs
        moe = _unsorted_moe(row_ref, dest_ref, y_hbm, buf, sem)
        x = xm_ref[...] + prev_mod_ref[0, 0][5:6] * moe
        x_out_ref[...] = x
    else:
        x = x_ref[...]
    m = mod_ref[0, 0]
    sh1, sc1 = m[0:1], m[1:2]
    ms = jnp.mean(x * x, axis=-1, keepdims=True)
    h = x * lax.rsqrt(ms + EPS) * n1w_ref[0]
    h = (h * (1.0 + sc1) + sh1).astype(BF16)
    p = jnp.dot(h, win_ref[0], preferred_element_type=F32)

    bd = bd_ref[...]
    q = _head_rms(p[:, 0:ATTN_WIDTH], bd, qw_ref[0])
    k = _head_rms(p[:, ATTN_WIDTH:2 * ATTN_WIDTH], bd, kw_ref[0])
    v = p[:, 2 * ATTN_WIDTH:3 * ATTN_WIDTH]
    q_ref[...] = (q * (HEAD_DIM ** -0.5 * LOG2E)).astype(q_ref.dtype)
    if transposed_kv:
        kt_ref[0, 0] = k.T
        vt_ref[0, 0] = v.T
    else:
        k_ref[...] = k.astype(k_ref.dtype)
    v_ref[...] = v.astype(v_ref.dtype)

    off_u = 3 * ATTN_WIDTH
    off_v = off_u + SGU_WIDTH
    for g in range(SGU_GROUPS):
        gl = slice(g * SGU_GROUP_DIM, (g + 1) * SGU_GROUP_DIM)
        u = _gelu_tanh(p[:, off_u + g * SGU_GROUP_DIM: off_u + (g + 1) * SGU_GROUP_DIM])
        t = _gelu_tanh(p[:, off_v + g * SGU_GROUP_DIM: off_v + (g + 1) * SGU_GROUP_DIM])
        mu = jnp.mean(t, axis=-1, keepdims=True)
        d = t - mu
        var = jnp.mean(d * d, axis=-1, keepdims=True)
        y = (d * lax.rsqrt(var + EPS) * lnw_ref[0, g:g + 1, :] + lnb_ref[0, g:g + 1, :]).astype(BF16)
        for c in range(tm // CHUNK):
            rows = slice(c * CHUNK, (c + 1) * CHUNK)
            sv = jnp.dot(ws_ref[0, g], y[rows], preferred_element_type=F32) + bs_ref[0, g]
            sg_ref[rows, gl] = (u[rows] * sv).astype(sg_ref.dtype)


def _inproj(x, layer, mod, row_of_tile, params, *, tm, kv_buffers=None):
    fused_moe = isinstance(x, tuple)
    tok = lambda i, *_: (i, 0)
    lyr3 = lambda i, *_: (layer, 0, 0)
    lyr4 = lambda i, *_: (layer, 0, 0, 0)
    mod_row = lambda lyr: pl.BlockSpec((1, 1, 6, D_MODEL), lambda i, *_: (lyr, row_of_tile(i), 0, 0))
    x_spec = pl.BlockSpec((tm, D_MODEL), tok)
    if fused_moe:
        assert tm == SRC_TILE
        xm, dest, tables, y = x
        t_tokens = xm.shape[0]
        prefetch = [tables[2]]
        in_specs = [x_spec, pl.BlockSpec((1, 1, SRC_TILE), lambda i, *_: (i, 0, 0)),
                    mod_row(layer - 1), pl.BlockSpec(memory_space=pl.ANY)]
        args = [xm, dest, mod, y]
        scratch = [pltpu.VMEM((2, SLAB_ROWS, D_MODEL), F32), pltpu.SemaphoreType.DMA((2,))]
    else:
        t_tokens = x.shape[0]
        prefetch, in_specs, args, scratch = [], [x_spec], [x], []
    act = pl.BlockSpec((tm, ATTN_WIDTH), tok)
    act_sds = jax.ShapeDtypeStruct((t_tokens, ATTN_WIDTH), BF16)
    in_specs += [
        mod_row(layer),
        pl.BlockSpec((1, 1, D_MODEL), lyr3),
        pl.BlockSpec((1, D_MODEL, IN_WIDTH), lyr3),
        pl.BlockSpec((1, 1, ATTN_WIDTH), lyr3),
        pl.BlockSpec((1, 1, ATTN_WIDTH), lyr3),
        pl.BlockSpec((512, 256), lambda i, *_: (0, 0)),
        pl.BlockSpec((1, SGU_GROUPS, CHUNK, CHUNK), lyr4),
        pl.BlockSpec((1, SGU_GROUPS, CHUNK, SGU_GROUP_DIM), lyr4),
        pl.BlockSpec((1, SGU_GROUPS, SGU_GROUP_DIM), lyr3),
        pl.BlockSpec((1, SGU_GROUPS, SGU_GROUP_DIM), lyr3),
    ]
    args += [mod, params["n1w"], params["win"], params["qw"], params["kw"], params["bd"],
             params["ws"], params["bs"], params["lnw"], params["lnb"]]
    out_specs = [x_spec] if fused_moe else []
    out_shape = [jax.ShapeDtypeStruct((t_tokens, D_MODEL), F32)] if fused_moe else []
    if kv_buffers is None:
        out_specs += [act] * 4
        out_shape += [act_sds] * 4
        aliases = {}
    else:
        kbuf, vbuf = kv_buffers
        kv_spec = pl.BlockSpec((1, 1, ATTN_WIDTH, tm), lambda i, *_: (i, layer, 0, 0))
        first_alias_in = len(prefetch) + len(args)
        first_alias_out = len(out_specs) + 1
        in_specs += [pl.BlockSpec(memory_space=pl.ANY)] * 2
        args += [kbuf, vbuf]
        out_specs += [act, kv_spec, kv_spec, act, act]
        out_shape += [act_sds, jax.ShapeDtypeStruct(kbuf.shape, F32),
                      jax.ShapeDtypeStruct(vbuf.shape, F32), act_sds, act_sds]
        aliases = {first_alias_in: first_alias_out, first_alias_in + 1: first_alias_out + 1}
    return pl.pallas_call(
        functools.partial(_inproj_kernel, tm=tm, transposed_kv=kv_buffers is not None,
                          fused_moe=fused_moe),
        grid_spec=pltpu.PrefetchScalarGridSpec(
            num_scalar_prefetch=len(prefetch),
            grid=(t_tokens // tm,),
            in_specs=in_specs,
            out_specs=out_specs,
            scratch_shapes=scratch,
        ),
        out_shape=out_shape,
        input_output_aliases=aliases,
        compiler_params=_cparams("arbitrary"),
        name="inproj",
    )(*prefetch, *args)


def _route(lg_t, br):
    s = jax.nn.sigmoid(lg_t)
    sel = s + br
    row = lambda a, i: a[i:i + 1]
    g_score = []
    for g in range(N_EXPERT_GROUPS):
        v = [row(sel, 4 * g + i) for i in range(4)]
        best_pair = None
        for i in range(4):
            for j in range(i + 1, 4):
                pair = v[i] + v[j]
                best_pair = pair if best_pair is None else jnp.maximum(best_pair, pair)
        g_score.append(best_pair)
    best = jnp.zeros_like(g_score[0], dtype=jnp.int32)
    top = g_score[0]
    for g in range(1, N_EXPERT_GROUPS):
        upd = g_score[g] > top
        best = jnp.where(upd, g, best)
        top = jnp.where(upd, g_score[g], top)
    cand, aff = [], []
    for i in range(4):
        ci, si = row(sel, i), row(s, i)
        for g in range(1, N_EXPERT_GROUPS):
            ci = jnp.where(best == g, row(sel, 4 * g + i), ci)
            si = jnp.where(best == g, row(s, 4 * g + i), si)
        cand.append(ci)
        aff.append(si)

    def first_argmax(vals):
        idx = jnp.zeros_like(best)
        top_v = vals[0]
        for i in range(1, 4):
            upd = vals[i] > top_v
            idx = jnp.where(upd, i, idx)
            top_v = jnp.where(upd, vals[i], top_v)
        return idx

    i1 = first_argmax(cand)
    i2 = first_argmax([jnp.where(i1 == i, -jnp.inf, cand[i]) for i in range(4)])
    pick = lambda idx: sum(jnp.where(idx == i, aff[i], 0.0) for i in range(4))
    den = pick(i1) + pick(i2)
    gate = [jnp.where((i1 == i) | (i2 == i), aff[i] / den, 0.0) for i in range(4)]
    hi = [x.astype(BF16) for x in gate]
    lo = [(x - h.astype(F32)).astype(BF16) for x, h in zip(gate, hi)]
    return best, hi + lo


def _slab_positions(best, tri):
    n = best.shape[1]
    onehot = [jnp.where(best == g, 1.0, 0.0) for g in range(N_EXPERT_GROUPS)]
    pad = [jnp.zeros_like(onehot[0])] * (8 - N_EXPERT_GROUPS)
    oh = jnp.concatenate(onehot + pad, axis=0).astype(BF16)
    counts = jnp.dot(oh, tri, preferred_element_type=F32)
    dest = jnp.zeros((1, n), F32)
    seg_start = jnp.zeros((1, 1), F32)
    chunks = []
    for g in range(N_EXPERT_GROUPS):
        cg = counts[g:g + 1]
        n_g = jnp.max(cg, axis=1, keepdims=True)
        c_g = jnp.floor((n_g + (CHUNK_ROWS - 1)) * (1.0 / CHUNK_ROWS))
        dest = dest + onehot[g] * (seg_start + cg - 1.0)
        seg_start = seg_start + c_g * CHUNK_ROWS
        chunks.append(c_g)
    return dest.astype(jnp.int32), chunks


def _permutation(dest):
    n = dest.shape[1]
    hit = lax.broadcasted_iota(jnp.int32, (SLAB_ROWS, n), 0) == dest
    return jnp.where(hit, 1.0, 0.0).astype(BF16)


def _post_attention(a, sg, x, m, wout_ref, n2w_ref, wr_ref, br_ref, tri_ref,
                    xm_ref, slab_ref, dest_ref, meta_ref):
    g1, sh2, sc2 = m[2:3], m[3:4], m[4:5]
    cat = jnp.concatenate([a, sg], axis=1)
    y = jnp.dot(cat, wout_ref[0], preferred_element_type=F32)
    xm = x + g1 * y
    xm_ref[...] = xm
    ms = jnp.mean(xm * xm, axis=-1, keepdims=True)
    h2 = xm * lax.rsqrt(ms + EPS) * n2w_ref[0]
    h2 = (h2 * (1.0 + sc2) + sh2).astype(BF16)
    lg_t = lax.dot_general(wr_ref[...], h2, (((1,), (1,)), ((), ())), preferred_element_type=F32)
    best, gates = _route(lg_t, br_ref[...])

    dest, chunks = _slab_positions(best, tri_ref[...])
    perm = _permutation(dest)
    slab_ref[:, 0:D_MODEL] = jnp.dot(perm, h2, preferred_element_type=F32).astype(BF16)
    gmat = jnp.concatenate(gates + [jnp.zeros((LANES - len(gates), dest.shape[1]), BF16)], axis=0)
    slab_ref[:, D_MODEL:] = lax.dot_general(perm, gmat, (((1,), (1,)), ((), ())),
                                            preferred_element_type=F32).astype(BF16)
    dest_ref[0] = dest
    meta = [jnp.broadcast_to(c, (1, LANES)) for c in chunks]
    meta += [jnp.zeros((8 - len(chunks), LANES), F32)]
    meta_ref[0] = jnp.concatenate(meta, axis=0).astype(jnp.int32)


def _pair_pos(axis):
    return lax.broadcasted_iota(jnp.int32, (1, LANES) if axis == 1 else (LANES, 1), axis)


def _head_mask(hh, axis):
    pos = _pair_pos(axis)
    return (pos >= hh * HEAD_DIM) & (pos < (hh + 1) * HEAD_DIM)


def _masked_values(v2, hh, axis):
    fill = jnp.where(_pair_pos(axis) == (1 - hh) * HEAD_DIM, 1.0, 0.0).astype(v2.dtype)
    return jnp.where(_head_mask(hh, axis), v2, fill)


def _normalised(o, hh):
    at = (1 - hh) * HEAD_DIM
    return jnp.where(_head_mask(hh, 1), o * (1.0 / o[:, at:at + 1]), 0.0)


def _ctx_attn_kernel(q_ref, kt_ref, v_ref, sg_ref, x_ref, mod_ref, wout_ref, n2w_ref, wr_ref, br_ref,
                     tri_ref, xm_ref, slab_ref, dest_ref, meta_ref):
    outs = []
    for j in range(N_HEAD_PAIRS):
        cols = slice(j * LANES, (j + 1) * LANES)
        q2 = q_ref[:, cols]
        kt2 = kt_ref[0, 0, cols, :].astype(BF16)
        v2 = v_ref[:, cols]
        o_pair = None
        for hh in range(HEADS_PER_VREG):
            ktm = jnp.where(_head_mask(hh, 0), kt2, jnp.zeros_like(kt2))
            s = jnp.dot(q2, ktm, preferred_element_type=F32)
            e = jnp.exp2(s - jnp.max(s, axis=-1, keepdims=True))
            o = _normalised(jnp.dot(e.astype(BF16), _masked_values(v2, hh, 1),
                                    preferred_element_type=F32), hh)
            o_pair = o if o_pair is None else o_pair + o
        outs.append(o_pair.astype(BF16))
    a = jnp.concatenate(outs, axis=1)
    _post_attention(a, sg_ref[...], x_ref[...], mod_ref[0, 0], wout_ref, n2w_ref, wr_ref, br_ref,
                    tri_ref, xm_ref, slab_ref, dest_ref, meta_ref)


def _post_attention_out(t_tokens, tile_index):
    n_src = t_tokens // SRC_TILE
    specs = [
        pl.BlockSpec((SRC_TILE, D_MODEL), lambda *g: (tile_index(*g), 0)),
        pl.BlockSpec((SLAB_ROWS, SLAB_WIDTH), lambda *g: (tile_index(*g), 0)),
        pl.BlockSpec((1, 1, SRC_TILE), lambda *g: (tile_index(*g), 0, 0)),
        pl.BlockSpec((1, 8, LANES), lambda *g: (tile_index(*g), 0, 0)),
    ]
    shapes = [
        jax.ShapeDtypeStruct((t_tokens, D_MODEL), F32),
        jax.ShapeDtypeStruct((n_src * SLAB_ROWS, SLAB_WIDTH), BF16),
        jax.ShapeDtypeStruct((n_src, 1, SRC_TILE), jnp.int32),
        jax.ShapeDtypeStruct((n_src, 8, LANES), jnp.int32),
    ]
    return specs, shapes


def _ctx_attention(q, kbuf, v, sg, x, layer, mod, params, *, seq):
    assert seq == SRC_TILE
    t_tokens = x.shape[0]
    tok = lambda b: (b, 0)
    lyr3 = lambda b: (layer, 0, 0)
    full2 = lambda b: (0, 0)
    out_specs, out_shape = _post_attention_out(t_tokens, lambda b: b)
    return pl.pallas_call(
        _ctx_attn_kernel,
        grid=(t_tokens // seq,),
        in_specs=[
            pl.BlockSpec((seq, ATTN_WIDTH), tok),
            pl.BlockSpec((1, 1, ATTN_WIDTH, seq), lambda b: (b, layer, 0, 0)),
            pl.BlockSpec((seq, ATTN_WIDTH), tok),
            pl.BlockSpec((seq, SGU_WIDTH), tok),
            pl.BlockSpec((seq, D_MODEL), tok),
            pl.BlockSpec((1, 1, 6, D_MODEL), lambda b: (layer, 0, 0, 0)),
            pl.BlockSpec((1, D_MODEL, D_MODEL), lyr3),
            pl.BlockSpec((1, 1, D_MODEL), lyr3),
            pl.BlockSpec((N_EXPERTS, D_MODEL), full2),
            pl.BlockSpec((N_EXPERTS, 1), full2),
            pl.BlockSpec((SRC_TILE, SRC_TILE), full2),
        ],
        out_specs=out_specs,
        out_shape=out_shape,
        compiler_params=_cparams("arbitrary"),
        name="ctx_attn",
    )(q, kbuf, v, sg, x, mod, params["wout"], params["n2w"], params["wr"], params["br"], params["tri"])


def _row_start(r, rows):
    return jnp.clip(r - WIN_H // 2, 0, rows - WIN_H)


def _nbr_attn_kernel(q_ref, k_ref, v_ref, ck_ref, cv_ref, tb_ref, sg_ref, x_ref, mod_ref, wout_ref,
                     n2w_ref, wr_ref, br_ref, tri_ref, xm_ref, slab_ref, dest_ref, meta_ref,
                     kwm_ref, vwm_ref, kcm_ref, vcm_ref, *, rows):
    t = pl.program_id(1)

    @pl.when(t == 0)
    def _():
        for j in range(N_HEAD_PAIRS):
            cols = slice(j * LANES, (j + 1) * LANES)
            k2, v2 = k_ref[:, cols], v_ref[:, cols]
            ck2 = ck_ref[0, 0, cols, :].astype(BF16)
            cv2 = cv_ref[0, 0, cols, :].astype(BF16)
            for hh in range(HEADS_PER_VREG):
                h = HEADS_PER_VREG * j + hh
                kwm_ref[h] = jnp.where(_head_mask(hh, 1), k2, jnp.zeros_like(k2))
                kcm_ref[h] = jnp.where(_head_mask(hh, 0), ck2, jnp.zeros_like(ck2))
                vwm_ref[h] = _masked_values(v2, hh, 1)
                vcm_ref[h] = _masked_values(cv2, hh, 0)

    r0 = t * Q_ROWS
    ws = jnp.minimum(_row_start(r0, rows), rows - WIN_ROWS)
    tok0 = pl.multiple_of(ws * GRID_W, GRID_W)

    lane = lax.broadcasted_iota(jnp.int32, (1, LANES), 1)
    blk_idx, row_mask = [], []
    for a in range(Q_ROWS):
        r = r0 + a
        rs = _row_start(r, rows)
        idx_a, mask_a = [], []
        for jp in range(WIN_ROWS // 2):
            kr = ws + 2 * jp
            idx_a.append(jnp.clip(kr - r + WIN_H, 0, 2 * WIN_H - 1))
            ok0 = (kr >= rs) & (kr < rs + WIN_H)
            ok1 = (kr + 1 >= rs) & (kr + 1 < rs + WIN_H)
            m0 = jnp.where(ok0, 0.0, NEG).astype(F32)
            m1 = jnp.where(ok1, 0.0, NEG).astype(F32)
            mask_a.append(jnp.where(lane < GRID_W, m0, m1))
        blk_idx.append(idx_a)
        row_mask.append(mask_a)

    nt = (((1,), (1,)), ((), ()))
    outs = []
    for j in range(N_HEAD_PAIRS):
        q2 = q_ref[:, j * LANES:(j + 1) * LANES]
        o_pair = None
        for hh in range(HEADS_PER_VREG):
            h = j * HEADS_PER_VREG + hh
            kwm = kwm_ref[h, pl.ds(tok0, WIN_ROWS * GRID_W), :]
            vwm = vwm_ref[h, pl.ds(tok0, WIN_ROWS * GRID_W), :]
            bias = jnp.concatenate([
                jnp.concatenate([tb_ref[0, h, blk_idx[a][jp]] + row_mask[a][jp]
                                 for jp in range(WIN_ROWS // 2)], axis=1)
                for a in range(Q_ROWS)], axis=0)
            s_w = lax.dot_general(q2, kwm, nt, preferred_element_type=F32) + bias
            s_c = jnp.dot(q2, kcm_ref[h], preferred_element_type=F32)
            mx = jnp.maximum(jnp.max(s_w, axis=-1, keepdims=True), jnp.max(s_c, axis=-1, keepdims=True))
            e_w = jnp.exp2(s_w - mx)
            e_c = jnp.exp2(s_c - mx)
            o = _normalised(jnp.dot(e_w.astype(BF16), vwm, preferred_element_type=F32)
                            + lax.dot_general(e_c.astype(BF16), vcm_ref[h], nt,
                                              preferred_element_type=F32), hh)
            o_pair = o if o_pair is None else o_pair + o
        outs.append(o_pair.astype(BF16))
    a_out = jnp.concatenate(outs, axis=1)
    _post_attention(a_out, sg_ref[...], x_ref[...], mod_ref[0, 0], wout_ref, n2w_ref, wr_ref, br_ref,
                    tri_ref, xm_ref, slab_ref, dest_ref, meta_ref)


def _nbr_attention(q, k, v, cache_k, cache_v, tb, sg, x, layer, mod, params, *, batch, seq):
    t_tokens = x.shape[0]
    rows = seq // GRID_W
    tq = Q_ROWS * GRID_W
    assert tq == SRC_TILE
    nt = seq // tq
    past = cache_k.shape[3]
    out_specs, out_shape = _post_attention_out(t_tokens, lambda b, t: b * nt + t)
    tok = lambda b, t: (b * nt + t, 0)
    per_b = lambda b, t: (b, 0)
    lyr3 = lambda b, t: (layer, 0, 0)
    full2 = lambda b, t: (0, 0)
    cache_spec = pl.BlockSpec((1, 1, ATTN_WIDTH, past), lambda b, t: (b, layer, 0, 0))
    masked = lambda n: pltpu.VMEM((N_HEADS, n, LANES), BF16)
    masked_t = pltpu.VMEM((N_HEADS, LANES, past), BF16)
    return pl.pallas_call(
        functools.partial(_nbr_attn_kernel, rows=rows),
        grid=(batch, nt),
        in_specs=[
            pl.BlockSpec((tq, ATTN_WIDTH), tok),
            pl.BlockSpec((seq, ATTN_WIDTH), per_b),
            pl.BlockSpec((seq, ATTN_WIDTH), per_b),
            cache_spec,
            cache_spec,
            pl.BlockSpec((1, N_HEADS, 2 * WIN_H, GRID_W, LANES), lambda b, t: (layer, 0, 0, 0, 0)),
            pl.BlockSpec((tq, SGU_WIDTH), tok),
            pl.BlockSpec((tq, D_MODEL), tok),
            pl.BlockSpec((1, 1, 6, D_MODEL), lambda b, t: (layer, 1 + b, 0, 0)),
            pl.BlockSpec((1, D_MODEL, D_MODEL), lyr3),
            pl.BlockSpec((1, 1, D_MODEL), lyr3),
            pl.BlockSpec((N_EXPERTS, D_MODEL), full2),
            pl.BlockSpec((N_EXPERTS, 1), full2),
            pl.BlockSpec((SRC_TILE, SRC_TILE), full2),
        ],
        out_specs=out_specs,
        out_shape=out_shape,
        scratch_shapes=[masked(seq), masked(seq), masked_t, masked_t],
        compiler_params=_cparams("arbitrary", "arbitrary"),
        name="nbr_attn",
    )(q, k, v, cache_k, cache_v, tb, sg, x, mod, params["wout"], params["n2w"], params["wr"],
      params["br"], params["tri"])


def _bias_tables(rpb):
    cols = jnp.arange(GRID_W)
    col_start = jnp.clip(cols - WIN_W // 2, 0, GRID_W - WIN_W)
    col_ok = (cols[None, :] >= col_start[:, None]) & (cols[None, :] < col_start[:, None] + WIN_W)
    rel = cols[None, :] - cols[:, None] + WIN_W - 1
    pick = ((rel[None] == jnp.arange(2 * WIN_W - 1)[:, None, None]) & col_ok[None]).astype(F32)
    t = jnp.einsum("lhxm,mqk->lhxqk", rpb, pick, precision=lax.Precision.HIGHEST)
    t = jnp.where(col_ok, t * LOG2E, NEG)
    t = jnp.pad(t, ((0, 0), (0, 0), (1, 1), (0, 0), (0, 0)))
    return jnp.concatenate([t[:, :, :-1], t[:, :, 1:]], axis=-1)


def _max_moe_tiles(n_src):
    max_chunks = n_src * SRC_TILE // CHUNK_ROWS + n_src * N_EXPERT_GROUPS
    return -(-max_chunks // TILE_CHUNKS) + N_EXPERT_GROUPS


def _chunk_tables(chunks, n_tiles):
    n_src = chunks.shape[0]
    groups = jnp.arange(N_EXPERT_GROUPS)
    seg_end = jnp.cumsum(chunks, axis=1)
    seg_start = seg_end - chunks
    src_end = jnp.cumsum(chunks, axis=0)
    src_start = src_end - chunks
    total = src_end[-1]
    tiles = (total + TILE_CHUNKS - 1) // TILE_CHUNKS
    tile_end = jnp.cumsum(tiles)
    tile_start = tile_end - tiles
    n_used = tile_end[-1:]

    u = jnp.arange(n_tiles)
    gid = jnp.minimum(jnp.sum(u[:, None] >= tile_end[None, :], axis=1), N_EXPERT_GROUPS - 1)

    p = jnp.arange(n_tiles * TILE_CHUNKS)
    g_hot = jnp.repeat(gid, TILE_CHUNKS)[:, None] == groups[None, :]
    by_group = lambda vec: jnp.sum(jnp.where(g_hot, vec[None, :], 0), axis=1)
    q = p - TILE_CHUNKS * by_group(tile_start)
    live = q < by_group(total)
    src_end_p = jnp.sum(jnp.where(g_hot[:, None, :], src_end[None], 0), axis=2)
    t = jnp.minimum(jnp.sum(q[:, None] >= src_end_p, axis=1), n_src - 1)
    t_hot = t[:, None] == jnp.arange(n_src)[None, :]
    by_seg = lambda tab: jnp.sum(jnp.where(t_hot[:, :, None] & g_hot[:, None, :], tab[None], 0),
                                 axis=(1, 2))
    slab_chunk = by_seg(seg_start) + q - by_seg(src_start)
    src_row = jnp.where(live, t * SLAB_ROWS + CHUNK_ROWS * slab_chunk, 0)

    s = jnp.arange(SLAB_CHUNKS)
    gs = jnp.sum(s[None, :, None] >= seg_end[:, None, :], axis=2)
    used = gs < N_EXPERT_GROUPS
    s_hot = jnp.minimum(gs, N_EXPERT_GROUPS - 1)[:, :, None] == groups
    pick = lambda tab: jnp.sum(jnp.where(s_hot, tab, 0), axis=2)
    pos = (TILE_CHUNKS * pick(tile_start[None, None, :]) + pick(src_start[:, None, :])
           + s[None, :] - pick(seg_start[:, None, :]))
    out_row = jnp.where(used, CHUNK_ROWS * pos, 0).reshape(-1)
    i32 = lambda a: a.astype(jnp.int32)
    return i32(gid), i32(src_row), i32(out_row), i32(n_used)


def _chunk_copies(src_hbm, row_ref, first, n_chunks, buf, sem, slot):
    return [
        pltpu.make_async_copy(
            src_hbm.at[pl.ds(pl.multiple_of(row_ref[first + k], CHUNK_ROWS), CHUNK_ROWS), :],
            buf.at[slot, pl.ds(k * CHUNK_ROWS, CHUNK_ROWS), :],
            sem.at[slot])
        for k in range(n_chunks)
    ]


def _gather_step(src_hbm, row_ref, n_chunks, buf, sem):
    step = pl.program_id(0)
    slot = step % 2

    @pl.when(step == 0)
    def _():
        for cp in _chunk_copies(src_hbm, row_ref, 0, n_chunks, buf, sem, 0):
            cp.start()

    @pl.when(step + 1 < pl.num_programs(0))
    def _():
        for cp in _chunk_copies(src_hbm, row_ref, (step + 1) * n_chunks, n_chunks, buf, sem, 1 - slot):
            cp.start()

    for cp in _chunk_copies(src_hbm, row_ref, step * n_chunks, n_chunks, buf, sem, slot):
        cp.wait()
    return slot


def _moe_kernel(gid_ref, src_ref, nused_ref, slab_hbm, ex_ref, w1_ref, w3_ref, w2_ref, o_ref,
                buf, sem, w1b, w3b, w2b):
    u = pl.program_id(0)
    slot = _gather_step(slab_hbm, src_ref, TILE_CHUNKS, buf, sem)

    @pl.when((u == 0) | (gid_ref[u] != gid_ref[jnp.maximum(u - 1, 0)]))
    def _():
        w1b[...] = w1_ref[0].astype(BF16)
        w3b[...] = w3_ref[0].astype(BF16)
        w2b[...] = w2_ref[0, 0].astype(BF16)

    @pl.when(u < nused_ref[0])
    def _():
        h = buf[slot, :, 0:D_MODEL]
        gexp = jnp.dot(buf[slot, :, D_MODEL:], ex_ref[...], preferred_element_type=F32)
        hid = []
        for e in range(EXPERTS_PER_GROUP):
            h1 = jnp.dot(h, w1b[e], preferred_element_type=F32)
            h3 = jnp.dot(h, w3b[e], preferred_element_type=F32)
            act = (h1 * jax.nn.sigmoid(h1)) * h3
            hid.append((act * gexp[:, e * D_EXPERT:(e + 1) * D_EXPERT]).astype(BF16))
        o_ref[...] = jnp.dot(jnp.concatenate(hid, axis=1), w2b[...], preferred_element_type=F32)

    @pl.when(u >= nused_ref[0])
    def _():
        o_ref[...] = jnp.zeros_like(o_ref)


def _moe(slab, tables, layer, params):
    gid, src_row, _, n_used = tables
    n_tiles = gid.shape[0]
    expert_w = pl.BlockSpec((1, EXPERTS_PER_GROUP, D_MODEL, D_EXPERT),
                            lambda u, gid, src, nu: (layer, gid[u], 0, 0))
    return pl.pallas_call(
        _moe_kernel,
        grid_spec=pltpu.PrefetchScalarGridSpec(
            num_scalar_prefetch=3,
            grid=(n_tiles,),
            in_specs=[
                pl.BlockSpec(memory_space=pl.ANY),
                pl.BlockSpec((LANES, GROUP_HIDDEN), lambda u, gid, src, nu: (0, 0)),
                expert_w,
                expert_w,
                pl.BlockSpec((1, 1, GROUP_HIDDEN, D_MODEL), lambda u, gid, src, nu: (layer, gid[u], 0, 0)),
            ],
            out_specs=pl.BlockSpec((MOE_TILE, D_MODEL), lambda u, gid, src, nu: (u, 0)),
            scratch_shapes=[
                pltpu.VMEM((2, MOE_TILE, SLAB_WIDTH), BF16),
                pltpu.SemaphoreType.DMA((2,)),
                pltpu.VMEM((EXPERTS_PER_GROUP, D_MODEL, D_EXPERT), BF16),
                pltpu.VMEM((EXPERTS_PER_GROUP, D_MODEL, D_EXPERT), BF16),
                pltpu.VMEM((GROUP_HIDDEN, D_MODEL), BF16),
            ],
        ),
        out_shape=jax.ShapeDtypeStruct((n_tiles * MOE_TILE, D_MODEL), F32),
        compiler_params=_cparams("arbitrary"),
        name="moe",
    )(gid, src_row, n_used, slab, params["expand"], params["w1"], params["w3"], params["w2"])


def _unsorted_moe(row_ref, dest_ref, y_hbm, buf, sem):
    slot = _gather_step(y_hbm, row_ref, SLAB_CHUNKS, buf, sem)
    y = buf[slot]
    hi = y.astype(BF16)
    lo = (y - hi.astype(F32)).astype(BF16)
    perm = _permutation(dest_ref[0])
    tn = (((0,), (0,)), ((), ()))
    return (lax.dot_general(perm, hi, tn, preferred_element_type=F32)
            + lax.dot_general(perm, lo, tn, preferred_element_type=F32))


def _unsort_kernel(row_ref, xm_ref, dest_ref, mod_ref, y_hbm, o_ref, buf, sem):
    moe = _unsorted_moe(row_ref, dest_ref, y_hbm, buf, sem)
    o_ref[...] = xm_ref[...] + mod_ref[0, 0][5:6] * moe


def _unsort(pending, layer, mod, row_of_tile):
    xm, dest, tables, y = pending
    t_tokens = xm.shape[0]
    out_row = tables[2]
    return pl.pallas_call(
        _unsort_kernel,
        grid_spec=pltpu.PrefetchScalarGridSpec(
            num_scalar_prefetch=1,
            grid=(t_tokens // SRC_TILE,),
            in_specs=[
                pl.BlockSpec((SRC_TILE, D_MODEL), lambda t, rows: (t, 0)),
                pl.BlockSpec((1, 1, SRC_TILE), lambda t, rows: (t, 0, 0)),
                pl.BlockSpec((1, 1, 6, D_MODEL), lambda t, rows: (layer, row_of_tile(t), 0, 0)),
                pl.BlockSpec(memory_space=pl.ANY),
            ],
            out_specs=pl.BlockSpec((SRC_TILE, D_MODEL), lambda t, rows: (t, 0)),
            scratch_shapes=[pltpu.VMEM((2, SLAB_ROWS, D_MODEL), F32), pltpu.SemaphoreType.DMA((2,))],
        ),
        out_shape=jax.ShapeDtypeStruct((t_tokens, D_MODEL), F32),
        compiler_params=_cparams("arbitrary"),
        name="unsort",
    )(out_row, xm, dest, mod, y)


def kernel(x_prompt, x_sample, cache_k, cache_v, c, c_ctx, w_ada, b_ada, norm1_w, norm2_w, w_in,
           q_norm_w, k_norm_w, rpb, w_sgu, b_sgu, sgu_ln_w, sgu_ln_b, w_out, w_router, b_router,
           w1, w3, w2):
    batch, seq, _ = x_prompt.shape
    dec_batch, dec_seq, _ = x_sample.shape
    past = cache_k.shape[2]
    assert 1 + dec_batch <= COND_ROWS and dec_seq % (Q_ROWS * GRID_W) == 0 and seq % CHUNK == 0

    cond = jnp.zeros((COND_ROWS, D_MODEL), F32).at[0].set(c_ctx).at[1:1 + dec_batch].set(c)
    mod = _adaln(cond, w_ada, b_ada).reshape(DEPTH, COND_ROWS, 6, D_MODEL)

    xp = x_prompt.reshape(batch * seq, D_MODEL)
    xs = x_sample.reshape(dec_batch * dec_seq, D_MODEL)
    from_cache_layout = lambda a: a.transpose(0, 1, 3, 4, 2).reshape(dec_batch, DEPTH, ATTN_WIDTH, past)
    ck = from_cache_layout(cache_k)
    cv = from_cache_layout(cache_v)

    head_of = jnp.arange(256) // HEAD_DIM
    bd = (head_of[:, None] == head_of[None, :]).astype(BF16)
    col_expert = jnp.arange(GROUP_HIDDEN) // D_EXPERT
    gate_lane = jnp.arange(LANES)
    expand = ((gate_lane[:, None] % EXPERTS_PER_GROUP == col_expert[None, :])
              & (gate_lane[:, None] < 2 * EXPERTS_PER_GROUP)).astype(BF16)
    tok_id = jnp.arange(SRC_TILE)
    tri = (tok_id[:, None] <= tok_id[None, :]).astype(BF16)

    params = dict(
        n1w=norm1_w.reshape(DEPTH, 1, D_MODEL),
        n2w=norm2_w.reshape(DEPTH, 1, D_MODEL),
        win=w_in.astype(BF16),
        wout=w_out.astype(BF16),
        qw=jnp.tile(q_norm_w, (1, N_HEADS)).reshape(DEPTH, 1, ATTN_WIDTH),
        kw=jnp.tile(k_norm_w, (1, N_HEADS)).reshape(DEPTH, 1, ATTN_WIDTH),
        bd=jnp.concatenate([bd, bd], axis=0),
        ws=w_sgu.astype(BF16),
        bs=jnp.broadcast_to(b_sgu[..., None], (DEPTH, SGU_GROUPS, CHUNK, SGU_GROUP_DIM)),
        lnw=sgu_ln_w,
        lnb=sgu_ln_b,
        wr=w_router.T.astype(BF16),
        br=b_router.reshape(N_EXPERTS, 1),
        expand=expand,
        tri=tri,
        w1=w1,
        w3=w3,
        w2=w2.reshape(DEPTH, N_EXPERT_GROUPS, GROUP_HIDDEN, D_MODEL),
    )
    tb = _bias_tables(rpb)

    lat_tiles_per_seq = dec_seq // SRC_TILE
    ctx_row = lambda i: 0
    lat_row = lambda i: 1 + i // lat_tiles_per_seq

    def moe_block(xm, slab, dest, meta, layer):
        n_src = meta.shape[0]
        tables = _chunk_tables(meta[:, :N_EXPERT_GROUPS, 0], _max_moe_tiles(n_src))
        return (xm, dest, tables, _moe(slab, tables, layer, params))

    kbuf = jnp.zeros((batch, DEPTH, ATTN_WIDTH, seq), F32)
    vbuf = jnp.zeros((batch, DEPTH, ATTN_WIDTH, seq), F32)

    for l in range(DEPTH):
        outs = _inproj(xp, l, mod, ctx_row, params, tm=seq, kv_buffers=(kbuf, vbuf))
        if l > 0:
            xp, outs = outs[0], outs[1:]
        q, kbuf, vbuf, v, sg = outs
        xp = moe_block(*_ctx_attention(q, kbuf, v, sg, xp, l, mod, params, seq=seq), l)

        outs = _inproj(xs, l, mod, lat_row, params, tm=SRC_TILE)
        if l > 0:
            xs, outs = outs[0], outs[1:]
        q, k, v, sg = outs
        xs = moe_block(*_nbr_attention(q, k, v, ck, cv, tb, sg, xs, l, mod, params,
                                       batch=dec_batch, seq=dec_seq), l)
    xp = _unsort(xp, DEPTH - 1, mod, ctx_row)
    xs = _unsort(xs, DEPTH - 1, mod, lat_row)

    to_cache_layout = lambda buf: buf.reshape(batch, DEPTH, N_HEADS, HEAD_DIM, seq).transpose(0, 1, 4, 2, 3)
    return (xp.reshape(batch, seq, D_MODEL), xs.reshape(dec_batch, dec_seq, D_MODEL),
            to_cache_layout(kbuf), to_cache_layout(vbuf))
```

```python
import functools

import jax
import jax.numpy as jnp
from jax import lax
from jax.experimental import pallas as pl
from jax.experimental.pallas import tpu as pltpu

F32 = jnp.float32
BF16 = jnp.bfloat16

D_MODEL = 1024
DEPTH = 4
N_HEADS = 8
HEAD_DIM = 64
ATTN_WIDTH = N_HEADS * HEAD_DIM
SGU_GROUPS = 4
SGU_GROUP_DIM = 128
SGU_WIDTH = SGU_GROUPS * SGU_GROUP_DIM
CHUNK = 128
IN_WIDTH = 3 * ATTN_WIDTH + 2 * SGU_WIDTH
GRID_W = 64
WIN_H = 8
WIN_W = 16
N_EXPERTS = 16
N_EXPERT_GROUPS = 4
EXPERTS_PER_GROUP = 4
D_EXPERT = 256
GROUP_HIDDEN = EXPERTS_PER_GROUP * D_EXPERT
EPS = 1e-6
NEG = -1e30
LOG2E = 1.4426950408889634

LANES = 128
HEADS_PER_VREG = LANES // HEAD_DIM
N_HEAD_PAIRS = N_HEADS // HEADS_PER_VREG
COND_ROWS = 16
Q_ROWS = 4
WIN_ROWS = Q_ROWS + WIN_H
VMEM_LIMIT = 48 * 1024 * 1024

SRC_TILE = 256
CHUNK_ROWS = 16
SLAB_ROWS = SRC_TILE + N_EXPERT_GROUPS * CHUNK_ROWS
SLAB_CHUNKS = SLAB_ROWS // CHUNK_ROWS
SLAB_WIDTH = D_MODEL + LANES
MOE_TILE = 256
TILE_CHUNKS = MOE_TILE // CHUNK_ROWS


def _cparams(*sem):
    return pltpu.CompilerParams(dimension_semantics=sem, vmem_limit_bytes=VMEM_LIMIT)


def _adaln_kernel(cond_ref, w_ref, b_ref, o_ref):
    c = cond_ref[...]
    a = (c * jax.nn.sigmoid(c)).astype(BF16)
    o_ref[0] = jnp.dot(a, w_ref[0].astype(BF16), preferred_element_type=F32) + b_ref[0]


def _adaln(cond, w_ada, b_ada):
    tn = 1536
    n = 6 * D_MODEL
    return pl.pallas_call(
        _adaln_kernel,
        grid=(DEPTH, n // tn),
        in_specs=[
            pl.BlockSpec((COND_ROWS, D_MODEL), lambda l, j: (0, 0)),
            pl.BlockSpec((1, D_MODEL, tn), lambda l, j: (l, 0, j)),
            pl.BlockSpec((1, 1, tn), lambda l, j: (l, 0, j)),
        ],
        out_specs=pl.BlockSpec((1, COND_ROWS, tn), lambda l, j: (l, 0, j)),
        out_shape=jax.ShapeDtypeStruct((DEPTH, COND_ROWS, n), F32),
        compiler_params=_cparams("arbitrary", "arbitrary"),
        name="adaln",
    )(cond, w_ada, b_ada.reshape(DEPTH, 1, n))


def _gelu_tanh(x):
    return 0.5 * x * (1.0 + jnp.tanh(0.7978845608028654 * (x + 0.044715 * (x * x * x))))


def _head_rms(t, bd, w):
    t2 = t * t
    hi = t2.astype(BF16)
    lo = (t2 - hi.astype(F32)).astype(BF16)
    outs = []
    for c in range(ATTN_WIDTH // 256):
        sl = slice(256 * c, 256 * c + 256)
        hl = jnp.concatenate([hi[:, sl], lo[:, sl]], axis=1)
        ss = jnp.dot(hl, bd, preferred_element_type=F32)
        outs.append(t[:, sl] * lax.rsqrt(ss * (1.0 / HEAD_DIM) + EPS))
    return jnp.concatenate(outs, axis=1) * w


def _inproj_kernel(*refs, tm, transposed_kv, fused_moe):
    refs = list(refs)
    if fused_moe:
        row_ref, xm_ref, dest_ref, prev_mod_ref, y_hbm = refs[:5]
        refs = refs[5:]
    else:
        x_ref = refs.pop(0)
    (mod_ref, n1w_ref, win_ref, qw_ref, kw_ref, bd_ref, ws_ref, bs_ref, lnw_ref, lnb_ref) = refs[:10]
    refs = refs[10:]
    if transposed_kv:
        refs = refs[2:]
    if fused_moe:
        x_out_ref = refs.pop(0)
    if transposed_kv:
        q_ref, kt_ref, vt_ref, v_ref, sg_ref = refs[:5]
        refs = refs[5:]
    else:
        q_ref, k_ref, v_ref, sg_ref = refs[:4]
        refs = refs[4:]
    if fused_moe:
        buf, sem = refs
        moe = _unsorted_moe(row_ref, dest_ref, y_hbm, buf, sem)
        x = xm_ref[...] + prev_mod_ref[0, 0][5:6] * moe
        x_out_ref[...] = x
    else:
        x = x_ref[...]
    m = mod_ref[0, 0]
    sh1, sc1 = m[0:1], m[1:2]
    ms = jnp.mean(x * x, axis=-1, keepdims=True)
    h = x * lax.rsqrt(ms + EPS) * n1w_ref[0]
    h = (h * (1.0 + sc1) + sh1).astype(BF16)
    p = jnp.dot(h, win_ref[0], preferred_element_type=F32)

    bd = bd_ref[...]
    q = _head_rms(p[:, 0:ATTN_WIDTH], bd, qw_ref[0])
    k = _head_rms(p[:, ATTN_WIDTH:2 * ATTN_WIDTH], bd, kw_ref[0])
    v = p[:, 2 * ATTN_WIDTH:3 * ATTN_WIDTH]
    q_ref[...] = (q * (HEAD_DIM ** -0.5 * LOG2E)).astype(q_ref.dtype)
    if transposed_kv:
        kt_ref[0, 0] = k.T
        vt_ref[0, 0] = v.T
    else:
        k_ref[...] = k.astype(k_ref.dtype)
    v_ref[...] = v.astype(v_ref.dtype)

    off_u = 3 * ATTN_WIDTH
    off_v = off_u + SGU_WIDTH
    for g in range(SGU_GROUPS):
        gl = slice(g * SGU_GROUP_DIM, (g + 1) * SGU_GROUP_DIM)
        u = _gelu_tanh(p[:, off_u + g * SGU_GROUP_DIM: off_u + (g + 1) * SGU_GROUP_DIM])
        t = _gelu_tanh(p[:, off_v + g * SGU_GROUP_DIM: off_v + (g + 1) * SGU_GROUP_DIM])
        mu = jnp.mean(t, axis=-1, keepdims=True)
        d = t - mu
        var = jnp.mean(d * d, axis=-1, keepdims=True)
        y = (d * lax.rsqrt(var + EPS) * lnw_ref[0, g:g + 1, :] + lnb_ref[0, g:g + 1, :]).astype(BF16)
        for c in range(tm // CHUNK):
            rows = slice(c * CHUNK, (c + 1) * CHUNK)
            sv = jnp.dot(ws_ref[0, g], y[rows], preferred_element_type=F32) + bs_ref[0, g]
            sg_ref[rows, gl] = (u[rows] * sv).astype(sg_ref.dtype)


def _inproj(x, layer, mod, row_of_tile, params, *, tm, kv_buffers=None):
    fused_moe = isinstance(x, tuple)
    tok = lambda i, *_: (i, 0)
    lyr3 = lambda i, *_: (layer, 0, 0)
    lyr4 = lambda i, *_: (layer, 0, 0, 0)
    mod_row = lambda lyr: pl.BlockSpec((1, 1, 6, D_MODEL), lambda i, *_: (lyr, row_of_tile(i), 0, 0))
    x_spec = pl.BlockSpec((tm, D_MODEL), tok)
    if fused_moe:
        assert tm == SRC_TILE
        xm, dest, tables, y = x
        t_tokens = xm.shape[0]
        prefetch = [tables[2]]
        in_specs = [x_spec, pl.BlockSpec((1, 1, SRC_TILE), lambda i, *_: (i, 0, 0)),
                    mod_row(layer - 1), pl.BlockSpec(memory_space=pl.ANY)]
        args = [xm, dest, mod, y]
        scratch = [pltpu.VMEM((2, SLAB_ROWS, D_MODEL), F32), pltpu.SemaphoreType.DMA((2,))]
    else:
        t_tokens = x.shape[0]
        prefetch, in_specs, args, scratch = [], [x_spec], [x], []
    act = pl.BlockSpec((tm, ATTN_WIDTH), tok)
    act_sds = jax.ShapeDtypeStruct((t_tokens, ATTN_WIDTH), BF16)
    in_specs += [
        mod_row(layer),
        pl.BlockSpec((1, 1, D_MODEL), lyr3),
        pl.BlockSpec((1, D_MODEL, IN_WIDTH), lyr3),
        pl.BlockSpec((1, 1, ATTN_WIDTH), lyr3),
        pl.BlockSpec((1, 1, ATTN_WIDTH), lyr3),
        pl.BlockSpec((512, 256), lambda i, *_: (0, 0)),
        pl.BlockSpec((1, SGU_GROUPS, CHUNK, CHUNK), lyr4),
        pl.BlockSpec((1, SGU_GROUPS, CHUNK, SGU_GROUP_DIM), lyr4),
        pl.BlockSpec((1, SGU_GROUPS, SGU_GROUP_DIM), lyr3),
        pl.BlockSpec((1, SGU_GROUPS, SGU_GROUP_DIM), lyr3),
    ]
    args += [mod, params["n1w"], params["win"], params["qw"], params["kw"], params["bd"],
             params["ws"], params["bs"], params["lnw"], params["lnb"]]
    out_specs = [x_spec] if fused_moe else []
    out_shape = [jax.ShapeDtypeStruct((t_tokens, D_MODEL), F32)] if fused_moe else []
    if kv_buffers is None:
        out_specs += [act] * 4
        out_shape += [act_sds] * 4
        aliases = {}
    else:
        kbuf, vbuf = kv_buffers
        kv_spec = pl.BlockSpec((1, 1, ATTN_WIDTH, tm), lambda i, *_: (i, layer, 0, 0))
        first_alias_in = len(prefetch) + len(args)
        first_alias_out = len(out_specs) + 1
        in_specs += [pl.BlockSpec(memory_space=pl.ANY)] * 2
        args += [kbuf, vbuf]
        out_specs += [act, kv_spec, kv_spec, act, act]
        out_shape += [act_sds, jax.ShapeDtypeStruct(kbuf.shape, F32),
                      jax.ShapeDtypeStruct(vbuf.shape, F32), act_sds, act_sds]
        aliases = {first_alias_in: first_alias_out, first_alias_in + 1: first_alias_out + 1}
    return pl.pallas_call(
        functools.partial(_inproj_kernel, tm=tm, transposed_kv=kv_buffers is not None,
                          fused_moe=fused_moe),
        grid_spec=pltpu.PrefetchScalarGridSpec(
            num_scalar_prefetch=len(prefetch),
            grid=(t_tokens // tm,),
            in_specs=in_specs,
            out_specs=out_specs,
            scratch_shapes=scratch,
        ),
        out_shape=out_shape,
        input_output_aliases=aliases,
        compiler_params=_cparams("arbitrary"),
        name="inproj",
    )(*prefetch, *args)


def _route(lg_t, br):
    s = jax.nn.sigmoid(lg_t)
    sel = s + br
    row = lambda a, i: a[i:i + 1]
    g_score = []
    for g in range(N_EXPERT_GROUPS):
        v = [row(sel, 4 * g + i) for i in range(4)]
        best_pair = None
        for i in range(4):
            for j in range(i + 1, 4):
                pair = v[i] + v[j]
                best_pair = pair if best_pair is None else jnp.maximum(best_pair, pair)
        g_score.append(best_pair)
    best = jnp.zeros_like(g_score[0], dtype=jnp.int32)
    top = g_score[0]
    for g in range(1, N_EXPERT_GROUPS):
        upd = g_score[g] > top
        best = jnp.where(upd, g, best)
        top = jnp.where(upd, g_score[g], top)
    cand, aff = [], []
    for i in range(4):
        ci, si = row(sel, i), row(s, i)
        for g in range(1, N_EXPERT_GROUPS):
            ci = jnp.where(best == g, row(sel, 4 * g + i), ci)
            si = jnp.where(best == g, row(s, 4 * g + i), si)
        cand.append(ci)
        aff.append(si)

    def first_argmax(vals):
        idx = jnp.zeros_like(best)
        top_v = vals[0]
        for i in range(1, 4):
            upd = vals[i] > top_v
            idx = jnp.where(upd, i, idx)
            top_v = jnp.where(upd, vals[i], top_v)
        return idx

    i1 = first_argmax(cand)
    i2 = first_argmax([jnp.where(i1 == i, -jnp.inf, cand[i]) for i in range(4)])
    pick = lambda idx: sum(jnp.where(idx == i, aff[i], 0.0) for i in range(4))
    den = pick(i1) + pick(i2)
    gate = [jnp.where((i1 == i) | (i2 == i), aff[i] / den, 0.0) for i in range(4)]
    hi = [x.astype(BF16) for x in gate]
    lo = [(x - h.astype(F32)).astype(BF16) for x, h in zip(gate, hi)]
    return best, hi + lo


def _slab_positions(best, tri):
    n = best.shape[1]
    onehot = [jnp.where(best == g, 1.0, 0.0) for g in range(N_EXPERT_GROUPS)]
    pad = [jnp.zeros_like(onehot[0])] * (8 - N_EXPERT_GROUPS)
    oh = jnp.concatenate(onehot + pad, axis=0).astype(BF16)
    counts = jnp.dot(oh, tri, preferred_element_type=F32)
    dest = jnp.zeros((1, n), F32)
    seg_start = jnp.zeros((1, 1), F32)
    chunks = []
    for g in range(N_EXPERT_GROUPS):
        cg = counts[g:g + 1]
        n_g = jnp.max(cg, axis=1, keepdims=True)
        c_g = jnp.floor((n_g + (CHUNK_ROWS - 1)) * (1.0 / CHUNK_ROWS))
        dest = dest + onehot[g] * (seg_start + cg - 1.0)
        seg_start = seg_start + c_g * CHUNK_ROWS
        chunks.append(c_g)
    return dest.astype(jnp.int32), chunks


def _permutation(dest):
    n = dest.shape[1]
    hit = lax.broadcasted_iota(jnp.int32, (SLAB_ROWS, n), 0) == dest
    return jnp.where(hit, 1.0, 0.0).astype(BF16)


def _post_attention(a, sg, x, m, wout_ref, n2w_ref, xm_ref):
    g1, sh2, sc2 = m[2:3], m[3:4], m[4:5]
    cat = jnp.concatenate([a, sg], axis=1)
    y = jnp.dot(cat, wout_ref[0], preferred_element_type=F32)
    xm = x + g1 * y
    xm_ref[...] = xm
    ms = jnp.mean(xm * xm, axis=-1, keepdims=True)
    h2 = xm * lax.rsqrt(ms + EPS) * n2w_ref[0]
    return (h2 * (1.0 + sc2) + sh2).astype(BF16)


def _route_previous(h2_scr, wr_ref, br_ref, tri_ref, slab_ref, dest_ref, meta_ref):
    h2 = h2_scr[...]
    lg_t = lax.dot_general(wr_ref[...], h2, (((1,), (1,)), ((), ())), preferred_element_type=F32)
    yield
    best, gates = _route(lg_t, br_ref[...])
    dest, chunks = _slab_positions(best, tri_ref[...])
    yield
    perm = _permutation(dest)
    slab_ref[:, 0:D_MODEL] = jnp.dot(perm, h2, preferred_element_type=F32).astype(BF16)
    gmat = jnp.concatenate(gates + [jnp.zeros((LANES - len(gates), dest.shape[1]), BF16)], axis=0)
    slab_ref[:, D_MODEL:] = lax.dot_general(perm, gmat, (((1,), (1,)), ((), ())),
                                            preferred_element_type=F32).astype(BF16)
    dest_ref[0] = dest
    meta = [jnp.broadcast_to(c, (1, LANES)) for c in chunks]
    meta += [jnp.zeros((8 - len(chunks), LANES), F32)]
    meta_ref[0] = jnp.concatenate(meta, axis=0).astype(jnp.int32)


def _pair_pos(axis):
    return lax.broadcasted_iota(jnp.int32, (1, LANES) if axis == 1 else (LANES, 1), axis)


def _head_mask(hh, axis):
    pos = _pair_pos(axis)
    return (pos >= hh * HEAD_DIM) & (pos < (hh + 1) * HEAD_DIM)


def _only_head(x2, hh, axis):
    return jnp.where(_head_mask(hh, axis), x2, jnp.zeros_like(x2))


def _ctx_attn_kernel(q_ref, kt_ref, v_ref, sg_ref, x_ref, mod_ref, wout_ref, n2w_ref, wr_ref, br_ref,
                     tri_ref, xm_ref, slab_ref, dest_ref, meta_ref, h2_scr):
    step = pl.program_id(0)

    @pl.when(step == 0)
    def _():
        h2_scr[...] = jnp.zeros_like(h2_scr)

    router = _route_previous(h2_scr, wr_ref, br_ref, tri_ref, slab_ref, dest_ref, meta_ref)
    next(router)

    outs = []
    for j in range(N_HEAD_PAIRS):
        if j == N_HEAD_PAIRS // 2:
            next(router)
        cols = slice(j * LANES, (j + 1) * LANES)
        q2 = q_ref[:, cols]
        kt2 = kt_ref[0, 0, cols, :].astype(BF16)
        v2 = v_ref[:, cols]
        o_pair = None
        for hh in range(HEADS_PER_VREG):
            s = jnp.dot(q2, _only_head(kt2, hh, 0), preferred_element_type=F32)
            e = jnp.exp2(s - jnp.max(s, axis=-1, keepdims=True))
            l = jnp.sum(e, axis=-1, keepdims=True)
            o = jnp.dot(e.astype(BF16), _only_head(v2, hh, 1), preferred_element_type=F32) * (1.0 / l)
            o_pair = o if o_pair is None else o_pair + o
        outs.append(o_pair.astype(BF16))
    for _ in router:
        pass
    a = jnp.concatenate(outs, axis=1)
    h2_scr[...] = _post_attention(a, sg_ref[...], x_ref[...], mod_ref[0, 0], wout_ref, n2w_ref, xm_ref)


def _post_attention_out(t_tokens):
    n_src = t_tokens // SRC_TILE
    prev = lambda s: jnp.maximum(s - 1, 0)
    specs = [
        pl.BlockSpec((SRC_TILE, D_MODEL), lambda s: (jnp.minimum(s, n_src - 1), 0)),
        pl.BlockSpec((SLAB_ROWS, SLAB_WIDTH), lambda s: (prev(s), 0)),
        pl.BlockSpec((1, 1, SRC_TILE), lambda s: (prev(s), 0, 0)),
        pl.BlockSpec((1, 8, LANES), lambda s: (prev(s), 0, 0)),
    ]
    shapes = [
        jax.ShapeDtypeStruct((t_tokens, D_MODEL), F32),
        jax.ShapeDtypeStruct((n_src * SLAB_ROWS, SLAB_WIDTH), BF16),
        jax.ShapeDtypeStruct((n_src, 1, SRC_TILE), jnp.int32),
        jax.ShapeDtypeStruct((n_src, 8, LANES), jnp.int32),
    ]
    return specs, shapes, pltpu.VMEM((SRC_TILE, D_MODEL), BF16)


def _ctx_attention(q, kbuf, v, sg, x, layer, mod, params, *, seq):
    assert seq == SRC_TILE
    t_tokens = x.shape[0]
    n_seq = t_tokens // seq
    seq_of = lambda s: jnp.minimum(s, n_seq - 1)
    tok = lambda s: (seq_of(s), 0)
    lyr3 = lambda s: (layer, 0, 0)
    full2 = lambda s: (0, 0)
    out_specs, out_shape, h2_scratch = _post_attention_out(t_tokens)
    return pl.pallas_call(
        _ctx_attn_kernel,
        grid=(n_seq + 1,),
        in_specs=[
            pl.BlockSpec((seq, ATTN_WIDTH), tok),
            pl.BlockSpec((1, 1, ATTN_WIDTH, seq), lambda s: (seq_of(s), layer, 0, 0)),
            pl.BlockSpec((seq, ATTN_WIDTH), tok),
            pl.BlockSpec((seq, SGU_WIDTH), tok),
            pl.BlockSpec((seq, D_MODEL), tok),
            pl.BlockSpec((1, 1, 6, D_MODEL), lambda s: (layer, 0, 0, 0)),
            pl.BlockSpec((1, D_MODEL, D_MODEL), lyr3),
            pl.BlockSpec((1, 1, D_MODEL), lyr3),
            pl.BlockSpec((N_EXPERTS, D_MODEL), full2),
            pl.BlockSpec((N_EXPERTS, 1), full2),
            pl.BlockSpec((SRC_TILE, SRC_TILE), full2),
        ],
        out_specs=out_specs,
        out_shape=out_shape,
        scratch_shapes=[h2_scratch],
        compiler_params=_cparams("arbitrary"),
        name="ctx_attn",
    )(q, kbuf, v, sg, x, mod, params["wout"], params["n2w"], params["wr"], params["br"], params["tri"])


def _row_start(r, rows):
    return jnp.clip(r - WIN_H // 2, 0, rows - WIN_H)


def _nbr_attn_kernel(q_ref, k_ref, v_ref, ck_ref, cv_ref, tb_ref, sg_ref, x_ref, mod_ref, wout_ref,
                     n2w_ref, wr_ref, br_ref, tri_ref, xm_ref, slab_ref, dest_ref, meta_ref,
                     kwm_ref, vwm_ref, kcm_ref, vcm_ref, h2_scr, *, rows, n_tiles):
    step = pl.program_id(0)
    tiles_per_seq = rows // Q_ROWS
    t = jnp.minimum(step, n_tiles - 1) % tiles_per_seq

    @pl.when(step == 0)
    def _():
        h2_scr[...] = jnp.zeros_like(h2_scr)

    @pl.when((t == 0) & (step < n_tiles))
    def _():
        for j in range(N_HEAD_PAIRS):
            cols = slice(j * LANES, (j + 1) * LANES)
            k2, v2 = k_ref[:, cols], v_ref[:, cols]
            ck2 = ck_ref[0, 0, cols, :].astype(BF16)
            cv2 = cv_ref[0, 0, cols, :].astype(BF16)
            for hh in range(HEADS_PER_VREG):
                h = HEADS_PER_VREG * j + hh
                kwm_ref[h] = _only_head(k2, hh, 1)
                vwm_ref[h] = _only_head(v2, hh, 1)
                kcm_ref[h] = _only_head(ck2, hh, 0)
                vcm_ref[h] = _only_head(cv2, hh, 0)

    router = _route_previous(h2_scr, wr_ref, br_ref, tri_ref, slab_ref, dest_ref, meta_ref)
    next(router)

    r0 = t * Q_ROWS
    ws = jnp.minimum(_row_start(r0, rows), rows - WIN_ROWS)
    tok0 = pl.multiple_of(ws * GRID_W, GRID_W)

    lane = lax.broadcasted_iota(jnp.int32, (1, LANES), 1)
    blk_idx, row_mask = [], []
    for a in range(Q_ROWS):
        r = r0 + a
        rs = _row_start(r, rows)
        idx_a, mask_a = [], []
        for jp in range(WIN_ROWS // 2):
            kr = ws + 2 * jp
            idx_a.append(jnp.clip(kr - r + WIN_H, 0, 2 * WIN_H - 1))
            ok0 = (kr >= rs) & (kr < rs + WIN_H)
            ok1 = (kr + 1 >= rs) & (kr + 1 < rs + WIN_H)
            m0 = jnp.where(ok0, 0.0, NEG).astype(F32)
            m1 = jnp.where(ok1, 0.0, NEG).astype(F32)
            mask_a.append(jnp.where(lane < GRID_W, m0, m1))
        blk_idx.append(idx_a)
        row_mask.append(mask_a)

    nt = (((1,), (1,)), ((), ()))
    outs = []
    for j in range(N_HEAD_PAIRS):
        if j == 1:
            next(router)
        q2 = q_ref[:, j * LANES:(j + 1) * LANES]
        o_pair = None
        for hh in range(HEADS_PER_VREG):
            h = j * HEADS_PER_VREG + hh
            kwm = kwm_ref[h, pl.ds(tok0, WIN_ROWS * GRID_W), :]
            vwm = vwm_ref[h, pl.ds(tok0, WIN_ROWS * GRID_W), :]
            bias = jnp.concatenate([
                jnp.concatenate([tb_ref[0, h, blk_idx[a][jp]] + row_mask[a][jp]
                                 for jp in range(WIN_ROWS // 2)], axis=1)
                for a in range(Q_ROWS)], axis=0)
            s_w = lax.dot_general(q2, kwm, nt, preferred_element_type=F32) + bias
            s_c = jnp.dot(q2, kcm_ref[h], preferred_element_type=F32)
            mx = jnp.maximum(jnp.max(s_w, axis=-1, keepdims=True), jnp.max(s_c, axis=-1, keepdims=True))
            e_w = jnp.exp2(s_w - mx)
            e_c = jnp.exp2(s_c - mx)
            l = jnp.sum(e_w, axis=-1, keepdims=True) + jnp.sum(e_c, axis=-1, keepdims=True)
            o = (jnp.dot(e_w.astype(BF16), vwm, preferred_element_type=F32)
                 + lax.dot_general(e_c.astype(BF16), vcm_ref[h], nt,
                                   preferred_element_type=F32)) * (1.0 / l)
            o_pair = o if o_pair is None else o_pair + o
        outs.append(o_pair.astype(BF16))
        if j == 2:
            for _ in router:
                pass
    a_out = jnp.concatenate(outs, axis=1)
    h2_scr[...] = _post_attention(a_out, sg_ref[...], x_ref[...], mod_ref[0, 0], wout_ref, n2w_ref,
                                  xm_ref)


def _nbr_attention(q, k, v, cache_k, cache_v, tb, sg, x, layer, mod, params, *, batch, seq):
    t_tokens = x.shape[0]
    rows = seq // GRID_W
    tq = Q_ROWS * GRID_W
    assert tq == SRC_TILE
    nt = seq // tq
    past = cache_k.shape[3]
    n_tiles = batch * nt
    out_specs, out_shape, h2_scratch = _post_attention_out(t_tokens)
    tile_of = lambda s: jnp.minimum(s, n_tiles - 1)
    seq_of = lambda s: tile_of(s) // nt
    tok = lambda s: (tile_of(s), 0)
    per_b = lambda s: (seq_of(s), 0)
    lyr3 = lambda s: (layer, 0, 0)
    full2 = lambda s: (0, 0)
    cache_spec = pl.BlockSpec((1, 1, ATTN_WIDTH, past), lambda s: (seq_of(s), layer, 0, 0))
    masked = lambda n: pltpu.VMEM((N_HEADS, n, LANES), BF16)
    masked_t = pltpu.VMEM((N_HEADS, LANES, past), BF16)
    return pl.pallas_call(
        functools.partial(_nbr_attn_kernel, rows=rows, n_tiles=n_tiles),
        grid=(n_tiles + 1,),
        in_specs=[
            pl.BlockSpec((tq, ATTN_WIDTH), tok),
            pl.BlockSpec((seq, ATTN_WIDTH), per_b),
            pl.BlockSpec((seq, ATTN_WIDTH), per_b),
            cache_spec,
            cache_spec,
            pl.BlockSpec((1, N_HEADS, 2 * WIN_H, GRID_W, LANES), lambda s: (layer, 0, 0, 0, 0)),
            pl.BlockSpec((tq, SGU_WIDTH), tok),
            pl.BlockSpec((tq, D_MODEL), tok),
            pl.BlockSpec((1, 1, 6, D_MODEL), lambda s: (layer, 1 + seq_of(s), 0, 0)),
            pl.BlockSpec((1, D_MODEL, D_MODEL), lyr3),
            pl.BlockSpec((1, 1, D_MODEL), lyr3),
            pl.BlockSpec((N_EXPERTS, D_MODEL), full2),
            pl.BlockSpec((N_EXPERTS, 1), full2),
            pl.BlockSpec((SRC_TILE, SRC_TILE), full2),
        ],
        out_specs=out_specs,
        out_shape=out_shape,
        scratch_shapes=[masked(seq), masked(seq), masked_t, masked_t, h2_scratch],
        compiler_params=_cparams("arbitrary"),
        name="nbr_attn",
    )(q, k, v, cache_k, cache_v, tb, sg, x, mod, params["wout"], params["n2w"], params["wr"],
      params["br"], params["tri"])


def _bias_tables(rpb):
    cols = jnp.arange(GRID_W)
    col_start = jnp.clip(cols - WIN_W // 2, 0, GRID_W - WIN_W)
    col_ok = (cols[None, :] >= col_start[:, None]) & (cols[None, :] < col_start[:, None] + WIN_W)
    rel = cols[None, :] - cols[:, None] + WIN_W - 1
    pick = ((rel[None] == jnp.arange(2 * WIN_W - 1)[:, None, None]) & col_ok[None]).astype(F32)
    t = jnp.einsum("lhxm,mqk->lhxqk", rpb, pick, precision=lax.Precision.HIGHEST)
    t = jnp.where(col_ok, t * LOG2E, NEG)
    t = jnp.pad(t, ((0, 0), (0, 0), (1, 1), (0, 0), (0, 0)))
    return jnp.concatenate([t[:, :, :-1], t[:, :, 1:]], axis=-1)


def _max_moe_tiles(n_src):
    max_chunks = n_src * SRC_TILE // CHUNK_ROWS + n_src * N_EXPERT_GROUPS
    return -(-max_chunks // TILE_CHUNKS) + N_EXPERT_GROUPS


def _chunk_tables(chunks, n_tiles):
    n_src = chunks.shape[0]
    groups = jnp.arange(N_EXPERT_GROUPS)
    seg_end = jnp.cumsum(chunks, axis=1)
    seg_start = seg_end - chunks
    src_end = jnp.cumsum(chunks, axis=0)
    src_start = src_end - chunks
    total = src_end[-1]
    tiles = (total + TILE_CHUNKS - 1) // TILE_CHUNKS
    tile_end = jnp.cumsum(tiles)
    tile_start = tile_end - tiles
    n_used = tile_end[-1:]

    u = jnp.arange(n_tiles)
    gid = jnp.minimum(jnp.sum(u[:, None] >= tile_end[None, :], axis=1), N_EXPERT_GROUPS - 1)

    p = jnp.arange(n_tiles * TILE_CHUNKS)
    g_hot = jnp.repeat(gid, TILE_CHUNKS)[:, None] == groups[None, :]
    by_group = lambda vec: jnp.sum(jnp.where(g_hot, vec[None, :], 0), axis=1)
    q = p - TILE_CHUNKS * by_group(tile_start)
    live = q < by_group(total)
    src_end_p = jnp.sum(jnp.where(g_hot[:, None, :], src_end[None], 0), axis=2)
    t = jnp.minimum(jnp.sum(q[:, None] >= src_end_p, axis=1), n_src - 1)
    t_hot = t[:, None] == jnp.arange(n_src)[None, :]
    by_seg = lambda tab: jnp.sum(jnp.where(t_hot[:, :, None] & g_hot[:, None, :], tab[None], 0),
                                 axis=(1, 2))
    slab_chunk = by_seg(seg_start) + q - by_seg(src_start)
    src_row = jnp.where(live, t * SLAB_ROWS + CHUNK_ROWS * slab_chunk, 0)

    s = jnp.arange(SLAB_CHUNKS)
    gs = jnp.sum(s[None, :, None] >= seg_end[:, None, :], axis=2)
    used = gs < N_EXPERT_GROUPS
    s_hot = jnp.minimum(gs, N_EXPERT_GROUPS - 1)[:, :, None] == groups
    pick = lambda tab: jnp.sum(jnp.where(s_hot, tab, 0), axis=2)
    pos = (TILE_CHUNKS * pick(tile_start[None, None, :]) + pick(src_start[:, None, :])
           + s[None, :] - pick(seg_start[:, None, :]))
    out_row = jnp.where(used, CHUNK_ROWS * pos, 0).reshape(-1)
    i32 = lambda a: a.astype(jnp.int32)
    return i32(gid), i32(src_row), i32(out_row), i32(n_used)


def _chunk_copies(src_hbm, row_ref, first, n_chunks, buf, sem, slot):
    return [
        pltpu.make_async_copy(
            src_hbm.at[pl.ds(pl.multiple_of(row_ref[first + k], CHUNK_ROWS), CHUNK_ROWS), :],
            buf.at[slot, pl.ds(k * CHUNK_ROWS, CHUNK_ROWS), :],
            sem.at[slot])
        for k in range(n_chunks)
    ]


def _gather_step(src_hbm, row_ref, n_chunks, buf, sem):
    step = pl.program_id(0)
    slot = step % 2

    @pl.when(step == 0)
    def _():
        for cp in _chunk_copies(src_hbm, row_ref, 0, n_chunks, buf, sem, 0):
            cp.start()

    @pl.when(step + 1 < pl.num_programs(0))
    def _():
        for cp in _chunk_copies(src_hbm, row_ref, (step + 1) * n_chunks, n_chunks, buf, sem, 1 - slot):
            cp.start()

    for cp in _chunk_copies(src_hbm, row_ref, step * n_chunks, n_chunks, buf, sem, slot):
        cp.wait()
    return slot


def _moe_kernel(gid_ref, src_ref, nused_ref, slab_hbm, ex_ref, w1_ref, w3_ref, w2_ref, o_ref,
                buf, sem, w1b, w3b, w2b):
    u = pl.program_id(0)
    slot = _gather_step(slab_hbm, src_ref, TILE_CHUNKS, buf, sem)

    @pl.when((u == 0) | (gid_ref[u] != gid_ref[jnp.maximum(u - 1, 0)]))
    def _():
        w1b[...] = w1_ref[0].astype(BF16)
        w3b[...] = w3_ref[0].astype(BF16)
        w2b[...] = w2_ref[0, 0].astype(BF16)

    @pl.when(u < nused_ref[0])
    def _():
        h = buf[slot, :, 0:D_MODEL]
        gexp = jnp.dot(buf[slot, :, D_MODEL:], ex_ref[...], preferred_element_type=F32)
        hid = []
        for e in range(EXPERTS_PER_GROUP):
            h1 = jnp.dot(h, w1b[e], preferred_element_type=F32)
            h3 = jnp.dot(h, w3b[e], preferred_element_type=F32)
            act = (h1 * jax.nn.sigmoid(h1)) * h3
            hid.append((act * gexp[:, e * D_EXPERT:(e + 1) * D_EXPERT]).astype(BF16))
        o_ref[...] = jnp.dot(jnp.concatenate(hid, axis=1), w2b[...], preferred_element_type=F32)

    @pl.when(u >= nused_ref[0])
    def _():
        o_ref[...] = jnp.zeros_like(o_ref)


def _moe(slab, tables, layer, params):
    gid, src_row, _, n_used = tables
    n_tiles = gid.shape[0]
    expert_w = pl.BlockSpec((1, EXPERTS_PER_GROUP, D_MODEL, D_EXPERT),
                            lambda u, gid, src, nu: (layer, gid[u], 0, 0))
    return pl.pallas_call(
        _moe_kernel,
        grid_spec=pltpu.PrefetchScalarGridSpec(
            num_scalar_prefetch=3,
            grid=(n_tiles,),
            in_specs=[
                pl.BlockSpec(memory_space=pl.ANY),
                pl.BlockSpec((LANES, GROUP_HIDDEN), lambda u, gid, src, nu: (0, 0)),
                expert_w,
                expert_w,
                pl.BlockSpec((1, 1, GROUP_HIDDEN, D_MODEL), lambda u, gid, src, nu: (layer, gid[u], 0, 0)),
            ],
            out_specs=pl.BlockSpec((MOE_TILE, D_MODEL), lambda u, gid, src, nu: (u, 0)),
            scratch_shapes=[
                pltpu.VMEM((2, MOE_TILE, SLAB_WIDTH), BF16),
                pltpu.SemaphoreType.DMA((2,)),
                pltpu.VMEM((EXPERTS_PER_GROUP, D_MODEL, D_EXPERT), BF16),
                pltpu.VMEM((EXPERTS_PER_GROUP, D_MODEL, D_EXPERT), BF16),
                pltpu.VMEM((GROUP_HIDDEN, D_MODEL), BF16),
            ],
        ),
        out_shape=jax.ShapeDtypeStruct((n_tiles * MOE_TILE, D_MODEL), F32),
        compiler_params=_cparams("arbitrary"),
        name="moe",
    )(gid, src_row, n_used, slab, params["expand"], params["w1"], params["w3"], params["w2"])


def _unsorted_moe(row_ref, dest_ref, y_hbm, buf, sem):
    slot = _gather_step(y_hbm, row_ref, SLAB_CHUNKS, buf, sem)
    y = buf[slot]
    hi = y.astype(BF16)
    lo = (y - hi.astype(F32)).astype(BF16)
    perm = _permutation(dest_ref[0])
    tn = (((0,), (0,)), ((), ()))
    return (lax.dot_general(perm, hi, tn, preferred_element_type=F32)
            + lax.dot_general(perm, lo, tn, preferred_element_type=F32))


def _unsort_kernel(row_ref, xm_ref, dest_ref, mod_ref, y_hbm, o_ref, buf, sem):
    moe = _unsorted_moe(row_ref, dest_ref, y_hbm, buf, sem)
    o_ref[...] = xm_ref[...] + mod_ref[0, 0][5:6] * moe


def _unsort(pending, layer, mod, row_of_tile):
    xm, dest, tables, y = pending
    t_tokens = xm.shape[0]
    out_row = tables[2]
    return pl.pallas_call(
        _unsort_kernel,
        grid_spec=pltpu.PrefetchScalarGridSpec(
            num_scalar_prefetch=1,
            grid=(t_tokens // SRC_TILE,),
            in_specs=[
                pl.BlockSpec((SRC_TILE, D_MODEL), lambda t, rows: (t, 0)),
                pl.BlockSpec((1, 1, SRC_TILE), lambda t, rows: (t, 0, 0)),
                pl.BlockSpec((1, 1, 6, D_MODEL), lambda t, rows: (layer, row_of_tile(t), 0, 0)),
                pl.BlockSpec(memory_space=pl.ANY),
            ],
            out_specs=pl.BlockSpec((SRC_TILE, D_MODEL), lambda t, rows: (t, 0)),
            scratch_shapes=[pltpu.VMEM((2, SLAB_ROWS, D_MODEL), F32), pltpu.SemaphoreType.DMA((2,))],
        ),
        out_shape=jax.ShapeDtypeStruct((t_tokens, D_MODEL), F32),
        compiler_params=_cparams("arbitrary"),
        name="unsort",
    )(out_row, xm, dest, mod, y)


def kernel(x_prompt, x_sample, cache_k, cache_v, c, c_ctx, w_ada, b_ada, norm1_w, norm2_w, w_in,
           q_norm_w, k_norm_w, rpb, w_sgu, b_sgu, sgu_ln_w, sgu_ln_b, w_out, w_router, b_router,
           w1, w3, w2):
    batch, seq, _ = x_prompt.shape
    dec_batch, dec_seq, _ = x_sample.shape
    past = cache_k.shape[2]
    assert 1 + dec_batch <= COND_ROWS and dec_seq % (Q_ROWS * GRID_W) == 0 and seq % CHUNK == 0

    cond = jnp.zeros((COND_ROWS, D_MODEL), F32).at[0].set(c_ctx).at[1:1 + dec_batch].set(c)
    mod = _adaln(cond, w_ada, b_ada).reshape(DEPTH, COND_ROWS, 6, D_MODEL)

    xp = x_prompt.reshape(batch * seq, D_MODEL)
    xs = x_sample.reshape(dec_batch * dec_seq, D_MODEL)
    from_cache_layout = lambda a: a.transpose(0, 1, 3, 4, 2).reshape(dec_batch, DEPTH, ATTN_WIDTH, past)
    ck = from_cache_layout(cache_k)
    cv = from_cache_layout(cache_v)

    head_of = jnp.arange(256) // HEAD_DIM
    bd = (head_of[:, None] == head_of[None, :]).astype(BF16)
    col_expert = jnp.arange(GROUP_HIDDEN) // D_EXPERT
    gate_lane = jnp.arange(LANES)
    expand = ((gate_lane[:, None] % EXPERTS_PER_GROUP == col_expert[None, :])
              & (gate_lane[:, None] < 2 * EXPERTS_PER_GROUP)).astype(BF16)
    tok_id = jnp.arange(SRC_TILE)
    tri = (tok_id[:, None] <= tok_id[None, :]).astype(BF16)

    params = dict(
        n1w=norm1_w.reshape(DEPTH, 1, D_MODEL),
        n2w=norm2_w.reshape(DEPTH, 1, D_MODEL),
        win=w_in.astype(BF16),
        wout=w_out.astype(BF16),
        qw=jnp.tile(q_norm_w, (1, N_HEADS)).reshape(DEPTH, 1, ATTN_WIDTH),
        kw=jnp.tile(k_norm_w, (1, N_HEADS)).reshape(DEPTH, 1, ATTN_WIDTH),
        bd=jnp.concatenate([bd, bd], axis=0),
        ws=w_sgu.astype(BF16),
        bs=jnp.broadcast_to(b_sgu[..., None], (DEPTH, SGU_GROUPS, CHUNK, SGU_GROUP_DIM)),
        lnw=sgu_ln_w,
        lnb=sgu_ln_b,
        wr=w_router.T.astype(BF16),
        br=b_router.reshape(N_EXPERTS, 1),
        expand=expand,
        tri=tri,
        w1=w1,
        w3=w3,
        w2=w2.reshape(DEPTH, N_EXPERT_GROUPS, GROUP_HIDDEN, D_MODEL),
    )
    tb = _bias_tables(rpb)

    lat_tiles_per_seq = dec_seq // SRC_TILE
    ctx_row = lambda i: 0
    lat_row = lambda i: 1 + i // lat_tiles_per_seq

    def moe_block(xm, slab, dest, meta, layer):
        n_src = meta.shape[0]
        tables = _chunk_tables(meta[:, :N_EXPERT_GROUPS, 0], _max_moe_tiles(n_src))
        return (xm, dest, tables, _moe(slab, tables, layer, params))

    kbuf = jnp.zeros((batch, DEPTH, ATTN_WIDTH, seq), F32)
    vbuf = jnp.zeros((batch, DEPTH, ATTN_WIDTH, seq), F32)

    for l in range(DEPTH):
        outs = _inproj(xp, l, mod, ctx_row, params, tm=seq, kv_buffers=(kbuf, vbuf))
        if l > 0:
            xp, outs = outs[0], outs[1:]
        q, kbuf, vbuf, v, sg = outs
        xp = moe_block(*_ctx_attention(q, kbuf, v, sg, xp, l, mod, params, seq=seq), l)

        outs = _inproj(xs, l, mod, lat_row, params, tm=SRC_TILE)
        if l > 0:
            xs, outs = outs[0], outs[1:]
        q, k, v, sg = outs
        xs = moe_block(*_nbr_attention(q, k, v, ck, cv, tb, sg, xs, l, mod, params,
                                       batch=dec_batch, seq=dec_seq), l)
    xp = _unsort(xp, DEPTH - 1, mod, ctx_row)
    xs = _unsort(xs, DEPTH - 1, mod, lat_row)

    to_cache_layout = lambda buf: buf.reshape(batch, DEPTH, N_HEADS, HEAD_DIM, seq).transpose(0, 1, 4, 2, 3)
    return (xp.reshape(batch, seq, D_MODEL), xs.reshape(dec_batch, dec_seq, D_MODEL),
            to_cache_layout(kbuf), to_cache_layout(vbuf))
```

```python
import functools

import jax
import jax.numpy as jnp
from jax import lax
from jax.experimental import pallas as pl
from jax.experimental.pallas import tpu as pltpu

F32 = jnp.float32
BF16 = jnp.bfloat16

D_MODEL = 1024
DEPTH = 4
N_HEADS = 8
HEAD_DIM = 64
ATTN_WIDTH = N_HEADS * HEAD_DIM
SGU_GROUPS = 4
SGU_GROUP_DIM = 128
SGU_WIDTH = SGU_GROUPS * SGU_GROUP_DIM
CHUNK = 128
IN_WIDTH = 3 * ATTN_WIDTH + 2 * SGU_WIDTH
GRID_W = 64
WIN_H = 8
WIN_W = 16
N_EXPERTS = 16
N_EXPERT_GROUPS = 4
EXPERTS_PER_GROUP = 4
D_EXPERT = 256
GROUP_HIDDEN = EXPERTS_PER_GROUP * D_EXPERT
EPS = 1e-6
NEG = -1e30
LOG2E = 1.4426950408889634

LANES = 128
HEADS_PER_VREG = LANES // HEAD_DIM
N_HEAD_PAIRS = N_HEADS // HEADS_PER_VREG
COND_ROWS = 16
Q_ROWS = 4
WIN_ROWS = Q_ROWS + WIN_H
VMEM_LIMIT = 48 * 1024 * 1024

SRC_TILE = 256
CHUNK_ROWS = 16
SLAB_ROWS = SRC_TILE + N_EXPERT_GROUPS * CHUNK_ROWS
SLAB_CHUNKS = SLAB_ROWS // CHUNK_ROWS
SLAB_WIDTH = D_MODEL + LANES
MOE_TILE = 256
TILE_CHUNKS = MOE_TILE // CHUNK_ROWS


def _cparams(*sem):
    return pltpu.CompilerParams(dimension_semantics=sem, vmem_limit_bytes=VMEM_LIMIT)


def _adaln_kernel(cond_ref, w_ref, b_ref, o_ref):
    c = cond_ref[...]
    a = (c * jax.nn.sigmoid(c)).astype(BF16)
    o_ref[0] = jnp.dot(a, w_ref[0].astype(BF16), preferred_element_type=F32) + b_ref[0]


def _adaln(cond, w_ada, b_ada):
    tn = 1536
    n = 6 * D_MODEL
    return pl.pallas_call(
        _adaln_kernel,
        grid=(DEPTH, n // tn),
        in_specs=[
            pl.BlockSpec((COND_ROWS, D_MODEL), lambda l, j: (0, 0)),
            pl.BlockSpec((1, D_MODEL, tn), lambda l, j: (l, 0, j)),
            pl.BlockSpec((1, 1, tn), lambda l, j: (l, 0, j)),
        ],
        out_specs=pl.BlockSpec((1, COND_ROWS, tn), lambda l, j: (l, 0, j)),
        out_shape=jax.ShapeDtypeStruct((DEPTH, COND_ROWS, n), F32),
        compiler_params=_cparams("arbitrary", "arbitrary"),
        name="adaln",
    )(cond, w_ada, b_ada.reshape(DEPTH, 1, n))


def _gelu_tanh(x):
    return 0.5 * x * (1.0 + jnp.tanh(0.7978845608028654 * (x + 0.044715 * (x * x * x))))


def _head_rms(t, bd, w):
    t2 = t * t
    hi = t2.astype(BF16)
    lo = (t2 - hi.astype(F32)).astype(BF16)
    outs = []
    for c in range(ATTN_WIDTH // 256):
        sl = slice(256 * c, 256 * c + 256)
        hl = jnp.concatenate([hi[:, sl], lo[:, sl]], axis=1)
        ss = jnp.dot(hl, bd, preferred_element_type=F32)
        outs.append(t[:, sl] * lax.rsqrt(ss * (1.0 / HEAD_DIM) + EPS))
    return jnp.concatenate(outs, axis=1) * w


def _inproj_kernel(*refs, tm, transposed_kv, fused_moe):
    refs = list(refs)
    if fused_moe:
        row_ref, xm_ref, dest_ref, prev_mod_ref, y_hbm = refs[:5]
        refs = refs[5:]
    else:
        x_ref = refs.pop(0)
    (mod_ref, n1w_ref, win_ref, qw_ref, kw_ref, bd_ref, ws_ref, bs_ref, lnw_ref, lnb_ref) = refs[:10]
    refs = refs[10:]
    if transposed_kv:
        refs = refs[2:]
    if fused_moe:
        x_out_ref = refs.pop(0)
    if transposed_kv:
        q_ref, kt_ref, vt_ref, v_ref, sg_ref = refs[:5]
        refs = refs[5:]
    else:
        q_ref, k_ref, v_ref, sg_ref = refs[:4]
        refs = refs[4:]
    if fused_moe:
        buf, sem = refs
        moe = _unsorted_moe(row_ref, dest_ref, y_hbm, buf, sem)
        x = xm_ref[...] + prev_mod_ref[0, 0][5:6] * moe
        x_out_ref[...] = x
    else:
        x = x_ref[...]
    m = mod_ref[0, 0]
    sh1, sc1 = m[0:1], m[1:2]
    ms = jnp.mean(x * x, axis=-1, keepdims=True)
    h = x * lax.rsqrt(ms + EPS) * n1w_ref[0]
    h = (h * (1.0 + sc1) + sh1).astype(BF16)
    p = jnp.dot(h, win_ref[0], preferred_element_type=F32)

    bd = bd_ref[...]
    q = _head_rms(p[:, 0:ATTN_WIDTH], bd, qw_ref[0])
    k = _head_rms(p[:, ATTN_WIDTH:2 * ATTN_WIDTH], bd, kw_ref[0])
    v = p[:, 2 * ATTN_WIDTH:3 * ATTN_WIDTH]
    q_ref[...] = (q * (HEAD_DIM ** -0.5 * LOG2E)).astype(q_ref.dtype)
    if transposed_kv:
        kt_ref[0, 0] = k.T
        vt_ref[0, 0] = v.T
    else:
        k_ref[...] = k.astype(k_ref.dtype)
    v_ref[...] = v.astype(v_ref.dtype)

    off_u = 3 * ATTN_WIDTH
    off_v = off_u + SGU_WIDTH
    for g in range(SGU_GROUPS):
        gl = slice(g * SGU_GROUP_DIM, (g + 1) * SGU_GROUP_DIM)
        u = _gelu_tanh(p[:, off_u + g * SGU_GROUP_DIM: off_u + (g + 1) * SGU_GROUP_DIM])
        t = _gelu_tanh(p[:, off_v + g * SGU_GROUP_DIM: off_v + (g + 1) * SGU_GROUP_DIM])
        mu = jnp.mean(t, axis=-1, keepdims=True)
        d = t - mu
        var = jnp.mean(d * d, axis=-1, keepdims=True)
        y = (d * lax.rsqrt(var + EPS) * lnw_ref[0, g:g + 1, :] + lnb_ref[0, g:g + 1, :]).astype(BF16)
        for c in range(tm // CHUNK):
            rows = slice(c * CHUNK, (c + 1) * CHUNK)
            sv = jnp.dot(ws_ref[0, g], y[rows], preferred_element_type=F32) + bs_ref[0, g]
            sg_ref[rows, gl] = (u[rows] * sv).astype(sg_ref.dtype)


def _inproj(x, layer, mod, row_of_tile, params, *, tm, kv_buffers=None):
    fused_moe = isinstance(x, tuple)
    tok = lambda i, *_: (i, 0)
    lyr3 = lambda i, *_: (layer, 0, 0)
    lyr4 = lambda i, *_: (layer, 0, 0, 0)
    mod_row = lambda lyr: pl.BlockSpec((1, 1, 6, D_MODEL), lambda i, *_: (lyr, row_of_tile(i), 0, 0))
    x_spec = pl.BlockSpec((tm, D_MODEL), tok)
    if fused_moe:
        assert tm == SRC_TILE
        xm, dest, out_row, y = x
        t_tokens = xm.shape[0]
        prefetch = [out_row]
        in_specs = [x_spec, pl.BlockSpec((1, 1, SRC_TILE), lambda i, *_: (i, 0, 0)),
                    mod_row(layer - 1), pl.BlockSpec(memory_space=pl.ANY)]
        args = [xm, dest, mod, y]
        scratch = [pltpu.VMEM((2, SLAB_ROWS, D_MODEL), F32), pltpu.SemaphoreType.DMA((2,))]
    else:
        t_tokens = x.shape[0]
        prefetch, in_specs, args, scratch = [], [x_spec], [x], []
    act = pl.BlockSpec((tm, ATTN_WIDTH), tok)
    act_sds = jax.ShapeDtypeStruct((t_tokens, ATTN_WIDTH), BF16)
    in_specs += [
        mod_row(layer),
        pl.BlockSpec((1, 1, D_MODEL), lyr3),
        pl.BlockSpec((1, D_MODEL, IN_WIDTH), lyr3),
        pl.BlockSpec((1, 1, ATTN_WIDTH), lyr3),
        pl.BlockSpec((1, 1, ATTN_WIDTH), lyr3),
        pl.BlockSpec((512, 256), lambda i, *_: (0, 0)),
        pl.BlockSpec((1, SGU_GROUPS, CHUNK, CHUNK), lyr4),
        pl.BlockSpec((1, SGU_GROUPS, CHUNK, SGU_GROUP_DIM), lyr4),
        pl.BlockSpec((1, SGU_GROUPS, SGU_GROUP_DIM), lyr3),
        pl.BlockSpec((1, SGU_GROUPS, SGU_GROUP_DIM), lyr3),
    ]
    args += [mod, params["n1w"], params["win"], params["qw"], params["kw"], params["bd"],
             params["ws"], params["bs"], params["lnw"], params["lnb"]]
    out_specs = [x_spec] if fused_moe else []
    out_shape = [jax.ShapeDtypeStruct((t_tokens, D_MODEL), F32)] if fused_moe else []
    if kv_buffers is None:
        out_specs += [act] * 4
        out_shape += [act_sds] * 4
        aliases = {}
    else:
        kbuf, vbuf = kv_buffers
        kv_spec = pl.BlockSpec((1, 1, ATTN_WIDTH, tm), lambda i, *_: (i, layer, 0, 0))
        first_alias_in = len(prefetch) + len(args)
        first_alias_out = len(out_specs) + 1
        in_specs += [pl.BlockSpec(memory_space=pl.ANY)] * 2
        args += [kbuf, vbuf]
        out_specs += [act, kv_spec, kv_spec, act, act]
        out_shape += [act_sds, jax.ShapeDtypeStruct(kbuf.shape, F32),
                      jax.ShapeDtypeStruct(vbuf.shape, F32), act_sds, act_sds]
        aliases = {first_alias_in: first_alias_out, first_alias_in + 1: first_alias_out + 1}
    return pl.pallas_call(
        functools.partial(_inproj_kernel, tm=tm, transposed_kv=kv_buffers is not None,
                          fused_moe=fused_moe),
        grid_spec=pltpu.PrefetchScalarGridSpec(
            num_scalar_prefetch=len(prefetch),
            grid=(t_tokens // tm,),
            in_specs=in_specs,
            out_specs=out_specs,
            scratch_shapes=scratch,
        ),
        out_shape=out_shape,
        input_output_aliases=aliases,
        compiler_params=_cparams("arbitrary"),
        name="inproj",
    )(*prefetch, *args)


def _route(lg_t, br):
    s = jax.nn.sigmoid(lg_t)
    sel = s + br
    row = lambda a, i: a[i:i + 1]
    g_score = []
    for g in range(N_EXPERT_GROUPS):
        v = [row(sel, 4 * g + i) for i in range(4)]
        best_pair = None
        for i in range(4):
            for j in range(i + 1, 4):
                pair = v[i] + v[j]
                best_pair = pair if best_pair is None else jnp.maximum(best_pair, pair)
        g_score.append(best_pair)
    best = jnp.zeros_like(g_score[0], dtype=jnp.int32)
    top = g_score[0]
    for g in range(1, N_EXPERT_GROUPS):
        upd = g_score[g] > top
        best = jnp.where(upd, g, best)
        top = jnp.where(upd, g_score[g], top)
    cand, aff = [], []
    for i in range(4):
        ci, si = row(sel, i), row(s, i)
        for g in range(1, N_EXPERT_GROUPS):
            ci = jnp.where(best == g, row(sel, 4 * g + i), ci)
            si = jnp.where(best == g, row(s, 4 * g + i), si)
        cand.append(ci)
        aff.append(si)

    def first_argmax(vals):
        idx = jnp.zeros_like(best)
        top_v = vals[0]
        for i in range(1, 4):
            upd = vals[i] > top_v
            idx = jnp.where(upd, i, idx)
            top_v = jnp.where(upd, vals[i], top_v)
        return idx

    i1 = first_argmax(cand)
    i2 = first_argmax([jnp.where(i1 == i, -jnp.inf, cand[i]) for i in range(4)])
    pick = lambda idx: sum(jnp.where(idx == i, aff[i], 0.0) for i in range(4))
    den = pick(i1) + pick(i2)
    gate = [jnp.where((i1 == i) | (i2 == i), aff[i] / den, 0.0) for i in range(4)]
    hi = [x.astype(BF16) for x in gate]
    lo = [(x - h.astype(F32)).astype(BF16) for x, h in zip(gate, hi)]
    return best, hi + lo


def _slab_positions(best, tri):
    n = best.shape[1]
    onehot = [jnp.where(best == g, 1.0, 0.0) for g in range(N_EXPERT_GROUPS)]
    pad = [jnp.zeros_like(onehot[0])] * (8 - N_EXPERT_GROUPS)
    oh = jnp.concatenate(onehot + pad, axis=0).astype(BF16)
    counts = jnp.dot(oh, tri, preferred_element_type=F32)
    dest = jnp.zeros((1, n), F32)
    seg_start = jnp.zeros((1, 1), F32)
    chunks = []
    for g in range(N_EXPERT_GROUPS):
        cg = counts[g:g + 1]
        n_g = jnp.max(cg, axis=1, keepdims=True)
        c_g = jnp.floor((n_g + (CHUNK_ROWS - 1)) * (1.0 / CHUNK_ROWS))
        dest = dest + onehot[g] * (seg_start + cg - 1.0)
        seg_start = seg_start + c_g * CHUNK_ROWS
        chunks.append(c_g)
    return dest.astype(jnp.int32), chunks


def _permutation(dest):
    n = dest.shape[1]
    hit = lax.broadcasted_iota(jnp.int32, (SLAB_ROWS, n), 0) == dest
    return jnp.where(hit, 1.0, 0.0).astype(BF16)


def _post_attention(a, sg, x, m, wout_ref, n2w_ref, xm_ref):
    g1, sh2, sc2 = m[2:3], m[3:4], m[4:5]
    cat = jnp.concatenate([a, sg], axis=1)
    y = jnp.dot(cat, wout_ref[0], preferred_element_type=F32)
    xm = x + g1 * y
    xm_ref[...] = xm
    ms = jnp.mean(xm * xm, axis=-1, keepdims=True)
    h2 = xm * lax.rsqrt(ms + EPS) * n2w_ref[0]
    return (h2 * (1.0 + sc2) + sh2).astype(BF16)


def _route_previous(h2_scr, wr_ref, br_ref, tri_ref, slab_ref, dest_ref, meta_ref):
    h2 = h2_scr[...]
    lg_t = lax.dot_general(wr_ref[...], h2, (((1,), (1,)), ((), ())), preferred_element_type=F32)
    yield
    best, gates = _route(lg_t, br_ref[...])
    dest, chunks = _slab_positions(best, tri_ref[...])
    yield
    perm = _permutation(dest)
    slab_ref[:, 0:D_MODEL] = jnp.dot(perm, h2, preferred_element_type=F32).astype(BF16)
    gmat = jnp.concatenate(gates + [jnp.zeros((LANES - len(gates), dest.shape[1]), BF16)], axis=0)
    slab_ref[:, D_MODEL:] = lax.dot_general(perm, gmat, (((1,), (1,)), ((), ())),
                                            preferred_element_type=F32).astype(BF16)
    dest_ref[0] = dest
    meta = [jnp.broadcast_to(c, (1, LANES)) for c in chunks]
    meta += [jnp.zeros((8 - len(chunks), LANES), F32)]
    meta_ref[0] = jnp.concatenate(meta, axis=0).astype(jnp.int32)


def _pair_pos(axis):
    return lax.broadcasted_iota(jnp.int32, (1, LANES) if axis == 1 else (LANES, 1), axis)


def _head_mask(hh, axis):
    pos = _pair_pos(axis)
    return (pos >= hh * HEAD_DIM) & (pos < (hh + 1) * HEAD_DIM)


def _only_head(x2, hh, axis):
    return jnp.where(_head_mask(hh, axis), x2, jnp.zeros_like(x2))


def _attend_heads(logits, finish, router):
    outs, even = [], None
    s_next = logits(0)
    for h in range(N_HEADS):
        s = s_next
        if h + 1 < N_HEADS:
            s_next = logits(h + 1)
        if h == N_HEADS // 2:
            next(router)
        o = finish(h, s)
        if h % HEADS_PER_VREG == 0:
            even = o
        else:
            outs.append((even + o).astype(BF16))
    for _ in router:
        pass
    return jnp.concatenate(outs, axis=1)


def _ctx_attn_kernel(q_ref, kt_ref, v_ref, sg_ref, x_ref, mod_ref, wout_ref, n2w_ref, wr_ref, br_ref,
                     tri_ref, xm_ref, slab_ref, dest_ref, meta_ref, h2_scr):
    step = pl.program_id(0)

    @pl.when(step == 0)
    def _():
        h2_scr[...] = jnp.zeros_like(h2_scr)

    router = _route_previous(h2_scr, wr_ref, br_ref, tri_ref, slab_ref, dest_ref, meta_ref)
    next(router)

    def logits(h):
        cols = slice(h // HEADS_PER_VREG * LANES, (h // HEADS_PER_VREG + 1) * LANES)
        kt2 = kt_ref[0, 0, cols, :].astype(BF16)
        return jnp.dot(q_ref[:, cols], _only_head(kt2, h % HEADS_PER_VREG, 0),
                       preferred_element_type=F32)

    def finish(h, s):
        cols = slice(h // HEADS_PER_VREG * LANES, (h // HEADS_PER_VREG + 1) * LANES)
        e = jnp.exp2(s - jnp.max(s, axis=-1, keepdims=True))
        l = jnp.sum(e, axis=-1, keepdims=True)
        vm = _only_head(v_ref[:, cols], h % HEADS_PER_VREG, 1)
        return jnp.dot(e.astype(BF16), vm, preferred_element_type=F32) * (1.0 / l)

    a = _attend_heads(logits, finish, router)
    h2_scr[...] = _post_attention(a, sg_ref[...], x_ref[...], mod_ref[0, 0], wout_ref, n2w_ref, xm_ref)


def _post_attention_out(t_tokens):
    n_src = t_tokens // SRC_TILE
    prev = lambda s: jnp.maximum(s - 1, 0)
    specs = [
        pl.BlockSpec((SRC_TILE, D_MODEL), lambda s: (jnp.minimum(s, n_src - 1), 0)),
        pl.BlockSpec((SLAB_ROWS, SLAB_WIDTH), lambda s: (prev(s), 0)),
        pl.BlockSpec((1, 1, SRC_TILE), lambda s: (prev(s), 0, 0)),
        pl.BlockSpec((1, 8, LANES), lambda s: (prev(s), 0, 0)),
    ]
    shapes = [
        jax.ShapeDtypeStruct((t_tokens, D_MODEL), F32),
        jax.ShapeDtypeStruct((n_src * SLAB_ROWS, SLAB_WIDTH), BF16),
        jax.ShapeDtypeStruct((n_src, 1, SRC_TILE), jnp.int32),
        jax.ShapeDtypeStruct((n_src, 8, LANES), jnp.int32),
    ]
    return specs, shapes, pltpu.VMEM((SRC_TILE, D_MODEL), BF16)


def _ctx_attention(q, kbuf, v, sg, x, layer, mod, params, *, seq):
    assert seq == SRC_TILE
    t_tokens = x.shape[0]
    n_seq = t_tokens // seq
    seq_of = lambda s: jnp.minimum(s, n_seq - 1)
    tok = lambda s: (seq_of(s), 0)
    lyr3 = lambda s: (layer, 0, 0)
    full2 = lambda s: (0, 0)
    out_specs, out_shape, h2_scratch = _post_attention_out(t_tokens)
    return pl.pallas_call(
        _ctx_attn_kernel,
        grid=(n_seq + 1,),
        in_specs=[
            pl.BlockSpec((seq, ATTN_WIDTH), tok),
            pl.BlockSpec((1, 1, ATTN_WIDTH, seq), lambda s: (seq_of(s), layer, 0, 0)),
            pl.BlockSpec((seq, ATTN_WIDTH), tok),
            pl.BlockSpec((seq, SGU_WIDTH), tok),
            pl.BlockSpec((seq, D_MODEL), tok),
            pl.BlockSpec((1, 1, 6, D_MODEL), lambda s: (layer, 0, 0, 0)),
            pl.BlockSpec((1, D_MODEL, D_MODEL), lyr3),
            pl.BlockSpec((1, 1, D_MODEL), lyr3),
            pl.BlockSpec((N_EXPERTS, D_MODEL), full2),
            pl.BlockSpec((N_EXPERTS, 1), full2),
            pl.BlockSpec((SRC_TILE, SRC_TILE), full2),
        ],
        out_specs=out_specs,
        out_shape=out_shape,
        scratch_shapes=[h2_scratch],
        compiler_params=_cparams("arbitrary"),
        name="ctx_attn",
    )(q, kbuf, v, sg, x, mod, params["wout"], params["n2w"], params["wr"], params["br"], params["tri"])


def _row_start(r, rows):
    return jnp.clip(r - WIN_H // 2, 0, rows - WIN_H)


def _nbr_attn_kernel(q_ref, k_ref, v_ref, ck_ref, cv_ref, tb_ref, sg_ref, x_ref, mod_ref, wout_ref,
                     n2w_ref, wr_ref, br_ref, tri_ref, xm_ref, slab_ref, dest_ref, meta_ref,
                     kwm_ref, vwm_ref, kcm_ref, vcm_ref, h2_scr, *, rows, n_tiles):
    step = pl.program_id(0)
    tiles_per_seq = rows // Q_ROWS
    t = jnp.minimum(step, n_tiles - 1) % tiles_per_seq

    @pl.when(step == 0)
    def _():
        h2_scr[...] = jnp.zeros_like(h2_scr)

    @pl.when((t == 0) & (step < n_tiles))
    def _():
        for j in range(N_HEAD_PAIRS):
            cols = slice(j * LANES, (j + 1) * LANES)
            k2, v2 = k_ref[:, cols], v_ref[:, cols]
            ck2 = ck_ref[0, 0, cols, :].astype(BF16)
            cv2 = cv_ref[0, 0, cols, :].astype(BF16)
            for hh in range(HEADS_PER_VREG):
                h = HEADS_PER_VREG * j + hh
                kwm_ref[h] = _only_head(k2, hh, 1)
                vwm_ref[h] = _only_head(v2, hh, 1)
                kcm_ref[h] = _only_head(ck2, hh, 0)
                vcm_ref[h] = _only_head(cv2, hh, 0)

    router = _route_previous(h2_scr, wr_ref, br_ref, tri_ref, slab_ref, dest_ref, meta_ref)
    next(router)

    r0 = t * Q_ROWS
    ws = jnp.minimum(_row_start(r0, rows), rows - WIN_ROWS)
    tok0 = pl.multiple_of(ws * GRID_W, GRID_W)

    lane = lax.broadcasted_iota(jnp.int32, (1, LANES), 1)
    blk_idx, row_mask = [], []
    for a in range(Q_ROWS):
        r = r0 + a
        rs = _row_start(r, rows)
        idx_a, mask_a = [], []
        for jp in range(WIN_ROWS // 2):
            kr = ws + 2 * jp
            idx_a.append(jnp.clip(kr - r + WIN_H, 0, 2 * WIN_H - 1))
            ok0 = (kr >= rs) & (kr < rs + WIN_H)
            ok1 = (kr + 1 >= rs) & (kr + 1 < rs + WIN_H)
            m0 = jnp.where(ok0, 0.0, NEG).astype(F32)
            m1 = jnp.where(ok1, 0.0, NEG).astype(F32)
            mask_a.append(jnp.where(lane < GRID_W, m0, m1))
        blk_idx.append(idx_a)
        row_mask.append(mask_a)

    nt = (((1,), (1,)), ((), ()))
    window = pl.ds(tok0, WIN_ROWS * GRID_W)

    def logits(h):
        q2 = q_ref[:, h // HEADS_PER_VREG * LANES:(h // HEADS_PER_VREG + 1) * LANES]
        bias = jnp.concatenate([
            jnp.concatenate([tb_ref[0, h, blk_idx[a][jp]] + row_mask[a][jp]
                             for jp in range(WIN_ROWS // 2)], axis=1)
            for a in range(Q_ROWS)], axis=0)
        s_w = lax.dot_general(q2, kwm_ref[h, window, :], nt, preferred_element_type=F32) + bias
        s_c = jnp.dot(q2, kcm_ref[h], preferred_element_type=F32)
        return s_w, s_c

    def finish(h, s):
        s_w, s_c = s
        mx = jnp.maximum(jnp.max(s_w, axis=-1, keepdims=True), jnp.max(s_c, axis=-1, keepdims=True))
        e_w = jnp.exp2(s_w - mx)
        e_c = jnp.exp2(s_c - mx)
        l = jnp.sum(e_w, axis=-1, keepdims=True) + jnp.sum(e_c, axis=-1, keepdims=True)
        return (jnp.dot(e_w.astype(BF16), vwm_ref[h, window, :], preferred_element_type=F32)
                + lax.dot_general(e_c.astype(BF16), vcm_ref[h], nt,
                                  preferred_element_type=F32)) * (1.0 / l)

    a_out = _attend_heads(logits, finish, router)
    h2_scr[...] = _post_attention(a_out, sg_ref[...], x_ref[...], mod_ref[0, 0], wout_ref, n2w_ref,
                                  xm_ref)


def _nbr_attention(q, k, v, cache_k, cache_v, tb, sg, x, layer, mod, params, *, batch, seq):
    t_tokens = x.shape[0]
    rows = seq // GRID_W
    tq = Q_ROWS * GRID_W
    assert tq == SRC_TILE
    nt = seq // tq
    past = cache_k.shape[3]
    n_tiles = batch * nt
    out_specs, out_shape, h2_scratch = _post_attention_out(t_tokens)
    tile_of = lambda s: jnp.minimum(s, n_tiles - 1)
    seq_of = lambda s: tile_of(s) // nt
    tok = lambda s: (tile_of(s), 0)
    per_b = lambda s: (seq_of(s), 0)
    lyr3 = lambda s: (layer, 0, 0)
    full2 = lambda s: (0, 0)
    cache_spec = pl.BlockSpec((1, 1, ATTN_WIDTH, past), lambda s: (seq_of(s), layer, 0, 0))
    masked = lambda n: pltpu.VMEM((N_HEADS, n, LANES), BF16)
    masked_t = pltpu.VMEM((N_HEADS, LANES, past), BF16)
    return pl.pallas_call(
        functools.partial(_nbr_attn_kernel, rows=rows, n_tiles=n_tiles),
        grid=(n_tiles + 1,),
        in_specs=[
            pl.BlockSpec((tq, ATTN_WIDTH), tok),
            pl.BlockSpec((seq, ATTN_WIDTH), per_b),
            pl.BlockSpec((seq, ATTN_WIDTH), per_b),
            cache_spec,
            cache_spec,
            pl.BlockSpec((1, N_HEADS, 2 * WIN_H, GRID_W, LANES), lambda s: (layer, 0, 0, 0, 0)),
            pl.BlockSpec((tq, SGU_WIDTH), tok),
            pl.BlockSpec((tq, D_MODEL), tok),
            pl.BlockSpec((1, 1, 6, D_MODEL), lambda s: (layer, 1 + seq_of(s), 0, 0)),
            pl.BlockSpec((1, D_MODEL, D_MODEL), lyr3),
            pl.BlockSpec((1, 1, D_MODEL), lyr3),
            pl.BlockSpec((N_EXPERTS, D_MODEL), full2),
            pl.BlockSpec((N_EXPERTS, 1), full2),
            pl.BlockSpec((SRC_TILE, SRC_TILE), full2),
        ],
        out_specs=out_specs,
        out_shape=out_shape,
        scratch_shapes=[masked(seq), masked(seq), masked_t, masked_t, h2_scratch],
        compiler_params=_cparams("arbitrary"),
        name="nbr_attn",
    )(q, k, v, cache_k, cache_v, tb, sg, x, mod, params["wout"], params["n2w"], params["wr"],
      params["br"], params["tri"])


def _bias_tables(rpb):
    cols = jnp.arange(GRID_W)
    col_start = jnp.clip(cols - WIN_W // 2, 0, GRID_W - WIN_W)
    col_ok = (cols[None, :] >= col_start[:, None]) & (cols[None, :] < col_start[:, None] + WIN_W)
    rel = cols[None, :] - cols[:, None] + WIN_W - 1
    pick = ((rel[None] == jnp.arange(2 * WIN_W - 1)[:, None, None]) & col_ok[None]).astype(F32)
    t = jnp.einsum("lhxm,mqk->lhxqk", rpb, pick, precision=lax.Precision.HIGHEST)
    t = jnp.where(col_ok, t * LOG2E, NEG)
    t = jnp.pad(t, ((0, 0), (0, 0), (1, 1), (0, 0), (0, 0)))
    return jnp.concatenate([t[:, :, :-1], t[:, :, 1:]], axis=-1)


def _max_moe_tiles(n_src):
    max_chunks = n_src * SRC_TILE // CHUNK_ROWS + n_src * N_EXPERT_GROUPS
    return -(-max_chunks // TILE_CHUNKS) + N_EXPERT_GROUPS


def _chunk_tables(chunks, n_tiles):
    n_src = chunks.shape[0]
    groups = jnp.arange(N_EXPERT_GROUPS)
    seg_end = jnp.cumsum(chunks, axis=1)
    seg_start = seg_end - chunks
    src_end = jnp.cumsum(chunks, axis=0)
    src_start = src_end - chunks
    total = src_end[-1]
    tiles = (total + TILE_CHUNKS - 1) // TILE_CHUNKS
    tile_end = jnp.cumsum(tiles)
    tile_start = tile_end - tiles
    n_used = tile_end[-1:]

    u = jnp.arange(n_tiles)
    gid = jnp.minimum(jnp.sum(u[:, None] >= tile_end[None, :], axis=1), N_EXPERT_GROUPS - 1)

    p = jnp.arange(n_tiles * TILE_CHUNKS)
    g_hot = jnp.repeat(gid, TILE_CHUNKS)[:, None] == groups[None, :]
    by_group = lambda vec: jnp.sum(jnp.where(g_hot, vec[None, :], 0), axis=1)
    q = p - TILE_CHUNKS * by_group(tile_start)
    live = q < by_group(total)
    src_end_p = jnp.sum(jnp.where(g_hot[:, None, :], src_end[None], 0), axis=2)
    t = jnp.minimum(jnp.sum(q[:, None] >= src_end_p, axis=1), n_src - 1)
    t_hot = t[:, None] == jnp.arange(n_src)[None, :]
    by_seg = lambda tab: jnp.sum(jnp.where(t_hot[:, :, None] & g_hot[:, None, :], tab[None], 0),
                                 axis=(1, 2))
    slab_chunk = by_seg(seg_start) + q - by_seg(src_start)
    src_row = jnp.where(live, t * SLAB_ROWS + CHUNK_ROWS * slab_chunk, 0)

    s = jnp.arange(SLAB_CHUNKS)
    gs = jnp.sum(s[None, :, None] >= seg_end[:, None, :], axis=2)
    used = gs < N_EXPERT_GROUPS
    s_hot = jnp.minimum(gs, N_EXPERT_GROUPS - 1)[:, :, None] == groups
    pick = lambda tab: jnp.sum(jnp.where(s_hot, tab, 0), axis=2)
    pos = (TILE_CHUNKS * pick(tile_start[None, None, :]) + pick(src_start[:, None, :])
           + s[None, :] - pick(seg_start[:, None, :]))
    out_row = jnp.where(used, CHUNK_ROWS * pos, 0).reshape(-1)
    i32 = lambda a: a.astype(jnp.int32)
    return i32(gid), i32(src_row), i32(out_row), i32(n_used)


def _chunk_copy(src_hbm, row, k, buf, sem, slot):
    return pltpu.make_async_copy(
        src_hbm.at[pl.ds(pl.multiple_of(row, CHUNK_ROWS), CHUNK_ROWS), :],
        buf.at[slot, pl.ds(k * CHUNK_ROWS, CHUNK_ROWS), :],
        sem.at[slot])


def _start_chunks(sources, row_ref, first, n_chunks, buf, sem, slot):
    bases = [sum(s.shape[0] for s in sources[:i]) for i in range(len(sources) + 1)]
    for k in range(n_chunks):
        row = row_ref[first + k]
        if len(sources) == 1:
            _chunk_copy(sources[0], row, k, buf, sem, slot).start()
        else:
            for src, lo, hi in zip(sources, bases[:-1], bases[1:]):
                @pl.when((row >= lo) & (row < hi))
                def _(src=src, lo=lo):
                    _chunk_copy(src, row - lo, k, buf, sem, slot).start()


def _gather_step(sources, row_ref, n_chunks, buf, sem):
    step = pl.program_id(0)
    slot = step % 2

    @pl.when(step == 0)
    def _():
        _start_chunks(sources, row_ref, 0, n_chunks, buf, sem, 0)

    @pl.when(step + 1 < pl.num_programs(0))
    def _():
        _start_chunks(sources, row_ref, (step + 1) * n_chunks, n_chunks, buf, sem, 1 - slot)

    for k in range(n_chunks):
        _chunk_copy(sources[0], 0, k, buf, sem, slot).wait()
    return slot


def _moe_kernel(gid_ref, src_ref, nused_ref, *refs, n_slabs):
    slabs = refs[:n_slabs]
    ex_ref, w1_ref, w3_ref, w2_ref, o_ref, buf, sem, w1b, w3b, w2b = refs[n_slabs:]
    u = pl.program_id(0)
    slot = _gather_step(slabs, src_ref, TILE_CHUNKS, buf, sem)

    @pl.when((u == 0) | (gid_ref[u] != gid_ref[jnp.maximum(u - 1, 0)]))
    def _():
        w1b[...] = w1_ref[0].astype(BF16)
        w3b[...] = w3_ref[0].astype(BF16)
        w2b[...] = w2_ref[0, 0].astype(BF16)

    @pl.when(u < nused_ref[0])
    def _():
        h = buf[slot, :, 0:D_MODEL]
        gexp = jnp.dot(buf[slot, :, D_MODEL:], ex_ref[...], preferred_element_type=F32)
        hid = []
        for e in range(EXPERTS_PER_GROUP):
            h1 = jnp.dot(h, w1b[e], preferred_element_type=F32)
            h3 = jnp.dot(h, w3b[e], preferred_element_type=F32)
            act = (h1 * jax.nn.sigmoid(h1)) * h3
            hid.append((act * gexp[:, e * D_EXPERT:(e + 1) * D_EXPERT]).astype(BF16))
        o_ref[...] = jnp.dot(jnp.concatenate(hid, axis=1), w2b[...], preferred_element_type=F32)

    @pl.when(u >= nused_ref[0])
    def _():
        o_ref[...] = jnp.zeros_like(o_ref)


def _moe(slabs, tables, layer, params):
    gid, src_row, _, n_used = tables
    n_tiles = gid.shape[0]
    expert_w = pl.BlockSpec((1, EXPERTS_PER_GROUP, D_MODEL, D_EXPERT),
                            lambda u, gid, src, nu: (layer, gid[u], 0, 0))
    return pl.pallas_call(
        functools.partial(_moe_kernel, n_slabs=len(slabs)),
        grid_spec=pltpu.PrefetchScalarGridSpec(
            num_scalar_prefetch=3,
            grid=(n_tiles,),
            in_specs=[pl.BlockSpec(memory_space=pl.ANY)] * len(slabs) + [
                pl.BlockSpec((LANES, GROUP_HIDDEN), lambda u, gid, src, nu: (0, 0)),
                expert_w,
                expert_w,
                pl.BlockSpec((1, 1, GROUP_HIDDEN, D_MODEL), lambda u, gid, src, nu: (layer, gid[u], 0, 0)),
            ],
            out_specs=pl.BlockSpec((MOE_TILE, D_MODEL), lambda u, gid, src, nu: (u, 0)),
            scratch_shapes=[
                pltpu.VMEM((2, MOE_TILE, SLAB_WIDTH), BF16),
                pltpu.SemaphoreType.DMA((2,)),
                pltpu.VMEM((EXPERTS_PER_GROUP, D_MODEL, D_EXPERT), BF16),
                pltpu.VMEM((EXPERTS_PER_GROUP, D_MODEL, D_EXPERT), BF16),
                pltpu.VMEM((GROUP_HIDDEN, D_MODEL), BF16),
            ],
        ),
        out_shape=jax.ShapeDtypeStruct((n_tiles * MOE_TILE, D_MODEL), F32),
        compiler_params=_cparams("arbitrary"),
        name="moe",
    )(gid, src_row, n_used, *slabs, params["expand"], params["w1"], params["w3"], params["w2"])


def _unsorted_moe(row_ref, dest_ref, y_hbm, buf, sem):
    slot = _gather_step([y_hbm], row_ref, SLAB_CHUNKS, buf, sem)
    y = buf[slot]
    hi = y.astype(BF16)
    lo = (y - hi.astype(F32)).astype(BF16)
    perm = _permutation(dest_ref[0])
    tn = (((0,), (0,)), ((), ()))
    return (lax.dot_general(perm, hi, tn, preferred_element_type=F32)
            + lax.dot_general(perm, lo, tn, preferred_element_type=F32))


def _unsort_kernel(row_ref, xm_ref, dest_ref, mod_ref, y_hbm, o_ref, buf, sem):
    moe = _unsorted_moe(row_ref, dest_ref, y_hbm, buf, sem)
    o_ref[...] = xm_ref[...] + mod_ref[0, 0][5:6] * moe


def _unsort(pending, layer, mod, row_of_tile):
    xm, dest, out_row, y = pending
    t_tokens = xm.shape[0]
    return pl.pallas_call(
        _unsort_kernel,
        grid_spec=pltpu.PrefetchScalarGridSpec(
            num_scalar_prefetch=1,
            grid=(t_tokens // SRC_TILE,),
            in_specs=[
                pl.BlockSpec((SRC_TILE, D_MODEL), lambda t, rows: (t, 0)),
                pl.BlockSpec((1, 1, SRC_TILE), lambda t, rows: (t, 0, 0)),
                pl.BlockSpec((1, 1, 6, D_MODEL), lambda t, rows: (layer, row_of_tile(t), 0, 0)),
                pl.BlockSpec(memory_space=pl.ANY),
            ],
            out_specs=pl.BlockSpec((SRC_TILE, D_MODEL), lambda t, rows: (t, 0)),
            scratch_shapes=[pltpu.VMEM((2, SLAB_ROWS, D_MODEL), F32), pltpu.SemaphoreType.DMA((2,))],
        ),
        out_shape=jax.ShapeDtypeStruct((t_tokens, D_MODEL), F32),
        compiler_params=_cparams("arbitrary"),
        name="unsort",
    )(out_row, xm, dest, mod, y)


def kernel(x_prompt, x_sample, cache_k, cache_v, c, c_ctx, w_ada, b_ada, norm1_w, norm2_w, w_in,
           q_norm_w, k_norm_w, rpb, w_sgu, b_sgu, sgu_ln_w, sgu_ln_b, w_out, w_router, b_router,
           w1, w3, w2):
    batch, seq, _ = x_prompt.shape
    dec_batch, dec_seq, _ = x_sample.shape
    past = cache_k.shape[2]
    assert 1 + dec_batch <= COND_ROWS and dec_seq % (Q_ROWS * GRID_W) == 0 and seq % CHUNK == 0

    cond = jnp.zeros((COND_ROWS, D_MODEL), F32).at[0].set(c_ctx).at[1:1 + dec_batch].set(c)
    mod = _adaln(cond, w_ada, b_ada).reshape(DEPTH, COND_ROWS, 6, D_MODEL)

    xp = x_prompt.reshape(batch * seq, D_MODEL)
    xs = x_sample.reshape(dec_batch * dec_seq, D_MODEL)
    from_cache_layout = lambda a: a.transpose(0, 1, 3, 4, 2).reshape(dec_batch, DEPTH, ATTN_WIDTH, past)
    ck = from_cache_layout(cache_k)
    cv = from_cache_layout(cache_v)

    head_of = jnp.arange(256) // HEAD_DIM
    bd = (head_of[:, None] == head_of[None, :]).astype(BF16)
    col_expert = jnp.arange(GROUP_HIDDEN) // D_EXPERT
    gate_lane = jnp.arange(LANES)
    expand = ((gate_lane[:, None] % EXPERTS_PER_GROUP == col_expert[None, :])
              & (gate_lane[:, None] < 2 * EXPERTS_PER_GROUP)).astype(BF16)
    tok_id = jnp.arange(SRC_TILE)
    tri = (tok_id[:, None] <= tok_id[None, :]).astype(BF16)

    params = dict(
        n1w=norm1_w.reshape(DEPTH, 1, D_MODEL),
        n2w=norm2_w.reshape(DEPTH, 1, D_MODEL),
        win=w_in.astype(BF16),
        wout=w_out.astype(BF16),
        qw=jnp.tile(q_norm_w, (1, N_HEADS)).reshape(DEPTH, 1, ATTN_WIDTH),
        kw=jnp.tile(k_norm_w, (1, N_HEADS)).reshape(DEPTH, 1, ATTN_WIDTH),
        bd=jnp.concatenate([bd, bd], axis=0),
        ws=w_sgu.astype(BF16),
        bs=jnp.broadcast_to(b_sgu[..., None], (DEPTH, SGU_GROUPS, CHUNK, SGU_GROUP_DIM)),
        lnw=sgu_ln_w,
        lnb=sgu_ln_b,
        wr=w_router.T.astype(BF16),
        br=b_router.reshape(N_EXPERTS, 1),
        expand=expand,
        tri=tri,
        w1=w1,
        w3=w3,
        w2=w2.reshape(DEPTH, N_EXPERT_GROUPS, GROUP_HIDDEN, D_MODEL),
    )
    tb = _bias_tables(rpb)

    lat_tiles_per_seq = dec_seq // SRC_TILE
    ctx_row = lambda i: 0
    lat_row = lambda i: 1 + i // lat_tiles_per_seq

    def moe_block(attended, layer):
        chunks = jnp.concatenate([meta[:, :N_EXPERT_GROUPS, 0] for _, _, _, meta in attended], axis=0)
        tables = _chunk_tables(chunks, _max_moe_tiles(chunks.shape[0]))
        y = _moe([slab for _, slab, _, _ in attended], tables, layer, params)
        pending, first = [], 0
        for xm, _, dest, meta in attended:
            n_rows = meta.shape[0] * SLAB_CHUNKS
            pending.append((xm, dest, tables[2][first:first + n_rows], y))
            first += n_rows
        return pending

    kbuf = jnp.zeros((batch, DEPTH, ATTN_WIDTH, seq), F32)
    vbuf = jnp.zeros((batch, DEPTH, ATTN_WIDTH, seq), F32)

    for l in range(DEPTH):
        outs = _inproj(xp, l, mod, ctx_row, params, tm=seq, kv_buffers=(kbuf, vbuf))
        if l > 0:
            xp, outs = outs[0], outs[1:]
        q, kbuf, vbuf, v, sg = outs
        attended_ctx = _ctx_attention(q, kbuf, v, sg, xp, l, mod, params, seq=seq)

        outs = _inproj(xs, l, mod, lat_row, params, tm=SRC_TILE)
        if l > 0:
            xs, outs = outs[0], outs[1:]
        q, k, v, sg = outs
        attended_lat = _nbr_attention(q, k, v, ck, cv, tb, sg, xs, l, mod, params,
                                      batch=dec_batch, seq=dec_seq)

        xp, xs = moe_block([attended_ctx, attended_lat], l)
    xp = _unsort(xp, DEPTH - 1, mod, ctx_row)
    xs = _unsort(xs, DEPTH - 1, mod, lat_row)

    to_cache_layout = lambda buf: buf.reshape(batch, DEPTH, N_HEADS, HEAD_DIM, seq).transpose(0, 1, 4, 2, 3)
    return (xp.reshape(batch, seq, D_MODEL), xs.reshape(dec_batch, dec_seq, D_MODEL),
            to_cache_layout(kbuf), to_cache_layout(vbuf))
```

```python
import functools

import jax
import jax.numpy as jnp
from jax import lax
from jax.experimental import pallas as pl
from jax.experimental.pallas import tpu as pltpu

F32 = jnp.float32
BF16 = jnp.bfloat16

D_MODEL = 1024
DEPTH = 4
N_HEADS = 8
HEAD_DIM = 64
ATTN_WIDTH = N_HEADS * HEAD_DIM
SGU_GROUPS = 4
SGU_GROUP_DIM = 128
SGU_WIDTH = SGU_GROUPS * SGU_GROUP_DIM
CHUNK = 128
IN_WIDTH = 3 * ATTN_WIDTH + 2 * SGU_WIDTH
GRID_W = 64
WIN_H = 8
WIN_W = 16
N_EXPERTS = 16
N_EXPERT_GROUPS = 4
EXPERTS_PER_GROUP = 4
D_EXPERT = 256
GROUP_HIDDEN = EXPERTS_PER_GROUP * D_EXPERT
EPS = 1e-6
NEG = -1e30
LOG2E = 1.4426950408889634

LANES = 128
HEADS_PER_VREG = LANES // HEAD_DIM
N_HEAD_PAIRS = N_HEADS // HEADS_PER_VREG
COND_ROWS = 16
Q_ROWS = 4
WIN_ROWS = Q_ROWS + WIN_H
VMEM_LIMIT = 48 * 1024 * 1024

SRC_TILE = 256
CHUNK_ROWS = 16
SLAB_ROWS = SRC_TILE + N_EXPERT_GROUPS * CHUNK_ROWS
SLAB_CHUNKS = SLAB_ROWS // CHUNK_ROWS
SLAB_WIDTH = D_MODEL + LANES
MOE_TILE = 256
TILE_CHUNKS = MOE_TILE // CHUNK_ROWS


def _cparams(*sem):
    return pltpu.CompilerParams(dimension_semantics=sem, vmem_limit_bytes=VMEM_LIMIT)


def _adaln_kernel(cond_ref, w_ref, b_ref, o_ref):
    c = cond_ref[...]
    a = (c * jax.nn.sigmoid(c)).astype(BF16)
    o_ref[0] = jnp.dot(a, w_ref[0].astype(BF16), preferred_element_type=F32) + b_ref[0]


def _adaln(cond, w_ada, b_ada):
    tn = 1536
    n = 6 * D_MODEL
    return pl.pallas_call(
        _adaln_kernel,
        grid=(DEPTH, n // tn),
        in_specs=[
            pl.BlockSpec((COND_ROWS, D_MODEL), lambda l, j: (0, 0)),
            pl.BlockSpec((1, D_MODEL, tn), lambda l, j: (l, 0, j)),
            pl.BlockSpec((1, 1, tn), lambda l, j: (l, 0, j)),
        ],
        out_specs=pl.BlockSpec((1, COND_ROWS, tn), lambda l, j: (l, 0, j)),
        out_shape=jax.ShapeDtypeStruct((DEPTH, COND_ROWS, n), F32),
        compiler_params=_cparams("arbitrary", "arbitrary"),
        name="adaln",
    )(cond, w_ada, b_ada.reshape(DEPTH, 1, n))


def _gelu_tanh(x):
    return 0.5 * x * (1.0 + jnp.tanh(0.7978845608028654 * (x + 0.044715 * (x * x * x))))


def _head_rms(t, bd, w):
    t2 = t * t
    hi = t2.astype(BF16)
    lo = (t2 - hi.astype(F32)).astype(BF16)
    outs = []
    for c in range(ATTN_WIDTH // 256):
        sl = slice(256 * c, 256 * c + 256)
        hl = jnp.concatenate([hi[:, sl], lo[:, sl]], axis=1)
        ss = jnp.dot(hl, bd, preferred_element_type=F32)
        outs.append(t[:, sl] * lax.rsqrt(ss * (1.0 / HEAD_DIM) + EPS))
    return jnp.concatenate(outs, axis=1) * w


def _inproj_kernel(*refs, tm, transposed_kv, fused_moe):
    refs = list(refs)
    if fused_moe:
        row_ref, xm_ref, dest_ref, prev_mod_ref, y_hbm = refs[:5]
        refs = refs[5:]
    else:
        x_ref = refs.pop(0)
    (mod_ref, n1w_ref, win_ref, qw_ref, kw_ref, bd_ref, ws_ref, bs_ref, lnw_ref, lnb_ref) = refs[:10]
    refs = refs[10:]
    if transposed_kv:
        refs = refs[2:]
    if fused_moe:
        x_out_ref = refs.pop(0)
    if transposed_kv:
        q_ref, kt_ref, vt_ref, v_ref, sg_ref = refs[:5]
        refs = refs[5:]
    else:
        q_ref, k_ref, v_ref, sg_ref = refs[:4]
        refs = refs[4:]
    if fused_moe:
        buf, sem = refs
        moe = _unsorted_moe(row_ref, dest_ref, y_hbm, buf, sem, tm // SRC_TILE)
        x = xm_ref[...] + prev_mod_ref[0, 0][5:6] * moe
        x_out_ref[...] = x
    else:
        x = x_ref[...]
    m = mod_ref[0, 0]
    sh1, sc1 = m[0:1], m[1:2]
    ms = jnp.mean(x * x, axis=-1, keepdims=True)
    h = x * lax.rsqrt(ms + EPS) * n1w_ref[0]
    h = (h * (1.0 + sc1) + sh1).astype(BF16)
    p = jnp.dot(h, win_ref[0], preferred_element_type=F32)

    bd = bd_ref[...]
    q = _head_rms(p[:, 0:ATTN_WIDTH], bd, qw_ref[0])
    k = _head_rms(p[:, ATTN_WIDTH:2 * ATTN_WIDTH], bd, kw_ref[0])
    v = p[:, 2 * ATTN_WIDTH:3 * ATTN_WIDTH]
    q_ref[...] = (q * (HEAD_DIM ** -0.5 * LOG2E)).astype(q_ref.dtype)
    if transposed_kv:
        seq = kt_ref.shape[3]
        for j in range(tm // seq):
            kt_ref[j, 0] = k[j * seq:(j + 1) * seq].T
            vt_ref[j, 0] = v[j * seq:(j + 1) * seq].T
    else:
        k_ref[...] = k.astype(k_ref.dtype)
    v_ref[...] = v.astype(v_ref.dtype)

    off_u = 3 * ATTN_WIDTH
    off_v = off_u + SGU_WIDTH
    for g in range(SGU_GROUPS):
        gl = slice(g * SGU_GROUP_DIM, (g + 1) * SGU_GROUP_DIM)
        u = _gelu_tanh(p[:, off_u + g * SGU_GROUP_DIM: off_u + (g + 1) * SGU_GROUP_DIM])
        t = _gelu_tanh(p[:, off_v + g * SGU_GROUP_DIM: off_v + (g + 1) * SGU_GROUP_DIM])
        mu = jnp.mean(t, axis=-1, keepdims=True)
        d = t - mu
        var = jnp.mean(d * d, axis=-1, keepdims=True)
        y = (d * lax.rsqrt(var + EPS) * lnw_ref[0, g:g + 1, :] + lnb_ref[0, g:g + 1, :]).astype(BF16)
        for c in range(tm // CHUNK):
            rows = slice(c * CHUNK, (c + 1) * CHUNK)
            sv = jnp.dot(ws_ref[0, g], y[rows], preferred_element_type=F32) + bs_ref[0, g]
            sg_ref[rows, gl] = (u[rows] * sv).astype(sg_ref.dtype)


def _inproj(x, layer, mod, row_of_tile, params, *, tm, kv_buffers=None):
    fused_moe = isinstance(x, tuple)
    tok = lambda i, *_: (i, 0)
    lyr3 = lambda i, *_: (layer, 0, 0)
    lyr4 = lambda i, *_: (layer, 0, 0, 0)
    mod_row = lambda lyr: pl.BlockSpec((1, 1, 6, D_MODEL), lambda i, *_: (lyr, row_of_tile(i), 0, 0))
    x_spec = pl.BlockSpec((tm, D_MODEL), tok)
    if fused_moe:
        n_slabs = tm // SRC_TILE
        xm, dest, out_row, y = x
        t_tokens = xm.shape[0]
        prefetch = [out_row]
        in_specs = [x_spec, pl.BlockSpec((n_slabs, 1, SRC_TILE), lambda i, *_: (i, 0, 0)),
                    mod_row(layer - 1), pl.BlockSpec(memory_space=pl.ANY)]
        args = [xm, dest, mod, y]
        scratch = [pltpu.VMEM((2, n_slabs * SLAB_ROWS, D_MODEL), F32), pltpu.SemaphoreType.DMA((2,))]
    else:
        t_tokens = x.shape[0]
        prefetch, in_specs, args, scratch = [], [x_spec], [x], []
    act = pl.BlockSpec((tm, ATTN_WIDTH), tok)
    act_sds = jax.ShapeDtypeStruct((t_tokens, ATTN_WIDTH), BF16)
    in_specs += [
        mod_row(layer),
        pl.BlockSpec((1, 1, D_MODEL), lyr3),
        pl.BlockSpec((1, D_MODEL, IN_WIDTH), lyr3),
        pl.BlockSpec((1, 1, ATTN_WIDTH), lyr3),
        pl.BlockSpec((1, 1, ATTN_WIDTH), lyr3),
        pl.BlockSpec((512, 256), lambda i, *_: (0, 0)),
        pl.BlockSpec((1, SGU_GROUPS, CHUNK, CHUNK), lyr4),
        pl.BlockSpec((1, SGU_GROUPS, CHUNK, SGU_GROUP_DIM), lyr4),
        pl.BlockSpec((1, SGU_GROUPS, SGU_GROUP_DIM), lyr3),
        pl.BlockSpec((1, SGU_GROUPS, SGU_GROUP_DIM), lyr3),
    ]
    args += [mod, params["n1w"], params["win"], params["qw"], params["kw"], params["bd"],
             params["ws"], params["bs"], params["lnw"], params["lnb"]]
    out_specs = [x_spec] if fused_moe else []
    out_shape = [jax.ShapeDtypeStruct((t_tokens, D_MODEL), F32)] if fused_moe else []
    if kv_buffers is None:
        out_specs += [act] * 4
        out_shape += [act_sds] * 4
        aliases = {}
    else:
        kbuf, vbuf = kv_buffers
        seq = kbuf.shape[3]
        kv_spec = pl.BlockSpec((tm // seq, 1, ATTN_WIDTH, seq), lambda i, *_: (i, layer, 0, 0))
        first_alias_in = len(prefetch) + len(args)
        first_alias_out = len(out_specs) + 1
        in_specs += [pl.BlockSpec(memory_space=pl.ANY)] * 2
        args += [kbuf, vbuf]
        out_specs += [act, kv_spec, kv_spec, act, act]
        out_shape += [act_sds, jax.ShapeDtypeStruct(kbuf.shape, F32),
                      jax.ShapeDtypeStruct(vbuf.shape, F32), act_sds, act_sds]
        aliases = {first_alias_in: first_alias_out, first_alias_in + 1: first_alias_out + 1}
    return pl.pallas_call(
        functools.partial(_inproj_kernel, tm=tm, transposed_kv=kv_buffers is not None,
                          fused_moe=fused_moe),
        grid_spec=pltpu.PrefetchScalarGridSpec(
            num_scalar_prefetch=len(prefetch),
            grid=(t_tokens // tm,),
            in_specs=in_specs,
            out_specs=out_specs,
            scratch_shapes=scratch,
        ),
        out_shape=out_shape,
        input_output_aliases=aliases,
        compiler_params=_cparams("arbitrary"),
        name="inproj",
    )(*prefetch, *args)


def _route(lg_t, br):
    s = jax.nn.sigmoid(lg_t)
    sel = s + br
    row = lambda a, i: a[i:i + 1]
    g_score = []
    for g in range(N_EXPERT_GROUPS):
        v = [row(sel, 4 * g + i) for i in range(4)]
        best_pair = None
        for i in range(4):
            for j in range(i + 1, 4):
                pair = v[i] + v[j]
                best_pair = pair if best_pair is None else jnp.maximum(best_pair, pair)
        g_score.append(best_pair)
    best = jnp.zeros_like(g_score[0], dtype=jnp.int32)
    top = g_score[0]
    for g in range(1, N_EXPERT_GROUPS):
        upd = g_score[g] > top
        best = jnp.where(upd, g, best)
        top = jnp.where(upd, g_score[g], top)
    cand, aff = [], []
    for i in range(4):
        ci, si = row(sel, i), row(s, i)
        for g in range(1, N_EXPERT_GROUPS):
            ci = jnp.where(best == g, row(sel, 4 * g + i), ci)
            si = jnp.where(best == g, row(s, 4 * g + i), si)
        cand.append(ci)
        aff.append(si)

    def first_argmax(vals):
        idx = jnp.zeros_like(best)
        top_v = vals[0]
        for i in range(1, 4):
            upd = vals[i] > top_v
            idx = jnp.where(upd, i, idx)
            top_v = jnp.where(upd, vals[i], top_v)
        return idx

    i1 = first_argmax(cand)
    i2 = first_argmax([jnp.where(i1 == i, -jnp.inf, cand[i]) for i in range(4)])
    pick = lambda idx: sum(jnp.where(idx == i, aff[i], 0.0) for i in range(4))
    den = pick(i1) + pick(i2)
    gate = [jnp.where((i1 == i) | (i2 == i), aff[i] / den, 0.0) for i in range(4)]
    hi = [x.astype(BF16) for x in gate]
    lo = [(x - h.astype(F32)).astype(BF16) for x, h in zip(gate, hi)]
    return best, hi + lo


def _slab_positions(best, tri):
    n = best.shape[1]
    onehot = [jnp.where(best == g, 1.0, 0.0) for g in range(N_EXPERT_GROUPS)]
    pad = [jnp.zeros_like(onehot[0])] * (8 - N_EXPERT_GROUPS)
    oh = jnp.concatenate(onehot + pad, axis=0).astype(BF16)
    counts = jnp.dot(oh, tri, preferred_element_type=F32)
    dest = jnp.zeros((1, n), F32)
    seg_start = jnp.zeros((1, 1), F32)
    chunks = []
    for g in range(N_EXPERT_GROUPS):
        cg = counts[g:g + 1]
        n_g = jnp.max(cg, axis=1, keepdims=True)
        c_g = jnp.floor((n_g + (CHUNK_ROWS - 1)) * (1.0 / CHUNK_ROWS))
        dest = dest + onehot[g] * (seg_start + cg - 1.0)
        seg_start = seg_start + c_g * CHUNK_ROWS
        chunks.append(c_g)
    return dest.astype(jnp.int32), chunks


def _permutation(dest):
    n = dest.shape[1]
    hit = lax.broadcasted_iota(jnp.int32, (SLAB_ROWS, n), 0) == dest
    return jnp.where(hit, 1.0, 0.0).astype(BF16)


def _post_attention(a, sg, x, m, wout_ref, n2w_ref, xm_ref):
    g1, sh2, sc2 = m[2:3], m[3:4], m[4:5]
    cat = jnp.concatenate([a, sg], axis=1)
    y = jnp.dot(cat, wout_ref[0], preferred_element_type=F32)
    xm = x + g1 * y
    xm_ref[...] = xm
    ms = jnp.mean(xm * xm, axis=-1, keepdims=True)
    h2 = xm * lax.rsqrt(ms + EPS) * n2w_ref[0]
    return (h2 * (1.0 + sc2) + sh2).astype(BF16)


def _route_previous(h2_scr, wr_ref, br_ref, tri_ref, slab_ref, dest_ref, meta_ref):
    h2 = h2_scr[...]
    lg_t = lax.dot_general(wr_ref[...], h2, (((1,), (1,)), ((), ())), preferred_element_type=F32)
    yield
    best, gates = _route(lg_t, br_ref[...])
    dest, chunks = _slab_positions(best, tri_ref[...])
    yield
    perm = _permutation(dest)
    slab_ref[:, 0:D_MODEL] = jnp.dot(perm, h2, preferred_element_type=F32).astype(BF16)
    gmat = jnp.concatenate(gates + [jnp.zeros((LANES - len(gates), dest.shape[1]), BF16)], axis=0)
    slab_ref[:, D_MODEL:] = lax.dot_general(perm, gmat, (((1,), (1,)), ((), ())),
                                            preferred_element_type=F32).astype(BF16)
    dest_ref[0] = dest
    meta = [jnp.broadcast_to(c, (1, LANES)) for c in chunks]
    meta += [jnp.zeros((8 - len(chunks), LANES), F32)]
    meta_ref[0] = jnp.concatenate(meta, axis=0).astype(jnp.int32)


def _pair_pos(axis):
    return lax.broadcasted_iota(jnp.int32, (1, LANES) if axis == 1 else (LANES, 1), axis)


def _head_mask(hh, axis):
    pos = _pair_pos(axis)
    return (pos >= hh * HEAD_DIM) & (pos < (hh + 1) * HEAD_DIM)


def _only_head(x2, hh, axis):
    return jnp.where(_head_mask(hh, axis), x2, jnp.zeros_like(x2))


def _attend_heads(logits, finish, router):
    outs, even = [], None
    s_next = logits(0)
    for h in range(N_HEADS):
        s = s_next
        if h + 1 < N_HEADS:
            s_next = logits(h + 1)
        if h == N_HEADS // 2:
            next(router)
        o = finish(h, s)
        if h % HEADS_PER_VREG == 0:
            even = o
        else:
            outs.append((even + o).astype(BF16))
    for _ in router:
        pass
    return jnp.concatenate(outs, axis=1)


def _ctx_attn_kernel(q_ref, kt_ref, v_ref, sg_ref, x_ref, mod_ref, wout_ref, n2w_ref, wr_ref, br_ref,
                     tri_ref, xm_ref, slab_ref, dest_ref, meta_ref, h2_scr):
    step = pl.program_id(0)

    @pl.when(step == 0)
    def _():
        h2_scr[...] = jnp.zeros_like(h2_scr)

    router = _route_previous(h2_scr, wr_ref, br_ref, tri_ref, slab_ref, dest_ref, meta_ref)
    next(router)

    def logits(h):
        cols = slice(h // HEADS_PER_VREG * LANES, (h // HEADS_PER_VREG + 1) * LANES)
        kt2 = kt_ref[0, 0, cols, :].astype(BF16)
        return jnp.dot(q_ref[:, cols], _only_head(kt2, h % HEADS_PER_VREG, 0),
                       preferred_element_type=F32)

    def finish(h, s):
        cols = slice(h // HEADS_PER_VREG * LANES, (h // HEADS_PER_VREG + 1) * LANES)
        e = jnp.exp2(s - jnp.max(s, axis=-1, keepdims=True))
        l = jnp.sum(e, axis=-1, keepdims=True)
        vm = _only_head(v_ref[:, cols], h % HEADS_PER_VREG, 1)
        return jnp.dot(e.astype(BF16), vm, preferred_element_type=F32) * (1.0 / l)

    a = _attend_heads(logits, finish, router)
    h2_scr[...] = _post_attention(a, sg_ref[...], x_ref[...], mod_ref[0, 0], wout_ref, n2w_ref, xm_ref)


def _post_attention_out(t_tokens):
    n_src = t_tokens // SRC_TILE
    prev = lambda s: jnp.maximum(s - 1, 0)
    specs = [
        pl.BlockSpec((SRC_TILE, D_MODEL), lambda s: (jnp.minimum(s, n_src - 1), 0)),
        pl.BlockSpec((SLAB_ROWS, SLAB_WIDTH), lambda s: (prev(s), 0)),
        pl.BlockSpec((1, 1, SRC_TILE), lambda s: (prev(s), 0, 0)),
        pl.BlockSpec((1, 8, LANES), lambda s: (prev(s), 0, 0)),
    ]
    shapes = [
        jax.ShapeDtypeStruct((t_tokens, D_MODEL), F32),
        jax.ShapeDtypeStruct((n_src * SLAB_ROWS, SLAB_WIDTH), BF16),
        jax.ShapeDtypeStruct((n_src, 1, SRC_TILE), jnp.int32),
        jax.ShapeDtypeStruct((n_src, 8, LANES), jnp.int32),
    ]
    return specs, shapes, pltpu.VMEM((SRC_TILE, D_MODEL), BF16)


def _ctx_attention(q, kbuf, v, sg, x, layer, mod, params, *, seq):
    assert seq == SRC_TILE
    t_tokens = x.shape[0]
    n_seq = t_tokens // seq
    seq_of = lambda s: jnp.minimum(s, n_seq - 1)
    tok = lambda s: (seq_of(s), 0)
    lyr3 = lambda s: (layer, 0, 0)
    full2 = lambda s: (0, 0)
    out_specs, out_shape, h2_scratch = _post_attention_out(t_tokens)
    return pl.pallas_call(
        _ctx_attn_kernel,
        grid=(n_seq + 1,),
        in_specs=[
            pl.BlockSpec((seq, ATTN_WIDTH), tok),
            pl.BlockSpec((1, 1, ATTN_WIDTH, seq), lambda s: (seq_of(s), layer, 0, 0)),
            pl.BlockSpec((seq, ATTN_WIDTH), tok),
            pl.BlockSpec((seq, SGU_WIDTH), tok),
            pl.BlockSpec((seq, D_MODEL), tok),
            pl.BlockSpec((1, 1, 6, D_MODEL), lambda s: (layer, 0, 0, 0)),
            pl.BlockSpec((1, D_MODEL, D_MODEL), lyr3),
            pl.BlockSpec((1, 1, D_MODEL), lyr3),
            pl.BlockSpec((N_EXPERTS, D_MODEL), full2),
            pl.BlockSpec((N_EXPERTS, 1), full2),
            pl.BlockSpec((SRC_TILE, SRC_TILE), full2),
        ],
        out_specs=out_specs,
        out_shape=out_shape,
        scratch_shapes=[h2_scratch],
        compiler_params=_cparams("arbitrary"),
        name="ctx_attn",
    )(q, kbuf, v, sg, x, mod, params["wout"], params["n2w"], params["wr"], params["br"], params["tri"])


def _row_start(r, rows):
    return jnp.clip(r - WIN_H // 2, 0, rows - WIN_H)


def _nbr_attn_kernel(q_ref, k_ref, v_ref, ck_ref, cv_ref, tb_ref, sg_ref, x_ref, mod_ref, wout_ref,
                     n2w_ref, wr_ref, br_ref, tri_ref, xm_ref, slab_ref, dest_ref, meta_ref,
                     kwm_ref, vwm_ref, kcm_ref, vcm_ref, h2_scr, *, rows, n_tiles):
    step = pl.program_id(0)
    tiles_per_seq = rows // Q_ROWS
    t = jnp.minimum(step, n_tiles - 1) % tiles_per_seq

    @pl.when(step == 0)
    def _():
        h2_scr[...] = jnp.zeros_like(h2_scr)

    @pl.when((t == 0) & (step < n_tiles))
    def _():
        for j in range(N_HEAD_PAIRS):
            cols = slice(j * LANES, (j + 1) * LANES)
            k2, v2 = k_ref[:, cols], v_ref[:, cols]
            ck2 = ck_ref[0, 0, cols, :].astype(BF16)
            cv2 = cv_ref[0, 0, cols, :].astype(BF16)
            for hh in range(HEADS_PER_VREG):
                h = HEADS_PER_VREG * j + hh
                kwm_ref[h] = _only_head(k2, hh, 1)
                vwm_ref[h] = _only_head(v2, hh, 1)
                kcm_ref[h] = _only_head(ck2, hh, 0)
                vcm_ref[h] = _only_head(cv2, hh, 0)

    router = _route_previous(h2_scr, wr_ref, br_ref, tri_ref, slab_ref, dest_ref, meta_ref)
    next(router)

    r0 = t * Q_ROWS
    ws = jnp.minimum(_row_start(r0, rows), rows - WIN_ROWS)
    tok0 = pl.multiple_of(ws * GRID_W, GRID_W)

    lane = lax.broadcasted_iota(jnp.int32, (1, LANES), 1)
    blk_idx, row_mask = [], []
    for a in range(Q_ROWS):
        r = r0 + a
        rs = _row_start(r, rows)
        idx_a, mask_a = [], []
        for jp in range(WIN_ROWS // 2):
            kr = ws + 2 * jp
            idx_a.append(jnp.clip(kr - r + WIN_H, 0, 2 * WIN_H - 1))
            ok0 = (kr >= rs) & (kr < rs + WIN_H)
            ok1 = (kr + 1 >= rs) & (kr + 1 < rs + WIN_H)
            m0 = jnp.where(ok0, 0.0, NEG).astype(F32)
            m1 = jnp.where(ok1, 0.0, NEG).astype(F32)
            mask_a.append(jnp.where(lane < GRID_W, m0, m1))
        blk_idx.append(idx_a)
        row_mask.append(mask_a)

    nt = (((1,), (1,)), ((), ()))
    window = pl.ds(tok0, WIN_ROWS * GRID_W)

    def logits(h):
        q2 = q_ref[:, h // HEADS_PER_VREG * LANES:(h // HEADS_PER_VREG + 1) * LANES]
        bias = jnp.concatenate([
            jnp.concatenate([tb_ref[0, h, blk_idx[a][jp]] + row_mask[a][jp]
                             for jp in range(WIN_ROWS // 2)], axis=1)
            for a in range(Q_ROWS)], axis=0)
        s_w = lax.dot_general(q2, kwm_ref[h, window, :], nt, preferred_element_type=F32) + bias
        s_c = jnp.dot(q2, kcm_ref[h], preferred_element_type=F32)
        return s_w, s_c

    def finish(h, s):
        s_w, s_c = s
        mx = jnp.maximum(jnp.max(s_w, axis=-1, keepdims=True), jnp.max(s_c, axis=-1, keepdims=True))
        e_w = jnp.exp2(s_w - mx)
        e_c = jnp.exp2(s_c - mx)
        l = jnp.sum(e_w, axis=-1, keepdims=True) + jnp.sum(e_c, axis=-1, keepdims=True)
        return (jnp.dot(e_w.astype(BF16), vwm_ref[h, window, :], preferred_element_type=F32)
                + lax.dot_general(e_c.astype(BF16), vcm_ref[h], nt,
                                  preferred_element_type=F32)) * (1.0 / l)

    a_out = _attend_heads(logits, finish, router)
    h2_scr[...] = _post_attention(a_out, sg_ref[...], x_ref[...], mod_ref[0, 0], wout_ref, n2w_ref,
                                  xm_ref)


def _nbr_attention(q, k, v, cache_k, cache_v, tb, sg, x, layer, mod, params, *, batch, seq):
    t_tokens = x.shape[0]
    rows = seq // GRID_W
    tq = Q_ROWS * GRID_W
    assert tq == SRC_TILE
    nt = seq // tq
    past = cache_k.shape[3]
    n_tiles = batch * nt
    out_specs, out_shape, h2_scratch = _post_attention_out(t_tokens)
    tile_of = lambda s: jnp.minimum(s, n_tiles - 1)
    seq_of = lambda s: tile_of(s) // nt
    tok = lambda s: (tile_of(s), 0)
    per_b = lambda s: (seq_of(s), 0)
    lyr3 = lambda s: (layer, 0, 0)
    full2 = lambda s: (0, 0)
    cache_spec = pl.BlockSpec((1, 1, ATTN_WIDTH, past), lambda s: (seq_of(s), layer, 0, 0))
    masked = lambda n: pltpu.VMEM((N_HEADS, n, LANES), BF16)
    masked_t = pltpu.VMEM((N_HEADS, LANES, past), BF16)
    return pl.pallas_call(
        functools.partial(_nbr_attn_kernel, rows=rows, n_tiles=n_tiles),
        grid=(n_tiles + 1,),
        in_specs=[
            pl.BlockSpec((tq, ATTN_WIDTH), tok),
            pl.BlockSpec((seq, ATTN_WIDTH), per_b),
            pl.BlockSpec((seq, ATTN_WIDTH), per_b),
            cache_spec,
            cache_spec,
            pl.BlockSpec((1, N_HEADS, 2 * WIN_H, GRID_W, LANES), lambda s: (layer, 0, 0, 0, 0)),
            pl.BlockSpec((tq, SGU_WIDTH), tok),
            pl.BlockSpec((tq, D_MODEL), tok),
            pl.BlockSpec((1, 1, 6, D_MODEL), lambda s: (layer, 1 + seq_of(s), 0, 0)),
            pl.BlockSpec((1, D_MODEL, D_MODEL), lyr3),
            pl.BlockSpec((1, 1, D_MODEL), lyr3),
            pl.BlockSpec((N_EXPERTS, D_MODEL), full2),
            pl.BlockSpec((N_EXPERTS, 1), full2),
            pl.BlockSpec((SRC_TILE, SRC_TILE), full2),
        ],
        out_specs=out_specs,
        out_shape=out_shape,
        scratch_shapes=[masked(seq), masked(seq), masked_t, masked_t, h2_scratch],
        compiler_params=_cparams("arbitrary"),
        name="nbr_attn",
    )(q, k, v, cache_k, cache_v, tb, sg, x, mod, params["wout"], params["n2w"], params["wr"],
      params["br"], params["tri"])


def _bias_tables(rpb):
    n_rel = 2 * WIN_W - 1
    cols = jnp.arange(GRID_W)
    col_start = jnp.clip(cols - WIN_W // 2, 0, GRID_W - WIN_W)
    col_ok = (cols[None, :] >= col_start[:, None]) & (cols[None, :] < col_start[:, None] + WIN_W)
    rel = cols[None, :] - cols[:, None] + WIN_W - 1
    pick = ((rel[None] == jnp.arange(n_rel)[:, None, None]) & col_ok[None]).astype(F32)
    zero = jnp.zeros_like(pick)
    halves = jnp.concatenate([jnp.concatenate([pick, zero], axis=2),
                              jnp.concatenate([zero, pick], axis=2)], axis=0)
    masked = jnp.where(jnp.concatenate([col_ok, col_ok], axis=1), 0.0, NEG)[None]
    select = jnp.concatenate([halves, masked], axis=0)

    by_offset = jnp.pad(rpb * LOG2E, ((0, 0), (0, 0), (1, 1), (0, 0)))
    feats = jnp.concatenate([by_offset[:, :, :-1], by_offset[:, :, 1:],
                             jnp.ones(by_offset.shape[:2] + (2 * WIN_H, 1), F32)], axis=-1)
    return jnp.einsum("lhdc,cqn->lhdqn", feats, select, precision=lax.Precision.HIGHEST)


def _max_moe_tiles(n_src):
    max_chunks = n_src * SRC_TILE // CHUNK_ROWS + n_src * N_EXPERT_GROUPS
    return -(-max_chunks // TILE_CHUNKS) + N_EXPERT_GROUPS


def _chunk_tables(chunks, n_tiles):
    n_src = chunks.shape[0]
    groups = jnp.arange(N_EXPERT_GROUPS)
    seg_end = jnp.cumsum(chunks, axis=1)
    seg_start = seg_end - chunks
    src_end = jnp.cumsum(chunks, axis=0)
    src_start = src_end - chunks
    total = src_end[-1]
    tiles = (total + TILE_CHUNKS - 1) // TILE_CHUNKS
    tile_end = jnp.cumsum(tiles)
    tile_start = tile_end - tiles
    n_used = tile_end[-1:]

    u = jnp.arange(n_tiles)
    gid = jnp.minimum(jnp.sum(u[:, None] >= tile_end[None, :], axis=1), N_EXPERT_GROUPS - 1)

    p = jnp.arange(n_tiles * TILE_CHUNKS)
    g_hot = jnp.repeat(gid, TILE_CHUNKS)[:, None] == groups[None, :]
    by_group = lambda vec: jnp.sum(jnp.where(g_hot, vec[None, :], 0), axis=1)
    q = p - TILE_CHUNKS * by_group(tile_start)
    live = q < by_group(total)
    src_end_p = jnp.sum(jnp.where(g_hot[:, None, :], src_end[None], 0), axis=2)
    t = jnp.minimum(jnp.sum(q[:, None] >= src_end_p, axis=1), n_src - 1)
    t_hot = t[:, None] == jnp.arange(n_src)[None, :]
    by_seg = lambda tab: jnp.sum(jnp.where(t_hot[:, :, None] & g_hot[:, None, :], tab[None], 0),
                                 axis=(1, 2))
    slab_chunk = by_seg(seg_start) + q - by_seg(src_start)
    src_row = jnp.where(live, t * SLAB_ROWS + CHUNK_ROWS * slab_chunk, 0)

    s = jnp.arange(SLAB_CHUNKS)
    gs = jnp.sum(s[None, :, None] >= seg_end[:, None, :], axis=2)
    used = gs < N_EXPERT_GROUPS
    s_hot = jnp.minimum(gs, N_EXPERT_GROUPS - 1)[:, :, None] == groups
    pick = lambda tab: jnp.sum(jnp.where(s_hot, tab, 0), axis=2)
    pos = (TILE_CHUNKS * pick(tile_start[None, None, :]) + pick(src_start[:, None, :])
           + s[None, :] - pick(seg_start[:, None, :]))
    out_row = jnp.where(used, CHUNK_ROWS * pos, 0).reshape(-1)
    i32 = lambda a: a.astype(jnp.int32)
    return i32(gid), i32(src_row), i32(out_row), i32(n_used)


def _chunk_copy(src_hbm, row, k, buf, sem, slot):
    return pltpu.make_async_copy(
        src_hbm.at[pl.ds(pl.multiple_of(row, CHUNK_ROWS), CHUNK_ROWS), :],
        buf.at[slot, pl.ds(k * CHUNK_ROWS, CHUNK_ROWS), :],
        sem.at[slot])


def _start_chunks(sources, row_ref, first, n_chunks, buf, sem, slot):
    bases = [sum(s.shape[0] for s in sources[:i]) for i in range(len(sources) + 1)]
    for k in range(n_chunks):
        row = row_ref[first + k]
        if len(sources) == 1:
            _chunk_copy(sources[0], row, k, buf, sem, slot).start()
        else:
            for src, lo, hi in zip(sources, bases[:-1], bases[1:]):
                @pl.when((row >= lo) & (row < hi))
                def _(src=src, lo=lo):
                    _chunk_copy(src, row - lo, k, buf, sem, slot).start()


def _gather_step(sources, row_ref, n_chunks, buf, sem):
    step = pl.program_id(0)
    slot = step % 2

    @pl.when(step == 0)
    def _():
        _start_chunks(sources, row_ref, 0, n_chunks, buf, sem, 0)

    @pl.when(step + 1 < pl.num_programs(0))
    def _():
        _start_chunks(sources, row_ref, (step + 1) * n_chunks, n_chunks, buf, sem, 1 - slot)

    for k in range(n_chunks):
        _chunk_copy(sources[0], 0, k, buf, sem, slot).wait()
    return slot


def _moe_kernel(gid_ref, src_ref, nused_ref, *refs, n_slabs):
    slabs = refs[:n_slabs]
    ex_ref, w1_ref, w3_ref, w2_ref, o_ref, buf, sem, w1b, w3b, w2b = refs[n_slabs:]
    u = pl.program_id(0)
    slot = _gather_step(slabs, src_ref, TILE_CHUNKS, buf, sem)

    @pl.when((u == 0) | (gid_ref[u] != gid_ref[jnp.maximum(u - 1, 0)]))
    def _():
        w1b[...] = w1_ref[0].astype(BF16)
        w3b[...] = w3_ref[0].astype(BF16)
        w2b[...] = w2_ref[0, 0].astype(BF16)

    @pl.when(u < nused_ref[0])
    def _():
        h = buf[slot, :, 0:D_MODEL]
        gexp = jnp.dot(buf[slot, :, D_MODEL:], ex_ref[...], preferred_element_type=F32)
        hid = []
        for e in range(EXPERTS_PER_GROUP):
            h1 = jnp.dot(h, w1b[e], preferred_element_type=F32)
            h3 = jnp.dot(h, w3b[e], preferred_element_type=F32)
            act = (h1 * jax.nn.sigmoid(h1)) * h3
            hid.append((act * gexp[:, e * D_EXPERT:(e + 1) * D_EXPERT]).astype(BF16))
        o_ref[...] = jnp.dot(jnp.concatenate(hid, axis=1), w2b[...], preferred_element_type=F32)

    @pl.when(u >= nused_ref[0])
    def _():
        o_ref[...] = jnp.zeros_like(o_ref)


def _moe(slabs, tables, layer, params):
    gid, src_row, _, n_used = tables
    n_tiles = gid.shape[0]
    expert_w = pl.BlockSpec((1, EXPERTS_PER_GROUP, D_MODEL, D_EXPERT),
                            lambda u, gid, src, nu: (layer, gid[u], 0, 0))
    return pl.pallas_call(
        functools.partial(_moe_kernel, n_slabs=len(slabs)),
        grid_spec=pltpu.PrefetchScalarGridSpec(
            num_scalar_prefetch=3,
            grid=(n_tiles,),
            in_specs=[pl.BlockSpec(memory_space=pl.ANY)] * len(slabs) + [
                pl.BlockSpec((LANES, GROUP_HIDDEN), lambda u, gid, src, nu: (0, 0)),
                expert_w,
                expert_w,
                pl.BlockSpec((1, 1, GROUP_HIDDEN, D_MODEL), lambda u, gid, src, nu: (layer, gid[u], 0, 0)),
            ],
            out_specs=pl.BlockSpec((MOE_TILE, D_MODEL), lambda u, gid, src, nu: (u, 0)),
            scratch_shapes=[
                pltpu.VMEM((2, MOE_TILE, SLAB_WIDTH), BF16),
                pltpu.SemaphoreType.DMA((2,)),
                pltpu.VMEM((EXPERTS_PER_GROUP, D_MODEL, D_EXPERT), BF16),
                pltpu.VMEM((EXPERTS_PER_GROUP, D_MODEL, D_EXPERT), BF16),
                pltpu.VMEM((GROUP_HIDDEN, D_MODEL), BF16),
            ],
        ),
        out_shape=jax.ShapeDtypeStruct((n_tiles * MOE_TILE, D_MODEL), F32),
        compiler_params=_cparams("arbitrary"),
        name="moe",
    )(gid, src_row, n_used, *slabs, params["expand"], params["w1"], params["w3"], params["w2"])


def _unsorted_moe(row_ref, dest_ref, y_hbm, buf, sem, n_slabs):
    slot = _gather_step([y_hbm], row_ref, n_slabs * SLAB_CHUNKS, buf, sem)
    tn = (((0,), (0,)), ((), ()))
    out = []
    for j in range(n_slabs):
        y = buf[slot, j * SLAB_ROWS:(j + 1) * SLAB_ROWS, :]
        hi = y.astype(BF16)
        lo = (y - hi.astype(F32)).astype(BF16)
        perm = _permutation(dest_ref[j])
        out.append(lax.dot_general(perm, hi, tn, preferred_element_type=F32)
                   + lax.dot_general(perm, lo, tn, preferred_element_type=F32))
    return jnp.concatenate(out, axis=0)


def _unsort_kernel(row_ref, xm_ref, dest_ref, mod_ref, y_hbm, o_ref, buf, sem, *, n_slabs):
    moe = _unsorted_moe(row_ref, dest_ref, y_hbm, buf, sem, n_slabs)
    o_ref[...] = xm_ref[...] + mod_ref[0, 0][5:6] * moe


def _unsort(pending, layer, mod, row_of_tile, *, tm):
    xm, dest, out_row, y = pending
    t_tokens = xm.shape[0]
    n_slabs = tm // SRC_TILE
    return pl.pallas_call(
        functools.partial(_unsort_kernel, n_slabs=n_slabs),
        grid_spec=pltpu.PrefetchScalarGridSpec(
            num_scalar_prefetch=1,
            grid=(t_tokens // tm,),
            in_specs=[
                pl.BlockSpec((tm, D_MODEL), lambda t, rows: (t, 0)),
                pl.BlockSpec((n_slabs, 1, SRC_TILE), lambda t, rows: (t, 0, 0)),
                pl.BlockSpec((1, 1, 6, D_MODEL), lambda t, rows: (layer, row_of_tile(t), 0, 0)),
                pl.BlockSpec(memory_space=pl.ANY),
            ],
            out_specs=pl.BlockSpec((tm, D_MODEL), lambda t, rows: (t, 0)),
            scratch_shapes=[pltpu.VMEM((2, n_slabs * SLAB_ROWS, D_MODEL), F32),
                            pltpu.SemaphoreType.DMA((2,))],
        ),
        out_shape=jax.ShapeDtypeStruct((t_tokens, D_MODEL), F32),
        compiler_params=_cparams("arbitrary"),
        name="unsort",
    )(out_row, xm, dest, mod, y)


def kernel(x_prompt, x_sample, cache_k, cache_v, c, c_ctx, w_ada, b_ada, norm1_w, norm2_w, w_in,
           q_norm_w, k_norm_w, rpb, w_sgu, b_sgu, sgu_ln_w, sgu_ln_b, w_out, w_router, b_router,
           w1, w3, w2):
    batch, seq, _ = x_prompt.shape
    dec_batch, dec_seq, _ = x_sample.shape
    past = cache_k.shape[2]
    assert 1 + dec_batch <= COND_ROWS and dec_seq % (Q_ROWS * GRID_W) == 0 and seq % CHUNK == 0

    cond = jnp.zeros((COND_ROWS, D_MODEL), F32).at[0].set(c_ctx).at[1:1 + dec_batch].set(c)
    mod = _adaln(cond, w_ada, b_ada).reshape(DEPTH, COND_ROWS, 6, D_MODEL)

    xp = x_prompt.reshape(batch * seq, D_MODEL)
    xs = x_sample.reshape(dec_batch * dec_seq, D_MODEL)
    from_cache_layout = lambda a: a.transpose(0, 1, 3, 4, 2).reshape(dec_batch, DEPTH, ATTN_WIDTH, past)
    ck = from_cache_layout(cache_k)
    cv = from_cache_layout(cache_v)

    head_of = jnp.arange(256) // HEAD_DIM
    bd = (head_of[:, None] == head_of[None, :]).astype(BF16)
    col_expert = jnp.arange(GROUP_HIDDEN) // D_EXPERT
    gate_lane = jnp.arange(LANES)
    expand = ((gate_lane[:, None] % EXPERTS_PER_GROUP == col_expert[None, :])
              & (gate_lane[:, None] < 2 * EXPERTS_PER_GROUP)).astype(BF16)
    tok_id = jnp.arange(SRC_TILE)
    tri = (tok_id[:, None] <= tok_id[None, :]).astype(BF16)

    params = dict(
        n1w=norm1_w.reshape(DEPTH, 1, D_MODEL),
        n2w=norm2_w.reshape(DEPTH, 1, D_MODEL),
        win=w_in.astype(BF16),
        wout=w_out.astype(BF16),
        qw=jnp.tile(q_norm_w, (1, N_HEADS)).reshape(DEPTH, 1, ATTN_WIDTH),
        kw=jnp.tile(k_norm_w, (1, N_HEADS)).reshape(DEPTH, 1, ATTN_WIDTH),
        bd=jnp.concatenate([bd, bd], axis=0),
        ws=w_sgu.astype(BF16),
        bs=jnp.broadcast_to(b_sgu[..., None], (DEPTH, SGU_GROUPS, CHUNK, SGU_GROUP_DIM)),
        lnw=sgu_ln_w,
        lnb=sgu_ln_b,
        wr=w_router.T.astype(BF16),
        br=b_router.reshape(N_EXPERTS, 1),
        expand=expand,
        tri=tri,
        w1=w1,
        w3=w3,
        w2=w2.reshape(DEPTH, N_EXPERT_GROUPS, GROUP_HIDDEN, D_MODEL),
    )
    tb = _bias_tables(rpb)

    tm = 2 * SRC_TILE
    assert dec_seq % tm == 0 and tm % seq == 0
    lat_tiles_per_seq = dec_seq // tm
    ctx_row = lambda i: 0
    lat_row = lambda i: 1 + i // lat_tiles_per_seq

    def moe_block(attended, layer):
        chunks = jnp.concatenate([meta[:, :N_EXPERT_GROUPS, 0] for _, _, _, meta in attended], axis=0)
        tables = _chunk_tables(chunks, _max_moe_tiles(chunks.shape[0]))
        y = _moe([slab for _, slab, _, _ in attended], tables, layer, params)
        pending, first = [], 0
        for xm, _, dest, meta in attended:
            n_rows = meta.shape[0] * SLAB_CHUNKS
            pending.append((xm, dest, tables[2][first:first + n_rows], y))
            first += n_rows
        return pending

    kbuf = jnp.zeros((batch, DEPTH, ATTN_WIDTH, seq), F32)
    vbuf = jnp.zeros((batch, DEPTH, ATTN_WIDTH, seq), F32)

    for l in range(DEPTH):
        outs = _inproj(xp, l, mod, ctx_row, params, tm=tm, kv_buffers=(kbuf, vbuf))
        if l > 0:
            xp, outs = outs[0], outs[1:]
        q, kbuf, vbuf, v, sg = outs
        attended_ctx = _ctx_attention(q, kbuf, v, sg, xp, l, mod, params, seq=seq)

        outs = _inproj(xs, l, mod, lat_row, params, tm=tm)
        if l > 0:
            xs, outs = outs[0], outs[1:]
        q, k, v, sg = outs
        attended_lat = _nbr_attention(q, k, v, ck, cv, tb, sg, xs, l, mod, params,
                                      batch=dec_batch, seq=dec_seq)

        xp, xs = moe_block([attended_ctx, attended_lat], l)
    xp = _unsort(xp, DEPTH - 1, mod, ctx_row, tm=tm)
    xs = _unsort(xs, DEPTH - 1, mod, lat_row, tm=tm)

    to_cache_layout = lambda buf: buf.reshape(batch, DEPTH, N_HEADS, HEAD_DIM, seq).transpose(0, 1, 4, 2, 3)
    return (xp.reshape(batch, seq, D_MODEL), xs.reshape(dec_batch, dec_seq, D_MODEL),
            to_cache_layout(kbuf), to_cache_layout(vbuf))
```

```python
import functools

import jax
import jax.numpy as jnp
from jax import lax
from jax.experimental import pallas as pl
from jax.experimental.pallas import tpu as pltpu

F32 = jnp.float32
BF16 = jnp.bfloat16

D_MODEL = 1024
DEPTH = 4
N_HEADS = 8
HEAD_DIM = 64
ATTN_WIDTH = N_HEADS * HEAD_DIM
SGU_GROUPS = 4
SGU_GROUP_DIM = 128
SGU_WIDTH = SGU_GROUPS * SGU_GROUP_DIM
CHUNK = 128
IN_WIDTH = 3 * ATTN_WIDTH + 2 * SGU_WIDTH
GRID_W = 64
WIN_H = 8
WIN_W = 16
N_EXPERTS = 16
N_EXPERT_GROUPS = 4
EXPERTS_PER_GROUP = 4
D_EXPERT = 256
GROUP_HIDDEN = EXPERTS_PER_GROUP * D_EXPERT
EPS = 1e-6
NEG = -1e30
LOG2E = 1.4426950408889634

LANES = 128
HEADS_PER_VREG = LANES // HEAD_DIM
N_HEAD_PAIRS = N_HEADS // HEADS_PER_VREG
COND_ROWS = 16
Q_ROWS = 4
WIN_ROWS = Q_ROWS + WIN_H
VMEM_LIMIT = 48 * 1024 * 1024

SRC_TILE = 256
CHUNK_ROWS = 16
SLAB_ROWS = SRC_TILE + N_EXPERT_GROUPS * CHUNK_ROWS
SLAB_CHUNKS = SLAB_ROWS // CHUNK_ROWS
SLAB_WIDTH = D_MODEL + LANES
MOE_TILE = 256
TILE_CHUNKS = MOE_TILE // CHUNK_ROWS


def _cparams(*sem):
    return pltpu.CompilerParams(dimension_semantics=sem, vmem_limit_bytes=VMEM_LIMIT)


def _adaln_kernel(cond_ref, w_ref, b_ref, o_ref):
    c = cond_ref[...]
    a = (c * jax.nn.sigmoid(c)).astype(BF16)
    o_ref[0] = jnp.dot(a, w_ref[0].astype(BF16), preferred_element_type=F32) + b_ref[0]


def _adaln(cond, w_ada, b_ada):
    tn = 1536
    n = 6 * D_MODEL
    return pl.pallas_call(
        _adaln_kernel,
        grid=(DEPTH, n // tn),
        in_specs=[
            pl.BlockSpec((COND_ROWS, D_MODEL), lambda l, j: (0, 0)),
            pl.BlockSpec((1, D_MODEL, tn), lambda l, j: (l, 0, j)),
            pl.BlockSpec((1, 1, tn), lambda l, j: (l, 0, j)),
        ],
        out_specs=pl.BlockSpec((1, COND_ROWS, tn), lambda l, j: (l, 0, j)),
        out_shape=jax.ShapeDtypeStruct((DEPTH, COND_ROWS, n), F32),
        compiler_params=_cparams("arbitrary", "arbitrary"),
        name="adaln",
    )(cond, w_ada, b_ada.reshape(DEPTH, 1, n))


def _gelu_tanh(x):
    return 0.5 * x * (1.0 + jnp.tanh(0.7978845608028654 * (x + 0.044715 * (x * x * x))))


def _head_rms(t, bd, w):
    t2 = t * t
    hi = t2.astype(BF16)
    lo = (t2 - hi.astype(F32)).astype(BF16)
    outs = []
    for c in range(ATTN_WIDTH // 256):
        sl = slice(256 * c, 256 * c + 256)
        hl = jnp.concatenate([hi[:, sl], lo[:, sl]], axis=1)
        ss = jnp.dot(hl, bd, preferred_element_type=F32)
        outs.append(t[:, sl] * lax.rsqrt(ss * (1.0 / HEAD_DIM) + EPS))
    return jnp.concatenate(outs, axis=1) * w


def _inproj_kernel(*refs, tm, transposed_kv, new_kv_layer, fused_moe):
    refs = list(refs)
    if fused_moe:
        row_ref, xm_ref, dest_ref, prev_mod_ref, y_hbm = refs[:5]
        refs = refs[5:]
    else:
        x_ref = refs.pop(0)
    (mod_ref, n1w_ref, win_ref, qw_ref, kw_ref, bd_ref, ws_ref, bs_ref, lnw_ref, lnb_ref) = refs[:10]
    refs = refs[10:]
    if transposed_kv and new_kv_layer is None:
        refs = refs[2:]
    if fused_moe:
        x_out_ref = refs.pop(0)
    if transposed_kv:
        q_ref, kt_ref, vt_ref, v_ref, sg_ref = refs[:5]
        refs = refs[5:]
    else:
        q_ref, k_ref, v_ref, sg_ref = refs[:4]
        refs = refs[4:]
    if fused_moe:
        buf, sem = refs
        moe = _unsorted_moe(row_ref, dest_ref, y_hbm, buf, sem, tm // SRC_TILE)
        x = xm_ref[...] + prev_mod_ref[0, 0][5:6] * moe
        x_out_ref[...] = x
    else:
        x = x_ref[...]
    m = mod_ref[0, 0]
    sh1, sc1 = m[0:1], m[1:2]
    ms = jnp.mean(x * x, axis=-1, keepdims=True)
    h = x * lax.rsqrt(ms + EPS) * n1w_ref[0]
    h = (h * (1.0 + sc1) + sh1).astype(BF16)
    p = jnp.dot(h, win_ref[0], preferred_element_type=F32)

    bd = bd_ref[...]
    q = _head_rms(p[:, 0:ATTN_WIDTH], bd, qw_ref[0])
    k = _head_rms(p[:, ATTN_WIDTH:2 * ATTN_WIDTH], bd, kw_ref[0])
    v = p[:, 2 * ATTN_WIDTH:3 * ATTN_WIDTH]
    q_ref[...] = (q * (HEAD_DIM ** -0.5 * LOG2E)).astype(q_ref.dtype)
    if transposed_kv:
        seq = kt_ref.shape[3]
        if new_kv_layer is None:
            slot = 0
        else:
            slot = new_kv_layer
            kt_ref[...] = jnp.zeros_like(kt_ref)
            vt_ref[...] = jnp.zeros_like(vt_ref)
        for j in range(tm // seq):
            kt_ref[j, slot] = k[j * seq:(j + 1) * seq].T
            vt_ref[j, slot] = v[j * seq:(j + 1) * seq].T
    else:
        k_ref[...] = k.astype(k_ref.dtype)
    v_ref[...] = v.astype(v_ref.dtype)

    off_u = 3 * ATTN_WIDTH
    off_v = off_u + SGU_WIDTH
    for g in range(SGU_GROUPS):
        gl = slice(g * SGU_GROUP_DIM, (g + 1) * SGU_GROUP_DIM)
        u = _gelu_tanh(p[:, off_u + g * SGU_GROUP_DIM: off_u + (g + 1) * SGU_GROUP_DIM])
        t = _gelu_tanh(p[:, off_v + g * SGU_GROUP_DIM: off_v + (g + 1) * SGU_GROUP_DIM])
        mu = jnp.mean(t, axis=-1, keepdims=True)
        d = t - mu
        var = jnp.mean(d * d, axis=-1, keepdims=True)
        y = (d * lax.rsqrt(var + EPS) * lnw_ref[0, g:g + 1, :] + lnb_ref[0, g:g + 1, :]).astype(BF16)
        for c in range(tm // CHUNK):
            rows = slice(c * CHUNK, (c + 1) * CHUNK)
            sv = jnp.dot(ws_ref[0, g], y[rows], preferred_element_type=F32) + bs_ref[0, g]
            sg_ref[rows, gl] = (u[rows] * sv).astype(sg_ref.dtype)


def _inproj(x, layer, mod, row_of_tile, params, *, tm, kv_buffers=None):
    fused_moe = isinstance(x, tuple)
    tok = lambda i, *_: (i, 0)
    lyr3 = lambda i, *_: (layer, 0, 0)
    lyr4 = lambda i, *_: (layer, 0, 0, 0)
    mod_row = lambda lyr: pl.BlockSpec((1, 1, 6, D_MODEL), lambda i, *_: (lyr, row_of_tile(i), 0, 0))
    x_spec = pl.BlockSpec((tm, D_MODEL), tok)
    if fused_moe:
        n_slabs = tm // SRC_TILE
        xm, dest, out_row, y = x
        t_tokens = xm.shape[0]
        prefetch = [out_row]
        in_specs = [x_spec, pl.BlockSpec((n_slabs, 1, SRC_TILE), lambda i, *_: (i, 0, 0)),
                    mod_row(layer - 1), pl.BlockSpec(memory_space=pl.ANY)]
        args = [xm, dest, mod, y]
        scratch = [pltpu.VMEM((2, n_slabs * SLAB_ROWS, D_MODEL), F32), pltpu.SemaphoreType.DMA((2,))]
    else:
        t_tokens = x.shape[0]
        prefetch, in_specs, args, scratch = [], [x_spec], [x], []
    act = pl.BlockSpec((tm, ATTN_WIDTH), tok)
    act_sds = jax.ShapeDtypeStruct((t_tokens, ATTN_WIDTH), BF16)
    in_specs += [
        mod_row(layer),
        pl.BlockSpec((1, 1, D_MODEL), lyr3),
        pl.BlockSpec((1, D_MODEL, IN_WIDTH), lyr3),
        pl.BlockSpec((1, 1, ATTN_WIDTH), lyr3),
        pl.BlockSpec((1, 1, ATTN_WIDTH), lyr3),
        pl.BlockSpec((512, 256), lambda i, *_: (0, 0)),
        pl.BlockSpec((1, SGU_GROUPS, CHUNK, CHUNK), lyr4),
        pl.BlockSpec((1, SGU_GROUPS, CHUNK, SGU_GROUP_DIM), lyr4),
        pl.BlockSpec((1, SGU_GROUPS, SGU_GROUP_DIM), lyr3),
        pl.BlockSpec((1, SGU_GROUPS, SGU_GROUP_DIM), lyr3),
    ]
    args += [mod, params["n1w"], params["win"], params["qw"], params["kw"], params["bd"],
             params["ws"], params["bs"], params["lnw"], params["lnb"]]
    out_specs = [x_spec] if fused_moe else []
    out_shape = [jax.ShapeDtypeStruct((t_tokens, D_MODEL), F32)] if fused_moe else []
    if kv_buffers is None:
        out_specs += [act] * 4
        out_shape += [act_sds] * 4
        aliases = {}
    elif isinstance(kv_buffers, jax.ShapeDtypeStruct):
        seq = kv_buffers.shape[3]
        kv_spec = pl.BlockSpec((tm // seq,) + kv_buffers.shape[1:], lambda i, *_: (i, 0, 0, 0))
        out_specs += [act, kv_spec, kv_spec, act, act]
        out_shape += [act_sds, kv_buffers, kv_buffers, act_sds, act_sds]
        aliases = {}
    else:
        kbuf, vbuf = kv_buffers
        seq = kbuf.shape[3]
        kv_spec = pl.BlockSpec((tm // seq, 1, ATTN_WIDTH, seq), lambda i, *_: (i, layer, 0, 0))
        first_alias_in = len(prefetch) + len(args)
        first_alias_out = len(out_specs) + 1
        in_specs += [pl.BlockSpec(memory_space=pl.ANY)] * 2
        args += [kbuf, vbuf]
        out_specs += [act, kv_spec, kv_spec, act, act]
        out_shape += [act_sds, jax.ShapeDtypeStruct(kbuf.shape, F32),
                      jax.ShapeDtypeStruct(vbuf.shape, F32), act_sds, act_sds]
        aliases = {first_alias_in: first_alias_out, first_alias_in + 1: first_alias_out + 1}
    return pl.pallas_call(
        functools.partial(_inproj_kernel, tm=tm, transposed_kv=kv_buffers is not None,
                          new_kv_layer=layer if isinstance(kv_buffers, jax.ShapeDtypeStruct) else None,
                          fused_moe=fused_moe),
        grid_spec=pltpu.PrefetchScalarGridSpec(
            num_scalar_prefetch=len(prefetch),
            grid=(t_tokens // tm,),
            in_specs=in_specs,
            out_specs=out_specs,
            scratch_shapes=scratch,
        ),
        out_shape=out_shape,
        input_output_aliases=aliases,
        compiler_params=_cparams("arbitrary"),
        name="inproj",
    )(*prefetch, *args)


def _route(lg_t, br):
    s = jax.nn.sigmoid(lg_t)
    sel = s + br
    row = lambda a, i: a[i:i + 1]
    g_score = []
    for g in range(N_EXPERT_GROUPS):
        v = [row(sel, 4 * g + i) for i in range(4)]
        best_pair = None
        for i in range(4):
            for j in range(i + 1, 4):
                pair = v[i] + v[j]
                best_pair = pair if best_pair is None else jnp.maximum(best_pair, pair)
        g_score.append(best_pair)
    best = jnp.zeros_like(g_score[0], dtype=jnp.int32)
    top = g_score[0]
    for g in range(1, N_EXPERT_GROUPS):
        upd = g_score[g] > top
        best = jnp.where(upd, g, best)
        top = jnp.where(upd, g_score[g], top)
    cand, aff = [], []
    for i in range(4):
        ci, si = row(sel, i), row(s, i)
        for g in range(1, N_EXPERT_GROUPS):
            ci = jnp.where(best == g, row(sel, 4 * g + i), ci)
            si = jnp.where(best == g, row(s, 4 * g + i), si)
        cand.append(ci)
        aff.append(si)

    def first_argmax(vals):
        idx = jnp.zeros_like(best)
        top_v = vals[0]
        for i in range(1, 4):
            upd = vals[i] > top_v
            idx = jnp.where(upd, i, idx)
            top_v = jnp.where(upd, vals[i], top_v)
        return idx

    i1 = first_argmax(cand)
    i2 = first_argmax([jnp.where(i1 == i, -jnp.inf, cand[i]) for i in range(4)])
    pick = lambda idx: sum(jnp.where(idx == i, aff[i], 0.0) for i in range(4))
    den = pick(i1) + pick(i2)
    gate = [jnp.where((i1 == i) | (i2 == i), aff[i] / den, 0.0) for i in range(4)]
    hi = [x.astype(BF16) for x in gate]
    lo = [(x - h.astype(F32)).astype(BF16) for x, h in zip(gate, hi)]
    return best, hi + lo


def _slab_positions(best, tri):
    n = best.shape[1]
    onehot = [jnp.where(best == g, 1.0, 0.0) for g in range(N_EXPERT_GROUPS)]
    pad = [jnp.zeros_like(onehot[0])] * (8 - N_EXPERT_GROUPS)
    oh = jnp.concatenate(onehot + pad, axis=0).astype(BF16)
    counts = jnp.dot(oh, tri, preferred_element_type=F32)
    dest = jnp.zeros((1, n), F32)
    seg_start = jnp.zeros((1, 1), F32)
    chunks = []
    for g in range(N_EXPERT_GROUPS):
        cg = counts[g:g + 1]
        n_g = jnp.max(cg, axis=1, keepdims=True)
        c_g = jnp.floor((n_g + (CHUNK_ROWS - 1)) * (1.0 / CHUNK_ROWS))
        dest = dest + onehot[g] * (seg_start + cg - 1.0)
        seg_start = seg_start + c_g * CHUNK_ROWS
        chunks.append(c_g)
    return dest.astype(jnp.int32), chunks


def _permutation(dest):
    n = dest.shape[1]
    hit = lax.broadcasted_iota(jnp.int32, (SLAB_ROWS, n), 0) == dest
    return jnp.where(hit, 1.0, 0.0).astype(BF16)


def _post_attention(a, sg, x, m, wout_ref, n2w_ref, xm_ref):
    g1, sh2, sc2 = m[2:3], m[3:4], m[4:5]
    cat = jnp.concatenate([a, sg], axis=1)
    y = jnp.dot(cat, wout_ref[0], preferred_element_type=F32)
    xm = x + g1 * y
    xm_ref[...] = xm
    ms = jnp.mean(xm * xm, axis=-1, keepdims=True)
    h2 = xm * lax.rsqrt(ms + EPS) * n2w_ref[0]
    return (h2 * (1.0 + sc2) + sh2).astype(BF16)


def _route_previous(h2_scr, wr_ref, br_ref, tri_ref, slab_ref, dest_ref, meta_ref):
    h2 = h2_scr[...]
    lg_t = lax.dot_general(wr_ref[...], h2, (((1,), (1,)), ((), ())), preferred_element_type=F32)
    yield
    best, gates = _route(lg_t, br_ref[...])
    dest, chunks = _slab_positions(best, tri_ref[...])
    yield
    perm = _permutation(dest)
    slab_ref[:, 0:D_MODEL] = jnp.dot(perm, h2, preferred_element_type=F32).astype(BF16)
    gmat = jnp.concatenate(gates + [jnp.zeros((LANES - len(gates), dest.shape[1]), BF16)], axis=0)
    slab_ref[:, D_MODEL:] = lax.dot_general(perm, gmat, (((1,), (1,)), ((), ())),
                                            preferred_element_type=F32).astype(BF16)
    dest_ref[0] = dest
    meta = [jnp.broadcast_to(c, (1, LANES)) for c in chunks]
    meta += [jnp.zeros((8 - len(chunks), LANES), F32)]
    meta_ref[0] = jnp.concatenate(meta, axis=0).astype(jnp.int32)


def _pair_pos(axis):
    return lax.broadcasted_iota(jnp.int32, (1, LANES) if axis == 1 else (LANES, 1), axis)


def _head_mask(hh, axis):
    pos = _pair_pos(axis)
    return (pos >= hh * HEAD_DIM) & (pos < (hh + 1) * HEAD_DIM)


def _only_head(x2, hh, axis):
    return jnp.where(_head_mask(hh, axis), x2, jnp.zeros_like(x2))


def _attend_heads(logits, finish, router):
    outs, even = [], None
    s_next = logits(0)
    for h in range(N_HEADS):
        s = s_next
        if h + 1 < N_HEADS:
            s_next = logits(h + 1)
        if h == N_HEADS // 2:
            next(router)
        o = finish(h, s)
        if h % HEADS_PER_VREG == 0:
            even = o
        else:
            outs.append((even + o).astype(BF16))
    for _ in router:
        pass
    return jnp.concatenate(outs, axis=1)


def _ctx_attn_kernel(q_ref, kt_ref, v_ref, sg_ref, x_ref, mod_ref, wout_ref, n2w_ref, wr_ref, br_ref,
                     tri_ref, xm_ref, slab_ref, dest_ref, meta_ref, h2_scr):
    step = pl.program_id(0)

    @pl.when(step == 0)
    def _():
        h2_scr[...] = jnp.zeros_like(h2_scr)

    def logits(h):
        cols = slice(h // HEADS_PER_VREG * LANES, (h // HEADS_PER_VREG + 1) * LANES)
        kt2 = kt_ref[0, 0, cols, :].astype(BF16)
        return jnp.dot(q_ref[:, cols], _only_head(kt2, h % HEADS_PER_VREG, 0),
                       preferred_element_type=F32)

    def finish(h, s):
        cols = slice(h // HEADS_PER_VREG * LANES, (h // HEADS_PER_VREG + 1) * LANES)
        e = jnp.exp2(s - jnp.max(s, axis=-1, keepdims=True))
        l = jnp.sum(e, axis=-1, keepdims=True)
        vm = _only_head(v_ref[:, cols], h % HEADS_PER_VREG, 1)
        return jnp.dot(e.astype(BF16), vm, preferred_element_type=F32) * (1.0 / l)

    new_router = lambda: _route_previous(h2_scr, wr_ref, br_ref, tri_ref, slab_ref, dest_ref, meta_ref)
    last = pl.num_programs(0) - 1

    @pl.when(step < last)
    def _():
        router = new_router()
        next(router)
        a = _attend_heads(logits, finish, router)
        h2_scr[...] = _post_attention(a, sg_ref[...], x_ref[...], mod_ref[0, 0], wout_ref, n2w_ref,
                                      xm_ref)

    @pl.when(step == last)
    def _():
        for _ in new_router():
            pass


def _post_attention_out(t_tokens):
    n_src = t_tokens // SRC_TILE
    prev = lambda s: jnp.maximum(s - 1, 0)
    specs = [
        pl.BlockSpec((SRC_TILE, D_MODEL), lambda s: (jnp.minimum(s, n_src - 1), 0)),
        pl.BlockSpec((SLAB_ROWS, SLAB_WIDTH), lambda s: (prev(s), 0)),
        pl.BlockSpec((1, 1, SRC_TILE), lambda s: (prev(s), 0, 0)),
        pl.BlockSpec((1, 8, LANES), lambda s: (prev(s), 0, 0)),
    ]
    shapes = [
        jax.ShapeDtypeStruct((t_tokens, D_MODEL), F32),
        jax.ShapeDtypeStruct((n_src * SLAB_ROWS, SLAB_WIDTH), BF16),
        jax.ShapeDtypeStruct((n_src, 1, SRC_TILE), jnp.int32),
        jax.ShapeDtypeStruct((n_src, 8, LANES), jnp.int32),
    ]
    return specs, shapes, pltpu.VMEM((SRC_TILE, D_MODEL), BF16)


def _ctx_attention(q, kbuf, v, sg, x, layer, mod, params, *, seq):
    assert seq == SRC_TILE
    t_tokens = x.shape[0]
    n_seq = t_tokens // seq
    seq_of = lambda s: jnp.minimum(s, n_seq - 1)
    tok = lambda s: (seq_of(s), 0)
    lyr3 = lambda s: (layer, 0, 0)
    full2 = lambda s: (0, 0)
    out_specs, out_shape, h2_scratch = _post_attention_out(t_tokens)
    return pl.pallas_call(
        _ctx_attn_kernel,
        grid=(n_seq + 1,),
        in_specs=[
            pl.BlockSpec((seq, ATTN_WIDTH), tok),
            pl.BlockSpec((1, 1, ATTN_WIDTH, seq), lambda s: (seq_of(s), layer, 0, 0)),
            pl.BlockSpec((seq, ATTN_WIDTH), tok),
            pl.BlockSpec((seq, SGU_WIDTH), tok),
            pl.BlockSpec((seq, D_MODEL), tok),
            pl.BlockSpec((1, 1, 6, D_MODEL), lambda s: (layer, 0, 0, 0)),
            pl.BlockSpec((1, D_MODEL, D_MODEL), lyr3),
            pl.BlockSpec((1, 1, D_MODEL), lyr3),
            pl.BlockSpec((N_EXPERTS, D_MODEL), full2),
            pl.BlockSpec((N_EXPERTS, 1), full2),
            pl.BlockSpec((SRC_TILE, SRC_TILE), full2),
        ],
        out_specs=out_specs,
        out_shape=out_shape,
        scratch_shapes=[h2_scratch],
        compiler_params=_cparams("arbitrary"),
        name="ctx_attn",
    )(q, kbuf, v, sg, x, mod, params["wout"], params["n2w"], params["wr"], params["br"], params["tri"])


def _row_start(r, rows):
    return jnp.clip(r - WIN_H // 2, 0, rows - WIN_H)


def _nbr_attn_kernel(q_ref, k_ref, v_ref, ck_ref, cv_ref, tb_ref, sg_ref, x_ref, mod_ref, wout_ref,
                     n2w_ref, wr_ref, br_ref, tri_ref, xm_ref, slab_ref, dest_ref, meta_ref,
                     kwm_ref, vwm_ref, kcm_ref, vcm_ref, h2_scr, *, rows, n_tiles):
    step = pl.program_id(0)
    tiles_per_seq = rows // Q_ROWS
    t = jnp.minimum(step, n_tiles - 1) % tiles_per_seq

    @pl.when(step == 0)
    def _():
        h2_scr[...] = jnp.zeros_like(h2_scr)

    @pl.when((t == 0) & (step < n_tiles))
    def _():
        for j in range(N_HEAD_PAIRS):
            cols = slice(j * LANES, (j + 1) * LANES)
            k2, v2 = k_ref[:, cols], v_ref[:, cols]
            ck2 = ck_ref[0, 0, cols, :].astype(BF16)
            cv2 = cv_ref[0, 0, cols, :].astype(BF16)
            for hh in range(HEADS_PER_VREG):
                h = HEADS_PER_VREG * j + hh
                kwm_ref[h] = _only_head(k2, hh, 1)
                vwm_ref[h] = _only_head(v2, hh, 1)
                kcm_ref[h] = _only_head(ck2, hh, 0)
                vcm_ref[h] = _only_head(cv2, hh, 0)

    new_router = lambda: _route_previous(h2_scr, wr_ref, br_ref, tri_ref, slab_ref, dest_ref, meta_ref)

    @pl.when(step == n_tiles)
    def _():
        for _ in new_router():
            pass

    @pl.when(step < n_tiles)
    def _():
        _attend_tile(t, new_router(), q_ref, tb_ref, sg_ref, x_ref, mod_ref, wout_ref, n2w_ref, xm_ref,
                     kwm_ref, vwm_ref, kcm_ref, vcm_ref, h2_scr, rows)


def _attend_tile(t, router, q_ref, tb_ref, sg_ref, x_ref, mod_ref, wout_ref, n2w_ref, xm_ref,
                 kwm_ref, vwm_ref, kcm_ref, vcm_ref, h2_scr, rows):
    next(router)
    r0 = t * Q_ROWS
    ws = jnp.minimum(_row_start(r0, rows), rows - WIN_ROWS)
    tok0 = pl.multiple_of(ws * GRID_W, GRID_W)

    lane = lax.broadcasted_iota(jnp.int32, (1, LANES), 1)
    blk_idx, row_mask = [], []
    for a in range(Q_ROWS):
        r = r0 + a
        rs = _row_start(r, rows)
        idx_a, mask_a = [], []
        for jp in range(WIN_ROWS // 2):
            kr = ws + 2 * jp
            idx_a.append(jnp.clip(kr - r + WIN_H, 0, 2 * WIN_H - 1))
            ok0 = (kr >= rs) & (kr < rs + WIN_H)
            ok1 = (kr + 1 >= rs) & (kr + 1 < rs + WIN_H)
            m0 = jnp.where(ok0, 0.0, NEG).astype(F32)
            m1 = jnp.where(ok1, 0.0, NEG).astype(F32)
            mask_a.append(jnp.where(lane < GRID_W, m0, m1))
        blk_idx.append(idx_a)
        row_mask.append(mask_a)

    nt = (((1,), (1,)), ((), ()))
    window = pl.ds(tok0, WIN_ROWS * GRID_W)

    def logits(h):
        q2 = q_ref[:, h // HEADS_PER_VREG * LANES:(h // HEADS_PER_VREG + 1) * LANES]
        bias = jnp.concatenate([
            jnp.concatenate([tb_ref[0, h, blk_idx[a][jp]] + row_mask[a][jp]
                             for jp in range(WIN_ROWS // 2)], axis=1)
            for a in range(Q_ROWS)], axis=0)
        s_w = lax.dot_general(q2, kwm_ref[h, window, :], nt, preferred_element_type=F32) + bias
        s_c = jnp.dot(q2, kcm_ref[h], preferred_element_type=F32)
        return s_w, s_c

    def finish(h, s):
        s_w, s_c = s
        mx = jnp.maximum(jnp.max(s_w, axis=-1, keepdims=True), jnp.max(s_c, axis=-1, keepdims=True))
        e_w = jnp.exp2(s_w - mx)
        e_c = jnp.exp2(s_c - mx)
        l = jnp.sum(e_w, axis=-1, keepdims=True) + jnp.sum(e_c, axis=-1, keepdims=True)
        return (jnp.dot(e_w.astype(BF16), vwm_ref[h, window, :], preferred_element_type=F32)
                + lax.dot_general(e_c.astype(BF16), vcm_ref[h], nt,
                                  preferred_element_type=F32)) * (1.0 / l)

    a_out = _attend_heads(logits, finish, router)
    h2_scr[...] = _post_attention(a_out, sg_ref[...], x_ref[...], mod_ref[0, 0], wout_ref, n2w_ref,
                                  xm_ref)


def _nbr_attention(q, k, v, cache_k, cache_v, tb, sg, x, layer, mod, params, *, batch, seq):
    t_tokens = x.shape[0]
    rows = seq // GRID_W
    tq = Q_ROWS * GRID_W
    assert tq == SRC_TILE
    nt = seq // tq
    past = cache_k.shape[3]
    n_tiles = batch * nt
    out_specs, out_shape, h2_scratch = _post_attention_out(t_tokens)
    tile_of = lambda s: jnp.minimum(s, n_tiles - 1)
    seq_of = lambda s: tile_of(s) // nt
    tok = lambda s: (tile_of(s), 0)
    per_b = lambda s: (seq_of(s), 0)
    lyr3 = lambda s: (layer, 0, 0)
    full2 = lambda s: (0, 0)
    cache_spec = pl.BlockSpec((1, 1, ATTN_WIDTH, past), lambda s: (seq_of(s), layer, 0, 0))
    masked = lambda n: pltpu.VMEM((N_HEADS, n, LANES), BF16)
    masked_t = pltpu.VMEM((N_HEADS, LANES, past), BF16)
    return pl.pallas_call(
        functools.partial(_nbr_attn_kernel, rows=rows, n_tiles=n_tiles),
        grid=(n_tiles + 1,),
        in_specs=[
            pl.BlockSpec((tq, ATTN_WIDTH), tok),
            pl.BlockSpec((seq, ATTN_WIDTH), per_b),
            pl.BlockSpec((seq, ATTN_WIDTH), per_b),
            cache_spec,
            cache_spec,
            pl.BlockSpec((1, N_HEADS, 2 * WIN_H, GRID_W, LANES), lambda s: (layer, 0, 0, 0, 0)),
            pl.BlockSpec((tq, SGU_WIDTH), tok),
            pl.BlockSpec((tq, D_MODEL), tok),
            pl.BlockSpec((1, 1, 6, D_MODEL), lambda s: (layer, 1 + seq_of(s), 0, 0)),
            pl.BlockSpec((1, D_MODEL, D_MODEL), lyr3),
            pl.BlockSpec((1, 1, D_MODEL), lyr3),
            pl.BlockSpec((N_EXPERTS, D_MODEL), full2),
            pl.BlockSpec((N_EXPERTS, 1), full2),
            pl.BlockSpec((SRC_TILE, SRC_TILE), full2),
        ],
        out_specs=out_specs,
        out_shape=out_shape,
        scratch_shapes=[masked(seq), masked(seq), masked_t, masked_t, h2_scratch],
        compiler_params=_cparams("arbitrary"),
        name="nbr_attn",
    )(q, k, v, cache_k, cache_v, tb, sg, x, mod, params["wout"], params["n2w"], params["wr"],
      params["br"], params["tri"])


def _bias_tables(rpb):
    n_rel = 2 * WIN_W - 1
    cols = jnp.arange(GRID_W)
    col_start = jnp.clip(cols - WIN_W // 2, 0, GRID_W - WIN_W)
    col_ok = (cols[None, :] >= col_start[:, None]) & (cols[None, :] < col_start[:, None] + WIN_W)
    rel = cols[None, :] - cols[:, None] + WIN_W - 1
    pick = ((rel[None] == jnp.arange(n_rel)[:, None, None]) & col_ok[None]).astype(F32)
    zero = jnp.zeros_like(pick)
    halves = jnp.concatenate([jnp.concatenate([pick, zero], axis=2),
                              jnp.concatenate([zero, pick], axis=2)], axis=0)
    masked = jnp.where(jnp.concatenate([col_ok, col_ok], axis=1), 0.0, NEG)[None]
    select = jnp.concatenate([halves, masked], axis=0)

    by_offset = jnp.pad(rpb * LOG2E, ((0, 0), (0, 0), (1, 1), (0, 0)))
    feats = jnp.concatenate([by_offset[:, :, :-1], by_offset[:, :, 1:],
                             jnp.ones(by_offset.shape[:2] + (2 * WIN_H, 1), F32)], axis=-1)
    return jnp.einsum("lhdc,cqn->lhdqn", feats, select, precision=lax.Precision.HIGHEST)


def _max_moe_tiles(n_src):
    max_chunks = n_src * SRC_TILE // CHUNK_ROWS + n_src * N_EXPERT_GROUPS
    return -(-max_chunks // TILE_CHUNKS) + N_EXPERT_GROUPS


def _chunk_tables(chunks, n_tiles):
    n_src = chunks.shape[0]
    groups = jnp.arange(N_EXPERT_GROUPS)
    seg_end = jnp.cumsum(chunks, axis=1)
    seg_start = seg_end - chunks
    src_end = jnp.cumsum(chunks, axis=0)
    src_start = src_end - chunks
    total = src_end[-1]
    tiles = (total + TILE_CHUNKS - 1) // TILE_CHUNKS
    tile_end = jnp.cumsum(tiles)
    tile_start = tile_end - tiles
    n_used = tile_end[-1:]

    u = jnp.arange(n_tiles)
    gid = jnp.minimum(jnp.sum(u[:, None] >= tile_end[None, :], axis=1), N_EXPERT_GROUPS - 1)

    p = jnp.arange(n_tiles * TILE_CHUNKS)
    g_hot = jnp.repeat(gid, TILE_CHUNKS)[:, None] == groups[None, :]
    by_group = lambda vec: jnp.sum(jnp.where(g_hot, vec[None, :], 0), axis=1)
    q = p - TILE_CHUNKS * by_group(tile_start)
    live = q < by_group(total)
    src_end_p = jnp.sum(jnp.where(g_hot[:, None, :], src_end[None], 0), axis=2)
    t = jnp.minimum(jnp.sum(q[:, None] >= src_end_p, axis=1), n_src - 1)
    t_hot = t[:, None] == jnp.arange(n_src)[None, :]
    by_seg = lambda tab: jnp.sum(jnp.where(t_hot[:, :, None] & g_hot[:, None, :], tab[None], 0),
                                 axis=(1, 2))
    slab_chunk = by_seg(seg_start) + q - by_seg(src_start)
    src_row = jnp.where(live, t * SLAB_ROWS + CHUNK_ROWS * slab_chunk, 0)

    s = jnp.arange(SLAB_CHUNKS)
    gs = jnp.sum(s[None, :, None] >= seg_end[:, None, :], axis=2)
    used = gs < N_EXPERT_GROUPS
    s_hot = jnp.minimum(gs, N_EXPERT_GROUPS - 1)[:, :, None] == groups
    pick = lambda tab: jnp.sum(jnp.where(s_hot, tab, 0), axis=2)
    pos = (TILE_CHUNKS * pick(tile_start[None, None, :]) + pick(src_start[:, None, :])
           + s[None, :] - pick(seg_start[:, None, :]))
    out_row = jnp.where(used, CHUNK_ROWS * pos, 0).reshape(-1)
    i32 = lambda a: a.astype(jnp.int32)
    return i32(gid), i32(src_row), i32(out_row), i32(n_used)


def _chunk_copy(src_hbm, row, k, buf, sem, slot):
    return pltpu.make_async_copy(
        src_hbm.at[pl.ds(pl.multiple_of(row, CHUNK_ROWS), CHUNK_ROWS), :],
        buf.at[slot, pl.ds(k * CHUNK_ROWS, CHUNK_ROWS), :],
        sem.at[slot])


def _start_chunks(sources, row_ref, first, n_chunks, buf, sem, slot):
    bases = [sum(s.shape[0] for s in sources[:i]) for i in range(len(sources) + 1)]
    for k in range(n_chunks):
        row = row_ref[first + k]
        if len(sources) == 1:
            _chunk_copy(sources[0], row, k, buf, sem, slot).start()
        else:
            for src, lo, hi in zip(sources, bases[:-1], bases[1:]):
                @pl.when((row >= lo) & (row < hi))
                def _(src=src, lo=lo):
                    _chunk_copy(src, row - lo, k, buf, sem, slot).start()


def _gather_step(sources, row_ref, n_chunks, buf, sem):
    step = pl.program_id(0)
    slot = step % 2

    @pl.when(step == 0)
    def _():
        _start_chunks(sources, row_ref, 0, n_chunks, buf, sem, 0)

    @pl.when(step + 1 < pl.num_programs(0))
    def _():
        _start_chunks(sources, row_ref, (step + 1) * n_chunks, n_chunks, buf, sem, 1 - slot)

    for k in range(n_chunks):
        _chunk_copy(sources[0], 0, k, buf, sem, slot).wait()
    return slot


def _moe_kernel(gid_ref, src_ref, nused_ref, *refs, n_slabs):
    slabs = refs[:n_slabs]
    ex_ref, w1_ref, w3_ref, w2_ref, o_ref, buf, sem, w1b, w3b, w2b = refs[n_slabs:]
    u = pl.program_id(0)
    slot = _gather_step(slabs, src_ref, TILE_CHUNKS, buf, sem)

    @pl.when((u == 0) | (gid_ref[u] != gid_ref[jnp.maximum(u - 1, 0)]))
    def _():
        w1b[...] = w1_ref[0].astype(BF16)
        w3b[...] = w3_ref[0].astype(BF16)
        w2b[...] = w2_ref[0, 0].astype(BF16)

    @pl.when(u < nused_ref[0])
    def _():
        h = buf[slot, :, 0:D_MODEL]
        gexp = jnp.dot(buf[slot, :, D_MODEL:], ex_ref[...], preferred_element_type=F32)
        hid = []
        for e in range(EXPERTS_PER_GROUP):
            h1 = jnp.dot(h, w1b[e], preferred_element_type=F32)
            h3 = jnp.dot(h, w3b[e], preferred_element_type=F32)
            act = (h1 * jax.nn.sigmoid(h1)) * h3
            hid.append((act * gexp[:, e * D_EXPERT:(e + 1) * D_EXPERT]).astype(BF16))
        o_ref[...] = jnp.dot(jnp.concatenate(hid, axis=1), w2b[...], preferred_element_type=F32)

    @pl.when(u >= nused_ref[0])
    def _():
        o_ref[...] = jnp.zeros_like(o_ref)


def _moe(slabs, tables, layer, params):
    gid, src_row, _, n_used = tables
    n_tiles = gid.shape[0]
    expert_w = pl.BlockSpec((1, EXPERTS_PER_GROUP, D_MODEL, D_EXPERT),
                            lambda u, gid, src, nu: (layer, gid[u], 0, 0))
    return pl.pallas_call(
        functools.partial(_moe_kernel, n_slabs=len(slabs)),
        grid_spec=pltpu.PrefetchScalarGridSpec(
            num_scalar_prefetch=3,
            grid=(n_tiles,),
            in_specs=[pl.BlockSpec(memory_space=pl.ANY)] * len(slabs) + [
                pl.BlockSpec((LANES, GROUP_HIDDEN), lambda u, gid, src, nu: (0, 0)),
                expert_w,
                expert_w,
                pl.BlockSpec((1, 1, GROUP_HIDDEN, D_MODEL), lambda u, gid, src, nu: (layer, gid[u], 0, 0)),
            ],
            out_specs=pl.BlockSpec((MOE_TILE, D_MODEL), lambda u, gid, src, nu: (u, 0)),
            scratch_shapes=[
                pltpu.VMEM((2, MOE_TILE, SLAB_WIDTH), BF16),
                pltpu.SemaphoreType.DMA((2,)),
                pltpu.VMEM((EXPERTS_PER_GROUP, D_MODEL, D_EXPERT), BF16),
                pltpu.VMEM((EXPERTS_PER_GROUP, D_MODEL, D_EXPERT), BF16),
                pltpu.VMEM((GROUP_HIDDEN, D_MODEL), BF16),
            ],
        ),
        out_shape=jax.ShapeDtypeStruct((n_tiles * MOE_TILE, D_MODEL), F32),
        compiler_params=_cparams("arbitrary"),
        name="moe",
    )(gid, src_row, n_used, *slabs, params["expand"], params["w1"], params["w3"], params["w2"])


def _unsorted_moe(row_ref, dest_ref, y_hbm, buf, sem, n_slabs):
    slot = _gather_step([y_hbm], row_ref, n_slabs * SLAB_CHUNKS, buf, sem)
    tn = (((0,), (0,)), ((), ()))
    out = []
    for j in range(n_slabs):
        y = buf[slot, j * SLAB_ROWS:(j + 1) * SLAB_ROWS, :]
        hi = y.astype(BF16)
        lo = (y - hi.astype(F32)).astype(BF16)
        perm = _permutation(dest_ref[j])
        out.append(lax.dot_general(perm, hi, tn, preferred_element_type=F32)
                   + lax.dot_general(perm, lo, tn, preferred_element_type=F32))
    return jnp.concatenate(out, axis=0)


def _unsort_kernel(row_ref, xm_ref, dest_ref, mod_ref, y_hbm, o_ref, buf, sem, *, n_slabs):
    moe = _unsorted_moe(row_ref, dest_ref, y_hbm, buf, sem, n_slabs)
    o_ref[...] = xm_ref[...] + mod_ref[0, 0][5:6] * moe


def _unsort(pending, layer, mod, row_of_tile, *, tm):
    xm, dest, out_row, y = pending
    t_tokens = xm.shape[0]
    n_slabs = tm // SRC_TILE
    return pl.pallas_call(
        functools.partial(_unsort_kernel, n_slabs=n_slabs),
        grid_spec=pltpu.PrefetchScalarGridSpec(
            num_scalar_prefetch=1,
            grid=(t_tokens // tm,),
            in_specs=[
                pl.BlockSpec((tm, D_MODEL), lambda t, rows: (t, 0)),
                pl.BlockSpec((n_slabs, 1, SRC_TILE), lambda t, rows: (t, 0, 0)),
                pl.BlockSpec((1, 1, 6, D_MODEL), lambda t, rows: (layer, row_of_tile(t), 0, 0)),
                pl.BlockSpec(memory_space=pl.ANY),
            ],
            out_specs=pl.BlockSpec((tm, D_MODEL), lambda t, rows: (t, 0)),
            scratch_shapes=[pltpu.VMEM((2, n_slabs * SLAB_ROWS, D_MODEL), F32),
                            pltpu.SemaphoreType.DMA((2,))],
        ),
        out_shape=jax.ShapeDtypeStruct((t_tokens, D_MODEL), F32),
        compiler_params=_cparams("arbitrary"),
        name="unsort",
    )(out_row, xm, dest, mod, y)


def kernel(x_prompt, x_sample, cache_k, cache_v, c, c_ctx, w_ada, b_ada, norm1_w, norm2_w, w_in,
           q_norm_w, k_norm_w, rpb, w_sgu, b_sgu, sgu_ln_w, sgu_ln_b, w_out, w_router, b_router,
           w1, w3, w2):
    batch, seq, _ = x_prompt.shape
    dec_batch, dec_seq, _ = x_sample.shape
    past = cache_k.shape[2]
    assert 1 + dec_batch <= COND_ROWS and dec_seq % (Q_ROWS * GRID_W) == 0 and seq % CHUNK == 0

    cond = jnp.zeros((COND_ROWS, D_MODEL), F32).at[0].set(c_ctx).at[1:1 + dec_batch].set(c)
    mod = _adaln(cond, w_ada, b_ada).reshape(DEPTH, COND_ROWS, 6, D_MODEL)

    xp = x_prompt.reshape(batch * seq, D_MODEL)
    xs = x_sample.reshape(dec_batch * dec_seq, D_MODEL)
    from_cache_layout = lambda a: a.transpose(0, 1, 3, 4, 2).reshape(dec_batch, DEPTH, ATTN_WIDTH, past)
    ck = from_cache_layout(cache_k)
    cv = from_cache_layout(cache_v)

    head_of = jnp.arange(256) // HEAD_DIM
    bd = (head_of[:, None] == head_of[None, :]).astype(BF16)
    col_expert = jnp.arange(GROUP_HIDDEN) // D_EXPERT
    gate_lane = jnp.arange(LANES)
    expand = ((gate_lane[:, None] % EXPERTS_PER_GROUP == col_expert[None, :])
              & (gate_lane[:, None] < 2 * EXPERTS_PER_GROUP)).astype(BF16)
    tok_id = jnp.arange(SRC_TILE)
    tri = (tok_id[:, None] <= tok_id[None, :]).astype(BF16)

    params = dict(
        n1w=norm1_w.reshape(DEPTH, 1, D_MODEL),
        n2w=norm2_w.reshape(DEPTH, 1, D_MODEL),
        win=w_in.astype(BF16),
        wout=w_out.astype(BF16),
        qw=jnp.tile(q_norm_w, (1, N_HEADS)).reshape(DEPTH, 1, ATTN_WIDTH),
        kw=jnp.tile(k_norm_w, (1, N_HEADS)).reshape(DEPTH, 1, ATTN_WIDTH),
        bd=jnp.concatenate([bd, bd], axis=0),
        ws=w_sgu.astype(BF16),
        bs=jnp.broadcast_to(b_sgu[..., None], (DEPTH, SGU_GROUPS, CHUNK, SGU_GROUP_DIM)),
        lnw=sgu_ln_w,
        lnb=sgu_ln_b,
        wr=w_router.T.astype(BF16),
        br=b_router.reshape(N_EXPERTS, 1),
        expand=expand,
        tri=tri,
        w1=w1,
        w3=w3,
        w2=w2.reshape(DEPTH, N_EXPERT_GROUPS, GROUP_HIDDEN, D_MODEL),
    )
    tb = _bias_tables(rpb)

    tm = 2 * SRC_TILE
    assert dec_seq % tm == 0 and tm % seq == 0
    lat_tiles_per_seq = dec_seq // tm
    ctx_row = lambda i: 0
    lat_row = lambda i: 1 + i // lat_tiles_per_seq

    def moe_block(attended, layer):
        chunks = jnp.concatenate([meta[:, :N_EXPERT_GROUPS, 0] for _, _, _, meta in attended], axis=0)
        tables = _chunk_tables(chunks, _max_moe_tiles(chunks.shape[0]))
        y = _moe([slab for _, slab, _, _ in attended], tables, layer, params)
        pending, first = [], 0
        for xm, _, dest, meta in attended:
            n_rows = meta.shape[0] * SLAB_CHUNKS
            pending.append((xm, dest, tables[2][first:first + n_rows], y))
            first += n_rows
        return pending

    kv_new = jax.ShapeDtypeStruct((batch, DEPTH, ATTN_WIDTH, seq), F32)

    for l in range(DEPTH):
        outs = _inproj(xp, l, mod, ctx_row, params, tm=tm,
                       kv_buffers=kv_new if l == 0 else (kbuf, vbuf))
        if l > 0:
            xp, outs = outs[0], outs[1:]
        q, kbuf, vbuf, v, sg = outs
        attended_ctx = _ctx_attention(q, kbuf, v, sg, xp, l, mod, params, seq=seq)

        outs = _inproj(xs, l, mod, lat_row, params, tm=tm)
        if l > 0:
            xs, outs = outs[0], outs[1:]
        q, k, v, sg = outs
        attended_lat = _nbr_attention(q, k, v, ck, cv, tb, sg, xs, l, mod, params,
                                      batch=dec_batch, seq=dec_seq)

        xp, xs = moe_block([attended_ctx, attended_lat], l)
    xp = _unsort(xp, DEPTH - 1, mod, ctx_row, tm=tm)
    xs = _unsort(xs, DEPTH - 1, mod, lat_row, tm=tm)

    to_cache_layout = lambda buf: buf.reshape(batch, DEPTH, N_HEADS, HEAD_DIM, seq).transpose(0, 1, 4, 2, 3)
    return (xp.reshape(batch, seq, D_MODEL), xs.reshape(dec_batch, dec_seq, D_MODEL),
            to_cache_layout(kbuf), to_cache_layout(vbuf))
```

```python
import functools

import jax
import jax.numpy as jnp
from jax import lax
from jax.experimental import pallas as pl
from jax.experimental.pallas import tpu as pltpu

F32 = jnp.float32
BF16 = jnp.bfloat16

D_MODEL = 1024
DEPTH = 4
N_HEADS = 8
HEAD_DIM = 64
ATTN_WIDTH = N_HEADS * HEAD_DIM
SGU_GROUPS = 4
SGU_GROUP_DIM = 128
SGU_WIDTH = SGU_GROUPS * SGU_GROUP_DIM
CHUNK = 128
IN_WIDTH = 3 * ATTN_WIDTH + 2 * SGU_WIDTH
GRID_W = 64
WIN_H = 8
WIN_W = 16
N_EXPERTS = 16
N_EXPERT_GROUPS = 4
EXPERTS_PER_GROUP = 4
D_EXPERT = 256
GROUP_HIDDEN = EXPERTS_PER_GROUP * D_EXPERT
EPS = 1e-6
NEG = -1e30
LOG2E = 1.4426950408889634

LANES = 128
HEADS_PER_VREG = LANES // HEAD_DIM
N_HEAD_PAIRS = N_HEADS // HEADS_PER_VREG
COND_ROWS = 16
Q_ROWS = 4
WIN_ROWS = Q_ROWS + WIN_H
VMEM_LIMIT = 48 * 1024 * 1024

SRC_TILE = 256
CHUNK_ROWS = 16
SLAB_ROWS = SRC_TILE + N_EXPERT_GROUPS * CHUNK_ROWS
SLAB_CHUNKS = SLAB_ROWS // CHUNK_ROWS
SLAB_WIDTH = D_MODEL + LANES
MOE_TILE = 512
TILE_CHUNKS = MOE_TILE // CHUNK_ROWS


def _cparams(*sem):
    return pltpu.CompilerParams(dimension_semantics=sem, vmem_limit_bytes=VMEM_LIMIT)


def _adaln_kernel(cond_ref, w_ref, b_ref, o_ref):
    c = cond_ref[...]
    a = (c * jax.nn.sigmoid(c)).astype(BF16)
    o_ref[0] = jnp.dot(a, w_ref[0].astype(BF16), preferred_element_type=F32) + b_ref[0]


def _adaln(cond, w_ada, b_ada):
    tn = 1536
    n = 6 * D_MODEL
    return pl.pallas_call(
        _adaln_kernel,
        grid=(DEPTH, n // tn),
        in_specs=[
            pl.BlockSpec((COND_ROWS, D_MODEL), lambda l, j: (0, 0)),
            pl.BlockSpec((1, D_MODEL, tn), lambda l, j: (l, 0, j)),
            pl.BlockSpec((1, 1, tn), lambda l, j: (l, 0, j)),
        ],
        out_specs=pl.BlockSpec((1, COND_ROWS, tn), lambda l, j: (l, 0, j)),
        out_shape=jax.ShapeDtypeStruct((DEPTH, COND_ROWS, n), F32),
        compiler_params=_cparams("arbitrary", "arbitrary"),
        name="adaln",
    )(cond, w_ada, b_ada.reshape(DEPTH, 1, n))


def _gelu_tanh(x):
    return 0.5 * x * (1.0 + jnp.tanh(0.7978845608028654 * (x + 0.044715 * (x * x * x))))


def _head_rms(t, bd, w):
    t2 = t * t
    hi = t2.astype(BF16)
    lo = (t2 - hi.astype(F32)).astype(BF16)
    outs = []
    for c in range(ATTN_WIDTH // 256):
        sl = slice(256 * c, 256 * c + 256)
        hl = jnp.concatenate([hi[:, sl], lo[:, sl]], axis=1)
        ss = jnp.dot(hl, bd, preferred_element_type=F32)
        outs.append(t[:, sl] * lax.rsqrt(ss * (1.0 / HEAD_DIM) + EPS))
    return jnp.concatenate(outs, axis=1) * w


def _inproj_kernel(*refs, tm, transposed_kv, new_kv_layer, fused_moe):
    refs = list(refs)
    if fused_moe:
        row_ref, xm_ref, dest_ref, prev_mod_ref, y_hbm = refs[:5]
        refs = refs[5:]
    else:
        x_ref = refs.pop(0)
    (mod_ref, n1w_ref, win_ref, qw_ref, kw_ref, bd_ref, ws_ref, bs_ref, lnw_ref, lnb_ref) = refs[:10]
    refs = refs[10:]
    if transposed_kv and new_kv_layer is None:
        refs = refs[2:]
    if fused_moe:
        x_out_ref = refs.pop(0)
    if transposed_kv:
        q_ref, kt_ref, vt_ref, v_ref, sg_ref = refs[:5]
        refs = refs[5:]
    else:
        q_ref, k_ref, v_ref, sg_ref = refs[:4]
        refs = refs[4:]
    if fused_moe:
        buf, sem = refs
        moe = _unsorted_moe(row_ref, dest_ref, y_hbm, buf, sem, tm // SRC_TILE)
        x = xm_ref[...] + prev_mod_ref[0, 0][5:6] * moe
        x_out_ref[...] = x
    else:
        x = x_ref[...]
    m = mod_ref[0, 0]
    sh1, sc1 = m[0:1], m[1:2]
    ms = jnp.mean(x * x, axis=-1, keepdims=True)
    h = x * lax.rsqrt(ms + EPS) * n1w_ref[0]
    h = (h * (1.0 + sc1) + sh1).astype(BF16)
    p = jnp.dot(h, win_ref[0], preferred_element_type=F32)

    bd = bd_ref[...]
    q = _head_rms(p[:, 0:ATTN_WIDTH], bd, qw_ref[0])
    k = _head_rms(p[:, ATTN_WIDTH:2 * ATTN_WIDTH], bd, kw_ref[0])
    v = p[:, 2 * ATTN_WIDTH:3 * ATTN_WIDTH]
    q_ref[...] = (q * (HEAD_DIM ** -0.5 * LOG2E)).astype(q_ref.dtype)
    if transposed_kv:
        seq = kt_ref.shape[3]
        if new_kv_layer is None:
            slot = 0
        else:
            slot = new_kv_layer
            kt_ref[...] = jnp.zeros_like(kt_ref)
            vt_ref[...] = jnp.zeros_like(vt_ref)
        for j in range(tm // seq):
            kt_ref[j, slot] = k[j * seq:(j + 1) * seq].T
            vt_ref[j, slot] = v[j * seq:(j + 1) * seq].T
    else:
        k_ref[...] = k.astype(k_ref.dtype)
    v_ref[...] = v.astype(v_ref.dtype)

    off_u = 3 * ATTN_WIDTH
    off_v = off_u + SGU_WIDTH
    for g in range(SGU_GROUPS):
        gl = slice(g * SGU_GROUP_DIM, (g + 1) * SGU_GROUP_DIM)
        u = _gelu_tanh(p[:, off_u + g * SGU_GROUP_DIM: off_u + (g + 1) * SGU_GROUP_DIM])
        t = _gelu_tanh(p[:, off_v + g * SGU_GROUP_DIM: off_v + (g + 1) * SGU_GROUP_DIM])
        mu = jnp.mean(t, axis=-1, keepdims=True)
        d = t - mu
        var = jnp.mean(d * d, axis=-1, keepdims=True)
        y = (d * lax.rsqrt(var + EPS) * lnw_ref[0, g:g + 1, :] + lnb_ref[0, g:g + 1, :]).astype(BF16)
        for c in range(tm // CHUNK):
            rows = slice(c * CHUNK, (c + 1) * CHUNK)
            sv = jnp.dot(ws_ref[0, g], y[rows], preferred_element_type=F32) + bs_ref[0, g]
            sg_ref[rows, gl] = (u[rows] * sv).astype(sg_ref.dtype)


def _inproj(x, layer, mod, row_of_tile, params, *, tm, kv_buffers=None):
    fused_moe = isinstance(x, tuple)
    tok = lambda i, *_: (i, 0)
    lyr3 = lambda i, *_: (layer, 0, 0)
    lyr4 = lambda i, *_: (layer, 0, 0, 0)
    mod_row = lambda lyr: pl.BlockSpec((1, 1, 6, D_MODEL), lambda i, *_: (lyr, row_of_tile(i), 0, 0))
    x_spec = pl.BlockSpec((tm, D_MODEL), tok)
    if fused_moe:
        n_slabs = tm // SRC_TILE
        xm, dest, out_row, y = x
        t_tokens = xm.shape[0]
        prefetch = [out_row]
        in_specs = [x_spec, pl.BlockSpec((n_slabs, 1, SRC_TILE), lambda i, *_: (i, 0, 0)),
                    mod_row(layer - 1), pl.BlockSpec(memory_space=pl.ANY)]
        args = [xm, dest, mod, y]
        scratch = [pltpu.VMEM((2, n_slabs * SLAB_ROWS, D_MODEL), F32), pltpu.SemaphoreType.DMA((2,))]
    else:
        t_tokens = x.shape[0]
        prefetch, in_specs, args, scratch = [], [x_spec], [x], []
    act = pl.BlockSpec((tm, ATTN_WIDTH), tok)
    act_sds = jax.ShapeDtypeStruct((t_tokens, ATTN_WIDTH), BF16)
    in_specs += [
        mod_row(layer),
        pl.BlockSpec((1, 1, D_MODEL), lyr3),
        pl.BlockSpec((1, D_MODEL, IN_WIDTH), lyr3),
        pl.BlockSpec((1, 1, ATTN_WIDTH), lyr3),
        pl.BlockSpec((1, 1, ATTN_WIDTH), lyr3),
        pl.BlockSpec((512, 256), lambda i, *_: (0, 0)),
        pl.BlockSpec((1, SGU_GROUPS, CHUNK, CHUNK), lyr4),
        pl.BlockSpec((1, SGU_GROUPS, CHUNK, SGU_GROUP_DIM), lyr4),
        pl.BlockSpec((1, SGU_GROUPS, SGU_GROUP_DIM), lyr3),
        pl.BlockSpec((1, SGU_GROUPS, SGU_GROUP_DIM), lyr3),
    ]
    args += [mod, params["n1w"], params["win"], params["qw"], params["kw"], params["bd"],
             params["ws"], params["bs"], params["lnw"], params["lnb"]]
    out_specs = [x_spec] if fused_moe else []
    out_shape = [jax.ShapeDtypeStruct((t_tokens, D_MODEL), F32)] if fused_moe else []
    if kv_buffers is None:
        out_specs += [act] * 4
        out_shape += [act_sds] * 4
        aliases = {}
    elif isinstance(kv_buffers, jax.ShapeDtypeStruct):
        seq = kv_buffers.shape[3]
        kv_spec = pl.BlockSpec((tm // seq,) + kv_buffers.shape[1:], lambda i, *_: (i, 0, 0, 0))
        out_specs += [act, kv_spec, kv_spec, act, act]
        out_shape += [act_sds, kv_buffers, kv_buffers, act_sds, act_sds]
        aliases = {}
    else:
        kbuf, vbuf = kv_buffers
        seq = kbuf.shape[3]
        kv_spec = pl.BlockSpec((tm // seq, 1, ATTN_WIDTH, seq), lambda i, *_: (i, layer, 0, 0))
        first_alias_in = len(prefetch) + len(args)
        first_alias_out = len(out_specs) + 1
        in_specs += [pl.BlockSpec(memory_space=pl.ANY)] * 2
        args += [kbuf, vbuf]
        out_specs += [act, kv_spec, kv_spec, act, act]
        out_shape += [act_sds, jax.ShapeDtypeStruct(kbuf.shape, F32),
                      jax.ShapeDtypeStruct(vbuf.shape, F32), act_sds, act_sds]
        aliases = {first_alias_in: first_alias_out, first_alias_in + 1: first_alias_out + 1}
    return pl.pallas_call(
        functools.partial(_inproj_kernel, tm=tm, transposed_kv=kv_buffers is not None,
                          new_kv_layer=layer if isinstance(kv_buffers, jax.ShapeDtypeStruct) else None,
                          fused_moe=fused_moe),
        grid_spec=pltpu.PrefetchScalarGridSpec(
            num_scalar_prefetch=len(prefetch),
            grid=(t_tokens // tm,),
            in_specs=in_specs,
            out_specs=out_specs,
            scratch_shapes=scratch,
        ),
        out_shape=out_shape,
        input_output_aliases=aliases,
        compiler_params=_cparams("arbitrary"),
        name="inproj",
    )(*prefetch, *args)


def _route(lg_t, br):
    s = jax.nn.sigmoid(lg_t)
    sel = s + br
    row = lambda a, i: a[i:i + 1]
    g_score = []
    for g in range(N_EXPERT_GROUPS):
        v = [row(sel, 4 * g + i) for i in range(4)]
        best_pair = None
        for i in range(4):
            for j in range(i + 1, 4):
                pair = v[i] + v[j]
                best_pair = pair if best_pair is None else jnp.maximum(best_pair, pair)
        g_score.append(best_pair)
    best = jnp.zeros_like(g_score[0], dtype=jnp.int32)
    top = g_score[0]
    for g in range(1, N_EXPERT_GROUPS):
        upd = g_score[g] > top
        best = jnp.where(upd, g, best)
        top = jnp.where(upd, g_score[g], top)
    cand, aff = [], []
    for i in range(4):
        ci, si = row(sel, i), row(s, i)
        for g in range(1, N_EXPERT_GROUPS):
            ci = jnp.where(best == g, row(sel, 4 * g + i), ci)
            si = jnp.where(best == g, row(s, 4 * g + i), si)
        cand.append(ci)
        aff.append(si)

    def first_argmax(vals):
        idx = jnp.zeros_like(best)
        top_v = vals[0]
        for i in range(1, 4):
            upd = vals[i] > top_v
            idx = jnp.where(upd, i, idx)
            top_v = jnp.where(upd, vals[i], top_v)
        return idx

    i1 = first_argmax(cand)
    i2 = first_argmax([jnp.where(i1 == i, -jnp.inf, cand[i]) for i in range(4)])
    pick = lambda idx: sum(jnp.where(idx == i, aff[i], 0.0) for i in range(4))
    den = pick(i1) + pick(i2)
    gate = [jnp.where((i1 == i) | (i2 == i), aff[i] / den, 0.0) for i in range(4)]
    hi = [x.astype(BF16) for x in gate]
    lo = [(x - h.astype(F32)).astype(BF16) for x, h in zip(gate, hi)]
    return best, hi + lo


def _slab_positions(best, tri):
    n = best.shape[1]
    onehot = [jnp.where(best == g, 1.0, 0.0) for g in range(N_EXPERT_GROUPS)]
    pad = [jnp.zeros_like(onehot[0])] * (8 - N_EXPERT_GROUPS)
    oh = jnp.concatenate(onehot + pad, axis=0).astype(BF16)
    counts = jnp.dot(oh, tri, preferred_element_type=F32)
    dest = jnp.zeros((1, n), F32)
    seg_start = jnp.zeros((1, 1), F32)
    chunks = []
    for g in range(N_EXPERT_GROUPS):
        cg = counts[g:g + 1]
        n_g = jnp.max(cg, axis=1, keepdims=True)
        c_g = jnp.floor((n_g + (CHUNK_ROWS - 1)) * (1.0 / CHUNK_ROWS))
        dest = dest + onehot[g] * (seg_start + cg - 1.0)
        seg_start = seg_start + c_g * CHUNK_ROWS
        chunks.append(c_g)
    return dest.astype(jnp.int32), chunks


def _permutation(dest):
    n = dest.shape[1]
    hit = lax.broadcasted_iota(jnp.int32, (SLAB_ROWS, n), 0) == dest
    return jnp.where(hit, 1.0, 0.0).astype(BF16)


def _post_attention(a, sg, x, m, wout_ref, n2w_ref, xm_ref):
    g1, sh2, sc2 = m[2:3], m[3:4], m[4:5]
    cat = jnp.concatenate([a, sg], axis=1)
    y = jnp.dot(cat, wout_ref[0], preferred_element_type=F32)
    xm = x + g1 * y
    xm_ref[...] = xm
    ms = jnp.mean(xm * xm, axis=-1, keepdims=True)
    h2 = xm * lax.rsqrt(ms + EPS) * n2w_ref[0]
    return (h2 * (1.0 + sc2) + sh2).astype(BF16)


def _route_previous(h2_scr, wr_ref, br_ref, tri_ref, slab_ref, dest_ref, meta_ref):
    h2 = h2_scr[...]
    lg_t = lax.dot_general(wr_ref[...], h2, (((1,), (1,)), ((), ())), preferred_element_type=F32)
    yield
    best, gates = _route(lg_t, br_ref[...])
    dest, chunks = _slab_positions(best, tri_ref[...])
    yield
    perm = _permutation(dest)
    slab_ref[:, 0:D_MODEL] = jnp.dot(perm, h2, preferred_element_type=F32).astype(BF16)
    gmat = jnp.concatenate(gates + [jnp.zeros((LANES - len(gates), dest.shape[1]), BF16)], axis=0)
    slab_ref[:, D_MODEL:] = lax.dot_general(perm, gmat, (((1,), (1,)), ((), ())),
                                            preferred_element_type=F32).astype(BF16)
    dest_ref[0] = dest
    meta = [jnp.broadcast_to(c, (1, LANES)) for c in chunks]
    meta += [jnp.zeros((8 - len(chunks), LANES), F32)]
    meta_ref[0] = jnp.concatenate(meta, axis=0).astype(jnp.int32)


def _pair_pos(axis):
    return lax.broadcasted_iota(jnp.int32, (1, LANES) if axis == 1 else (LANES, 1), axis)


def _head_mask(hh, axis):
    pos = _pair_pos(axis)
    return (pos >= hh * HEAD_DIM) & (pos < (hh + 1) * HEAD_DIM)


def _only_head(x2, hh, axis):
    return jnp.where(_head_mask(hh, axis), x2, jnp.zeros_like(x2))


def _attend_heads(logits, finish, router):
    outs, even = [], None
    s_next = logits(0)
    for h in range(N_HEADS):
        s = s_next
        if h + 1 < N_HEADS:
            s_next = logits(h + 1)
        if h == N_HEADS // 2:
            next(router)
        o = finish(h, s)
        if h % HEADS_PER_VREG == 0:
            even = o
        else:
            outs.append((even + o).astype(BF16))
    for _ in router:
        pass
    return jnp.concatenate(outs, axis=1)


def _ctx_attn_kernel(q_ref, kt_ref, v_ref, sg_ref, x_ref, mod_ref, wout_ref, n2w_ref, wr_ref, br_ref,
                     tri_ref, xm_ref, slab_ref, dest_ref, meta_ref, h2_scr):
    step = pl.program_id(0)

    @pl.when(step == 0)
    def _():
        h2_scr[...] = jnp.zeros_like(h2_scr)

    def logits(h):
        cols = slice(h // HEADS_PER_VREG * LANES, (h // HEADS_PER_VREG + 1) * LANES)
        kt2 = kt_ref[0, 0, cols, :].astype(BF16)
        return jnp.dot(q_ref[:, cols], _only_head(kt2, h % HEADS_PER_VREG, 0),
                       preferred_element_type=F32)

    def finish(h, s):
        cols = slice(h // HEADS_PER_VREG * LANES, (h // HEADS_PER_VREG + 1) * LANES)
        e = jnp.exp2(s - jnp.max(s, axis=-1, keepdims=True))
        l = jnp.sum(e, axis=-1, keepdims=True)
        vm = _only_head(v_ref[:, cols], h % HEADS_PER_VREG, 1)
        return jnp.dot(e.astype(BF16), vm, preferred_element_type=F32) * (1.0 / l)

    new_router = lambda: _route_previous(h2_scr, wr_ref, br_ref, tri_ref, slab_ref, dest_ref, meta_ref)
    last = pl.num_programs(0) - 1

    @pl.when(step < last)
    def _():
        router = new_router()
        next(router)
        a = _attend_heads(logits, finish, router)
        h2_scr[...] = _post_attention(a, sg_ref[...], x_ref[...], mod_ref[0, 0], wout_ref, n2w_ref,
                                      xm_ref)

    @pl.when(step == last)
    def _():
        for _ in new_router():
            pass


def _post_attention_out(t_tokens):
    n_src = t_tokens // SRC_TILE
    prev = lambda s: jnp.maximum(s - 1, 0)
    specs = [
        pl.BlockSpec((SRC_TILE, D_MODEL), lambda s: (jnp.minimum(s, n_src - 1), 0)),
        pl.BlockSpec((SLAB_ROWS, SLAB_WIDTH), lambda s: (prev(s), 0)),
        pl.BlockSpec((1, 1, SRC_TILE), lambda s: (prev(s), 0, 0)),
        pl.BlockSpec((1, 8, LANES), lambda s: (prev(s), 0, 0)),
    ]
    shapes = [
        jax.ShapeDtypeStruct((t_tokens, D_MODEL), F32),
        jax.ShapeDtypeStruct((n_src * SLAB_ROWS, SLAB_WIDTH), BF16),
        jax.ShapeDtypeStruct((n_src, 1, SRC_TILE), jnp.int32),
        jax.ShapeDtypeStruct((n_src, 8, LANES), jnp.int32),
    ]
    return specs, shapes, pltpu.VMEM((SRC_TILE, D_MODEL), BF16)


def _ctx_attention(q, kbuf, v, sg, x, layer, mod, params, *, seq):
    assert seq == SRC_TILE
    t_tokens = x.shape[0]
    n_seq = t_tokens // seq
    seq_of = lambda s: jnp.minimum(s, n_seq - 1)
    tok = lambda s: (seq_of(s), 0)
    lyr3 = lambda s: (layer, 0, 0)
    full2 = lambda s: (0, 0)
    out_specs, out_shape, h2_scratch = _post_attention_out(t_tokens)
    return pl.pallas_call(
        _ctx_attn_kernel,
        grid=(n_seq + 1,),
        in_specs=[
            pl.BlockSpec((seq, ATTN_WIDTH), tok),
            pl.BlockSpec((1, 1, ATTN_WIDTH, seq), lambda s: (seq_of(s), layer, 0, 0)),
            pl.BlockSpec((seq, ATTN_WIDTH), tok),
            pl.BlockSpec((seq, SGU_WIDTH), tok),
            pl.BlockSpec((seq, D_MODEL), tok),
            pl.BlockSpec((1, 1, 6, D_MODEL), lambda s: (layer, 0, 0, 0)),
            pl.BlockSpec((1, D_MODEL, D_MODEL), lyr3),
            pl.BlockSpec((1, 1, D_MODEL), lyr3),
            pl.BlockSpec((N_EXPERTS, D_MODEL), full2),
            pl.BlockSpec((N_EXPERTS, 1), full2),
            pl.BlockSpec((SRC_TILE, SRC_TILE), full2),
        ],
        out_specs=out_specs,
        out_shape=out_shape,
        scratch_shapes=[h2_scratch],
        compiler_params=_cparams("arbitrary"),
        name="ctx_attn",
    )(q, kbuf, v, sg, x, mod, params["wout"], params["n2w"], params["wr"], params["br"], params["tri"])


def _row_start(r, rows):
    return jnp.clip(r - WIN_H // 2, 0, rows - WIN_H)


def _nbr_attn_kernel(q_ref, k_ref, v_ref, ck_ref, cv_ref, tb_ref, sg_ref, x_ref, mod_ref, wout_ref,
                     n2w_ref, wr_ref, br_ref, tri_ref, xm_ref, slab_ref, dest_ref, meta_ref,
                     kwm_ref, vwm_ref, kcm_ref, vcm_ref, h2_scr, *, rows, n_tiles):
    step = pl.program_id(0)
    tiles_per_seq = rows // Q_ROWS
    t = jnp.minimum(step, n_tiles - 1) % tiles_per_seq

    @pl.when(step == 0)
    def _():
        h2_scr[...] = jnp.zeros_like(h2_scr)

    @pl.when((t == 0) & (step < n_tiles))
    def _():
        for j in range(N_HEAD_PAIRS):
            cols = slice(j * LANES, (j + 1) * LANES)
            k2, v2 = k_ref[:, cols], v_ref[:, cols]
            ck2 = ck_ref[0, 0, cols, :].astype(BF16)
            cv2 = cv_ref[0, 0, cols, :].astype(BF16)
            for hh in range(HEADS_PER_VREG):
                h = HEADS_PER_VREG * j + hh
                kwm_ref[h] = _only_head(k2, hh, 1)
                vwm_ref[h] = _only_head(v2, hh, 1)
                kcm_ref[h] = _only_head(ck2, hh, 0)
                vcm_ref[h] = _only_head(cv2, hh, 0)

    new_router = lambda: _route_previous(h2_scr, wr_ref, br_ref, tri_ref, slab_ref, dest_ref, meta_ref)

    @pl.when(step == n_tiles)
    def _():
        for _ in new_router():
            pass

    @pl.when(step < n_tiles)
    def _():
        _attend_tile(t, new_router(), q_ref, tb_ref, sg_ref, x_ref, mod_ref, wout_ref, n2w_ref, xm_ref,
                     kwm_ref, vwm_ref, kcm_ref, vcm_ref, h2_scr, rows)


def _attend_tile(t, router, q_ref, tb_ref, sg_ref, x_ref, mod_ref, wout_ref, n2w_ref, xm_ref,
                 kwm_ref, vwm_ref, kcm_ref, vcm_ref, h2_scr, rows):
    next(router)
    r0 = t * Q_ROWS
    ws = jnp.minimum(_row_start(r0, rows), rows - WIN_ROWS)
    tok0 = pl.multiple_of(ws * GRID_W, GRID_W)

    lane = lax.broadcasted_iota(jnp.int32, (1, LANES), 1)
    blk_idx, row_mask = [], []
    for a in range(Q_ROWS):
        r = r0 + a
        rs = _row_start(r, rows)
        idx_a, mask_a = [], []
        for jp in range(WIN_ROWS // 2):
            kr = ws + 2 * jp
            idx_a.append(jnp.clip(kr - r + WIN_H, 0, 2 * WIN_H - 1))
            ok0 = (kr >= rs) & (kr < rs + WIN_H)
            ok1 = (kr + 1 >= rs) & (kr + 1 < rs + WIN_H)
            m0 = jnp.where(ok0, 0.0, NEG).astype(F32)
            m1 = jnp.where(ok1, 0.0, NEG).astype(F32)
            mask_a.append(jnp.where(lane < GRID_W, m0, m1))
        blk_idx.append(idx_a)
        row_mask.append(mask_a)

    nt = (((1,), (1,)), ((), ()))
    window = pl.ds(tok0, WIN_ROWS * GRID_W)

    def logits(h):
        q2 = q_ref[:, h // HEADS_PER_VREG * LANES:(h // HEADS_PER_VREG + 1) * LANES]
        bias = jnp.concatenate([
            jnp.concatenate([tb_ref[0, h, blk_idx[a][jp]] + row_mask[a][jp]
                             for jp in range(WIN_ROWS // 2)], axis=1)
            for a in range(Q_ROWS)], axis=0)
        s_w = lax.dot_general(q2, kwm_ref[h, window, :], nt, preferred_element_type=F32) + bias
        s_c = jnp.dot(q2, kcm_ref[h], preferred_element_type=F32)
        return s_w, s_c

    def finish(h, s):
        s_w, s_c = s
        mx = jnp.maximum(jnp.max(s_w, axis=-1, keepdims=True), jnp.max(s_c, axis=-1, keepdims=True))
        e_w = jnp.exp2(s_w - mx)
        e_c = jnp.exp2(s_c - mx)
        l = jnp.sum(e_w, axis=-1, keepdims=True) + jnp.sum(e_c, axis=-1, keepdims=True)
        return (jnp.dot(e_w.astype(BF16), vwm_ref[h, window, :], preferred_element_type=F32)
                + lax.dot_general(e_c.astype(BF16), vcm_ref[h], nt,
                                  preferred_element_type=F32)) * (1.0 / l)

    a_out = _attend_heads(logits, finish, router)
    h2_scr[...] = _post_attention(a_out, sg_ref[...], x_ref[...], mod_ref[0, 0], wout_ref, n2w_ref,
                                  xm_ref)


def _nbr_attention(q, k, v, cache_k, cache_v, tb, sg, x, layer, mod, params, *, batch, seq):
    t_tokens = x.shape[0]
    rows = seq // GRID_W
    tq = Q_ROWS * GRID_W
    assert tq == SRC_TILE
    nt = seq // tq
    past = cache_k.shape[3]
    n_tiles = batch * nt
    out_specs, out_shape, h2_scratch = _post_attention_out(t_tokens)
    tile_of = lambda s: jnp.minimum(s, n_tiles - 1)
    seq_of = lambda s: tile_of(s) // nt
    tok = lambda s: (tile_of(s), 0)
    per_b = lambda s: (seq_of(s), 0)
    lyr3 = lambda s: (layer, 0, 0)
    full2 = lambda s: (0, 0)
    cache_spec = pl.BlockSpec((1, 1, ATTN_WIDTH, past), lambda s: (seq_of(s), layer, 0, 0))
    masked = lambda n: pltpu.VMEM((N_HEADS, n, LANES), BF16)
    masked_t = pltpu.VMEM((N_HEADS, LANES, past), BF16)
    return pl.pallas_call(
        functools.partial(_nbr_attn_kernel, rows=rows, n_tiles=n_tiles),
        grid=(n_tiles + 1,),
        in_specs=[
            pl.BlockSpec((tq, ATTN_WIDTH), tok),
            pl.BlockSpec((seq, ATTN_WIDTH), per_b),
            pl.BlockSpec((seq, ATTN_WIDTH), per_b),
            cache_spec,
            cache_spec,
            pl.BlockSpec((1, N_HEADS, 2 * WIN_H, GRID_W, LANES), lambda s: (layer, 0, 0, 0, 0)),
            pl.BlockSpec((tq, SGU_WIDTH), tok),
            pl.BlockSpec((tq, D_MODEL), tok),
            pl.BlockSpec((1, 1, 6, D_MODEL), lambda s: (layer, 1 + seq_of(s), 0, 0)),
            pl.BlockSpec((1, D_MODEL, D_MODEL), lyr3),
            pl.BlockSpec((1, 1, D_MODEL), lyr3),
            pl.BlockSpec((N_EXPERTS, D_MODEL), full2),
            pl.BlockSpec((N_EXPERTS, 1), full2),
            pl.BlockSpec((SRC_TILE, SRC_TILE), full2),
        ],
        out_specs=out_specs,
        out_shape=out_shape,
        scratch_shapes=[masked(seq), masked(seq), masked_t, masked_t, h2_scratch],
        compiler_params=_cparams("arbitrary"),
        name="nbr_attn",
    )(q, k, v, cache_k, cache_v, tb, sg, x, mod, params["wout"], params["n2w"], params["wr"],
      params["br"], params["tri"])


def _bias_tables(rpb):
    n_rel = 2 * WIN_W - 1
    cols = jnp.arange(GRID_W)
    col_start = jnp.clip(cols - WIN_W // 2, 0, GRID_W - WIN_W)
    col_ok = (cols[None, :] >= col_start[:, None]) & (cols[None, :] < col_start[:, None] + WIN_W)
    rel = cols[None, :] - cols[:, None] + WIN_W - 1
    pick = ((rel[None] == jnp.arange(n_rel)[:, None, None]) & col_ok[None]).astype(F32)
    zero = jnp.zeros_like(pick)
    halves = jnp.concatenate([jnp.concatenate([pick, zero], axis=2),
                              jnp.concatenate([zero, pick], axis=2)], axis=0)
    masked = jnp.where(jnp.concatenate([col_ok, col_ok], axis=1), 0.0, NEG)[None]
    select = jnp.concatenate([halves, masked], axis=0)

    by_offset = jnp.pad(rpb * LOG2E, ((0, 0), (0, 0), (1, 1), (0, 0)))
    feats = jnp.concatenate([by_offset[:, :, :-1], by_offset[:, :, 1:],
                             jnp.ones(by_offset.shape[:2] + (2 * WIN_H, 1), F32)], axis=-1)
    return jnp.einsum("lhdc,cqn->lhdqn", feats, select, precision=lax.Precision.HIGHEST)


def _max_moe_tiles(n_src):
    max_chunks = n_src * SRC_TILE // CHUNK_ROWS + n_src * N_EXPERT_GROUPS
    return -(-max_chunks // TILE_CHUNKS) + N_EXPERT_GROUPS


def _chunk_tables(chunks, n_tiles):
    n_src = chunks.shape[0]
    groups = jnp.arange(N_EXPERT_GROUPS)
    seg_end = jnp.cumsum(chunks, axis=1)
    seg_start = seg_end - chunks
    src_end = jnp.cumsum(chunks, axis=0)
    src_start = src_end - chunks
    total = src_end[-1]
    tiles = (total + TILE_CHUNKS - 1) // TILE_CHUNKS
    tile_end = jnp.cumsum(tiles)
    tile_start = tile_end - tiles
    n_used = tile_end[-1:]

    u = jnp.arange(n_tiles)
    gid = jnp.minimum(jnp.sum(u[:, None] >= tile_end[None, :], axis=1), N_EXPERT_GROUPS - 1)

    p = jnp.arange(n_tiles * TILE_CHUNKS)
    g_hot = jnp.repeat(gid, TILE_CHUNKS)[:, None] == groups[None, :]
    by_group = lambda vec: jnp.sum(jnp.where(g_hot, vec[None, :], 0), axis=1)
    q = p - TILE_CHUNKS * by_group(tile_start)
    live = q < by_group(total)
    src_end_p = jnp.sum(jnp.where(g_hot[:, None, :], src_end[None], 0), axis=2)
    t = jnp.minimum(jnp.sum(q[:, None] >= src_end_p, axis=1), n_src - 1)
    t_hot = t[:, None] == jnp.arange(n_src)[None, :]
    by_seg = lambda tab: jnp.sum(jnp.where(t_hot[:, :, None] & g_hot[:, None, :], tab[None], 0),
                                 axis=(1, 2))
    slab_chunk = by_seg(seg_start) + q - by_seg(src_start)
    src_row = jnp.where(live, t * SLAB_ROWS + CHUNK_ROWS * slab_chunk, 0)

    s = jnp.arange(SLAB_CHUNKS)
    gs = jnp.sum(s[None, :, None] >= seg_end[:, None, :], axis=2)
    used = gs < N_EXPERT_GROUPS
    s_hot = jnp.minimum(gs, N_EXPERT_GROUPS - 1)[:, :, None] == groups
    pick = lambda tab: jnp.sum(jnp.where(s_hot, tab, 0), axis=2)
    pos = (TILE_CHUNKS * pick(tile_start[None, None, :]) + pick(src_start[:, None, :])
           + s[None, :] - pick(seg_start[:, None, :]))
    out_row = jnp.where(used, CHUNK_ROWS * pos, 0).reshape(-1)
    i32 = lambda a: a.astype(jnp.int32)
    return i32(gid), i32(src_row), i32(out_row), i32(n_used)


def _chunk_copy(src_hbm, row, k, buf, sem, slot):
    return pltpu.make_async_copy(
        src_hbm.at[pl.ds(pl.multiple_of(row, CHUNK_ROWS), CHUNK_ROWS), :],
        buf.at[slot, pl.ds(k * CHUNK_ROWS, CHUNK_ROWS), :],
        sem.at[slot])


def _start_chunks(sources, row_ref, first, n_chunks, buf, sem, slot):
    bases = [sum(s.shape[0] for s in sources[:i]) for i in range(len(sources) + 1)]
    for k in range(n_chunks):
        row = row_ref[first + k]
        if len(sources) == 1:
            _chunk_copy(sources[0], row, k, buf, sem, slot).start()
        else:
            for src, lo, hi in zip(sources, bases[:-1], bases[1:]):
                @pl.when((row >= lo) & (row < hi))
                def _(src=src, lo=lo):
                    _chunk_copy(src, row - lo, k, buf, sem, slot).start()


def _gather_step(sources, row_ref, n_chunks, buf, sem):
    step = pl.program_id(0)
    slot = step % 2

    @pl.when(step == 0)
    def _():
        _start_chunks(sources, row_ref, 0, n_chunks, buf, sem, 0)

    @pl.when(step + 1 < pl.num_programs(0))
    def _():
        _start_chunks(sources, row_ref, (step + 1) * n_chunks, n_chunks, buf, sem, 1 - slot)

    for k in range(n_chunks):
        _chunk_copy(sources[0], 0, k, buf, sem, slot).wait()
    return slot


def _moe_kernel(gid_ref, src_ref, nused_ref, *refs, n_slabs):
    slabs = refs[:n_slabs]
    w1_ref, w3_ref, w2_ref, o_ref, buf, sem, w1b, w3b, w2b = refs[n_slabs:]
    u = pl.program_id(0)
    slot = _gather_step(slabs, src_ref, TILE_CHUNKS, buf, sem)

    @pl.when((u == 0) | (gid_ref[u] != gid_ref[jnp.maximum(u - 1, 0)]))
    def _():
        w1b[...] = w1_ref[0].astype(BF16)
        w3b[...] = w3_ref[0].astype(BF16)
        w2b[...] = w2_ref[0, 0].astype(BF16)

    @pl.when(u < nused_ref[0])
    def _():
        h = buf[slot, :, 0:D_MODEL]
        gates = buf[slot, :, D_MODEL:].astype(F32)
        hid = []
        for e in range(EXPERTS_PER_GROUP):
            h1 = jnp.dot(h, w1b[e], preferred_element_type=F32)
            h3 = jnp.dot(h, w3b[e], preferred_element_type=F32)
            act = (h1 * jax.nn.sigmoid(h1)) * h3
            gate = gates[:, e:e + 1] + gates[:, EXPERTS_PER_GROUP + e:EXPERTS_PER_GROUP + e + 1]
            hid.append((act * gate).astype(BF16))
        o_ref[...] = jnp.dot(jnp.concatenate(hid, axis=1), w2b[...], preferred_element_type=F32)

    @pl.when(u >= nused_ref[0])
    def _():
        o_ref[...] = jnp.zeros_like(o_ref)


def _moe(slabs, tables, layer, params):
    gid, src_row, _, n_used = tables
    n_tiles = gid.shape[0]
    expert_w = pl.BlockSpec((1, EXPERTS_PER_GROUP, D_MODEL, D_EXPERT),
                            lambda u, gid, src, nu: (layer, gid[u], 0, 0))
    return pl.pallas_call(
        functools.partial(_moe_kernel, n_slabs=len(slabs)),
        grid_spec=pltpu.PrefetchScalarGridSpec(
            num_scalar_prefetch=3,
            grid=(n_tiles,),
            in_specs=[pl.BlockSpec(memory_space=pl.ANY)] * len(slabs) + [
                expert_w,
                expert_w,
                pl.BlockSpec((1, 1, GROUP_HIDDEN, D_MODEL), lambda u, gid, src, nu: (layer, gid[u], 0, 0)),
            ],
            out_specs=pl.BlockSpec((MOE_TILE, D_MODEL), lambda u, gid, src, nu: (u, 0)),
            scratch_shapes=[
                pltpu.VMEM((2, MOE_TILE, SLAB_WIDTH), BF16),
                pltpu.SemaphoreType.DMA((2,)),
                pltpu.VMEM((EXPERTS_PER_GROUP, D_MODEL, D_EXPERT), BF16),
                pltpu.VMEM((EXPERTS_PER_GROUP, D_MODEL, D_EXPERT), BF16),
                pltpu.VMEM((GROUP_HIDDEN, D_MODEL), BF16),
            ],
        ),
        out_shape=jax.ShapeDtypeStruct((n_tiles * MOE_TILE, D_MODEL), F32),
        compiler_params=_cparams("arbitrary"),
        name="moe",
    )(gid, src_row, n_used, *slabs, params["w1"], params["w3"], params["w2"])


def _unsorted_moe(row_ref, dest_ref, y_hbm, buf, sem, n_slabs):
    slot = _gather_step([y_hbm], row_ref, n_slabs * SLAB_CHUNKS, buf, sem)
    tn = (((0,), (0,)), ((), ()))
    out = []
    for j in range(n_slabs):
        y = buf[slot, j * SLAB_ROWS:(j + 1) * SLAB_ROWS, :]
        hi = y.astype(BF16)
        lo = (y - hi.astype(F32)).astype(BF16)
        perm = _permutation(dest_ref[j])
        out.append(lax.dot_general(perm, hi, tn, preferred_element_type=F32)
                   + lax.dot_general(perm, lo, tn, preferred_element_type=F32))
    return jnp.concatenate(out, axis=0)


def _unsort_kernel(row_ref, xm_ref, dest_ref, mod_ref, y_hbm, o_ref, buf, sem, *, n_slabs):
    moe = _unsorted_moe(row_ref, dest_ref, y_hbm, buf, sem, n_slabs)
    o_ref[...] = xm_ref[...] + mod_ref[0, 0][5:6] * moe


def _unsort(pending, layer, mod, row_of_tile, *, tm):
    xm, dest, out_row, y = pending
    t_tokens = xm.shape[0]
    n_slabs = tm // SRC_TILE
    return pl.pallas_call(
        functools.partial(_unsort_kernel, n_slabs=n_slabs),
        grid_spec=pltpu.PrefetchScalarGridSpec(
            num_scalar_prefetch=1,
            grid=(t_tokens // tm,),
            in_specs=[
                pl.BlockSpec((tm, D_MODEL), lambda t, rows: (t, 0)),
                pl.BlockSpec((n_slabs, 1, SRC_TILE), lambda t, rows: (t, 0, 0)),
                pl.BlockSpec((1, 1, 6, D_MODEL), lambda t, rows: (layer, row_of_tile(t), 0, 0)),
                pl.BlockSpec(memory_space=pl.ANY),
            ],
            out_specs=pl.BlockSpec((tm, D_MODEL), lambda t, rows: (t, 0)),
            scratch_shapes=[pltpu.VMEM((2, n_slabs * SLAB_ROWS, D_MODEL), F32),
                            pltpu.SemaphoreType.DMA((2,))],
        ),
        out_shape=jax.ShapeDtypeStruct((t_tokens, D_MODEL), F32),
        compiler_params=_cparams("arbitrary"),
        name="unsort",
    )(out_row, xm, dest, mod, y)


def kernel(x_prompt, x_sample, cache_k, cache_v, c, c_ctx, w_ada, b_ada, norm1_w, norm2_w, w_in,
           q_norm_w, k_norm_w, rpb, w_sgu, b_sgu, sgu_ln_w, sgu_ln_b, w_out, w_router, b_router,
           w1, w3, w2):
    batch, seq, _ = x_prompt.shape
    dec_batch, dec_seq, _ = x_sample.shape
    past = cache_k.shape[2]
    assert 1 + dec_batch <= COND_ROWS and dec_seq % (Q_ROWS * GRID_W) == 0 and seq % CHUNK == 0

    cond = jnp.zeros((COND_ROWS, D_MODEL), F32).at[0].set(c_ctx).at[1:1 + dec_batch].set(c)
    mod = _adaln(cond, w_ada, b_ada).reshape(DEPTH, COND_ROWS, 6, D_MODEL)

    xp = x_prompt.reshape(batch * seq, D_MODEL)
    xs = x_sample.reshape(dec_batch * dec_seq, D_MODEL)
    from_cache_layout = lambda a: a.transpose(0, 1, 3, 4, 2).reshape(dec_batch, DEPTH, ATTN_WIDTH, past)
    ck = from_cache_layout(cache_k)
    cv = from_cache_layout(cache_v)

    head_of = jnp.arange(256) // HEAD_DIM
    bd = (head_of[:, None] == head_of[None, :]).astype(BF16)
    tok_id = jnp.arange(SRC_TILE)
    tri = (tok_id[:, None] <= tok_id[None, :]).astype(BF16)

    params = dict(
        n1w=norm1_w.reshape(DEPTH, 1, D_MODEL),
        n2w=norm2_w.reshape(DEPTH, 1, D_MODEL),
        win=w_in.astype(BF16),
        wout=w_out.astype(BF16),
        qw=jnp.tile(q_norm_w, (1, N_HEADS)).reshape(DEPTH, 1, ATTN_WIDTH),
        kw=jnp.tile(k_norm_w, (1, N_HEADS)).reshape(DEPTH, 1, ATTN_WIDTH),
        bd=jnp.concatenate([bd, bd], axis=0),
        ws=w_sgu.astype(BF16),
        bs=jnp.broadcast_to(b_sgu[..., None], (DEPTH, SGU_GROUPS, CHUNK, SGU_GROUP_DIM)),
        lnw=sgu_ln_w,
        lnb=sgu_ln_b,
        wr=w_router.T.astype(BF16),
        br=b_router.reshape(N_EXPERTS, 1),
        tri=tri,
        w1=w1,
        w3=w3,
        w2=w2.reshape(DEPTH, N_EXPERT_GROUPS, GROUP_HIDDEN, D_MODEL),
    )
    tb = _bias_tables(rpb)

    tm = 2 * SRC_TILE
    assert dec_seq % tm == 0 and tm % seq == 0
    lat_tiles_per_seq = dec_seq // tm
    ctx_row = lambda i: 0
    lat_row = lambda i: 1 + i // lat_tiles_per_seq

    def moe_block(attended, layer):
        chunks = jnp.concatenate([meta[:, :N_EXPERT_GROUPS, 0] for _, _, _, meta in attended], axis=0)
        tables = _chunk_tables(chunks, _max_moe_tiles(chunks.shape[0]))
        y = _moe([slab for _, slab, _, _ in attended], tables, layer, params)
        pending, first = [], 0
        for xm, _, dest, meta in attended:
            n_rows = meta.shape[0] * SLAB_CHUNKS
            pending.append((xm, dest, tables[2][first:first + n_rows], y))
            first += n_rows
        return pending

    kv_new = jax.ShapeDtypeStruct((batch, DEPTH, ATTN_WIDTH, seq), F32)

    for l in range(DEPTH):
        outs = _inproj(xp, l, mod, ctx_row, params, tm=tm,
                       kv_buffers=kv_new if l == 0 else (kbuf, vbuf))
        if l > 0:
            xp, outs = outs[0], outs[1:]
        q, kbuf, vbuf, v, sg = outs
        attended_ctx = _ctx_attention(q, kbuf, v, sg, xp, l, mod, params, seq=seq)

        outs = _inproj(xs, l, mod, lat_row, params, tm=tm)
        if l > 0:
            xs, outs = outs[0], outs[1:]
        q, k, v, sg = outs
        attended_lat = _nbr_attention(q, k, v, ck, cv, tb, sg, xs, l, mod, params,
                                      batch=dec_batch, seq=dec_seq)

        xp, xs = moe_block([attended_ctx, attended_lat], l)
    xp = _unsort(xp, DEPTH - 1, mod, ctx_row, tm=tm)
    xs = _unsort(xs, DEPTH - 1, mod, lat_row, tm=tm)

    to_cache_layout = lambda buf: buf.reshape(batch, DEPTH, N_HEADS, HEAD_DIM, seq).transpose(0, 1, 4, 2, 3)
    return (xp.reshape(batch, seq, D_MODEL), xs.reshape(dec_batch, dec_seq, D_MODEL),
            to_cache_layout(kbuf), to_cache_layout(vbuf))
```

```python
import functools

import jax
import jax.numpy as jnp
from jax import lax
from jax.experimental import pallas as pl
from jax.experimental.pallas import tpu as pltpu

F32 = jnp.float32
BF16 = jnp.bfloat16

D_MODEL = 1024
DEPTH = 4
N_HEADS = 8
HEAD_DIM = 64
ATTN_WIDTH = N_HEADS * HEAD_DIM
SGU_GROUPS = 4
SGU_GROUP_DIM = 128
SGU_WIDTH = SGU_GROUPS * SGU_GROUP_DIM
CHUNK = 128
IN_WIDTH = 3 * ATTN_WIDTH + 2 * SGU_WIDTH
GRID_W = 64
WIN_H = 8
WIN_W = 16
N_EXPERTS = 16
N_EXPERT_GROUPS = 4
EXPERTS_PER_GROUP = 4
D_EXPERT = 256
GROUP_HIDDEN = EXPERTS_PER_GROUP * D_EXPERT
EPS = 1e-6
NEG = -1e30
LOG2E = 1.4426950408889634

LANES = 128
HEADS_PER_VREG = LANES // HEAD_DIM
N_HEAD_PAIRS = N_HEADS // HEADS_PER_VREG
COND_ROWS = 16
Q_ROWS = 4
WIN_ROWS = Q_ROWS + WIN_H
VMEM_LIMIT = 48 * 1024 * 1024

SRC_TILE = 256
CHUNK_ROWS = 16
SLAB_ROWS = SRC_TILE + N_EXPERT_GROUPS * CHUNK_ROWS
SLAB_CHUNKS = SLAB_ROWS // CHUNK_ROWS
SLAB_WIDTH = D_MODEL + LANES
MOE_TILE = 512
TILE_CHUNKS = MOE_TILE // CHUNK_ROWS


def _cparams(*sem):
    return pltpu.CompilerParams(dimension_semantics=sem, vmem_limit_bytes=VMEM_LIMIT)


def _adaln_kernel(cond_ref, w_ref, b_ref, o_ref):
    c = cond_ref[...]
    a = (c * jax.nn.sigmoid(c)).astype(BF16)
    o_ref[0] = jnp.dot(a, w_ref[0].astype(BF16), preferred_element_type=F32) + b_ref[0]


def _adaln(cond, w_ada, b_ada):
    tn = 1536
    n = 6 * D_MODEL
    return pl.pallas_call(
        _adaln_kernel,
        grid=(DEPTH, n // tn),
        in_specs=[
            pl.BlockSpec((COND_ROWS, D_MODEL), lambda l, j: (0, 0)),
            pl.BlockSpec((1, D_MODEL, tn), lambda l, j: (l, 0, j)),
            pl.BlockSpec((1, 1, tn), lambda l, j: (l, 0, j)),
        ],
        out_specs=pl.BlockSpec((1, COND_ROWS, tn), lambda l, j: (l, 0, j)),
        out_shape=jax.ShapeDtypeStruct((DEPTH, COND_ROWS, n), F32),
        compiler_params=_cparams("arbitrary", "arbitrary"),
        name="adaln",
    )(cond, w_ada, b_ada.reshape(DEPTH, 1, n))


def _gelu_tanh(x):
    return 0.5 * x * (1.0 + jnp.tanh(0.7978845608028654 * (x + 0.044715 * (x * x * x))))


def _head_rms(t, bd, w):
    t2 = t * t
    hi = t2.astype(BF16)
    lo = (t2 - hi.astype(F32)).astype(BF16)
    outs = []
    for c in range(ATTN_WIDTH // 256):
        sl = slice(256 * c, 256 * c + 256)
        hl = jnp.concatenate([hi[:, sl], lo[:, sl]], axis=1)
        ss = jnp.dot(hl, bd, preferred_element_type=F32)
        outs.append(t[:, sl] * lax.rsqrt(ss * (1.0 / HEAD_DIM) + EPS))
    return jnp.concatenate(outs, axis=1) * w


def _inproj_kernel(*refs, tm, transposed_kv, new_kv_layer, fused_moe):
    refs = list(refs)
    if fused_moe:
        row_ref, xm_ref, dest_ref, prev_mod_ref, y_hbm = refs[:5]
        refs = refs[5:]
    else:
        x_ref = refs.pop(0)
    (mod_ref, n1w_ref, win_ref, qw_ref, kw_ref, bd_ref, ws_ref, bs_ref, lnw_ref, lnb_ref) = refs[:10]
    refs = refs[10:]
    if transposed_kv and new_kv_layer is None:
        refs = refs[2:]
    if fused_moe:
        x_out_ref = refs.pop(0)
    if transposed_kv:
        q_ref, kt_ref, vt_ref, v_ref, sg_ref = refs[:5]
        refs = refs[5:]
    else:
        q_ref, k_ref, v_ref, sg_ref = refs[:4]
        refs = refs[4:]
    if fused_moe:
        buf, sem = refs
        moe = _unsorted_moe(row_ref, dest_ref, y_hbm, buf, sem, tm // SRC_TILE)
        x = xm_ref[...] + prev_mod_ref[0, 0][5:6] * moe
        x_out_ref[...] = x
    else:
        x = x_ref[...]
    m = mod_ref[0, 0]
    sh1, sc1 = m[0:1], m[1:2]
    ms = jnp.mean(x * x, axis=-1, keepdims=True)
    h = x * lax.rsqrt(ms + EPS) * n1w_ref[0]
    h = (h * (1.0 + sc1) + sh1).astype(BF16)
    p = jnp.dot(h, win_ref[0], preferred_element_type=F32)

    bd = bd_ref[...]
    q = _head_rms(p[:, 0:ATTN_WIDTH], bd, qw_ref[0])
    k = _head_rms(p[:, ATTN_WIDTH:2 * ATTN_WIDTH], bd, kw_ref[0])
    v = p[:, 2 * ATTN_WIDTH:3 * ATTN_WIDTH]
    q_ref[...] = (q * (HEAD_DIM ** -0.5 * LOG2E)).astype(q_ref.dtype)
    if transposed_kv:
        seq = kt_ref.shape[3]
        if new_kv_layer is None:
            slot = 0
        else:
            slot = new_kv_layer
            kt_ref[...] = jnp.zeros_like(kt_ref)
            vt_ref[...] = jnp.zeros_like(vt_ref)
        for j in range(tm // seq):
            kt_ref[j, slot] = k[j * seq:(j + 1) * seq].T
            vt_ref[j, slot] = v[j * seq:(j + 1) * seq].T
    else:
        k_ref[...] = k.astype(k_ref.dtype)
    v_ref[...] = v.astype(v_ref.dtype)

    off_u = 3 * ATTN_WIDTH
    off_v = off_u + SGU_WIDTH
    for g in range(SGU_GROUPS):
        gl = slice(g * SGU_GROUP_DIM, (g + 1) * SGU_GROUP_DIM)
        u = _gelu_tanh(p[:, off_u + g * SGU_GROUP_DIM: off_u + (g + 1) * SGU_GROUP_DIM])
        t = _gelu_tanh(p[:, off_v + g * SGU_GROUP_DIM: off_v + (g + 1) * SGU_GROUP_DIM])
        mu = jnp.mean(t, axis=-1, keepdims=True)
        d = t - mu
        var = jnp.mean(d * d, axis=-1, keepdims=True)
        y = (d * lax.rsqrt(var + EPS) * lnw_ref[0, g:g + 1, :] + lnb_ref[0, g:g + 1, :]).astype(BF16)
        for c in range(tm // CHUNK):
            rows = slice(c * CHUNK, (c + 1) * CHUNK)
            sv = jnp.dot(ws_ref[0, g], y[rows], preferred_element_type=F32) + bs_ref[0, g]
            sg_ref[rows, gl] = (u[rows] * sv).astype(sg_ref.dtype)


def _inproj(x, layer, mod, row_of_tile, params, *, tm, kv_buffers=None):
    fused_moe = isinstance(x, tuple)
    tok = lambda i, *_: (i, 0)
    lyr3 = lambda i, *_: (layer, 0, 0)
    lyr4 = lambda i, *_: (layer, 0, 0, 0)
    mod_row = lambda lyr: pl.BlockSpec((1, 1, 6, D_MODEL), lambda i, *_: (lyr, row_of_tile(i), 0, 0))
    x_spec = pl.BlockSpec((tm, D_MODEL), tok)
    if fused_moe:
        n_slabs = tm // SRC_TILE
        xm, dest, out_row, y = x
        t_tokens = xm.shape[0]
        prefetch = [out_row]
        in_specs = [x_spec, pl.BlockSpec((n_slabs, 1, SRC_TILE), lambda i, *_: (i, 0, 0)),
                    mod_row(layer - 1), pl.BlockSpec(memory_space=pl.ANY)]
        args = [xm, dest, mod, y]
        scratch = [pltpu.VMEM((2, n_slabs * SLAB_ROWS, D_MODEL), F32), pltpu.SemaphoreType.DMA((2,))]
    else:
        t_tokens = x.shape[0]
        prefetch, in_specs, args, scratch = [], [x_spec], [x], []
    act = pl.BlockSpec((tm, ATTN_WIDTH), tok)
    act_sds = jax.ShapeDtypeStruct((t_tokens, ATTN_WIDTH), BF16)
    in_specs += [
        mod_row(layer),
        pl.BlockSpec((1, 1, D_MODEL), lyr3),
        pl.BlockSpec((1, D_MODEL, IN_WIDTH), lyr3),
        pl.BlockSpec((1, 1, ATTN_WIDTH), lyr3),
        pl.BlockSpec((1, 1, ATTN_WIDTH), lyr3),
        pl.BlockSpec((512, 256), lambda i, *_: (0, 0)),
        pl.BlockSpec((1, SGU_GROUPS, CHUNK, CHUNK), lyr4),
        pl.BlockSpec((1, SGU_GROUPS, CHUNK, SGU_GROUP_DIM), lyr4),
        pl.BlockSpec((1, SGU_GROUPS, SGU_GROUP_DIM), lyr3),
        pl.BlockSpec((1, SGU_GROUPS, SGU_GROUP_DIM), lyr3),
    ]
    args += [mod, params["n1w"], params["win"], params["qw"], params["kw"], params["bd"],
             params["ws"], params["bs"], params["lnw"], params["lnb"]]
    out_specs = [x_spec] if fused_moe else []
    out_shape = [jax.ShapeDtypeStruct((t_tokens, D_MODEL), F32)] if fused_moe else []
    if kv_buffers is None:
        out_specs += [act] * 4
        out_shape += [act_sds] * 4
        aliases = {}
    elif isinstance(kv_buffers, jax.ShapeDtypeStruct):
        seq = kv_buffers.shape[3]
        kv_spec = pl.BlockSpec((tm // seq,) + kv_buffers.shape[1:], lambda i, *_: (i, 0, 0, 0))
        out_specs += [act, kv_spec, kv_spec, act, act]
        out_shape += [act_sds, kv_buffers, kv_buffers, act_sds, act_sds]
        aliases = {}
    else:
        kbuf, vbuf = kv_buffers
        seq = kbuf.shape[3]
        kv_spec = pl.BlockSpec((tm // seq, 1, ATTN_WIDTH, seq), lambda i, *_: (i, layer, 0, 0))
        first_alias_in = len(prefetch) + len(args)
        first_alias_out = len(out_specs) + 1
        in_specs += [pl.BlockSpec(memory_space=pl.ANY)] * 2
        args += [kbuf, vbuf]
        out_specs += [act, kv_spec, kv_spec, act, act]
        out_shape += [act_sds, jax.ShapeDtypeStruct(kbuf.shape, F32),
                      jax.ShapeDtypeStruct(vbuf.shape, F32), act_sds, act_sds]
        aliases = {first_alias_in: first_alias_out, first_alias_in + 1: first_alias_out + 1}
    return pl.pallas_call(
        functools.partial(_inproj_kernel, tm=tm, transposed_kv=kv_buffers is not None,
                          new_kv_layer=layer if isinstance(kv_buffers, jax.ShapeDtypeStruct) else None,
                          fused_moe=fused_moe),
        grid_spec=pltpu.PrefetchScalarGridSpec(
            num_scalar_prefetch=len(prefetch),
            grid=(t_tokens // tm,),
            in_specs=in_specs,
            out_specs=out_specs,
            scratch_shapes=scratch,
        ),
        out_shape=out_shape,
        input_output_aliases=aliases,
        compiler_params=_cparams("arbitrary"),
        name="inproj",
    )(*prefetch, *args)


def _route(lg_t, br):
    s = jax.nn.sigmoid(lg_t)
    sel = s + br
    row = lambda a, i: a[i:i + 1]
    g_score = []
    for g in range(N_EXPERT_GROUPS):
        v = [row(sel, 4 * g + i) for i in range(4)]
        best_pair = None
        for i in range(4):
            for j in range(i + 1, 4):
                pair = v[i] + v[j]
                best_pair = pair if best_pair is None else jnp.maximum(best_pair, pair)
        g_score.append(best_pair)
    best = jnp.zeros_like(g_score[0], dtype=jnp.int32)
    top = g_score[0]
    for g in range(1, N_EXPERT_GROUPS):
        upd = g_score[g] > top
        best = jnp.where(upd, g, best)
        top = jnp.where(upd, g_score[g], top)
    cand, aff = [], []
    for i in range(4):
        ci, si = row(sel, i), row(s, i)
        for g in range(1, N_EXPERT_GROUPS):
            ci = jnp.where(best == g, row(sel, 4 * g + i), ci)
            si = jnp.where(best == g, row(s, 4 * g + i), si)
        cand.append(ci)
        aff.append(si)

    def first_argmax(vals):
        idx = jnp.zeros_like(best)
        top_v = vals[0]
        for i in range(1, 4):
            upd = vals[i] > top_v
            idx = jnp.where(upd, i, idx)
            top_v = jnp.where(upd, vals[i], top_v)
        return idx

    i1 = first_argmax(cand)
    i2 = first_argmax([jnp.where(i1 == i, -jnp.inf, cand[i]) for i in range(4)])
    pick = lambda idx: sum(jnp.where(idx == i, aff[i], 0.0) for i in range(4))
    den = pick(i1) + pick(i2)
    gate = [jnp.where((i1 == i) | (i2 == i), aff[i] / den, 0.0) for i in range(4)]
    hi = [x.astype(BF16) for x in gate]
    lo = [(x - h.astype(F32)).astype(BF16) for x, h in zip(gate, hi)]
    return best, hi + lo


def _slab_positions(best, tri):
    n = best.shape[1]
    onehot = [jnp.where(best == g, 1.0, 0.0) for g in range(N_EXPERT_GROUPS)]
    pad = [jnp.zeros_like(onehot[0])] * (8 - N_EXPERT_GROUPS)
    oh = jnp.concatenate(onehot + pad, axis=0).astype(BF16)
    counts = jnp.dot(oh, tri, preferred_element_type=F32)
    dest = jnp.zeros((1, n), F32)
    seg_start = jnp.zeros((1, 1), F32)
    chunks = []
    for g in range(N_EXPERT_GROUPS):
        cg = counts[g:g + 1]
        n_g = jnp.max(cg, axis=1, keepdims=True)
        c_g = jnp.floor((n_g + (CHUNK_ROWS - 1)) * (1.0 / CHUNK_ROWS))
        dest = dest + onehot[g] * (seg_start + cg - 1.0)
        seg_start = seg_start + c_g * CHUNK_ROWS
        chunks.append(c_g)
    return dest.astype(jnp.int32), chunks


def _permutation(dest):
    n = dest.shape[1]
    hit = lax.broadcasted_iota(jnp.int32, (SLAB_ROWS, n), 0) == dest
    return jnp.where(hit, 1.0, 0.0).astype(BF16)


def _post_attention(a, sg, x, m, wout_ref, n2w_ref, xm_ref):
    g1, sh2, sc2 = m[2:3], m[3:4], m[4:5]
    cat = jnp.concatenate([a, sg], axis=1)
    y = jnp.dot(cat, wout_ref[0], preferred_element_type=F32)
    xm = x + g1 * y
    xm_ref[...] = xm
    ms = jnp.mean(xm * xm, axis=-1, keepdims=True)
    h2 = xm * lax.rsqrt(ms + EPS) * n2w_ref[0]
    return (h2 * (1.0 + sc2) + sh2).astype(BF16)


def _route_previous(h2_scr, wr_ref, br_ref, tri_ref, slab_ref, dest_ref, meta_ref):
    h2 = h2_scr[...]
    lg_t = lax.dot_general(wr_ref[...], h2, (((1,), (1,)), ((), ())), preferred_element_type=F32)
    yield
    best, gates = _route(lg_t, br_ref[...])
    dest, chunks = _slab_positions(best, tri_ref[...])
    yield
    perm = _permutation(dest)
    slab_ref[:, 0:D_MODEL] = jnp.dot(perm, h2, preferred_element_type=F32).astype(BF16)
    gmat = jnp.concatenate(gates + [jnp.zeros((LANES - len(gates), dest.shape[1]), BF16)], axis=0)
    slab_ref[:, D_MODEL:] = lax.dot_general(perm, gmat, (((1,), (1,)), ((), ())),
                                            preferred_element_type=F32).astype(BF16)
    dest_ref[0] = dest
    meta = [jnp.broadcast_to(c, (1, LANES)) for c in chunks]
    meta += [jnp.zeros((8 - len(chunks), LANES), F32)]
    meta_ref[0] = jnp.concatenate(meta, axis=0).astype(jnp.int32)


def _pair_pos(axis):
    return lax.broadcasted_iota(jnp.int32, (1, LANES) if axis == 1 else (LANES, 1), axis)


def _head_mask(hh, axis):
    pos = _pair_pos(axis)
    return (pos >= hh * HEAD_DIM) & (pos < (hh + 1) * HEAD_DIM)


def _only_head(x2, hh, axis):
    return jnp.where(_head_mask(hh, axis), x2, jnp.zeros_like(x2))


def _attend_heads(logits, finish, router):
    outs, even = [], None
    s_next = logits(0)
    for h in range(N_HEADS):
        s = s_next
        if h + 1 < N_HEADS:
            s_next = logits(h + 1)
        if h == N_HEADS // 2:
            next(router)
        o = finish(h, s)
        if h % HEADS_PER_VREG == 0:
            even = o
        else:
            outs.append((even + o).astype(BF16))
    for _ in router:
        pass
    return jnp.concatenate(outs, axis=1)


def _ctx_attn_kernel(q_ref, kt_ref, v_ref, sg_ref, x_ref, mod_ref, wout_ref, n2w_ref, wr_ref, br_ref,
                     tri_ref, xm_ref, slab_ref, dest_ref, meta_ref, h2_scr):
    step = pl.program_id(0)

    @pl.when(step == 0)
    def _():
        h2_scr[...] = jnp.zeros_like(h2_scr)

    own_slab = slab_ref.at[pl.ds(0, SLAB_ROWS)]
    slab_ref[SLAB_ROWS:, :] = jnp.zeros((slab_ref.shape[0] - SLAB_ROWS, SLAB_WIDTH), BF16)

    def logits(h):
        cols = slice(h // HEADS_PER_VREG * LANES, (h // HEADS_PER_VREG + 1) * LANES)
        kt2 = kt_ref[0, 0, cols, :].astype(BF16)
        return jnp.dot(q_ref[:, cols], _only_head(kt2, h % HEADS_PER_VREG, 0),
                       preferred_element_type=F32)

    def finish(h, s):
        cols = slice(h // HEADS_PER_VREG * LANES, (h // HEADS_PER_VREG + 1) * LANES)
        e = jnp.exp2(s - jnp.max(s, axis=-1, keepdims=True))
        l = jnp.sum(e, axis=-1, keepdims=True)
        vm = _only_head(v_ref[:, cols], h % HEADS_PER_VREG, 1)
        return jnp.dot(e.astype(BF16), vm, preferred_element_type=F32) * (1.0 / l)

    new_router = lambda: _route_previous(h2_scr, wr_ref, br_ref, tri_ref, own_slab, dest_ref, meta_ref)
    last = pl.num_programs(0) - 1

    @pl.when(step < last)
    def _():
        router = new_router()
        next(router)
        a = _attend_heads(logits, finish, router)
        h2_scr[...] = _post_attention(a, sg_ref[...], x_ref[...], mod_ref[0, 0], wout_ref, n2w_ref,
                                      xm_ref)

    @pl.when(step == last)
    def _():
        for _ in new_router():
            pass


def _post_attention_out(t_tokens, slab_blocks, slab_rows_total, slab_block_of):
    n_src = t_tokens // SRC_TILE
    prev = lambda s: jnp.maximum(s - 1, 0)
    specs = [
        pl.BlockSpec((SRC_TILE, D_MODEL), lambda s: (jnp.minimum(s, n_src - 1), 0)),
        pl.BlockSpec((slab_blocks * SLAB_ROWS, SLAB_WIDTH), lambda s: (slab_block_of(prev(s)), 0)),
        pl.BlockSpec((1, 1, SRC_TILE), lambda s: (prev(s), 0, 0)),
        pl.BlockSpec((1, 8, LANES), lambda s: (prev(s), 0, 0)),
    ]
    shapes = [
        jax.ShapeDtypeStruct((t_tokens, D_MODEL), F32),
        jax.ShapeDtypeStruct((slab_rows_total, SLAB_WIDTH), BF16),
        jax.ShapeDtypeStruct((n_src, 1, SRC_TILE), jnp.int32),
        jax.ShapeDtypeStruct((n_src, 8, LANES), jnp.int32),
    ]
    return specs, shapes, pltpu.VMEM((SRC_TILE, D_MODEL), BF16)


def _ctx_attention(q, kbuf, v, sg, x, layer, mod, params, *, seq, slab_share):
    assert seq == SRC_TILE
    t_tokens = x.shape[0]
    n_seq = t_tokens // seq
    seq_of = lambda s: jnp.minimum(s, n_seq - 1)
    tok = lambda s: (seq_of(s), 0)
    lyr3 = lambda s: (layer, 0, 0)
    full2 = lambda s: (0, 0)
    out_specs, out_shape, h2_scratch = _post_attention_out(
        t_tokens, slab_share, n_seq * slab_share * SLAB_ROWS, lambda j: j)
    return pl.pallas_call(
        _ctx_attn_kernel,
        grid=(n_seq + 1,),
        in_specs=[
            pl.BlockSpec((seq, ATTN_WIDTH), tok),
            pl.BlockSpec((1, 1, ATTN_WIDTH, seq), lambda s: (seq_of(s), layer, 0, 0)),
            pl.BlockSpec((seq, ATTN_WIDTH), tok),
            pl.BlockSpec((seq, SGU_WIDTH), tok),
            pl.BlockSpec((seq, D_MODEL), tok),
            pl.BlockSpec((1, 1, 6, D_MODEL), lambda s: (layer, 0, 0, 0)),
            pl.BlockSpec((1, D_MODEL, D_MODEL), lyr3),
            pl.BlockSpec((1, 1, D_MODEL), lyr3),
            pl.BlockSpec((N_EXPERTS, D_MODEL), full2),
            pl.BlockSpec((N_EXPERTS, 1), full2),
            pl.BlockSpec((SRC_TILE, SRC_TILE), full2),
        ],
        out_specs=out_specs,
        out_shape=out_shape,
        scratch_shapes=[h2_scratch],
        compiler_params=_cparams("arbitrary"),
        name="ctx_attn",
    )(q, kbuf, v, sg, x, mod, params["wout"], params["n2w"], params["wr"], params["br"], params["tri"])


def _row_start(r, rows):
    return jnp.clip(r - WIN_H // 2, 0, rows - WIN_H)


def _nbr_attn_kernel(q_ref, k_ref, v_ref, ck_ref, cv_ref, tb_ref, sg_ref, x_ref, mod_ref, wout_ref,
                     n2w_ref, wr_ref, br_ref, tri_ref, _shared_slabs, xm_ref, slab_ref, dest_ref,
                     meta_ref, kwm_ref, vwm_ref, kcm_ref, vcm_ref, h2_scr, *, rows, n_tiles):
    step = pl.program_id(0)
    tiles_per_seq = rows // Q_ROWS
    t = jnp.minimum(step, n_tiles - 1) % tiles_per_seq

    @pl.when(step == 0)
    def _():
        h2_scr[...] = jnp.zeros_like(h2_scr)

    @pl.when((t == 0) & (step < n_tiles))
    def _():
        for j in range(N_HEAD_PAIRS):
            cols = slice(j * LANES, (j + 1) * LANES)
            k2, v2 = k_ref[:, cols], v_ref[:, cols]
            ck2 = ck_ref[0, 0, cols, :].astype(BF16)
            cv2 = cv_ref[0, 0, cols, :].astype(BF16)
            for hh in range(HEADS_PER_VREG):
                h = HEADS_PER_VREG * j + hh
                kwm_ref[h] = _only_head(k2, hh, 1)
                vwm_ref[h] = _only_head(v2, hh, 1)
                kcm_ref[h] = _only_head(ck2, hh, 0)
                vcm_ref[h] = _only_head(cv2, hh, 0)

    new_router = lambda: _route_previous(h2_scr, wr_ref, br_ref, tri_ref, slab_ref, dest_ref, meta_ref)

    @pl.when(step == n_tiles)
    def _():
        for _ in new_router():
            pass

    @pl.when(step < n_tiles)
    def _():
        _attend_tile(t, new_router(), q_ref, tb_ref, sg_ref, x_ref, mod_ref, wout_ref, n2w_ref, xm_ref,
                     kwm_ref, vwm_ref, kcm_ref, vcm_ref, h2_scr, rows)


def _attend_tile(t, router, q_ref, tb_ref, sg_ref, x_ref, mod_ref, wout_ref, n2w_ref, xm_ref,
                 kwm_ref, vwm_ref, kcm_ref, vcm_ref, h2_scr, rows):
    next(router)
    r0 = t * Q_ROWS
    ws = jnp.minimum(_row_start(r0, rows), rows - WIN_ROWS)
    tok0 = pl.multiple_of(ws * GRID_W, GRID_W)

    lane = lax.broadcasted_iota(jnp.int32, (1, LANES), 1)
    blk_idx, row_mask = [], []
    for a in range(Q_ROWS):
        r = r0 + a
        rs = _row_start(r, rows)
        idx_a, mask_a = [], []
        for jp in range(WIN_ROWS // 2):
            kr = ws + 2 * jp
            idx_a.append(jnp.clip(kr - r + WIN_H, 0, 2 * WIN_H - 1))
            ok0 = (kr >= rs) & (kr < rs + WIN_H)
            ok1 = (kr + 1 >= rs) & (kr + 1 < rs + WIN_H)
            m0 = jnp.where(ok0, 0.0, NEG).astype(F32)
            m1 = jnp.where(ok1, 0.0, NEG).astype(F32)
            mask_a.append(jnp.where(lane < GRID_W, m0, m1))
        blk_idx.append(idx_a)
        row_mask.append(mask_a)

    nt = (((1,), (1,)), ((), ()))
    window = pl.ds(tok0, WIN_ROWS * GRID_W)

    def logits(h):
        q2 = q_ref[:, h // HEADS_PER_VREG * LANES:(h // HEADS_PER_VREG + 1) * LANES]
        bias = jnp.concatenate([
            jnp.concatenate([tb_ref[0, h, blk_idx[a][jp]] + row_mask[a][jp]
                             for jp in range(WIN_ROWS // 2)], axis=1)
            for a in range(Q_ROWS)], axis=0)
        s_w = lax.dot_general(q2, kwm_ref[h, window, :], nt, preferred_element_type=F32) + bias
        s_c = jnp.dot(q2, kcm_ref[h], preferred_element_type=F32)
        return s_w, s_c

    def finish(h, s):
        s_w, s_c = s
        mx = jnp.maximum(jnp.max(s_w, axis=-1, keepdims=True), jnp.max(s_c, axis=-1, keepdims=True))
        e_w = jnp.exp2(s_w - mx)
        e_c = jnp.exp2(s_c - mx)
        l = jnp.sum(e_w, axis=-1, keepdims=True) + jnp.sum(e_c, axis=-1, keepdims=True)
        return (jnp.dot(e_w.astype(BF16), vwm_ref[h, window, :], preferred_element_type=F32)
                + lax.dot_general(e_c.astype(BF16), vcm_ref[h], nt,
                                  preferred_element_type=F32)) * (1.0 / l)

    a_out = _attend_heads(logits, finish, router)
    h2_scr[...] = _post_attention(a_out, sg_ref[...], x_ref[...], mod_ref[0, 0], wout_ref, n2w_ref,
                                  xm_ref)


def _nbr_attention(q, k, v, cache_k, cache_v, tb, sg, x, layer, mod, params, shared_slabs, *,
                   batch, seq, slab_share):
    t_tokens = x.shape[0]
    rows = seq // GRID_W
    tq = Q_ROWS * GRID_W
    assert tq == SRC_TILE
    nt = seq // tq
    past = cache_k.shape[3]
    n_tiles = batch * nt
    per_block = slab_share - 1
    assert shared_slabs.shape[0] * per_block == n_tiles * slab_share * SLAB_ROWS
    out_specs, out_shape, h2_scratch = _post_attention_out(
        t_tokens, 1, shared_slabs.shape[0], lambda j: slab_share * (j // per_block) + 1 + j % per_block)
    tile_of = lambda s: jnp.minimum(s, n_tiles - 1)
    seq_of = lambda s: tile_of(s) // nt
    tok = lambda s: (tile_of(s), 0)
    per_b = lambda s: (seq_of(s), 0)
    lyr3 = lambda s: (layer, 0, 0)
    full2 = lambda s: (0, 0)
    cache_spec = pl.BlockSpec((1, 1, ATTN_WIDTH, past), lambda s: (seq_of(s), layer, 0, 0))
    masked = lambda n: pltpu.VMEM((N_HEADS, n, LANES), BF16)
    masked_t = pltpu.VMEM((N_HEADS, LANES, past), BF16)
    return pl.pallas_call(
        functools.partial(_nbr_attn_kernel, rows=rows, n_tiles=n_tiles),
        grid=(n_tiles + 1,),
        in_specs=[
            pl.BlockSpec((tq, ATTN_WIDTH), tok),
            pl.BlockSpec((seq, ATTN_WIDTH), per_b),
            pl.BlockSpec((seq, ATTN_WIDTH), per_b),
            cache_spec,
            cache_spec,
            pl.BlockSpec((1, N_HEADS, 2 * WIN_H, GRID_W, LANES), lambda s: (layer, 0, 0, 0, 0)),
            pl.BlockSpec((tq, SGU_WIDTH), tok),
            pl.BlockSpec((tq, D_MODEL), tok),
            pl.BlockSpec((1, 1, 6, D_MODEL), lambda s: (layer, 1 + seq_of(s), 0, 0)),
            pl.BlockSpec((1, D_MODEL, D_MODEL), lyr3),
            pl.BlockSpec((1, 1, D_MODEL), lyr3),
            pl.BlockSpec((N_EXPERTS, D_MODEL), full2),
            pl.BlockSpec((N_EXPERTS, 1), full2),
            pl.BlockSpec((SRC_TILE, SRC_TILE), full2),
            pl.BlockSpec(memory_space=pl.ANY),
        ],
        out_specs=out_specs,
        out_shape=out_shape,
        input_output_aliases={14: 1},
        scratch_shapes=[masked(seq), masked(seq), masked_t, masked_t, h2_scratch],
        compiler_params=_cparams("arbitrary"),
        name="nbr_attn",
    )(q, k, v, cache_k, cache_v, tb, sg, x, mod, params["wout"], params["n2w"], params["wr"],
      params["br"], params["tri"], shared_slabs)


def _bias_tables(rpb):
    n_rel = 2 * WIN_W - 1
    cols = jnp.arange(GRID_W)
    col_start = jnp.clip(cols - WIN_W // 2, 0, GRID_W - WIN_W)
    col_ok = (cols[None, :] >= col_start[:, None]) & (cols[None, :] < col_start[:, None] + WIN_W)
    rel = cols[None, :] - cols[:, None] + WIN_W - 1
    pick = ((rel[None] == jnp.arange(n_rel)[:, None, None]) & col_ok[None]).astype(F32)
    zero = jnp.zeros_like(pick)
    halves = jnp.concatenate([jnp.concatenate([pick, zero], axis=2),
                              jnp.concatenate([zero, pick], axis=2)], axis=0)
    masked = jnp.where(jnp.concatenate([col_ok, col_ok], axis=1), 0.0, NEG)[None]
    select = jnp.concatenate([halves, masked], axis=0)

    by_offset = jnp.pad(rpb * LOG2E, ((0, 0), (0, 0), (1, 1), (0, 0)))
    feats = jnp.concatenate([by_offset[:, :, :-1], by_offset[:, :, 1:],
                             jnp.ones(by_offset.shape[:2] + (2 * WIN_H, 1), F32)], axis=-1)
    return jnp.einsum("lhdc,cqn->lhdqn", feats, select, precision=lax.Precision.HIGHEST)


def _max_moe_tiles(n_src):
    max_chunks = n_src * SRC_TILE // CHUNK_ROWS + n_src * N_EXPERT_GROUPS
    return -(-max_chunks // TILE_CHUNKS) + N_EXPERT_GROUPS


def _chunk_tables(chunks, slab_base, n_tiles):
    n_src = chunks.shape[0]
    groups = jnp.arange(N_EXPERT_GROUPS)
    seg_end = jnp.cumsum(chunks, axis=1)
    seg_start = seg_end - chunks
    src_end = jnp.cumsum(chunks, axis=0)
    src_start = src_end - chunks
    total = src_end[-1]
    tiles = (total + TILE_CHUNKS - 1) // TILE_CHUNKS
    tile_end = jnp.cumsum(tiles)
    tile_start = tile_end - tiles
    n_used = tile_end[-1:]

    u = jnp.arange(n_tiles)
    gid = jnp.minimum(jnp.sum(u[:, None] >= tile_end[None, :], axis=1), N_EXPERT_GROUPS - 1)

    p = jnp.arange(n_tiles * TILE_CHUNKS)
    g_hot = jnp.repeat(gid, TILE_CHUNKS)[:, None] == groups[None, :]
    by_group = lambda vec: jnp.sum(jnp.where(g_hot, vec[None, :], 0), axis=1)
    q = p - TILE_CHUNKS * by_group(tile_start)
    live = q < by_group(total)
    src_end_p = jnp.sum(jnp.where(g_hot[:, None, :], src_end[None], 0), axis=2)
    t = jnp.minimum(jnp.sum(q[:, None] >= src_end_p, axis=1), n_src - 1)
    t_hot = t[:, None] == jnp.arange(n_src)[None, :]
    by_seg = lambda tab: jnp.sum(jnp.where(t_hot[:, :, None] & g_hot[:, None, :], tab[None], 0),
                                 axis=(1, 2))
    slab_chunk = by_seg(seg_start) + q - by_seg(src_start)
    base = jnp.sum(jnp.where(t_hot, slab_base[None, :], 0), axis=1)
    src_row = jnp.where(live, base + CHUNK_ROWS * slab_chunk, 0)

    s = jnp.arange(SLAB_CHUNKS)
    gs = jnp.sum(s[None, :, None] >= seg_end[:, None, :], axis=2)
    used = gs < N_EXPERT_GROUPS
    s_hot = jnp.minimum(gs, N_EXPERT_GROUPS - 1)[:, :, None] == groups
    pick = lambda tab: jnp.sum(jnp.where(s_hot, tab, 0), axis=2)
    pos = (TILE_CHUNKS * pick(tile_start[None, None, :]) + pick(src_start[:, None, :])
           + s[None, :] - pick(seg_start[:, None, :]))
    out_row = jnp.where(used, CHUNK_ROWS * pos, 0).reshape(-1)
    i32 = lambda a: a.astype(jnp.int32)
    return i32(gid), i32(src_row), i32(out_row), i32(n_used)


def _chunk_copy(src_hbm, row, k, buf, sem, slot):
    return pltpu.make_async_copy(
        src_hbm.at[pl.ds(pl.multiple_of(row, CHUNK_ROWS), CHUNK_ROWS), :],
        buf.at[slot, pl.ds(k * CHUNK_ROWS, CHUNK_ROWS), :],
        sem.at[slot])


def _start_chunks(src_hbm, row_ref, first, n_chunks, buf, sem, slot):
    for k in range(n_chunks):
        _chunk_copy(src_hbm, row_ref[first + k], k, buf, sem, slot).start()


def _gather_step(src_hbm, row_ref, n_chunks, buf, sem):
    step = pl.program_id(0)
    slot = step % 2

    @pl.when(step == 0)
    def _():
        _start_chunks(src_hbm, row_ref, 0, n_chunks, buf, sem, 0)

    @pl.when(step + 1 < pl.num_programs(0))
    def _():
        _start_chunks(src_hbm, row_ref, (step + 1) * n_chunks, n_chunks, buf, sem, 1 - slot)

    for k in range(n_chunks):
        _chunk_copy(src_hbm, 0, k, buf, sem, slot).wait()
    return slot


def _moe_kernel(gid_ref, src_ref, nused_ref, slab_hbm, w1_ref, w3_ref, w2_ref, o_ref,
                buf, sem, w1b, w3b, w2b):
    u = pl.program_id(0)
    slot = _gather_step(slab_hbm, src_ref, TILE_CHUNKS, buf, sem)

    @pl.when((u == 0) | (gid_ref[u] != gid_ref[jnp.maximum(u - 1, 0)]))
    def _():
        w1b[...] = w1_ref[0].astype(BF16)
        w3b[...] = w3_ref[0].astype(BF16)
        w2b[...] = w2_ref[0, 0].astype(BF16)

    @pl.when(u < nused_ref[0])
    def _():
        h = buf[slot, :, 0:D_MODEL]
        gates = buf[slot, :, D_MODEL:].astype(F32)
        hid = []
        for e in range(EXPERTS_PER_GROUP):
            h1 = jnp.dot(h, w1b[e], preferred_element_type=F32)
            h3 = jnp.dot(h, w3b[e], preferred_element_type=F32)
            act = (h1 * jax.nn.sigmoid(h1)) * h3
            gate = gates[:, e:e + 1] + gates[:, EXPERTS_PER_GROUP + e:EXPERTS_PER_GROUP + e + 1]
            hid.append((act * gate).astype(BF16))
        o_ref[...] = jnp.dot(jnp.concatenate(hid, axis=1), w2b[...], preferred_element_type=F32)

    @pl.when(u >= nused_ref[0])
    def _():
        o_ref[...] = jnp.zeros_like(o_ref)


def _moe(slabs, tables, layer, params):
    gid, src_row, _, n_used = tables
    n_tiles = gid.shape[0]
    expert_w = pl.BlockSpec((1, EXPERTS_PER_GROUP, D_MODEL, D_EXPERT),
                            lambda u, gid, src, nu: (layer, gid[u], 0, 0))
    return pl.pallas_call(
        _moe_kernel,
        grid_spec=pltpu.PrefetchScalarGridSpec(
            num_scalar_prefetch=3,
            grid=(n_tiles,),
            in_specs=[
                pl.BlockSpec(memory_space=pl.ANY),
                expert_w,
                expert_w,
                pl.BlockSpec((1, 1, GROUP_HIDDEN, D_MODEL), lambda u, gid, src, nu: (layer, gid[u], 0, 0)),
            ],
            out_specs=pl.BlockSpec((MOE_TILE, D_MODEL), lambda u, gid, src, nu: (u, 0)),
            scratch_shapes=[
                pltpu.VMEM((2, MOE_TILE, SLAB_WIDTH), BF16),
                pltpu.SemaphoreType.DMA((2,)),
                pltpu.VMEM((EXPERTS_PER_GROUP, D_MODEL, D_EXPERT), BF16),
                pltpu.VMEM((EXPERTS_PER_GROUP, D_MODEL, D_EXPERT), BF16),
                pltpu.VMEM((GROUP_HIDDEN, D_MODEL), BF16),
            ],
        ),
        out_shape=jax.ShapeDtypeStruct((n_tiles * MOE_TILE, D_MODEL), F32),
        compiler_params=_cparams("arbitrary"),
        name="moe",
    )(gid, src_row, n_used, slabs, params["w1"], params["w3"], params["w2"])


def _unsorted_moe(row_ref, dest_ref, y_hbm, buf, sem, n_slabs):
    slot = _gather_step(y_hbm, row_ref, n_slabs * SLAB_CHUNKS, buf, sem)
    tn = (((0,), (0,)), ((), ()))
    out = []
    for j in range(n_slabs):
        y = buf[slot, j * SLAB_ROWS:(j + 1) * SLAB_ROWS, :]
        hi = y.astype(BF16)
        lo = (y - hi.astype(F32)).astype(BF16)
        perm = _permutation(dest_ref[j])
        out.append(lax.dot_general(perm, hi, tn, preferred_element_type=F32)
                   + lax.dot_general(perm, lo, tn, preferred_element_type=F32))
    return jnp.concatenate(out, axis=0)


def _unsort_kernel(row_ref, xm_ref, dest_ref, mod_ref, y_hbm, o_ref, buf, sem, *, n_slabs):
    moe = _unsorted_moe(row_ref, dest_ref, y_hbm, buf, sem, n_slabs)
    o_ref[...] = xm_ref[...] + mod_ref[0, 0][5:6] * moe


def _unsort(pending, layer, mod, row_of_tile, *, tm):
    xm, dest, out_row, y = pending
    t_tokens = xm.shape[0]
    n_slabs = tm // SRC_TILE
    return pl.pallas_call(
        functools.partial(_unsort_kernel, n_slabs=n_slabs),
        grid_spec=pltpu.PrefetchScalarGridSpec(
            num_scalar_prefetch=1,
            grid=(t_tokens // tm,),
            in_specs=[
                pl.BlockSpec((tm, D_MODEL), lambda t, rows: (t, 0)),
                pl.BlockSpec((n_slabs, 1, SRC_TILE), lambda t, rows: (t, 0, 0)),
                pl.BlockSpec((1, 1, 6, D_MODEL), lambda t, rows: (layer, row_of_tile(t), 0, 0)),
                pl.BlockSpec(memory_space=pl.ANY),
            ],
            out_specs=pl.BlockSpec((tm, D_MODEL), lambda t, rows: (t, 0)),
            scratch_shapes=[pltpu.VMEM((2, n_slabs * SLAB_ROWS, D_MODEL), F32),
                            pltpu.SemaphoreType.DMA((2,))],
        ),
        out_shape=jax.ShapeDtypeStruct((t_tokens, D_MODEL), F32),
        compiler_params=_cparams("arbitrary"),
        name="unsort",
    )(out_row, xm, dest, mod, y)


def kernel(x_prompt, x_sample, cache_k, cache_v, c, c_ctx, w_ada, b_ada, norm1_w, norm2_w, w_in,
           q_norm_w, k_norm_w, rpb, w_sgu, b_sgu, sgu_ln_w, sgu_ln_b, w_out, w_router, b_router,
           w1, w3, w2):
    batch, seq, _ = x_prompt.shape
    dec_batch, dec_seq, _ = x_sample.shape
    past = cache_k.shape[2]
    assert 1 + dec_batch <= COND_ROWS and dec_seq % (Q_ROWS * GRID_W) == 0 and seq % CHUNK == 0

    cond = jnp.zeros((COND_ROWS, D_MODEL), F32).at[0].set(c_ctx).at[1:1 + dec_batch].set(c)
    mod = _adaln(cond, w_ada, b_ada).reshape(DEPTH, COND_ROWS, 6, D_MODEL)

    xp = x_prompt.reshape(batch * seq, D_MODEL)
    xs = x_sample.reshape(dec_batch * dec_seq, D_MODEL)
    from_cache_layout = lambda a: a.transpose(0, 1, 3, 4, 2).reshape(dec_batch, DEPTH, ATTN_WIDTH, past)
    ck = from_cache_layout(cache_k)
    cv = from_cache_layout(cache_v)

    head_of = jnp.arange(256) // HEAD_DIM
    bd = (head_of[:, None] == head_of[None, :]).astype(BF16)
    tok_id = jnp.arange(SRC_TILE)
    tri = (tok_id[:, None] <= tok_id[None, :]).astype(BF16)

    params = dict(
        n1w=norm1_w.reshape(DEPTH, 1, D_MODEL),
        n2w=norm2_w.reshape(DEPTH, 1, D_MODEL),
        win=w_in.astype(BF16),
        wout=w_out.astype(BF16),
        qw=jnp.tile(q_norm_w, (1, N_HEADS)).reshape(DEPTH, 1, ATTN_WIDTH),
        kw=jnp.tile(k_norm_w, (1, N_HEADS)).reshape(DEPTH, 1, ATTN_WIDTH),
        bd=jnp.concatenate([bd, bd], axis=0),
        ws=w_sgu.astype(BF16),
        bs=jnp.broadcast_to(b_sgu[..., None], (DEPTH, SGU_GROUPS, CHUNK, SGU_GROUP_DIM)),
        lnw=sgu_ln_w,
        lnb=sgu_ln_b,
        wr=w_router.T.astype(BF16),
        br=b_router.reshape(N_EXPERTS, 1),
        tri=tri,
        w1=w1,
        w3=w3,
        w2=w2.reshape(DEPTH, N_EXPERT_GROUPS, GROUP_HIDDEN, D_MODEL),
    )
    tb = _bias_tables(rpb)

    tm = 2 * SRC_TILE
    assert dec_seq % tm == 0 and tm % seq == 0
    lat_tiles_per_seq = dec_seq // tm
    ctx_row = lambda i: 0
    lat_row = lambda i: 1 + i // lat_tiles_per_seq

    def moe_block(attended, layer):
        chunks = jnp.concatenate([meta[:, :N_EXPERT_GROUPS, 0] for _, _, _, meta in attended], axis=0)
        tables = _chunk_tables(chunks, slab_base, _max_moe_tiles(chunks.shape[0]))
        y = _moe(attended[-1][1], tables, layer, params)
        pending, first = [], 0
        for xm, _, dest, meta in attended:
            n_rows = meta.shape[0] * SLAB_CHUNKS
            pending.append((xm, dest, tables[2][first:first + n_rows], y))
            first += n_rows
        return pending

    n_ctx, n_lat = batch * seq // SRC_TILE, dec_batch * dec_seq // SRC_TILE
    assert n_lat % n_ctx == 0
    slab_share = 1 + n_lat // n_ctx
    lat_tile = jnp.arange(n_lat)
    slab_base = SLAB_ROWS * jnp.concatenate([
        slab_share * jnp.arange(n_ctx),
        slab_share * (lat_tile // (slab_share - 1)) + 1 + lat_tile % (slab_share - 1)])

    kv_new = jax.ShapeDtypeStruct((batch, DEPTH, ATTN_WIDTH, seq), F32)

    for l in range(DEPTH):
        outs = _inproj(xp, l, mod, ctx_row, params, tm=tm,
                       kv_buffers=kv_new if l == 0 else (kbuf, vbuf))
        if l > 0:
            xp, outs = outs[0], outs[1:]
        q, kbuf, vbuf, v, sg = outs
        attended_ctx = _ctx_attention(q, kbuf, v, sg, xp, l, mod, params, seq=seq, slab_share=slab_share)

        outs = _inproj(xs, l, mod, lat_row, params, tm=tm)
        if l > 0:
            xs, outs = outs[0], outs[1:]
        q, k, v, sg = outs
        attended_lat = _nbr_attention(q, k, v, ck, cv, tb, sg, xs, l, mod, params, attended_ctx[1],
                                      batch=dec_batch, seq=dec_seq, slab_share=slab_share)

        xp, xs = moe_block([attended_ctx, attended_lat], l)
    xp = _unsort(xp, DEPTH - 1, mod, ctx_row, tm=tm)
    xs = _unsort(xs, DEPTH - 1, mod, lat_row, tm=tm)

    to_cache_layout = lambda buf: buf.reshape(batch, DEPTH, N_HEADS, HEAD_DIM, seq).transpose(0, 1, 4, 2, 3)
    return (xp.reshape(batch, seq, D_MODEL), xs.reshape(dec_batch, dec_seq, D_MODEL),
            to_cache_layout(kbuf), to_cache_layout(vbuf))
```

```python
import functools

import jax
import jax.numpy as jnp
from jax import lax
from jax.experimental import pallas as pl
from jax.experimental.pallas import tpu as pltpu

F32 = jnp.float32
BF16 = jnp.bfloat16

D_MODEL = 1024
DEPTH = 4
N_HEADS = 8
HEAD_DIM = 64
ATTN_WIDTH = N_HEADS * HEAD_DIM
SGU_GROUPS = 4
SGU_GROUP_DIM = 128
SGU_WIDTH = SGU_GROUPS * SGU_GROUP_DIM
CHUNK = 128
IN_WIDTH = 3 * ATTN_WIDTH + 2 * SGU_WIDTH
GRID_W = 64
WIN_H = 8
WIN_W = 16
N_EXPERTS = 16
N_EXPERT_GROUPS = 4
EXPERTS_PER_GROUP = 4
D_EXPERT = 256
GROUP_HIDDEN = EXPERTS_PER_GROUP * D_EXPERT
EPS = 1e-6
NEG = -1e30
LOG2E = 1.4426950408889634

LANES = 128
HEADS_PER_VREG = LANES // HEAD_DIM
N_HEAD_PAIRS = N_HEADS // HEADS_PER_VREG
COND_ROWS = 16
Q_ROWS = 4
WIN_ROWS = Q_ROWS + WIN_H
VMEM_LIMIT = 48 * 1024 * 1024

SRC_TILE = 256
CHUNK_ROWS = 16
SLAB_ROWS = SRC_TILE + N_EXPERT_GROUPS * CHUNK_ROWS
SLAB_CHUNKS = SLAB_ROWS // CHUNK_ROWS
SLAB_WIDTH = D_MODEL + LANES
MOE_TILE = 512
TILE_CHUNKS = MOE_TILE // CHUNK_ROWS


def _cparams(*sem):
    return pltpu.CompilerParams(dimension_semantics=sem, vmem_limit_bytes=VMEM_LIMIT)


def _adaln_kernel(cond_ref, w_ref, b_ref, o_ref):
    c = cond_ref[...]
    a = (c * jax.nn.sigmoid(c)).astype(BF16)
    o_ref[0] = jnp.dot(a, w_ref[0].astype(BF16), preferred_element_type=F32) + b_ref[0]


def _adaln(cond, w_ada, b_ada):
    tn = 1536
    n = 6 * D_MODEL
    return pl.pallas_call(
        _adaln_kernel,
        grid=(DEPTH, n // tn),
        in_specs=[
            pl.BlockSpec((COND_ROWS, D_MODEL), lambda l, j: (0, 0)),
            pl.BlockSpec((1, D_MODEL, tn), lambda l, j: (l, 0, j)),
            pl.BlockSpec((1, 1, tn), lambda l, j: (l, 0, j)),
        ],
        out_specs=pl.BlockSpec((1, COND_ROWS, tn), lambda l, j: (l, 0, j)),
        out_shape=jax.ShapeDtypeStruct((DEPTH, COND_ROWS, n), F32),
        compiler_params=_cparams("arbitrary", "arbitrary"),
        name="adaln",
    )(cond, w_ada, b_ada.reshape(DEPTH, 1, n))


def _gelu_tanh(x):
    return 0.5 * x * (1.0 + jnp.tanh(0.7978845608028654 * (x + 0.044715 * (x * x * x))))


def _head_rms(t, bd, w):
    t2 = t * t
    hi = t2.astype(BF16)
    lo = (t2 - hi.astype(F32)).astype(BF16)
    outs = []
    for c in range(ATTN_WIDTH // 256):
        sl = slice(256 * c, 256 * c + 256)
        hl = jnp.concatenate([hi[:, sl], lo[:, sl]], axis=1)
        ss = jnp.dot(hl, bd, preferred_element_type=F32)
        outs.append(t[:, sl] * lax.rsqrt(ss * (1.0 / HEAD_DIM) + EPS))
    return jnp.concatenate(outs, axis=1) * w


def _inproj_kernel(*refs, tm, transposed_kv, new_kv_layer, fused_moe):
    refs = list(refs)
    if fused_moe:
        row_ref, xm_ref, dest_ref, prev_mod_ref, y_hbm = refs[:5]
        refs = refs[5:]
    else:
        x_ref = refs.pop(0)
    (mod_ref, n1w_ref, win_ref, qw_ref, kw_ref, bd_ref, ws_ref, bs_ref, lnw_ref, lnb_ref) = refs[:10]
    refs = refs[10:]
    if transposed_kv and new_kv_layer is None:
        refs = refs[2:]
    if fused_moe:
        x_out_ref = refs.pop(0)
    if transposed_kv:
        q_ref, kt_ref, vt_ref, v_ref, sg_ref = refs[:5]
        refs = refs[5:]
    else:
        q_ref, k_ref, v_ref, sg_ref = refs[:4]
        refs = refs[4:]
    if fused_moe:
        buf, sem = refs
        moe = _unsorted_moe(row_ref, dest_ref, y_hbm, buf, sem, tm // SRC_TILE)
        x = xm_ref[...] + prev_mod_ref[0, 0][5:6] * moe
        x_out_ref[...] = x
    else:
        x = x_ref[...]
    m = mod_ref[0, 0]
    sh1, sc1 = m[0:1], m[1:2]
    ms = jnp.mean(x * x, axis=-1, keepdims=True)
    h = x * lax.rsqrt(ms + EPS) * n1w_ref[0]
    h = (h * (1.0 + sc1) + sh1).astype(BF16)
    p = jnp.dot(h, win_ref[0], preferred_element_type=F32)

    bd = bd_ref[...]
    q = _head_rms(p[:, 0:ATTN_WIDTH], bd, qw_ref[0])
    k = _head_rms(p[:, ATTN_WIDTH:2 * ATTN_WIDTH], bd, kw_ref[0])
    v = p[:, 2 * ATTN_WIDTH:3 * ATTN_WIDTH]
    q_ref[...] = (q * (HEAD_DIM ** -0.5 * LOG2E)).astype(q_ref.dtype)
    if transposed_kv:
        seq = kt_ref.shape[3]
        if new_kv_layer is None:
            slot = 0
        else:
            slot = new_kv_layer
            kt_ref[...] = jnp.zeros_like(kt_ref)
            vt_ref[...] = jnp.zeros_like(vt_ref)
        for j in range(tm // seq):
            kt_ref[j, slot] = k[j * seq:(j + 1) * seq].T
            vt_ref[j, slot] = v[j * seq:(j + 1) * seq].T
    else:
        k_ref[...] = k.astype(k_ref.dtype)
    v_ref[...] = v.astype(v_ref.dtype)

    off_u = 3 * ATTN_WIDTH
    off_v = off_u + SGU_WIDTH
    for g in range(SGU_GROUPS):
        gl = slice(g * SGU_GROUP_DIM, (g + 1) * SGU_GROUP_DIM)
        u = _gelu_tanh(p[:, off_u + g * SGU_GROUP_DIM: off_u + (g + 1) * SGU_GROUP_DIM])
        t = _gelu_tanh(p[:, off_v + g * SGU_GROUP_DIM: off_v + (g + 1) * SGU_GROUP_DIM])
        mu = jnp.mean(t, axis=-1, keepdims=True)
        d = t - mu
        var = jnp.mean(d * d, axis=-1, keepdims=True)
        y = (d * lax.rsqrt(var + EPS) * lnw_ref[0, g:g + 1, :] + lnb_ref[0, g:g + 1, :]).astype(BF16)
        for c in range(tm // CHUNK):
            rows = slice(c * CHUNK, (c + 1) * CHUNK)
            sv = jnp.dot(ws_ref[0, g], y[rows], preferred_element_type=F32) + bs_ref[0, g]
            sg_ref[rows, gl] = (u[rows] * sv).astype(sg_ref.dtype)


def _inproj(x, layer, mod, row_of_tile, params, *, tm, kv_buffers=None):
    fused_moe = isinstance(x, tuple)
    tok = lambda i, *_: (i, 0)
    lyr3 = lambda i, *_: (layer, 0, 0)
    lyr4 = lambda i, *_: (layer, 0, 0, 0)
    mod_row = lambda lyr: pl.BlockSpec((1, 1, 6, D_MODEL), lambda i, *_: (lyr, row_of_tile(i), 0, 0))
    x_spec = pl.BlockSpec((tm, D_MODEL), tok)
    if fused_moe:
        n_slabs = tm // SRC_TILE
        xm, dest, out_row, y = x
        t_tokens = xm.shape[0]
        prefetch = [out_row]
        in_specs = [x_spec, pl.BlockSpec((n_slabs, 1, SRC_TILE), lambda i, *_: (i, 0, 0)),
                    mod_row(layer - 1), pl.BlockSpec(memory_space=pl.ANY)]
        args = [xm, dest, mod, y]
        scratch = [pltpu.VMEM((2, n_slabs * SLAB_ROWS, D_MODEL), y.dtype), pltpu.SemaphoreType.DMA((2,))]
    else:
        t_tokens = x.shape[0]
        prefetch, in_specs, args, scratch = [], [x_spec], [x], []
    act = pl.BlockSpec((tm, ATTN_WIDTH), tok)
    act_sds = jax.ShapeDtypeStruct((t_tokens, ATTN_WIDTH), BF16)
    in_specs += [
        mod_row(layer),
        pl.BlockSpec((1, 1, D_MODEL), lyr3),
        pl.BlockSpec((1, D_MODEL, IN_WIDTH), lyr3),
        pl.BlockSpec((1, 1, ATTN_WIDTH), lyr3),
        pl.BlockSpec((1, 1, ATTN_WIDTH), lyr3),
        pl.BlockSpec((512, 256), lambda i, *_: (0, 0)),
        pl.BlockSpec((1, SGU_GROUPS, CHUNK, CHUNK), lyr4),
        pl.BlockSpec((1, SGU_GROUPS, CHUNK, SGU_GROUP_DIM), lyr4),
        pl.BlockSpec((1, SGU_GROUPS, SGU_GROUP_DIM), lyr3),
        pl.BlockSpec((1, SGU_GROUPS, SGU_GROUP_DIM), lyr3),
    ]
    args += [mod, params["n1w"], params["win"], params["qw"], params["kw"], params["bd"],
             params["ws"], params["bs"], params["lnw"], params["lnb"]]
    out_specs = [x_spec] if fused_moe else []
    out_shape = [jax.ShapeDtypeStruct((t_tokens, D_MODEL), F32)] if fused_moe else []
    if kv_buffers is None:
        out_specs += [act] * 4
        out_shape += [act_sds] * 4
        aliases = {}
    elif isinstance(kv_buffers, jax.ShapeDtypeStruct):
        seq = kv_buffers.shape[3]
        kv_spec = pl.BlockSpec((tm // seq,) + kv_buffers.shape[1:], lambda i, *_: (i, 0, 0, 0))
        out_specs += [act, kv_spec, kv_spec, act, act]
        out_shape += [act_sds, kv_buffers, kv_buffers, act_sds, act_sds]
        aliases = {}
    else:
        kbuf, vbuf = kv_buffers
        seq = kbuf.shape[3]
        kv_spec = pl.BlockSpec((tm // seq, 1, ATTN_WIDTH, seq), lambda i, *_: (i, layer, 0, 0))
        first_alias_in = len(prefetch) + len(args)
        first_alias_out = len(out_specs) + 1
        in_specs += [pl.BlockSpec(memory_space=pl.ANY)] * 2
        args += [kbuf, vbuf]
        out_specs += [act, kv_spec, kv_spec, act, act]
        out_shape += [act_sds, jax.ShapeDtypeStruct(kbuf.shape, F32),
                      jax.ShapeDtypeStruct(vbuf.shape, F32), act_sds, act_sds]
        aliases = {first_alias_in: first_alias_out, first_alias_in + 1: first_alias_out + 1}
    return pl.pallas_call(
        functools.partial(_inproj_kernel, tm=tm, transposed_kv=kv_buffers is not None,
                          new_kv_layer=layer if isinstance(kv_buffers, jax.ShapeDtypeStruct) else None,
                          fused_moe=fused_moe),
        grid_spec=pltpu.PrefetchScalarGridSpec(
            num_scalar_prefetch=len(prefetch),
            grid=(t_tokens // tm,),
            in_specs=in_specs,
            out_specs=out_specs,
            scratch_shapes=scratch,
        ),
        out_shape=out_shape,
        input_output_aliases=aliases,
        compiler_params=_cparams("arbitrary"),
        name="inproj",
    )(*prefetch, *args)


def _route(lg_t, br):
    s = jax.nn.sigmoid(lg_t)
    sel = s + br
    row = lambda a, i: a[i:i + 1]
    g_score = []
    for g in range(N_EXPERT_GROUPS):
        v = [row(sel, 4 * g + i) for i in range(4)]
        best_pair = None
        for i in range(4):
            for j in range(i + 1, 4):
                pair = v[i] + v[j]
                best_pair = pair if best_pair is None else jnp.maximum(best_pair, pair)
        g_score.append(best_pair)
    best = jnp.zeros_like(g_score[0], dtype=jnp.int32)
    top = g_score[0]
    for g in range(1, N_EXPERT_GROUPS):
        upd = g_score[g] > top
        best = jnp.where(upd, g, best)
        top = jnp.where(upd, g_score[g], top)
    cand, aff = [], []
    for i in range(4):
        ci, si = row(sel, i), row(s, i)
        for g in range(1, N_EXPERT_GROUPS):
            ci = jnp.where(best == g, row(sel, 4 * g + i), ci)
            si = jnp.where(best == g, row(s, 4 * g + i), si)
        cand.append(ci)
        aff.append(si)

    def first_argmax(vals):
        idx = jnp.zeros_like(best)
        top_v = vals[0]
        for i in range(1, 4):
            upd = vals[i] > top_v
            idx = jnp.where(upd, i, idx)
            top_v = jnp.where(upd, vals[i], top_v)
        return idx

    i1 = first_argmax(cand)
    i2 = first_argmax([jnp.where(i1 == i, -jnp.inf, cand[i]) for i in range(4)])
    pick = lambda idx: sum(jnp.where(idx == i, aff[i], 0.0) for i in range(4))
    den = pick(i1) + pick(i2)
    gate = [jnp.where((i1 == i) | (i2 == i), aff[i] / den, 0.0) for i in range(4)]
    hi = [x.astype(BF16) for x in gate]
    lo = [(x - h.astype(F32)).astype(BF16) for x, h in zip(gate, hi)]
    return best, hi + lo


def _slab_positions(best, tri):
    n = best.shape[1]
    onehot = [jnp.where(best == g, 1.0, 0.0) for g in range(N_EXPERT_GROUPS)]
    pad = [jnp.zeros_like(onehot[0])] * (8 - N_EXPERT_GROUPS)
    oh = jnp.concatenate(onehot + pad, axis=0).astype(BF16)
    counts = jnp.dot(oh, tri, preferred_element_type=F32)
    dest = jnp.zeros((1, n), F32)
    seg_start = jnp.zeros((1, 1), F32)
    chunks = []
    for g in range(N_EXPERT_GROUPS):
        cg = counts[g:g + 1]
        n_g = jnp.max(cg, axis=1, keepdims=True)
        c_g = jnp.floor((n_g + (CHUNK_ROWS - 1)) * (1.0 / CHUNK_ROWS))
        dest = dest + onehot[g] * (seg_start + cg - 1.0)
        seg_start = seg_start + c_g * CHUNK_ROWS
        chunks.append(c_g)
    return dest.astype(jnp.int32), chunks


def _permutation(dest):
    n = dest.shape[1]
    hit = lax.broadcasted_iota(jnp.int32, (SLAB_ROWS, n), 0) == dest
    return jnp.where(hit, 1.0, 0.0).astype(BF16)


def _post_attention(a, sg, x, m, wout_ref, n2w_ref, xm_ref):
    g1, sh2, sc2 = m[2:3], m[3:4], m[4:5]
    cat = jnp.concatenate([a, sg], axis=1)
    y = jnp.dot(cat, wout_ref[0], preferred_element_type=F32)
    xm = x + g1 * y
    xm_ref[...] = xm
    ms = jnp.mean(xm * xm, axis=-1, keepdims=True)
    h2 = xm * lax.rsqrt(ms + EPS) * n2w_ref[0]
    return (h2 * (1.0 + sc2) + sh2).astype(BF16)


def _route_previous(h2_scr, wr_ref, br_ref, tri_ref, slab_ref, dest_ref, meta_ref):
    h2 = h2_scr[...]
    lg_t = lax.dot_general(wr_ref[...], h2, (((1,), (1,)), ((), ())), preferred_element_type=F32)
    yield
    best, gates = _route(lg_t, br_ref[...])
    dest, chunks = _slab_positions(best, tri_ref[...])
    yield
    perm = _permutation(dest)
    slab_ref[:, 0:D_MODEL] = jnp.dot(perm, h2, preferred_element_type=F32).astype(BF16)
    gmat = jnp.concatenate(gates + [jnp.zeros((LANES - len(gates), dest.shape[1]), BF16)], axis=0)
    slab_ref[:, D_MODEL:] = lax.dot_general(perm, gmat, (((1,), (1,)), ((), ())),
                                            preferred_element_type=F32).astype(BF16)
    dest_ref[0] = dest
    meta = [jnp.broadcast_to(c, (1, LANES)) for c in chunks]
    meta += [jnp.zeros((8 - len(chunks), LANES), F32)]
    meta_ref[0] = jnp.concatenate(meta, axis=0).astype(jnp.int32)


def _pair_pos(axis):
    return lax.broadcasted_iota(jnp.int32, (1, LANES) if axis == 1 else (LANES, 1), axis)


def _head_mask(hh, axis):
    pos = _pair_pos(axis)
    return (pos >= hh * HEAD_DIM) & (pos < (hh + 1) * HEAD_DIM)


def _only_head(x2, hh, axis):
    return jnp.where(_head_mask(hh, axis), x2, jnp.zeros_like(x2))


def _attend_heads(logits, finish, router):
    outs, even = [], None
    s_next = logits(0)
    for h in range(N_HEADS):
        s = s_next
        if h + 1 < N_HEADS:
            s_next = logits(h + 1)
        if h == N_HEADS // 2:
            next(router)
        o = finish(h, s)
        if h % HEADS_PER_VREG == 0:
            even = o
        else:
            outs.append((even + o).astype(BF16))
    for _ in router:
        pass
    return jnp.concatenate(outs, axis=1)


def _ctx_attn_kernel(q_ref, kt_ref, v_ref, sg_ref, x_ref, mod_ref, wout_ref, n2w_ref, wr_ref, br_ref,
                     tri_ref, xm_ref, slab_ref, dest_ref, meta_ref, h2_scr):
    step = pl.program_id(0)

    @pl.when(step == 0)
    def _():
        h2_scr[...] = jnp.zeros_like(h2_scr)

    def logits(h):
        cols = slice(h // HEADS_PER_VREG * LANES, (h // HEADS_PER_VREG + 1) * LANES)
        kt2 = kt_ref[0, 0, cols, :].astype(BF16)
        return jnp.dot(q_ref[:, cols], _only_head(kt2, h % HEADS_PER_VREG, 0),
                       preferred_element_type=F32)

    def finish(h, s):
        cols = slice(h // HEADS_PER_VREG * LANES, (h // HEADS_PER_VREG + 1) * LANES)
        e = jnp.exp2(s - jnp.max(s, axis=-1, keepdims=True))
        l = jnp.sum(e, axis=-1, keepdims=True)
        vm = _only_head(v_ref[:, cols], h % HEADS_PER_VREG, 1)
        return jnp.dot(e.astype(BF16), vm, preferred_element_type=F32) * (1.0 / l)

    new_router = lambda: _route_previous(h2_scr, wr_ref, br_ref, tri_ref, slab_ref, dest_ref, meta_ref)
    last = pl.num_programs(0) - 1

    @pl.when(step < last)
    def _():
        router = new_router()
        next(router)
        a = _attend_heads(logits, finish, router)
        h2_scr[...] = _post_attention(a, sg_ref[...], x_ref[...], mod_ref[0, 0], wout_ref, n2w_ref,
                                      xm_ref)

    @pl.when(step == last)
    def _():
        for _ in new_router():
            pass


def _post_attention_out(t_tokens):
    n_src = t_tokens // SRC_TILE
    prev = lambda s: jnp.maximum(s - 1, 0)
    specs = [
        pl.BlockSpec((SRC_TILE, D_MODEL), lambda s: (jnp.minimum(s, n_src - 1), 0)),
        pl.BlockSpec((SLAB_ROWS, SLAB_WIDTH), lambda s: (prev(s), 0)),
        pl.BlockSpec((1, 1, SRC_TILE), lambda s: (prev(s), 0, 0)),
        pl.BlockSpec((1, 8, LANES), lambda s: (prev(s), 0, 0)),
    ]
    shapes = [
        jax.ShapeDtypeStruct((t_tokens, D_MODEL), F32),
        jax.ShapeDtypeStruct((n_src * SLAB_ROWS, SLAB_WIDTH), BF16),
        jax.ShapeDtypeStruct((n_src, 1, SRC_TILE), jnp.int32),
        jax.ShapeDtypeStruct((n_src, 8, LANES), jnp.int32),
    ]
    return specs, shapes, pltpu.VMEM((SRC_TILE, D_MODEL), BF16)


def _ctx_attention(q, kbuf, v, sg, x, layer, mod, params, *, seq):
    assert seq == SRC_TILE
    t_tokens = x.shape[0]
    n_seq = t_tokens // seq
    seq_of = lambda s: jnp.minimum(s, n_seq - 1)
    tok = lambda s: (seq_of(s), 0)
    lyr3 = lambda s: (layer, 0, 0)
    full2 = lambda s: (0, 0)
    out_specs, out_shape, h2_scratch = _post_attention_out(t_tokens)
    return pl.pallas_call(
        _ctx_attn_kernel,
        grid=(n_seq + 1,),
        in_specs=[
            pl.BlockSpec((seq, ATTN_WIDTH), tok),
            pl.BlockSpec((1, 1, ATTN_WIDTH, seq), lambda s: (seq_of(s), layer, 0, 0)),
            pl.BlockSpec((seq, ATTN_WIDTH), tok),
            pl.BlockSpec((seq, SGU_WIDTH), tok),
            pl.BlockSpec((seq, D_MODEL), tok),
            pl.BlockSpec((1, 1, 6, D_MODEL), lambda s: (layer, 0, 0, 0)),
            pl.BlockSpec((1, D_MODEL, D_MODEL), lyr3),
            pl.BlockSpec((1, 1, D_MODEL), lyr3),
            pl.BlockSpec((N_EXPERTS, D_MODEL), full2),
            pl.BlockSpec((N_EXPERTS, 1), full2),
            pl.BlockSpec((SRC_TILE, SRC_TILE), full2),
        ],
        out_specs=out_specs,
        out_shape=out_shape,
        scratch_shapes=[h2_scratch],
        compiler_params=_cparams("arbitrary"),
        name="ctx_attn",
    )(q, kbuf, v, sg, x, mod, params["wout"], params["n2w"], params["wr"], params["br"], params["tri"])


def _row_start(r, rows):
    return jnp.clip(r - WIN_H // 2, 0, rows - WIN_H)


def _nbr_attn_kernel(q_ref, k_ref, v_ref, ck_ref, cv_ref, tb_ref, sg_ref, x_ref, mod_ref, wout_ref,
                     n2w_ref, wr_ref, br_ref, tri_ref, xm_ref, slab_ref, dest_ref, meta_ref,
                     kwm_ref, vwm_ref, kcm_ref, vcm_ref, h2_scr, *, rows, n_tiles):
    step = pl.program_id(0)
    tiles_per_seq = rows // Q_ROWS
    t = jnp.minimum(step, n_tiles - 1) % tiles_per_seq

    @pl.when(step == 0)
    def _():
        h2_scr[...] = jnp.zeros_like(h2_scr)

    @pl.when((t == 0) & (step < n_tiles))
    def _():
        for j in range(N_HEAD_PAIRS):
            cols = slice(j * LANES, (j + 1) * LANES)
            k2, v2 = k_ref[:, cols], v_ref[:, cols]
            ck2 = ck_ref[0, 0, cols, :].astype(BF16)
            cv2 = cv_ref[0, 0, cols, :].astype(BF16)
            for hh in range(HEADS_PER_VREG):
                h = HEADS_PER_VREG * j + hh
                kwm_ref[h] = _only_head(k2, hh, 1)
                vwm_ref[h] = _only_head(v2, hh, 1)
                kcm_ref[h] = _only_head(ck2, hh, 0)
                vcm_ref[h] = _only_head(cv2, hh, 0)

    new_router = lambda: _route_previous(h2_scr, wr_ref, br_ref, tri_ref, slab_ref, dest_ref, meta_ref)

    @pl.when(step == n_tiles)
    def _():
        for _ in new_router():
            pass

    @pl.when(step < n_tiles)
    def _():
        _attend_tile(t, new_router(), q_ref, tb_ref, sg_ref, x_ref, mod_ref, wout_ref, n2w_ref, xm_ref,
                     kwm_ref, vwm_ref, kcm_ref, vcm_ref, h2_scr, rows)


def _attend_tile(t, router, q_ref, tb_ref, sg_ref, x_ref, mod_ref, wout_ref, n2w_ref, xm_ref,
                 kwm_ref, vwm_ref, kcm_ref, vcm_ref, h2_scr, rows):
    next(router)
    r0 = t * Q_ROWS
    ws = jnp.minimum(_row_start(r0, rows), rows - WIN_ROWS)
    tok0 = pl.multiple_of(ws * GRID_W, GRID_W)

    lane = lax.broadcasted_iota(jnp.int32, (1, LANES), 1)
    blk_idx, row_mask = [], []
    for a in range(Q_ROWS):
        r = r0 + a
        rs = _row_start(r, rows)
        idx_a, mask_a = [], []
        for jp in range(WIN_ROWS // 2):
            kr = ws + 2 * jp
            idx_a.append(jnp.clip(kr - r + WIN_H, 0, 2 * WIN_H - 1))
            ok0 = (kr >= rs) & (kr < rs + WIN_H)
            ok1 = (kr + 1 >= rs) & (kr + 1 < rs + WIN_H)
            m0 = jnp.where(ok0, 0.0, NEG).astype(F32)
            m1 = jnp.where(ok1, 0.0, NEG).astype(F32)
            mask_a.append(jnp.where(lane < GRID_W, m0, m1))
        blk_idx.append(idx_a)
        row_mask.append(mask_a)

    nt = (((1,), (1,)), ((), ()))
    window = pl.ds(tok0, WIN_ROWS * GRID_W)

    def logits(h):
        q2 = q_ref[:, h // HEADS_PER_VREG * LANES:(h // HEADS_PER_VREG + 1) * LANES]
        bias = jnp.concatenate([
            jnp.concatenate([tb_ref[0, h, blk_idx[a][jp]] + row_mask[a][jp]
                             for jp in range(WIN_ROWS // 2)], axis=1)
            for a in range(Q_ROWS)], axis=0)
        s_w = lax.dot_general(q2, kwm_ref[h, window, :], nt, preferred_element_type=F32) + bias
        s_c = jnp.dot(q2, kcm_ref[h], preferred_element_type=F32)
        return s_w, s_c

    def finish(h, s):
        s_w, s_c = s
        mx = jnp.maximum(jnp.max(s_w, axis=-1, keepdims=True), jnp.max(s_c, axis=-1, keepdims=True))
        e_w = jnp.exp2(s_w - mx)
        e_c = jnp.exp2(s_c - mx)
        l = jnp.sum(e_w, axis=-1, keepdims=True) + jnp.sum(e_c, axis=-1, keepdims=True)
        return (jnp.dot(e_w.astype(BF16), vwm_ref[h, window, :], preferred_element_type=F32)
                + lax.dot_general(e_c.astype(BF16), vcm_ref[h], nt,
                                  preferred_element_type=F32)) * (1.0 / l)

    a_out = _attend_heads(logits, finish, router)
    h2_scr[...] = _post_attention(a_out, sg_ref[...], x_ref[...], mod_ref[0, 0], wout_ref, n2w_ref,
                                  xm_ref)


def _nbr_attention(q, k, v, cache_k, cache_v, tb, sg, x, layer, mod, params, *, batch, seq):
    t_tokens = x.shape[0]
    rows = seq // GRID_W
    tq = Q_ROWS * GRID_W
    assert tq == SRC_TILE
    nt = seq // tq
    past = cache_k.shape[3]
    n_tiles = batch * nt
    out_specs, out_shape, h2_scratch = _post_attention_out(t_tokens)
    tile_of = lambda s: jnp.minimum(s, n_tiles - 1)
    seq_of = lambda s: tile_of(s) // nt
    tok = lambda s: (tile_of(s), 0)
    per_b = lambda s: (seq_of(s), 0)
    lyr3 = lambda s: (layer, 0, 0)
    full2 = lambda s: (0, 0)
    cache_spec = pl.BlockSpec((1, 1, ATTN_WIDTH, past), lambda s: (seq_of(s), layer, 0, 0))
    masked = lambda n: pltpu.VMEM((N_HEADS, n, LANES), BF16)
    masked_t = pltpu.VMEM((N_HEADS, LANES, past), BF16)
    return pl.pallas_call(
        functools.partial(_nbr_attn_kernel, rows=rows, n_tiles=n_tiles),
        grid=(n_tiles + 1,),
        in_specs=[
            pl.BlockSpec((tq, ATTN_WIDTH), tok),
            pl.BlockSpec((seq, ATTN_WIDTH), per_b),
            pl.BlockSpec((seq, ATTN_WIDTH), per_b),
            cache_spec,
            cache_spec,
            pl.BlockSpec((1, N_HEADS, 2 * WIN_H, GRID_W, LANES), lambda s: (layer, 0, 0, 0, 0)),
            pl.BlockSpec((tq, SGU_WIDTH), tok),
            pl.BlockSpec((tq, D_MODEL), tok),
            pl.BlockSpec((1, 1, 6, D_MODEL), lambda s: (layer, 1 + seq_of(s), 0, 0)),
            pl.BlockSpec((1, D_MODEL, D_MODEL), lyr3),
            pl.BlockSpec((1, 1, D_MODEL), lyr3),
            pl.BlockSpec((N_EXPERTS, D_MODEL), full2),
            pl.BlockSpec((N_EXPERTS, 1), full2),
            pl.BlockSpec((SRC_TILE, SRC_TILE), full2),
        ],
        out_specs=out_specs,
        out_shape=out_shape,
        scratch_shapes=[masked(seq), masked(seq), masked_t, masked_t, h2_scratch],
        compiler_params=_cparams("arbitrary"),
        name="nbr_attn",
    )(q, k, v, cache_k, cache_v, tb, sg, x, mod, params["wout"], params["n2w"], params["wr"],
      params["br"], params["tri"])


def _bias_tables(rpb):
    n_rel = 2 * WIN_W - 1
    cols = jnp.arange(GRID_W)
    col_start = jnp.clip(cols - WIN_W // 2, 0, GRID_W - WIN_W)
    col_ok = (cols[None, :] >= col_start[:, None]) & (cols[None, :] < col_start[:, None] + WIN_W)
    rel = cols[None, :] - cols[:, None] + WIN_W - 1
    pick = ((rel[None] == jnp.arange(n_rel)[:, None, None]) & col_ok[None]).astype(F32)
    zero = jnp.zeros_like(pick)
    halves = jnp.concatenate([jnp.concatenate([pick, zero], axis=2),
                              jnp.concatenate([zero, pick], axis=2)], axis=0)
    masked = jnp.where(jnp.concatenate([col_ok, col_ok], axis=1), 0.0, NEG)[None]
    select = jnp.concatenate([halves, masked], axis=0)

    by_offset = jnp.pad(rpb * LOG2E, ((0, 0), (0, 0), (1, 1), (0, 0)))
    feats = jnp.concatenate([by_offset[:, :, :-1], by_offset[:, :, 1:],
                             jnp.ones(by_offset.shape[:2] + (2 * WIN_H, 1), F32)], axis=-1)
    return jnp.einsum("lhdc,cqn->lhdqn", feats, select, precision=lax.Precision.HIGHEST)


def _max_moe_tiles(n_src):
    max_chunks = n_src * SRC_TILE // CHUNK_ROWS + n_src * N_EXPERT_GROUPS
    return -(-max_chunks // TILE_CHUNKS) + N_EXPERT_GROUPS


def _chunk_tables(chunks, n_tiles):
    n_src = chunks.shape[0]
    groups = jnp.arange(N_EXPERT_GROUPS)
    seg_end = jnp.cumsum(chunks, axis=1)
    seg_start = seg_end - chunks
    src_end = jnp.cumsum(chunks, axis=0)
    src_start = src_end - chunks
    total = src_end[-1]
    tiles = (total + TILE_CHUNKS - 1) // TILE_CHUNKS
    tile_end = jnp.cumsum(tiles)
    tile_start = tile_end - tiles
    n_used = tile_end[-1:]

    u = jnp.arange(n_tiles)
    gid = jnp.minimum(jnp.sum(u[:, None] >= tile_end[None, :], axis=1), N_EXPERT_GROUPS - 1)

    p = jnp.arange(n_tiles * TILE_CHUNKS)
    g_hot = jnp.repeat(gid, TILE_CHUNKS)[:, None] == groups[None, :]
    by_group = lambda vec: jnp.sum(jnp.where(g_hot, vec[None, :], 0), axis=1)
    q = p - TILE_CHUNKS * by_group(tile_start)
    live = q < by_group(total)
    src_end_p = jnp.sum(jnp.where(g_hot[:, None, :], src_end[None], 0), axis=2)
    t = jnp.minimum(jnp.sum(q[:, None] >= src_end_p, axis=1), n_src - 1)
    t_hot = t[:, None] == jnp.arange(n_src)[None, :]
    by_seg = lambda tab: jnp.sum(jnp.where(t_hot[:, :, None] & g_hot[:, None, :], tab[None], 0),
                                 axis=(1, 2))
    slab_chunk = by_seg(seg_start) + q - by_seg(src_start)
    src_row = jnp.where(live, t * SLAB_ROWS + CHUNK_ROWS * slab_chunk, 0)

    s = jnp.arange(SLAB_CHUNKS)
    gs = jnp.sum(s[None, :, None] >= seg_end[:, None, :], axis=2)
    used = gs < N_EXPERT_GROUPS
    s_hot = jnp.minimum(gs, N_EXPERT_GROUPS - 1)[:, :, None] == groups
    pick = lambda tab: jnp.sum(jnp.where(s_hot, tab, 0), axis=2)
    pos = (TILE_CHUNKS * pick(tile_start[None, None, :]) + pick(src_start[:, None, :])
           + s[None, :] - pick(seg_start[:, None, :]))
    out_row = jnp.where(used, CHUNK_ROWS * pos, 0).reshape(-1)
    i32 = lambda a: a.astype(jnp.int32)
    return i32(gid), i32(src_row), i32(out_row), i32(n_used)


def _chunk_copy(src_hbm, row, k, buf, sem, slot):
    return pltpu.make_async_copy(
        src_hbm.at[pl.ds(pl.multiple_of(row, CHUNK_ROWS), CHUNK_ROWS), :],
        buf.at[slot, pl.ds(k * CHUNK_ROWS, CHUNK_ROWS), :],
        sem.at[slot])


def _start_chunks(sources, row_ref, first, n_chunks, buf, sem, slot):
    bases = [sum(s.shape[0] for s in sources[:i]) for i in range(len(sources) + 1)]
    for k in range(n_chunks):
        row = row_ref[first + k]
        if len(sources) == 1:
            _chunk_copy(sources[0], row, k, buf, sem, slot).start()
        else:
            for src, lo, hi in zip(sources, bases[:-1], bases[1:]):
                @pl.when((row >= lo) & (row < hi))
                def _(src=src, lo=lo):
                    _chunk_copy(src, row - lo, k, buf, sem, slot).start()


def _gather_step(sources, row_ref, n_chunks, buf, sem):
    step = pl.program_id(0)
    slot = step % 2

    @pl.when(step == 0)
    def _():
        _start_chunks(sources, row_ref, 0, n_chunks, buf, sem, 0)

    @pl.when(step + 1 < pl.num_programs(0))
    def _():
        _start_chunks(sources, row_ref, (step + 1) * n_chunks, n_chunks, buf, sem, 1 - slot)

    for k in range(n_chunks):
        _chunk_copy(sources[0], 0, k, buf, sem, slot).wait()
    return slot


def _moe_kernel(gid_ref, src_ref, nused_ref, *refs, n_slabs):
    slabs = refs[:n_slabs]
    w1_ref, w3_ref, w2_ref, o_ref, buf, sem, w1b, w3b, w2b = refs[n_slabs:]
    u = pl.program_id(0)
    slot = _gather_step(slabs, src_ref, TILE_CHUNKS, buf, sem)

    @pl.when((u == 0) | (gid_ref[u] != gid_ref[jnp.maximum(u - 1, 0)]))
    def _():
        w1b[...] = w1_ref[0].astype(BF16)
        w3b[...] = w3_ref[0].astype(BF16)
        w2b[...] = w2_ref[0, 0].astype(BF16)

    @pl.when(u < nused_ref[0])
    def _():
        h = buf[slot, :, 0:D_MODEL]
        gates = buf[slot, :, D_MODEL:].astype(F32)
        hid = []
        for e in range(EXPERTS_PER_GROUP):
            h1 = jnp.dot(h, w1b[e], preferred_element_type=F32)
            h3 = jnp.dot(h, w3b[e], preferred_element_type=F32)
            act = (h1 * jax.nn.sigmoid(h1)) * h3
            gate = gates[:, e:e + 1] + gates[:, EXPERTS_PER_GROUP + e:EXPERTS_PER_GROUP + e + 1]
            hid.append((act * gate).astype(BF16))
        o_ref[...] = jnp.dot(jnp.concatenate(hid, axis=1), w2b[...],
                             preferred_element_type=F32).astype(o_ref.dtype)

    @pl.when(u >= nused_ref[0])
    def _():
        o_ref[...] = jnp.zeros_like(o_ref)


def _moe(slabs, tables, layer, params):
    gid, src_row, _, n_used = tables
    n_tiles = gid.shape[0]
    expert_w = pl.BlockSpec((1, EXPERTS_PER_GROUP, D_MODEL, D_EXPERT),
                            lambda u, gid, src, nu: (layer, gid[u], 0, 0))
    return pl.pallas_call(
        functools.partial(_moe_kernel, n_slabs=len(slabs)),
        grid_spec=pltpu.PrefetchScalarGridSpec(
            num_scalar_prefetch=3,
            grid=(n_tiles,),
            in_specs=[pl.BlockSpec(memory_space=pl.ANY)] * len(slabs) + [
                expert_w,
                expert_w,
                pl.BlockSpec((1, 1, GROUP_HIDDEN, D_MODEL), lambda u, gid, src, nu: (layer, gid[u], 0, 0)),
            ],
            out_specs=pl.BlockSpec((MOE_TILE, D_MODEL), lambda u, gid, src, nu: (u, 0)),
            scratch_shapes=[
                pltpu.VMEM((2, MOE_TILE, SLAB_WIDTH), BF16),
                pltpu.SemaphoreType.DMA((2,)),
                pltpu.VMEM((EXPERTS_PER_GROUP, D_MODEL, D_EXPERT), BF16),
                pltpu.VMEM((EXPERTS_PER_GROUP, D_MODEL, D_EXPERT), BF16),
                pltpu.VMEM((GROUP_HIDDEN, D_MODEL), BF16),
            ],
        ),
        out_shape=jax.ShapeDtypeStruct((n_tiles * MOE_TILE, D_MODEL), BF16),
        compiler_params=_cparams("arbitrary"),
        name="moe",
    )(gid, src_row, n_used, *slabs, params["w1"], params["w3"], params["w2"])


def _unsorted_moe(row_ref, dest_ref, y_hbm, buf, sem, n_slabs):
    slot = _gather_step([y_hbm], row_ref, n_slabs * SLAB_CHUNKS, buf, sem)
    tn = (((0,), (0,)), ((), ()))
    out = []
    for j in range(n_slabs):
        y = buf[slot, j * SLAB_ROWS:(j + 1) * SLAB_ROWS, :]
        out.append(lax.dot_general(_permutation(dest_ref[j]), y, tn, preferred_element_type=F32))
    return jnp.concatenate(out, axis=0)


def _unsort_kernel(row_ref, xm_ref, dest_ref, mod_ref, y_hbm, o_ref, buf, sem, *, n_slabs):
    moe = _unsorted_moe(row_ref, dest_ref, y_hbm, buf, sem, n_slabs)
    o_ref[...] = xm_ref[...] + mod_ref[0, 0][5:6] * moe


def _unsort(pending, layer, mod, row_of_tile, *, tm):
    xm, dest, out_row, y = pending
    t_tokens = xm.shape[0]
    n_slabs = tm // SRC_TILE
    return pl.pallas_call(
        functools.partial(_unsort_kernel, n_slabs=n_slabs),
        grid_spec=pltpu.PrefetchScalarGridSpec(
            num_scalar_prefetch=1,
            grid=(t_tokens // tm,),
            in_specs=[
                pl.BlockSpec((tm, D_MODEL), lambda t, rows: (t, 0)),
                pl.BlockSpec((n_slabs, 1, SRC_TILE), lambda t, rows: (t, 0, 0)),
                pl.BlockSpec((1, 1, 6, D_MODEL), lambda t, rows: (layer, row_of_tile(t), 0, 0)),
                pl.BlockSpec(memory_space=pl.ANY),
            ],
            out_specs=pl.BlockSpec((tm, D_MODEL), lambda t, rows: (t, 0)),
            scratch_shapes=[pltpu.VMEM((2, n_slabs * SLAB_ROWS, D_MODEL), y.dtype),
                            pltpu.SemaphoreType.DMA((2,))],
        ),
        out_shape=jax.ShapeDtypeStruct((t_tokens, D_MODEL), F32),
        compiler_params=_cparams("arbitrary"),
        name="unsort",
    )(out_row, xm, dest, mod, y)


def kernel(x_prompt, x_sample, cache_k, cache_v, c, c_ctx, w_ada, b_ada, norm1_w, norm2_w, w_in,
           q_norm_w, k_norm_w, rpb, w_sgu, b_sgu, sgu_ln_w, sgu_ln_b, w_out, w_router, b_router,
           w1, w3, w2):
    batch, seq, _ = x_prompt.shape
    dec_batch, dec_seq, _ = x_sample.shape
    past = cache_k.shape[2]
    assert 1 + dec_batch <= COND_ROWS and dec_seq % (Q_ROWS * GRID_W) == 0 and seq % CHUNK == 0

    cond = jnp.zeros((COND_ROWS, D_MODEL), F32).at[0].set(c_ctx).at[1:1 + dec_batch].set(c)
    mod = _adaln(cond, w_ada, b_ada).reshape(DEPTH, COND_ROWS, 6, D_MODEL)

    xp = x_prompt.reshape(batch * seq, D_MODEL)
    xs = x_sample.reshape(dec_batch * dec_seq, D_MODEL)
    from_cache_layout = lambda a: a.transpose(0, 1, 3, 4, 2).reshape(dec_batch, DEPTH, ATTN_WIDTH, past)
    ck = from_cache_layout(cache_k)
    cv = from_cache_layout(cache_v)

    head_of = jnp.arange(256) // HEAD_DIM
    bd = (head_of[:, None] == head_of[None, :]).astype(BF16)
    tok_id = jnp.arange(SRC_TILE)
    tri = (tok_id[:, None] <= tok_id[None, :]).astype(BF16)

    params = dict(
        n1w=norm1_w.reshape(DEPTH, 1, D_MODEL),
        n2w=norm2_w.reshape(DEPTH, 1, D_MODEL),
        win=w_in.astype(BF16),
        wout=w_out.astype(BF16),
        qw=jnp.tile(q_norm_w, (1, N_HEADS)).reshape(DEPTH, 1, ATTN_WIDTH),
        kw=jnp.tile(k_norm_w, (1, N_HEADS)).reshape(DEPTH, 1, ATTN_WIDTH),
        bd=jnp.concatenate([bd, bd], axis=0),
        ws=w_sgu.astype(BF16),
        bs=jnp.broadcast_to(b_sgu[..., None], (DEPTH, SGU_GROUPS, CHUNK, SGU_GROUP_DIM)),
        lnw=sgu_ln_w,
        lnb=sgu_ln_b,
        wr=w_router.T.astype(BF16),
        br=b_router.reshape(N_EXPERTS, 1),
        tri=tri,
        w1=w1,
        w3=w3,
        w2=w2.reshape(DEPTH, N_EXPERT_GROUPS, GROUP_HIDDEN, D_MODEL),
    )
    tb = _bias_tables(rpb)

    tm = 2 * SRC_TILE
    assert dec_seq % tm == 0 and tm % seq == 0
    lat_tiles_per_seq = dec_seq // tm
    ctx_row = lambda i: 0
    lat_row = lambda i: 1 + i // lat_tiles_per_seq

    def moe_block(attended, layer):
        chunks = jnp.concatenate([meta[:, :N_EXPERT_GROUPS, 0] for _, _, _, meta in attended], axis=0)
        tables = _chunk_tables(chunks, _max_moe_tiles(chunks.shape[0]))
        y = _moe([slab for _, slab, _, _ in attended], tables, layer, params)
        pending, first = [], 0
        for xm, _, dest, meta in attended:
            n_rows = meta.shape[0] * SLAB_CHUNKS
            pending.append((xm, dest, tables[2][first:first + n_rows], y))
            first += n_rows
        return pending

    kv_new = jax.ShapeDtypeStruct((batch, DEPTH, ATTN_WIDTH, seq), F32)

    for l in range(DEPTH):
        outs = _inproj(xp, l, mod, ctx_row, params, tm=tm,
                       kv_buffers=kv_new if l == 0 else (kbuf, vbuf))
        if l > 0:
            xp, outs = outs[0], outs[1:]
        q, kbuf, vbuf, v, sg = outs
        attended_ctx = _ctx_attention(q, kbuf, v, sg, xp, l, mod, params, seq=seq)

        outs = _inproj(xs, l, mod, lat_row, params, tm=tm)
        if l > 0:
            xs, outs = outs[0], outs[1:]
        q, k, v, sg = outs
        attended_lat = _nbr_attention(q, k, v, ck, cv, tb, sg, xs, l, mod, params,
                                      batch=dec_batch, seq=dec_seq)

        xp, xs = moe_block([attended_ctx, attended_lat], l)
    xp = _unsort(xp, DEPTH - 1, mod, ctx_row, tm=tm)
    xs = _unsort(xs, DEPTH - 1, mod, lat_row, tm=tm)

    to_cache_layout = lambda buf: buf.reshape(batch, DEPTH, N_HEADS, HEAD_DIM, seq).transpose(0, 1, 4, 2, 3)
    return (xp.reshape(batch, seq, D_MODEL), xs.reshape(dec_batch, dec_seq, D_MODEL),
            to_cache_layout(kbuf), to_cache_layout(vbuf))
```

```python
import functools

import jax
import jax.numpy as jnp
from jax import lax
from jax.experimental import pallas as pl
from jax.experimental.pallas import tpu as pltpu

F32 = jnp.float32
BF16 = jnp.bfloat16

D_MODEL = 1024
DEPTH = 4
N_HEADS = 8
HEAD_DIM = 64
ATTN_WIDTH = N_HEADS * HEAD_DIM
SGU_GROUPS = 4
SGU_GROUP_DIM = 128
SGU_WIDTH = SGU_GROUPS * SGU_GROUP_DIM
CHUNK = 128
IN_WIDTH = 3 * ATTN_WIDTH + 2 * SGU_WIDTH
GRID_W = 64
WIN_H = 8
WIN_W = 16
N_EXPERTS = 16
N_EXPERT_GROUPS = 4
EXPERTS_PER_GROUP = 4
D_EXPERT = 256
GROUP_HIDDEN = EXPERTS_PER_GROUP * D_EXPERT
EPS = 1e-6
NEG = -1e30
LOG2E = 1.4426950408889634

LANES = 128
HEADS_PER_VREG = LANES // HEAD_DIM
N_HEAD_PAIRS = N_HEADS // HEADS_PER_VREG
COND_ROWS = 16
Q_ROWS = 4
WIN_ROWS = Q_ROWS + WIN_H
VMEM_LIMIT = 48 * 1024 * 1024

SRC_TILE = 256
CHUNK_ROWS = 16
SLAB_ROWS = SRC_TILE + N_EXPERT_GROUPS * CHUNK_ROWS
SLAB_CHUNKS = SLAB_ROWS // CHUNK_ROWS
SLAB_WIDTH = D_MODEL + LANES
MOE_TILE = 512
TILE_CHUNKS = MOE_TILE // CHUNK_ROWS


def _cparams(*sem):
    return pltpu.CompilerParams(dimension_semantics=sem, vmem_limit_bytes=VMEM_LIMIT)


def _adaln_kernel(cond_ref, w_ref, b_ref, o_ref):
    c = cond_ref[...]
    a = (c * jax.nn.sigmoid(c)).astype(BF16)
    o_ref[0] = jnp.dot(a, w_ref[0].astype(BF16), preferred_element_type=F32) + b_ref[0]


def _adaln(cond, w_ada, b_ada):
    tn = 1536
    n = 6 * D_MODEL
    return pl.pallas_call(
        _adaln_kernel,
        grid=(DEPTH, n // tn),
        in_specs=[
            pl.BlockSpec((COND_ROWS, D_MODEL), lambda l, j: (0, 0)),
            pl.BlockSpec((1, D_MODEL, tn), lambda l, j: (l, 0, j)),
            pl.BlockSpec((1, 1, tn), lambda l, j: (l, 0, j)),
        ],
        out_specs=pl.BlockSpec((1, COND_ROWS, tn), lambda l, j: (l, 0, j)),
        out_shape=jax.ShapeDtypeStruct((DEPTH, COND_ROWS, n), F32),
        compiler_params=_cparams("arbitrary", "arbitrary"),
        name="adaln",
    )(cond, w_ada, b_ada.reshape(DEPTH, 1, n))


def _gelu_tanh(x):
    return 0.5 * x * (1.0 + jnp.tanh(0.7978845608028654 * (x + 0.044715 * (x * x * x))))


def _head_rms(t, bd, w):
    t2 = t * t
    hi = t2.astype(BF16)
    lo = (t2 - hi.astype(F32)).astype(BF16)
    outs = []
    for c in range(ATTN_WIDTH // 256):
        sl = slice(256 * c, 256 * c + 256)
        hl = jnp.concatenate([hi[:, sl], lo[:, sl]], axis=1)
        ss = jnp.dot(hl, bd, preferred_element_type=F32)
        outs.append(t[:, sl] * lax.rsqrt(ss * (1.0 / HEAD_DIM) + EPS))
    return jnp.concatenate(outs, axis=1) * w


def _inproj_kernel(*refs, tm, transposed_kv, new_kv_layer, fused_moe):
    refs = list(refs)
    if fused_moe:
        row_ref, xm_ref, dest_ref, prev_mod_ref, y_hbm = refs[:5]
        refs = refs[5:]
    else:
        x_ref = refs.pop(0)
    (mod_ref, n1w_ref, win_ref, qw_ref, kw_ref, bd_ref, ws_ref, bs_ref, lnw_ref, lnb_ref) = refs[:10]
    refs = refs[10:]
    if transposed_kv and new_kv_layer is None:
        refs = refs[2:]
    if fused_moe:
        x_out_ref = refs.pop(0)
    if transposed_kv:
        q_ref, kt_ref, vt_ref, v_ref, sg_ref = refs[:5]
        refs = refs[5:]
    else:
        q_ref, k_ref, v_ref, sg_ref = refs[:4]
        refs = refs[4:]
    if fused_moe:
        buf, sem = refs
        moe = _unsorted_moe(row_ref, dest_ref, y_hbm, buf, sem, tm // SRC_TILE)
        x = xm_ref[...] + prev_mod_ref[0, 0][5:6] * moe
        x_out_ref[...] = x
    else:
        x = x_ref[...]
    m = mod_ref[0, 0]
    sh1, sc1 = m[0:1], m[1:2]
    ms = jnp.mean(x * x, axis=-1, keepdims=True)
    h = x * lax.rsqrt(ms + EPS) * n1w_ref[0]
    h = (h * (1.0 + sc1) + sh1).astype(BF16)
    p = jnp.dot(h, win_ref[0], preferred_element_type=F32)

    bd = bd_ref[...]
    q = _head_rms(p[:, 0:ATTN_WIDTH], bd, qw_ref[0])
    k = _head_rms(p[:, ATTN_WIDTH:2 * ATTN_WIDTH], bd, kw_ref[0])
    v = p[:, 2 * ATTN_WIDTH:3 * ATTN_WIDTH]
    q_ref[...] = (q * (HEAD_DIM ** -0.5 * LOG2E)).astype(q_ref.dtype)
    if transposed_kv:
        seq = kt_ref.shape[3]
        if new_kv_layer is None:
            slot = 0
        else:
            slot = new_kv_layer
            kt_ref[...] = jnp.zeros_like(kt_ref)
            vt_ref[...] = jnp.zeros_like(vt_ref)
        for j in range(tm // seq):
            kt_ref[j, slot] = k[j * seq:(j + 1) * seq].T
            vt_ref[j, slot] = v[j * seq:(j + 1) * seq].T
    else:
        k_ref[...] = k.astype(k_ref.dtype)
    v_ref[...] = v.astype(v_ref.dtype)

    off_u = 3 * ATTN_WIDTH
    off_v = off_u + SGU_WIDTH
    for g in range(SGU_GROUPS):
        gl = slice(g * SGU_GROUP_DIM, (g + 1) * SGU_GROUP_DIM)
        u = _gelu_tanh(p[:, off_u + g * SGU_GROUP_DIM: off_u + (g + 1) * SGU_GROUP_DIM])
        t = _gelu_tanh(p[:, off_v + g * SGU_GROUP_DIM: off_v + (g + 1) * SGU_GROUP_DIM])
        mu = jnp.mean(t, axis=-1, keepdims=True)
        d = t - mu
        var = jnp.mean(d * d, axis=-1, keepdims=True)
        y = (d * lax.rsqrt(var + EPS) * lnw_ref[0, g:g + 1, :] + lnb_ref[0, g:g + 1, :]).astype(BF16)
        for c in range(tm // CHUNK):
            rows = slice(c * CHUNK, (c + 1) * CHUNK)
            sv = jnp.dot(ws_ref[0, g], y[rows], preferred_element_type=F32) + bs_ref[0, g]
            sg_ref[rows, gl] = (u[rows] * sv).astype(sg_ref.dtype)


def _inproj(x, layer, mod, row_of_tile, params, *, tm, kv_buffers=None):
    fused_moe = isinstance(x, tuple)
    tok = lambda i, *_: (i, 0)
    lyr3 = lambda i, *_: (layer, 0, 0)
    lyr4 = lambda i, *_: (layer, 0, 0, 0)
    mod_row = lambda lyr: pl.BlockSpec((1, 1, 6, D_MODEL), lambda i, *_: (lyr, row_of_tile(i), 0, 0))
    x_spec = pl.BlockSpec((tm, D_MODEL), tok)
    if fused_moe:
        n_slabs = tm // SRC_TILE
        xm, dest, out_row, y = x
        t_tokens = xm.shape[0]
        prefetch = [out_row]
        in_specs = [x_spec, pl.BlockSpec((n_slabs, 1, SRC_TILE), lambda i, *_: (i, 0, 0)),
                    mod_row(layer - 1), pl.BlockSpec(memory_space=pl.ANY)]
        args = [xm, dest, mod, y]
        scratch = [pltpu.VMEM((2, n_slabs * SLAB_ROWS, D_MODEL), y.dtype), pltpu.SemaphoreType.DMA((2,))]
    else:
        t_tokens = x.shape[0]
        prefetch, in_specs, args, scratch = [], [x_spec], [x], []
    act = pl.BlockSpec((tm, ATTN_WIDTH), tok)
    act_sds = jax.ShapeDtypeStruct((t_tokens, ATTN_WIDTH), BF16)
    in_specs += [
        mod_row(layer),
        pl.BlockSpec((1, 1, D_MODEL), lyr3),
        pl.BlockSpec((1, D_MODEL, IN_WIDTH), lyr3),
        pl.BlockSpec((1, 1, ATTN_WIDTH), lyr3),
        pl.BlockSpec((1, 1, ATTN_WIDTH), lyr3),
        pl.BlockSpec((512, 256), lambda i, *_: (0, 0)),
        pl.BlockSpec((1, SGU_GROUPS, CHUNK, CHUNK), lyr4),
        pl.BlockSpec((1, SGU_GROUPS, CHUNK, SGU_GROUP_DIM), lyr4),
        pl.BlockSpec((1, SGU_GROUPS, SGU_GROUP_DIM), lyr3),
        pl.BlockSpec((1, SGU_GROUPS, SGU_GROUP_DIM), lyr3),
    ]
    args += [mod, params["n1w"], params["win"], params["qw"], params["kw"], params["bd"],
             params["ws"], params["bs"], params["lnw"], params["lnb"]]
    out_specs = [x_spec] if fused_moe else []
    out_shape = [jax.ShapeDtypeStruct((t_tokens, D_MODEL), F32)] if fused_moe else []
    if kv_buffers is None:
        out_specs += [act] * 4
        out_shape += [act_sds] * 4
        aliases = {}
    elif isinstance(kv_buffers, jax.ShapeDtypeStruct):
        seq = kv_buffers.shape[3]
        kv_spec = pl.BlockSpec((tm // seq,) + kv_buffers.shape[1:], lambda i, *_: (i, 0, 0, 0))
        out_specs += [act, kv_spec, kv_spec, act, act]
        out_shape += [act_sds, kv_buffers, kv_buffers, act_sds, act_sds]
        aliases = {}
    else:
        kbuf, vbuf = kv_buffers
        seq = kbuf.shape[3]
        kv_spec = pl.BlockSpec((tm // seq, 1, ATTN_WIDTH, seq), lambda i, *_: (i, layer, 0, 0))
        first_alias_in = len(prefetch) + len(args)
        first_alias_out = len(out_specs) + 1
        in_specs += [pl.BlockSpec(memory_space=pl.ANY)] * 2
        args += [kbuf, vbuf]
        out_specs += [act, kv_spec, kv_spec, act, act]
        out_shape += [act_sds, jax.ShapeDtypeStruct(kbuf.shape, F32),
                      jax.ShapeDtypeStruct(vbuf.shape, F32), act_sds, act_sds]
        aliases = {first_alias_in: first_alias_out, first_alias_in + 1: first_alias_out + 1}
    return pl.pallas_call(
        functools.partial(_inproj_kernel, tm=tm, transposed_kv=kv_buffers is not None,
                          new_kv_layer=layer if isinstance(kv_buffers, jax.ShapeDtypeStruct) else None,
                          fused_moe=fused_moe),
        grid_spec=pltpu.PrefetchScalarGridSpec(
            num_scalar_prefetch=len(prefetch),
            grid=(t_tokens // tm,),
            in_specs=in_specs,
            out_specs=out_specs,
            scratch_shapes=scratch,
        ),
        out_shape=out_shape,
        input_output_aliases=aliases,
        compiler_params=_cparams("arbitrary"),
        name="inproj",
    )(*prefetch, *args)


def _route(lg_t, br):
    s = jax.nn.sigmoid(lg_t)
    sel = s + br
    row = lambda a, i: a[i:i + 1]
    g_score = []
    for g in range(N_EXPERT_GROUPS):
        v = [row(sel, 4 * g + i) for i in range(4)]
        best_pair = None
        for i in range(4):
            for j in range(i + 1, 4):
                pair = v[i] + v[j]
                best_pair = pair if best_pair is None else jnp.maximum(best_pair, pair)
        g_score.append(best_pair)
    best = jnp.zeros_like(g_score[0], dtype=jnp.int32)
    top = g_score[0]
    for g in range(1, N_EXPERT_GROUPS):
        upd = g_score[g] > top
        best = jnp.where(upd, g, best)
        top = jnp.where(upd, g_score[g], top)
    cand, aff = [], []
    for i in range(4):
        ci, si = row(sel, i), row(s, i)
        for g in range(1, N_EXPERT_GROUPS):
            ci = jnp.where(best == g, row(sel, 4 * g + i), ci)
            si = jnp.where(best == g, row(s, 4 * g + i), si)
        cand.append(ci)
        aff.append(si)

    def first_argmax(vals):
        idx = jnp.zeros_like(best)
        top_v = vals[0]
        for i in range(1, 4):
            upd = vals[i] > top_v
            idx = jnp.where(upd, i, idx)
            top_v = jnp.where(upd, vals[i], top_v)
        return idx

    i1 = first_argmax(cand)
    i2 = first_argmax([jnp.where(i1 == i, -jnp.inf, cand[i]) for i in range(4)])
    pick = lambda idx: sum(jnp.where(idx == i, aff[i], 0.0) for i in range(4))
    den = pick(i1) + pick(i2)
    gate = [jnp.where((i1 == i) | (i2 == i), aff[i] / den, 0.0) for i in range(4)]
    hi = [x.astype(BF16) for x in gate]
    lo = [(x - h.astype(F32)).astype(BF16) for x, h in zip(gate, hi)]
    return best, hi + lo


def _slab_positions(best, tri):
    n = best.shape[1]
    onehot = [jnp.where(best == g, 1.0, 0.0) for g in range(N_EXPERT_GROUPS)]
    pad = [jnp.zeros_like(onehot[0])] * (8 - N_EXPERT_GROUPS)
    oh = jnp.concatenate(onehot + pad, axis=0).astype(BF16)
    counts = jnp.dot(oh, tri, preferred_element_type=F32)
    dest = jnp.zeros((1, n), F32)
    seg_start = jnp.zeros((1, 1), F32)
    chunks = []
    for g in range(N_EXPERT_GROUPS):
        cg = counts[g:g + 1]
        n_g = jnp.max(cg, axis=1, keepdims=True)
        c_g = jnp.floor((n_g + (CHUNK_ROWS - 1)) * (1.0 / CHUNK_ROWS))
        dest = dest + onehot[g] * (seg_start + cg - 1.0)
        seg_start = seg_start + c_g * CHUNK_ROWS
        chunks.append(c_g)
    return dest.astype(jnp.int32), chunks


def _permutation(dest):
    n = dest.shape[1]
    hit = lax.broadcasted_iota(jnp.int32, (SLAB_ROWS, n), 0) == dest
    return jnp.where(hit, 1.0, 0.0).astype(BF16)


def _post_attention(a, sg, x, m, wout_ref, n2w_ref, xm_ref):
    g1, sh2, sc2 = m[2:3], m[3:4], m[4:5]
    cat = jnp.concatenate([a, sg], axis=1)
    y = jnp.dot(cat, wout_ref[0], preferred_element_type=F32)
    xm = x + g1 * y
    xm_ref[...] = xm
    ms = jnp.mean(xm * xm, axis=-1, keepdims=True)
    h2 = xm * lax.rsqrt(ms + EPS) * n2w_ref[0]
    return (h2 * (1.0 + sc2) + sh2).astype(BF16)


def _route_previous(h2_scr, wr_ref, br_ref, tri_ref, slab_ref, dest_ref, meta_ref):
    h2 = h2_scr[...]
    lg_t = lax.dot_general(wr_ref[...], h2, (((1,), (1,)), ((), ())), preferred_element_type=F32)
    yield
    best, gates = _route(lg_t, br_ref[...])
    dest, chunks = _slab_positions(best, tri_ref[...])
    yield
    perm = _permutation(dest)
    slab_ref[:, 0:D_MODEL] = jnp.dot(perm, h2, preferred_element_type=F32).astype(BF16)
    gmat = jnp.concatenate(gates + [jnp.zeros((LANES - len(gates), dest.shape[1]), BF16)], axis=0)
    slab_ref[:, D_MODEL:] = lax.dot_general(perm, gmat, (((1,), (1,)), ((), ())),
                                            preferred_element_type=F32).astype(BF16)
    dest_ref[0] = dest
    meta = [jnp.broadcast_to(c, (1, LANES)) for c in chunks]
    meta += [jnp.zeros((8 - len(chunks), LANES), F32)]
    meta_ref[0] = jnp.concatenate(meta, axis=0).astype(jnp.int32)


def _pair_pos(axis):
    return lax.broadcasted_iota(jnp.int32, (1, LANES) if axis == 1 else (LANES, 1), axis)


def _head_mask(hh, axis):
    pos = _pair_pos(axis)
    return (pos >= hh * HEAD_DIM) & (pos < (hh + 1) * HEAD_DIM)


def _only_head(x2, hh, axis):
    return jnp.where(_head_mask(hh, axis), x2, jnp.zeros_like(x2))


def _attend_heads(logits, finish, router, depth):
    outs, even = [], None
    ahead = [logits(h) for h in range(depth)]
    for h in range(N_HEADS):
        s = ahead.pop(0)
        if h + depth < N_HEADS:
            ahead.append(logits(h + depth))
        if h == N_HEADS // 2:
            next(router)
        o = finish(h, s)
        if h % HEADS_PER_VREG == 0:
            even = o
        else:
            outs.append((even + o).astype(BF16))
    for _ in router:
        pass
    return jnp.concatenate(outs, axis=1)


def _ctx_attn_kernel(q_ref, kt_ref, v_ref, sg_ref, x_ref, mod_ref, wout_ref, n2w_ref, wr_ref, br_ref,
                     tri_ref, xm_ref, slab_ref, dest_ref, meta_ref, h2_scr):
    step = pl.program_id(0)

    @pl.when(step == 0)
    def _():
        h2_scr[...] = jnp.zeros_like(h2_scr)

    def logits(h):
        cols = slice(h // HEADS_PER_VREG * LANES, (h // HEADS_PER_VREG + 1) * LANES)
        kt2 = kt_ref[0, 0, cols, :].astype(BF16)
        return jnp.dot(q_ref[:, cols], _only_head(kt2, h % HEADS_PER_VREG, 0),
                       preferred_element_type=F32)

    def finish(h, s):
        cols = slice(h // HEADS_PER_VREG * LANES, (h // HEADS_PER_VREG + 1) * LANES)
        e = jnp.exp2(s - jnp.max(s, axis=-1, keepdims=True))
        l = jnp.sum(e, axis=-1, keepdims=True)
        vm = _only_head(v_ref[:, cols], h % HEADS_PER_VREG, 1)
        return jnp.dot(e.astype(BF16), vm, preferred_element_type=F32) * (1.0 / l)

    new_router = lambda: _route_previous(h2_scr, wr_ref, br_ref, tri_ref, slab_ref, dest_ref, meta_ref)
    last = pl.num_programs(0) - 1

    @pl.when(step < last)
    def _():
        router = new_router()
        next(router)
        a = _attend_heads(logits, finish, router, depth=2)
        h2_scr[...] = _post_attention(a, sg_ref[...], x_ref[...], mod_ref[0, 0], wout_ref, n2w_ref,
                                      xm_ref)

    @pl.when(step == last)
    def _():
        for _ in new_router():
            pass


def _post_attention_out(t_tokens):
    n_src = t_tokens // SRC_TILE
    prev = lambda s: jnp.maximum(s - 1, 0)
    specs = [
        pl.BlockSpec((SRC_TILE, D_MODEL), lambda s: (jnp.minimum(s, n_src - 1), 0)),
        pl.BlockSpec((SLAB_ROWS, SLAB_WIDTH), lambda s: (prev(s), 0)),
        pl.BlockSpec((1, 1, SRC_TILE), lambda s: (prev(s), 0, 0)),
        pl.BlockSpec((1, 8, LANES), lambda s: (prev(s), 0, 0)),
    ]
    shapes = [
        jax.ShapeDtypeStruct((t_tokens, D_MODEL), F32),
        jax.ShapeDtypeStruct((n_src * SLAB_ROWS, SLAB_WIDTH), BF16),
        jax.ShapeDtypeStruct((n_src, 1, SRC_TILE), jnp.int32),
        jax.ShapeDtypeStruct((n_src, 8, LANES), jnp.int32),
    ]
    return specs, shapes, pltpu.VMEM((SRC_TILE, D_MODEL), BF16)


def _ctx_attention(q, kbuf, v, sg, x, layer, mod, params, *, seq):
    assert seq == SRC_TILE
    t_tokens = x.shape[0]
    n_seq = t_tokens // seq
    seq_of = lambda s: jnp.minimum(s, n_seq - 1)
    tok = lambda s: (seq_of(s), 0)
    lyr3 = lambda s: (layer, 0, 0)
    full2 = lambda s: (0, 0)
    out_specs, out_shape, h2_scratch = _post_attention_out(t_tokens)
    return pl.pallas_call(
        _ctx_attn_kernel,
        grid=(n_seq + 1,),
        in_specs=[
            pl.BlockSpec((seq, ATTN_WIDTH), tok),
            pl.BlockSpec((1, 1, ATTN_WIDTH, seq), lambda s: (seq_of(s), layer, 0, 0)),
            pl.BlockSpec((seq, ATTN_WIDTH), tok),
            pl.BlockSpec((seq, SGU_WIDTH), tok),
            pl.BlockSpec((seq, D_MODEL), tok),
            pl.BlockSpec((1, 1, 6, D_MODEL), lambda s: (layer, 0, 0, 0)),
            pl.BlockSpec((1, D_MODEL, D_MODEL), lyr3),
            pl.BlockSpec((1, 1, D_MODEL), lyr3),
            pl.BlockSpec((N_EXPERTS, D_MODEL), full2),
            pl.BlockSpec((N_EXPERTS, 1), full2),
            pl.BlockSpec((SRC_TILE, SRC_TILE), full2),
        ],
        out_specs=out_specs,
        out_shape=out_shape,
        scratch_shapes=[h2_scratch],
        compiler_params=_cparams("arbitrary"),
        name="ctx_attn",
    )(q, kbuf, v, sg, x, mod, params["wout"], params["n2w"], params["wr"], params["br"], params["tri"])


def _row_start(r, rows):
    return jnp.clip(r - WIN_H // 2, 0, rows - WIN_H)


def _nbr_attn_kernel(q_ref, k_ref, v_ref, ck_ref, cv_ref, tb_ref, sg_ref, x_ref, mod_ref, wout_ref,
                     n2w_ref, wr_ref, br_ref, tri_ref, xm_ref, slab_ref, dest_ref, meta_ref,
                     kwm_ref, vwm_ref, kcm_ref, vcm_ref, h2_scr, *, rows, n_tiles):
    step = pl.program_id(0)
    tiles_per_seq = rows // Q_ROWS
    t = jnp.minimum(step, n_tiles - 1) % tiles_per_seq

    @pl.when(step == 0)
    def _():
        h2_scr[...] = jnp.zeros_like(h2_scr)

    @pl.when((t == 0) & (step < n_tiles))
    def _():
        for j in range(N_HEAD_PAIRS):
            cols = slice(j * LANES, (j + 1) * LANES)
            k2, v2 = k_ref[:, cols], v_ref[:, cols]
            ck2 = ck_ref[0, 0, cols, :].astype(BF16)
            cv2 = cv_ref[0, 0, cols, :].astype(BF16)
            for hh in range(HEADS_PER_VREG):
                h = HEADS_PER_VREG * j + hh
                kwm_ref[h] = _only_head(k2, hh, 1)
                vwm_ref[h] = _only_head(v2, hh, 1)
                kcm_ref[h] = _only_head(ck2, hh, 0)
                vcm_ref[h] = _only_head(cv2, hh, 0)

    new_router = lambda: _route_previous(h2_scr, wr_ref, br_ref, tri_ref, slab_ref, dest_ref, meta_ref)

    @pl.when(step == n_tiles)
    def _():
        for _ in new_router():
            pass

    @pl.when(step < n_tiles)
    def _():
        _attend_tile(t, new_router(), q_ref, tb_ref, sg_ref, x_ref, mod_ref, wout_ref, n2w_ref, xm_ref,
                     kwm_ref, vwm_ref, kcm_ref, vcm_ref, h2_scr, rows)


def _attend_tile(t, router, q_ref, tb_ref, sg_ref, x_ref, mod_ref, wout_ref, n2w_ref, xm_ref,
                 kwm_ref, vwm_ref, kcm_ref, vcm_ref, h2_scr, rows):
    next(router)
    r0 = t * Q_ROWS
    ws = jnp.minimum(_row_start(r0, rows), rows - WIN_ROWS)
    tok0 = pl.multiple_of(ws * GRID_W, GRID_W)

    lane = lax.broadcasted_iota(jnp.int32, (1, LANES), 1)
    blk_idx, row_mask = [], []
    for a in range(Q_ROWS):
        r = r0 + a
        rs = _row_start(r, rows)
        idx_a, mask_a = [], []
        for jp in range(WIN_ROWS // 2):
            kr = ws + 2 * jp
            idx_a.append(jnp.clip(kr - r + WIN_H, 0, 2 * WIN_H - 1))
            ok0 = (kr >= rs) & (kr < rs + WIN_H)
            ok1 = (kr + 1 >= rs) & (kr + 1 < rs + WIN_H)
            m0 = jnp.where(ok0, 0.0, NEG).astype(F32)
            m1 = jnp.where(ok1, 0.0, NEG).astype(F32)
            mask_a.append(jnp.where(lane < GRID_W, m0, m1))
        blk_idx.append(idx_a)
        row_mask.append(mask_a)

    nt = (((1,), (1,)), ((), ()))
    window = pl.ds(tok0, WIN_ROWS * GRID_W)

    def logits(h):
        q2 = q_ref[:, h // HEADS_PER_VREG * LANES:(h // HEADS_PER_VREG + 1) * LANES]
        bias = jnp.concatenate([
            jnp.concatenate([tb_ref[0, h, blk_idx[a][jp]] + row_mask[a][jp]
                             for jp in range(WIN_ROWS // 2)], axis=1)
            for a in range(Q_ROWS)], axis=0)
        s_w = lax.dot_general(q2, kwm_ref[h, window, :], nt, preferred_element_type=F32) + bias
        s_c = jnp.dot(q2, kcm_ref[h], preferred_element_type=F32)
        return s_w, s_c

    def finish(h, s):
        s_w, s_c = s
        mx = jnp.maximum(jnp.max(s_w, axis=-1, keepdims=True), jnp.max(s_c, axis=-1, keepdims=True))
        e_w = jnp.exp2(s_w - mx)
        e_c = jnp.exp2(s_c - mx)
        l = jnp.sum(e_w, axis=-1, keepdims=True) + jnp.sum(e_c, axis=-1, keepdims=True)
        return (jnp.dot(e_w.astype(BF16), vwm_ref[h, window, :], preferred_element_type=F32)
                + lax.dot_general(e_c.astype(BF16), vcm_ref[h], nt,
                                  preferred_element_type=F32)) * (1.0 / l)

    a_out = _attend_heads(logits, finish, router, depth=1)
    h2_scr[...] = _post_attention(a_out, sg_ref[...], x_ref[...], mod_ref[0, 0], wout_ref, n2w_ref,
                                  xm_ref)


def _nbr_attention(q, k, v, cache_k, cache_v, tb, sg, x, layer, mod, params, *, batch, seq):
    t_tokens = x.shape[0]
    rows = seq // GRID_W
    tq = Q_ROWS * GRID_W
    assert tq == SRC_TILE
    nt = seq // tq
    past = cache_k.shape[3]
    n_tiles = batch * nt
    out_specs, out_shape, h2_scratch = _post_attention_out(t_tokens)
    tile_of = lambda s: jnp.minimum(s, n_tiles - 1)
    seq_of = lambda s: tile_of(s) // nt
    tok = lambda s: (tile_of(s), 0)
    per_b = lambda s: (seq_of(s), 0)
    lyr3 = lambda s: (layer, 0, 0)
    full2 = lambda s: (0, 0)
    cache_spec = pl.BlockSpec((1, 1, ATTN_WIDTH, past), lambda s: (seq_of(s), layer, 0, 0))
    masked = lambda n: pltpu.VMEM((N_HEADS, n, LANES), BF16)
    masked_t = pltpu.VMEM((N_HEADS, LANES, past), BF16)
    return pl.pallas_call(
        functools.partial(_nbr_attn_kernel, rows=rows, n_tiles=n_tiles),
        grid=(n_tiles + 1,),
        in_specs=[
            pl.BlockSpec((tq, ATTN_WIDTH), tok),
            pl.BlockSpec((seq, ATTN_WIDTH), per_b),
            pl.BlockSpec((seq, ATTN_WIDTH), per_b),
            cache_spec,
            cache_spec,
            pl.BlockSpec((1, N_HEADS, 2 * WIN_H, GRID_W, LANES), lambda s: (layer, 0, 0, 0, 0)),
            pl.BlockSpec((tq, SGU_WIDTH), tok),
            pl.BlockSpec((tq, D_MODEL), tok),
            pl.BlockSpec((1, 1, 6, D_MODEL), lambda s: (layer, 1 + seq_of(s), 0, 0)),
            pl.BlockSpec((1, D_MODEL, D_MODEL), lyr3),
            pl.BlockSpec((1, 1, D_MODEL), lyr3),
            pl.BlockSpec((N_EXPERTS, D_MODEL), full2),
            pl.BlockSpec((N_EXPERTS, 1), full2),
            pl.BlockSpec((SRC_TILE, SRC_TILE), full2),
        ],
        out_specs=out_specs,
        out_shape=out_shape,
        scratch_shapes=[masked(seq), masked(seq), masked_t, masked_t, h2_scratch],
        compiler_params=_cparams("arbitrary"),
        name="nbr_attn",
    )(q, k, v, cache_k, cache_v, tb, sg, x, mod, params["wout"], params["n2w"], params["wr"],
      params["br"], params["tri"])


def _bias_tables(rpb):
    n_rel = 2 * WIN_W - 1
    cols = jnp.arange(GRID_W)
    col_start = jnp.clip(cols - WIN_W // 2, 0, GRID_W - WIN_W)
    col_ok = (cols[None, :] >= col_start[:, None]) & (cols[None, :] < col_start[:, None] + WIN_W)
    rel = cols[None, :] - cols[:, None] + WIN_W - 1
    pick = ((rel[None] == jnp.arange(n_rel)[:, None, None]) & col_ok[None]).astype(F32)
    zero = jnp.zeros_like(pick)
    halves = jnp.concatenate([jnp.concatenate([pick, zero], axis=2),
                              jnp.concatenate([zero, pick], axis=2)], axis=0)
    masked = jnp.where(jnp.concatenate([col_ok, col_ok], axis=1), 0.0, NEG)[None]
    select = jnp.concatenate([halves, masked], axis=0)

    by_offset = jnp.pad(rpb * LOG2E, ((0, 0), (0, 0), (1, 1), (0, 0)))
    feats = jnp.concatenate([by_offset[:, :, :-1], by_offset[:, :, 1:],
                             jnp.ones(by_offset.shape[:2] + (2 * WIN_H, 1), F32)], axis=-1)
    return jnp.einsum("lhdc,cqn->lhdqn", feats, select, precision=lax.Precision.HIGHEST)


def _max_moe_tiles(n_src):
    max_chunks = n_src * SRC_TILE // CHUNK_ROWS + n_src * N_EXPERT_GROUPS
    return -(-max_chunks // TILE_CHUNKS) + N_EXPERT_GROUPS


def _chunk_tables(chunks, n_tiles):
    n_src = chunks.shape[0]
    groups = jnp.arange(N_EXPERT_GROUPS)
    seg_end = jnp.cumsum(chunks, axis=1)
    seg_start = seg_end - chunks
    src_end = jnp.cumsum(chunks, axis=0)
    src_start = src_end - chunks
    total = src_end[-1]
    tiles = (total + TILE_CHUNKS - 1) // TILE_CHUNKS
    tile_end = jnp.cumsum(tiles)
    tile_start = tile_end - tiles
    n_used = tile_end[-1:]

    u = jnp.arange(n_tiles)
    gid = jnp.minimum(jnp.sum(u[:, None] >= tile_end[None, :], axis=1), N_EXPERT_GROUPS - 1)

    p = jnp.arange(n_tiles * TILE_CHUNKS)
    g_hot = jnp.repeat(gid, TILE_CHUNKS)[:, None] == groups[None, :]
    by_group = lambda vec: jnp.sum(jnp.where(g_hot, vec[None, :], 0), axis=1)
    q = p - TILE_CHUNKS * by_group(tile_start)
    live = q < by_group(total)
    src_end_p = jnp.sum(jnp.where(g_hot[:, None, :], src_end[None], 0), axis=2)
    t = jnp.minimum(jnp.sum(q[:, None] >= src_end_p, axis=1), n_src - 1)
    t_hot = t[:, None] == jnp.arange(n_src)[None, :]
    by_seg = lambda tab: jnp.sum(jnp.where(t_hot[:, :, None] & g_hot[:, None, :], tab[None], 0),
                                 axis=(1, 2))
    slab_chunk = by_seg(seg_start) + q - by_seg(src_start)
    src_row = jnp.where(live, t * SLAB_ROWS + CHUNK_ROWS * slab_chunk, 0)

    s = jnp.arange(SLAB_CHUNKS)
    gs = jnp.sum(s[None, :, None] >= seg_end[:, None, :], axis=2)
    used = gs < N_EXPERT_GROUPS
    s_hot = jnp.minimum(gs, N_EXPERT_GROUPS - 1)[:, :, None] == groups
    pick = lambda tab: jnp.sum(jnp.where(s_hot, tab, 0), axis=2)
    pos = (TILE_CHUNKS * pick(tile_start[None, None, :]) + pick(src_start[:, None, :])
           + s[None, :] - pick(seg_start[:, None, :]))
    out_row = jnp.where(used, CHUNK_ROWS * pos, 0).reshape(-1)
    i32 = lambda a: a.astype(jnp.int32)
    return i32(gid), i32(src_row), i32(out_row), i32(n_used)


def _chunk_copy(src_hbm, row, k, buf, sem, slot):
    return pltpu.make_async_copy(
        src_hbm.at[pl.ds(pl.multiple_of(row, CHUNK_ROWS), CHUNK_ROWS), :],
        buf.at[slot, pl.ds(k * CHUNK_ROWS, CHUNK_ROWS), :],
        sem.at[slot])


def _start_chunks(sources, row_ref, first, n_chunks, buf, sem, slot):
    bases = [sum(s.shape[0] for s in sources[:i]) for i in range(len(sources) + 1)]
    for k in range(n_chunks):
        row = row_ref[first + k]
        if len(sources) == 1:
            _chunk_copy(sources[0], row, k, buf, sem, slot).start()
        else:
            for src, lo, hi in zip(sources, bases[:-1], bases[1:]):
                @pl.when((row >= lo) & (row < hi))
                def _(src=src, lo=lo):
                    _chunk_copy(src, row - lo, k, buf, sem, slot).start()


def _gather_step(sources, row_ref, n_chunks, buf, sem):
    step = pl.program_id(0)
    slot = step % 2

    @pl.when(step == 0)
    def _():
        _start_chunks(sources, row_ref, 0, n_chunks, buf, sem, 0)

    @pl.when(step + 1 < pl.num_programs(0))
    def _():
        _start_chunks(sources, row_ref, (step + 1) * n_chunks, n_chunks, buf, sem, 1 - slot)

    for k in range(n_chunks):
        _chunk_copy(sources[0], 0, k, buf, sem, slot).wait()
    return slot


def _moe_kernel(gid_ref, src_ref, nused_ref, *refs, n_slabs):
    slabs = refs[:n_slabs]
    w1_ref, w3_ref, w2_ref, o_ref, buf, sem, w1b, w3b, w2b = refs[n_slabs:]
    u = pl.program_id(0)
    slot = _gather_step(slabs, src_ref, TILE_CHUNKS, buf, sem)

    @pl.when((u == 0) | (gid_ref[u] != gid_ref[jnp.maximum(u - 1, 0)]))
    def _():
        w1b[...] = w1_ref[0].astype(BF16)
        w3b[...] = w3_ref[0].astype(BF16)
        w2b[...] = w2_ref[0, 0].astype(BF16)

    @pl.when(u < nused_ref[0])
    def _():
        h = buf[slot, :, 0:D_MODEL]
        gates = buf[slot, :, D_MODEL:].astype(F32)
        hid = []
        for e in range(EXPERTS_PER_GROUP):
            h1 = jnp.dot(h, w1b[e], preferred_element_type=F32)
            h3 = jnp.dot(h, w3b[e], preferred_element_type=F32)
            act = (h1 * jax.nn.sigmoid(h1)) * h3
            gate = gates[:, e:e + 1] + gates[:, EXPERTS_PER_GROUP + e:EXPERTS_PER_GROUP + e + 1]
            hid.append((act * gate).astype(BF16))
        o_ref[...] = jnp.dot(jnp.concatenate(hid, axis=1), w2b[...],
                             preferred_element_type=F32).astype(o_ref.dtype)

    @pl.when(u >= nused_ref[0])
    def _():
        o_ref[...] = jnp.zeros_like(o_ref)


def _moe(slabs, tables, layer, params):
    gid, src_row, _, n_used = tables
    n_tiles = gid.shape[0]
    expert_w = pl.BlockSpec((1, EXPERTS_PER_GROUP, D_MODEL, D_EXPERT),
                            lambda u, gid, src, nu: (layer, gid[u], 0, 0))
    return pl.pallas_call(
        functools.partial(_moe_kernel, n_slabs=len(slabs)),
        grid_spec=pltpu.PrefetchScalarGridSpec(
            num_scalar_prefetch=3,
            grid=(n_tiles,),
            in_specs=[pl.BlockSpec(memory_space=pl.ANY)] * len(slabs) + [
                expert_w,
                expert_w,
                pl.BlockSpec((1, 1, GROUP_HIDDEN, D_MODEL), lambda u, gid, src, nu: (layer, gid[u], 0, 0)),
            ],
            out_specs=pl.BlockSpec((MOE_TILE, D_MODEL), lambda u, gid, src, nu: (u, 0)),
            scratch_shapes=[
                pltpu.VMEM((2, MOE_TILE, SLAB_WIDTH), BF16),
                pltpu.SemaphoreType.DMA((2,)),
                pltpu.VMEM((EXPERTS_PER_GROUP, D_MODEL, D_EXPERT), BF16),
                pltpu.VMEM((EXPERTS_PER_GROUP, D_MODEL, D_EXPERT), BF16),
                pltpu.VMEM((GROUP_HIDDEN, D_MODEL), BF16),
            ],
        ),
        out_shape=jax.ShapeDtypeStruct((n_tiles * MOE_TILE, D_MODEL), BF16),
        compiler_params=_cparams("arbitrary"),
        name="moe",
    )(gid, src_row, n_used, *slabs, params["w1"], params["w3"], params["w2"])


def _unsorted_moe(row_ref, dest_ref, y_hbm, buf, sem, n_slabs):
    slot = _gather_step([y_hbm], row_ref, n_slabs * SLAB_CHUNKS, buf, sem)
    tn = (((0,), (0,)), ((), ()))
    out = []
    for j in range(n_slabs):
        y = buf[slot, j * SLAB_ROWS:(j + 1) * SLAB_ROWS, :]
        out.append(lax.dot_general(_permutation(dest_ref[j]), y, tn, preferred_element_type=F32))
    return jnp.concatenate(out, axis=0)


def _unsort_kernel(row_ref, xm_ref, dest_ref, mod_ref, y_hbm, o_ref, buf, sem, *, n_slabs):
    moe = _unsorted_moe(row_ref, dest_ref, y_hbm, buf, sem, n_slabs)
    o_ref[...] = xm_ref[...] + mod_ref[0, 0][5:6] * moe


def _unsort(pending, layer, mod, row_of_tile, *, tm):
    xm, dest, out_row, y = pending
    t_tokens = xm.shape[0]
    n_slabs = tm // SRC_TILE
    return pl.pallas_call(
        functools.partial(_unsort_kernel, n_slabs=n_slabs),
        grid_spec=pltpu.PrefetchScalarGridSpec(
            num_scalar_prefetch=1,
            grid=(t_tokens // tm,),
            in_specs=[
                pl.BlockSpec((tm, D_MODEL), lambda t, rows: (t, 0)),
                pl.BlockSpec((n_slabs, 1, SRC_TILE), lambda t, rows: (t, 0, 0)),
                pl.BlockSpec((1, 1, 6, D_MODEL), lambda t, rows: (layer, row_of_tile(t), 0, 0)),
                pl.BlockSpec(memory_space=pl.ANY),
            ],
            out_specs=pl.BlockSpec((tm, D_MODEL), lambda t, rows: (t, 0)),
            scratch_shapes=[pltpu.VMEM((2, n_slabs * SLAB_ROWS, D_MODEL), y.dtype),
                            pltpu.SemaphoreType.DMA((2,))],
        ),
        out_shape=jax.ShapeDtypeStruct((t_tokens, D_MODEL), F32),
        compiler_params=_cparams("arbitrary"),
        name="unsort",
    )(out_row, xm, dest, mod, y)


def kernel(x_prompt, x_sample, cache_k, cache_v, c, c_ctx, w_ada, b_ada, norm1_w, norm2_w, w_in,
           q_norm_w, k_norm_w, rpb, w_sgu, b_sgu, sgu_ln_w, sgu_ln_b, w_out, w_router, b_router,
           w1, w3, w2):
    batch, seq, _ = x_prompt.shape
    dec_batch, dec_seq, _ = x_sample.shape
    past = cache_k.shape[2]
    assert 1 + dec_batch <= COND_ROWS and dec_seq % (Q_ROWS * GRID_W) == 0 and seq % CHUNK == 0

    cond = jnp.zeros((COND_ROWS, D_MODEL), F32).at[0].set(c_ctx).at[1:1 + dec_batch].set(c)
    mod = _adaln(cond, w_ada, b_ada).reshape(DEPTH, COND_ROWS, 6, D_MODEL)

    xp = x_prompt.reshape(batch * seq, D_MODEL)
    xs = x_sample.reshape(dec_batch * dec_seq, D_MODEL)
    from_cache_layout = lambda a: a.transpose(0, 1, 3, 4, 2).reshape(dec_batch, DEPTH, ATTN_WIDTH, past)
    ck = from_cache_layout(cache_k)
    cv = from_cache_layout(cache_v)

    head_of = jnp.arange(256) // HEAD_DIM
    bd = (head_of[:, None] == head_of[None, :]).astype(BF16)
    tok_id = jnp.arange(SRC_TILE)
    tri = (tok_id[:, None] <= tok_id[None, :]).astype(BF16)

    params = dict(
        n1w=norm1_w.reshape(DEPTH, 1, D_MODEL),
        n2w=norm2_w.reshape(DEPTH, 1, D_MODEL),
        win=w_in.astype(BF16),
        wout=w_out.astype(BF16),
        qw=jnp.tile(q_norm_w, (1, N_HEADS)).reshape(DEPTH, 1, ATTN_WIDTH),
        kw=jnp.tile(k_norm_w, (1, N_HEADS)).reshape(DEPTH, 1, ATTN_WIDTH),
        bd=jnp.concatenate([bd, bd], axis=0),
        ws=w_sgu.astype(BF16),
        bs=jnp.broadcast_to(b_sgu[..., None], (DEPTH, SGU_GROUPS, CHUNK, SGU_GROUP_DIM)),
        lnw=sgu_ln_w,
        lnb=sgu_ln_b,
        wr=w_router.T.astype(BF16),
        br=b_router.reshape(N_EXPERTS, 1),
        tri=tri,
        w1=w1,
        w3=w3,
        w2=w2.reshape(DEPTH, N_EXPERT_GROUPS, GROUP_HIDDEN, D_MODEL),
    )
    tb = _bias_tables(rpb)

    tm = 2 * SRC_TILE
    assert dec_seq % tm == 0 and tm % seq == 0
    lat_tiles_per_seq = dec_seq // tm
    ctx_row = lambda i: 0
    lat_row = lambda i: 1 + i // lat_tiles_per_seq

    def moe_block(attended, layer):
        chunks = jnp.concatenate([meta[:, :N_EXPERT_GROUPS, 0] for _, _, _, meta in attended], axis=0)
        tables = _chunk_tables(chunks, _max_moe_tiles(chunks.shape[0]))
        y = _moe([slab for _, slab, _, _ in attended], tables, layer, params)
        pending, first = [], 0
        for xm, _, dest, meta in attended:
            n_rows = meta.shape[0] * SLAB_CHUNKS
            pending.append((xm, dest, tables[2][first:first + n_rows], y))
            first += n_rows
        return pending

    kv_new = jax.ShapeDtypeStruct((batch, DEPTH, ATTN_WIDTH, seq), F32)

    for l in range(DEPTH):
        outs = _inproj(xp, l, mod, ctx_row, params, tm=tm,
                       kv_buffers=kv_new if l == 0 else (kbuf, vbuf))
        if l > 0:
            xp, outs = outs[0], outs[1:]
        q, kbuf, vbuf, v, sg = outs
        attended_ctx = _ctx_attention(q, kbuf, v, sg, xp, l, mod, params, seq=seq)

        outs = _inproj(xs, l, mod, lat_row, params, tm=tm)
        if l > 0:
            xs, outs = outs[0], outs[1:]
        q, k, v, sg = outs
        attended_lat = _nbr_attention(q, k, v, ck, cv, tb, sg, xs, l, mod, params,
                                      batch=dec_batch, seq=dec_seq)

        xp, xs = moe_block([attended_ctx, attended_lat], l)
    xp = _unsort(xp, DEPTH - 1, mod, ctx_row, tm=tm)
    xs = _unsort(xs, DEPTH - 1, mod, lat_row, tm=tm)

    to_cache_layout = lambda buf: buf.reshape(batch, DEPTH, N_HEADS, HEAD_DIM, seq).transpose(0, 1, 4, 2, 3)
    return (xp.reshape(batch, seq, D_MODEL), xs.reshape(dec_batch, dec_seq, D_MODEL),
            to_cache_layout(kbuf), to_cache_layout(vbuf))
```

```python
import functools

import jax
import jax.numpy as jnp
from jax import lax
from jax.experimental import pallas as pl
from jax.experimental.pallas import tpu as pltpu

F32 = jnp.float32
BF16 = jnp.bfloat16

D_MODEL = 1024
DEPTH = 4
N_HEADS = 8
HEAD_DIM = 64
ATTN_WIDTH = N_HEADS * HEAD_DIM
SGU_GROUPS = 4
SGU_GROUP_DIM = 128
SGU_WIDTH = SGU_GROUPS * SGU_GROUP_DIM
CHUNK = 128
IN_WIDTH = 3 * ATTN_WIDTH + 2 * SGU_WIDTH
GRID_W = 64
WIN_H = 8
WIN_W = 16
N_EXPERTS = 16
N_EXPERT_GROUPS = 4
EXPERTS_PER_GROUP = 4
D_EXPERT = 256
GROUP_HIDDEN = EXPERTS_PER_GROUP * D_EXPERT
EPS = 1e-6
NEG = -1e30
LOG2E = 1.4426950408889634

LANES = 128
HEADS_PER_VREG = LANES // HEAD_DIM
N_HEAD_PAIRS = N_HEADS // HEADS_PER_VREG
COND_ROWS = 16
Q_ROWS = 4
WIN_ROWS = Q_ROWS + WIN_H
VMEM_LIMIT = 48 * 1024 * 1024

SRC_TILE = 256
CHUNK_ROWS = 16
SLAB_ROWS = SRC_TILE + N_EXPERT_GROUPS * CHUNK_ROWS
SLAB_CHUNKS = SLAB_ROWS // CHUNK_ROWS
SLAB_WIDTH = D_MODEL + LANES
MOE_TILE = 512
TILE_CHUNKS = MOE_TILE // CHUNK_ROWS


def _cparams(*sem):
    return pltpu.CompilerParams(dimension_semantics=sem, vmem_limit_bytes=VMEM_LIMIT)


def _adaln_kernel(cond_ref, w_ref, b_ref, o_ref):
    c = cond_ref[...]
    a = (c * jax.nn.sigmoid(c)).astype(BF16)
    o_ref[0] = jnp.dot(a, w_ref[0].astype(BF16), preferred_element_type=F32) + b_ref[0]


def _adaln(cond, w_ada, b_ada):
    tn = 1536
    n = 6 * D_MODEL
    return pl.pallas_call(
        _adaln_kernel,
        grid=(DEPTH, n // tn),
        in_specs=[
            pl.BlockSpec((COND_ROWS, D_MODEL), lambda l, j: (0, 0)),
            pl.BlockSpec((1, D_MODEL, tn), lambda l, j: (l, 0, j)),
            pl.BlockSpec((1, 1, tn), lambda l, j: (l, 0, j)),
        ],
        out_specs=pl.BlockSpec((1, COND_ROWS, tn), lambda l, j: (l, 0, j)),
        out_shape=jax.ShapeDtypeStruct((DEPTH, COND_ROWS, n), F32),
        compiler_params=_cparams("arbitrary", "arbitrary"),
        name="adaln",
    )(cond, w_ada, b_ada.reshape(DEPTH, 1, n))


def _gelu_tanh(x):
    return 0.5 * x * (1.0 + jnp.tanh(0.7978845608028654 * (x + 0.044715 * (x * x * x))))


def _head_rms(t, bd, w):
    t2 = t * t
    hi = t2.astype(BF16)
    lo = (t2 - hi.astype(F32)).astype(BF16)
    outs = []
    for c in range(ATTN_WIDTH // 256):
        sl = slice(256 * c, 256 * c + 256)
        hl = jnp.concatenate([hi[:, sl], lo[:, sl]], axis=1)
        ss = jnp.dot(hl, bd, preferred_element_type=F32)
        outs.append(t[:, sl] * lax.rsqrt(ss * (1.0 / HEAD_DIM) + EPS))
    return jnp.concatenate(outs, axis=1) * w


def _inproj_kernel(*refs, tm, transposed_kv, new_kv_layer, fused_moe):
    refs = list(refs)
    if fused_moe:
        row_ref, xm_ref, dest_ref, prev_mod_ref, y_hbm = refs[:5]
        refs = refs[5:]
    else:
        x_ref = refs.pop(0)
    (mod_ref, n1w_ref, win_ref, qw_ref, kw_ref, bd_ref, ws_ref, bs_ref, lnw_ref, lnb_ref) = refs[:10]
    refs = refs[10:]
    if transposed_kv and new_kv_layer is None:
        refs = refs[2:]
    if fused_moe:
        x_out_ref = refs.pop(0)
    if transposed_kv:
        q_ref, kt_ref, vt_ref, v_ref, sg_ref = refs[:5]
        refs = refs[5:]
    else:
        q_ref, k_ref, v_ref, sg_ref = refs[:4]
        refs = refs[4:]
    if fused_moe:
        buf, sem = refs
        moe = _unsorted_moe(row_ref, dest_ref, y_hbm, buf, sem, tm // SRC_TILE)
        x = xm_ref[...] + prev_mod_ref[0, 0][5:6] * moe
        x_out_ref[...] = x
    else:
        x = x_ref[...]
    m = mod_ref[0, 0]
    sh1, sc1 = m[0:1], m[1:2]
    ms = jnp.mean(x * x, axis=-1, keepdims=True)
    h = x * lax.rsqrt(ms + EPS) * n1w_ref[0]
    h = (h * (1.0 + sc1) + sh1).astype(BF16)
    p = jnp.dot(h, win_ref[0], preferred_element_type=F32)

    bd = bd_ref[...]
    q = _head_rms(p[:, 0:ATTN_WIDTH], bd, qw_ref[0])
    k = _head_rms(p[:, ATTN_WIDTH:2 * ATTN_WIDTH], bd, kw_ref[0])
    v = p[:, 2 * ATTN_WIDTH:3 * ATTN_WIDTH]
    q_ref[...] = (q * (HEAD_DIM ** -0.5 * LOG2E)).astype(q_ref.dtype)
    if transposed_kv:
        seq = kt_ref.shape[3]
        if new_kv_layer is None:
            slot = 0
        else:
            slot = new_kv_layer
            kt_ref[...] = jnp.zeros_like(kt_ref)
            vt_ref[...] = jnp.zeros_like(vt_ref)
        for j in range(tm // seq):
            kt_ref[j, slot] = k[j * seq:(j + 1) * seq].T
            vt_ref[j, slot] = v[j * seq:(j + 1) * seq].T
    else:
        k_ref[...] = k.astype(k_ref.dtype)
    v_ref[...] = v.astype(v_ref.dtype)

    off_u = 3 * ATTN_WIDTH
    off_v = off_u + SGU_WIDTH
    for g in range(SGU_GROUPS):
        gl = slice(g * SGU_GROUP_DIM, (g + 1) * SGU_GROUP_DIM)
        u = _gelu_tanh(p[:, off_u + g * SGU_GROUP_DIM: off_u + (g + 1) * SGU_GROUP_DIM])
        t = _gelu_tanh(p[:, off_v + g * SGU_GROUP_DIM: off_v + (g + 1) * SGU_GROUP_DIM])
        mu = jnp.mean(t, axis=-1, keepdims=True)
        d = t - mu
        var = jnp.mean(d * d, axis=-1, keepdims=True)
        y = (d * lax.rsqrt(var + EPS) * lnw_ref[0, g:g + 1, :] + lnb_ref[0, g:g + 1, :]).astype(BF16)
        for c in range(tm // CHUNK):
            rows = slice(c * CHUNK, (c + 1) * CHUNK)
            sv = jnp.dot(ws_ref[0, g], y[rows], preferred_element_type=F32) + bs_ref[0, g]
            sg_ref[rows, gl] = (u[rows] * sv).astype(sg_ref.dtype)


def _inproj(x, layer, mod, row_of_tile, params, *, tm, kv_buffers=None):
    fused_moe = isinstance(x, tuple)
    tok = lambda i, *_: (i, 0)
    lyr3 = lambda i, *_: (layer, 0, 0)
    lyr4 = lambda i, *_: (layer, 0, 0, 0)
    mod_row = lambda lyr: pl.BlockSpec((1, 1, 6, D_MODEL), lambda i, *_: (lyr, row_of_tile(i), 0, 0))
    x_spec = pl.BlockSpec((tm, D_MODEL), tok)
    if fused_moe:
        n_slabs = tm // SRC_TILE
        xm, dest, out_row, y = x
        t_tokens = xm.shape[0]
        prefetch = [out_row]
        in_specs = [x_spec, pl.BlockSpec((n_slabs, 1, SRC_TILE), lambda i, *_: (i, 0, 0)),
                    mod_row(layer - 1), pl.BlockSpec(memory_space=pl.ANY)]
        args = [xm, dest, mod, y]
        scratch = [pltpu.VMEM((2, n_slabs * SLAB_ROWS, D_MODEL), y.dtype), pltpu.SemaphoreType.DMA((2,))]
    else:
        t_tokens = x.shape[0]
        prefetch, in_specs, args, scratch = [], [x_spec], [x], []
    act = pl.BlockSpec((tm, ATTN_WIDTH), tok)
    act_sds = jax.ShapeDtypeStruct((t_tokens, ATTN_WIDTH), BF16)
    in_specs += [
        mod_row(layer),
        pl.BlockSpec((1, 1, D_MODEL), lyr3),
        pl.BlockSpec((1, D_MODEL, IN_WIDTH), lyr3),
        pl.BlockSpec((1, 1, ATTN_WIDTH), lyr3),
        pl.BlockSpec((1, 1, ATTN_WIDTH), lyr3),
        pl.BlockSpec((512, 256), lambda i, *_: (0, 0)),
        pl.BlockSpec((1, SGU_GROUPS, CHUNK, CHUNK), lyr4),
        pl.BlockSpec((1, SGU_GROUPS, CHUNK, SGU_GROUP_DIM), lyr4),
        pl.BlockSpec((1, SGU_GROUPS, SGU_GROUP_DIM), lyr3),
        pl.BlockSpec((1, SGU_GROUPS, SGU_GROUP_DIM), lyr3),
    ]
    args += [mod, params["n1w"], params["win"], params["qw"], params["kw"], params["bd"],
             params["ws"], params["bs"], params["lnw"], params["lnb"]]
    out_specs = [x_spec] if fused_moe else []
    out_shape = [jax.ShapeDtypeStruct((t_tokens, D_MODEL), F32)] if fused_moe else []
    if kv_buffers is None:
        out_specs += [act] * 4
        out_shape += [act_sds] * 4
        aliases = {}
    elif isinstance(kv_buffers, jax.ShapeDtypeStruct):
        seq = kv_buffers.shape[3]
        kv_spec = pl.BlockSpec((tm // seq,) + kv_buffers.shape[1:], lambda i, *_: (i, 0, 0, 0))
        out_specs += [act, kv_spec, kv_spec, act, act]
        out_shape += [act_sds, kv_buffers, kv_buffers, act_sds, act_sds]
        aliases = {}
    else:
        kbuf, vbuf = kv_buffers
        seq = kbuf.shape[3]
        kv_spec = pl.BlockSpec((tm // seq, 1, ATTN_WIDTH, seq), lambda i, *_: (i, layer, 0, 0))
        first_alias_in = len(prefetch) + len(args)
        first_alias_out = len(out_specs) + 1
        in_specs += [pl.BlockSpec(memory_space=pl.ANY)] * 2
        args += [kbuf, vbuf]
        out_specs += [act, kv_spec, kv_spec, act, act]
        out_shape += [act_sds, jax.ShapeDtypeStruct(kbuf.shape, F32),
                      jax.ShapeDtypeStruct(vbuf.shape, F32), act_sds, act_sds]
        aliases = {first_alias_in: first_alias_out, first_alias_in + 1: first_alias_out + 1}
    return pl.pallas_call(
        functools.partial(_inproj_kernel, tm=tm, transposed_kv=kv_buffers is not None,
                          new_kv_layer=layer if isinstance(kv_buffers, jax.ShapeDtypeStruct) else None,
                          fused_moe=fused_moe),
        grid_spec=pltpu.PrefetchScalarGridSpec(
            num_scalar_prefetch=len(prefetch),
            grid=(t_tokens // tm,),
            in_specs=in_specs,
            out_specs=out_specs,
            scratch_shapes=scratch,
        ),
        out_shape=out_shape,
        input_output_aliases=aliases,
        compiler_params=_cparams("arbitrary"),
        name="inproj",
    )(*prefetch, *args)


def _route(lg_t, br):
    s = jax.nn.sigmoid(lg_t)
    sel = s + br
    row = lambda a, i: a[i:i + 1]
    g_score = []
    for g in range(N_EXPERT_GROUPS):
        v = [row(sel, 4 * g + i) for i in range(4)]
        best_pair = None
        for i in range(4):
            for j in range(i + 1, 4):
                pair = v[i] + v[j]
                best_pair = pair if best_pair is None else jnp.maximum(best_pair, pair)
        g_score.append(best_pair)
    best = jnp.zeros_like(g_score[0], dtype=jnp.int32)
    top = g_score[0]
    for g in range(1, N_EXPERT_GROUPS):
        upd = g_score[g] > top
        best = jnp.where(upd, g, best)
        top = jnp.where(upd, g_score[g], top)
    cand, aff = [], []
    for i in range(4):
        ci, si = row(sel, i), row(s, i)
        for g in range(1, N_EXPERT_GROUPS):
            ci = jnp.where(best == g, row(sel, 4 * g + i), ci)
            si = jnp.where(best == g, row(s, 4 * g + i), si)
        cand.append(ci)
        aff.append(si)

    def first_argmax(vals):
        idx = jnp.zeros_like(best)
        top_v = vals[0]
        for i in range(1, 4):
            upd = vals[i] > top_v
            idx = jnp.where(upd, i, idx)
            top_v = jnp.where(upd, vals[i], top_v)
        return idx

    i1 = first_argmax(cand)
    i2 = first_argmax([jnp.where(i1 == i, -jnp.inf, cand[i]) for i in range(4)])
    pick = lambda idx: sum(jnp.where(idx == i, aff[i], 0.0) for i in range(4))
    den = pick(i1) + pick(i2)
    gate = [jnp.where((i1 == i) | (i2 == i), aff[i] / den, 0.0) for i in range(4)]
    hi = [x.astype(BF16) for x in gate]
    lo = [(x - h.astype(F32)).astype(BF16) for x, h in zip(gate, hi)]
    return best, hi + lo


def _slab_positions(best, tri):
    n = best.shape[1]
    onehot = [jnp.where(best == g, 1.0, 0.0) for g in range(N_EXPERT_GROUPS)]
    pad = [jnp.zeros_like(onehot[0])] * (8 - N_EXPERT_GROUPS)
    oh = jnp.concatenate(onehot + pad, axis=0).astype(BF16)
    counts = jnp.dot(oh, tri, preferred_element_type=F32)
    dest = jnp.zeros((1, n), F32)
    seg_start = jnp.zeros((1, 1), F32)
    chunks = []
    for g in range(N_EXPERT_GROUPS):
        cg = counts[g:g + 1]
        n_g = jnp.max(cg, axis=1, keepdims=True)
        c_g = jnp.floor((n_g + (CHUNK_ROWS - 1)) * (1.0 / CHUNK_ROWS))
        dest = dest + onehot[g] * (seg_start + cg - 1.0)
        seg_start = seg_start + c_g * CHUNK_ROWS
        chunks.append(c_g)
    return dest.astype(jnp.int32), chunks


def _permutation(dest):
    n = dest.shape[1]
    hit = lax.broadcasted_iota(jnp.int32, (SLAB_ROWS, n), 0) == dest
    return jnp.where(hit, 1.0, 0.0).astype(BF16)


def _post_attention(a, sg, x, m, wout_ref, n2w_ref, xm_ref):
    g1, sh2, sc2 = m[2:3], m[3:4], m[4:5]
    cat = jnp.concatenate([a, sg], axis=1)
    y = jnp.dot(cat, wout_ref[0], preferred_element_type=F32)
    xm = x + g1 * y
    xm_ref[...] = xm
    ms = jnp.mean(xm * xm, axis=-1, keepdims=True)
    h2 = xm * lax.rsqrt(ms + EPS) * n2w_ref[0]
    return (h2 * (1.0 + sc2) + sh2).astype(BF16)


def _route_previous(h2_scr, wr_ref, br_ref, tri_ref, slab_ref, dest_ref, meta_ref):
    h2 = h2_scr[...]
    lg_t = lax.dot_general(wr_ref[...], h2, (((1,), (1,)), ((), ())), preferred_element_type=F32)
    yield
    best, gates = _route(lg_t, br_ref[...])
    dest, chunks = _slab_positions(best, tri_ref[...])
    yield
    perm = _permutation(dest)
    slab_ref[:, 0:D_MODEL] = jnp.dot(perm, h2, preferred_element_type=F32).astype(BF16)
    gmat = jnp.concatenate(gates + [jnp.zeros((LANES - len(gates), dest.shape[1]), BF16)], axis=0)
    slab_ref[:, D_MODEL:] = lax.dot_general(perm, gmat, (((1,), (1,)), ((), ())),
                                            preferred_element_type=F32).astype(BF16)
    dest_ref[0] = dest
    meta = [jnp.broadcast_to(c, (1, LANES)) for c in chunks]
    meta += [jnp.zeros((8 - len(chunks), LANES), F32)]
    meta_ref[0] = jnp.concatenate(meta, axis=0).astype(jnp.int32)


def _pair_pos(axis):
    return lax.broadcasted_iota(jnp.int32, (1, LANES) if axis == 1 else (LANES, 1), axis)


def _head_mask(hh, axis):
    pos = _pair_pos(axis)
    return (pos >= hh * HEAD_DIM) & (pos < (hh + 1) * HEAD_DIM)


def _only_head(x2, hh, axis):
    return jnp.where(_head_mask(hh, axis), x2, jnp.zeros_like(x2))


def _attend_heads(logits, finish, router, depth):
    outs, even = [], None
    ahead = [logits(h) for h in range(depth)]
    for h in range(N_HEADS):
        s = ahead.pop(0)
        if h + depth < N_HEADS:
            ahead.append(logits(h + depth))
        if h == N_HEADS // 2:
            next(router)
        o = finish(h, s)
        if h % HEADS_PER_VREG == 0:
            even = o
        else:
            outs.append((even + o).astype(BF16))
    for _ in router:
        pass
    return jnp.concatenate(outs, axis=1)


def _ctx_attn_kernel(q_ref, kt_ref, v_ref, sg_ref, x_ref, mod_ref, wout_ref, n2w_ref, wr_ref, br_ref,
                     tri_ref, xm_ref, slab_ref, dest_ref, meta_ref, h2_scr):
    step = pl.program_id(0)

    @pl.when(step == 0)
    def _():
        h2_scr[...] = jnp.zeros_like(h2_scr)

    def logits(h):
        cols = slice(h // HEADS_PER_VREG * LANES, (h // HEADS_PER_VREG + 1) * LANES)
        kt2 = kt_ref[0, 0, cols, :].astype(BF16)
        return jnp.dot(q_ref[:, cols], _only_head(kt2, h % HEADS_PER_VREG, 0),
                       preferred_element_type=F32)

    def finish(h, s):
        cols = slice(h // HEADS_PER_VREG * LANES, (h // HEADS_PER_VREG + 1) * LANES)
        e = jnp.exp2(s - jnp.max(s, axis=-1, keepdims=True))
        l = jnp.sum(e, axis=-1, keepdims=True)
        vm = _only_head(v_ref[:, cols], h % HEADS_PER_VREG, 1)
        return jnp.dot(e.astype(BF16), vm, preferred_element_type=F32) * (1.0 / l)

    new_router = lambda: _route_previous(h2_scr, wr_ref, br_ref, tri_ref, slab_ref, dest_ref, meta_ref)
    last = pl.num_programs(0) - 1

    @pl.when(step < last)
    def _():
        router = new_router()
        next(router)
        a = _attend_heads(logits, finish, router, depth=2)
        h2_scr[...] = _post_attention(a, sg_ref[...], x_ref[...], mod_ref[0, 0], wout_ref, n2w_ref,
                                      xm_ref)

    @pl.when(step == last)
    def _():
        for _ in new_router():
            pass


def _post_attention_out(t_tokens):
    n_src = t_tokens // SRC_TILE
    prev = lambda s: jnp.maximum(s - 1, 0)
    specs = [
        pl.BlockSpec((SRC_TILE, D_MODEL), lambda s: (jnp.minimum(s, n_src - 1), 0)),
        pl.BlockSpec((SLAB_ROWS, SLAB_WIDTH), lambda s: (prev(s), 0)),
        pl.BlockSpec((1, 1, SRC_TILE), lambda s: (prev(s), 0, 0)),
        pl.BlockSpec((1, 8, LANES), lambda s: (prev(s), 0, 0)),
    ]
    shapes = [
        jax.ShapeDtypeStruct((t_tokens, D_MODEL), F32),
        jax.ShapeDtypeStruct((n_src * SLAB_ROWS, SLAB_WIDTH), BF16),
        jax.ShapeDtypeStruct((n_src, 1, SRC_TILE), jnp.int32),
        jax.ShapeDtypeStruct((n_src, 8, LANES), jnp.int32),
    ]
    return specs, shapes, pltpu.VMEM((SRC_TILE, D_MODEL), BF16)


def _ctx_attention(q, kbuf, v, sg, x, layer, mod, params, *, seq):
    assert seq == SRC_TILE
    t_tokens = x.shape[0]
    n_seq = t_tokens // seq
    seq_of = lambda s: jnp.minimum(s, n_seq - 1)
    tok = lambda s: (seq_of(s), 0)
    lyr3 = lambda s: (layer, 0, 0)
    full2 = lambda s: (0, 0)
    out_specs, out_shape, h2_scratch = _post_attention_out(t_tokens)
    return pl.pallas_call(
        _ctx_attn_kernel,
        grid=(n_seq + 1,),
        in_specs=[
            pl.BlockSpec((seq, ATTN_WIDTH), tok),
            pl.BlockSpec((1, 1, ATTN_WIDTH, seq), lambda s: (seq_of(s), layer, 0, 0)),
            pl.BlockSpec((seq, ATTN_WIDTH), tok),
            pl.BlockSpec((seq, SGU_WIDTH), tok),
            pl.BlockSpec((seq, D_MODEL), tok),
            pl.BlockSpec((1, 1, 6, D_MODEL), lambda s: (layer, 0, 0, 0)),
            pl.BlockSpec((1, D_MODEL, D_MODEL), lyr3),
            pl.BlockSpec((1, 1, D_MODEL), lyr3),
            pl.BlockSpec((N_EXPERTS, D_MODEL), full2),
            pl.BlockSpec((N_EXPERTS, 1), full2),
            pl.BlockSpec((SRC_TILE, SRC_TILE), full2),
        ],
        out_specs=out_specs,
        out_shape=out_shape,
        scratch_shapes=[h2_scratch],
        compiler_params=_cparams("arbitrary"),
        name="ctx_attn",
    )(q, kbuf, v, sg, x, mod, params["wout"], params["n2w"], params["wr"], params["br"], params["tri"])


def _row_start(r, rows):
    return jnp.clip(r - WIN_H // 2, 0, rows - WIN_H)


def _nbr_attn_kernel(q_ref, k_ref, v_ref, ck_ref, cv_ref, tb_ref, sg_ref, x_ref, mod_ref, wout_ref,
                     n2w_ref, wr_ref, br_ref, tri_ref, xm_ref, slab_ref, dest_ref, meta_ref,
                     kwm_ref, vwm_ref, kcm_ref, vcm_ref, h2_scr, *, rows, n_tiles):
    step = pl.program_id(0)
    tiles_per_seq = rows // Q_ROWS
    t = jnp.minimum(step, n_tiles - 1) % tiles_per_seq

    @pl.when(step == 0)
    def _():
        h2_scr[...] = jnp.zeros_like(h2_scr)

    @pl.when((t == 0) & (step < n_tiles))
    def _():
        for j in range(N_HEAD_PAIRS):
            cols = slice(j * LANES, (j + 1) * LANES)
            k2, v2 = k_ref[:, cols], v_ref[:, cols]
            ck2 = ck_ref[0, 0, cols, :].astype(BF16)
            cv2 = cv_ref[0, 0, cols, :].astype(BF16)
            key_row = lax.broadcasted_iota(jnp.int32, (k2.shape[0], 1), 0) // GRID_W
            for hh in range(HEADS_PER_VREG):
                h = HEADS_PER_VREG * j + hh
                row_lane = _pair_pos(1) == (1 - hh) * HEAD_DIM + key_row
                kwm_ref[h] = jnp.where(_head_mask(hh, 1), k2, jnp.where(row_lane, 1.0, 0.0).astype(BF16))
                vwm_ref[h] = _only_head(v2, hh, 1)
                kcm_ref[h] = _only_head(ck2, hh, 0)
                vcm_ref[h] = _only_head(cv2, hh, 0)

    new_router = lambda: _route_previous(h2_scr, wr_ref, br_ref, tri_ref, slab_ref, dest_ref, meta_ref)

    @pl.when(step == n_tiles)
    def _():
        for _ in new_router():
            pass

    @pl.when(step < n_tiles)
    def _():
        _attend_tile(t, new_router(), q_ref, tb_ref, sg_ref, x_ref, mod_ref, wout_ref, n2w_ref, xm_ref,
                     kwm_ref, vwm_ref, kcm_ref, vcm_ref, h2_scr, rows)


def _attend_tile(t, router, q_ref, tb_ref, sg_ref, x_ref, mod_ref, wout_ref, n2w_ref, xm_ref,
                 kwm_ref, vwm_ref, kcm_ref, vcm_ref, h2_scr, rows):
    next(router)
    r0 = t * Q_ROWS
    ws = jnp.minimum(_row_start(r0, rows), rows - WIN_ROWS)
    tok0 = pl.multiple_of(ws * GRID_W, GRID_W)

    lane = _pair_pos(1)
    blk_idx = []
    row_pen = [[], []]
    for a in range(Q_ROWS):
        r = r0 + a
        rs = _row_start(r, rows)
        blk_idx.append([jnp.clip(ws + 2 * jp - r + WIN_H, 0, 2 * WIN_H - 1)
                        for jp in range(WIN_ROWS // 2)])
        for hh in range(HEADS_PER_VREG):
            kr = lane - (1 - hh) * HEAD_DIM
            in_band = (kr >= rs) & (kr < rs + WIN_H)
            pen = jnp.where(in_band | (kr < 0) | (kr >= rows), 0.0, NEG)
            row_pen[hh].append(jnp.broadcast_to(pen, (GRID_W, LANES)))
    row_pen = [jnp.concatenate(p, axis=0).astype(BF16) for p in row_pen]

    nt = (((1,), (1,)), ((), ()))
    window = pl.ds(tok0, WIN_ROWS * GRID_W)

    def logits(h):
        hh = h % HEADS_PER_VREG
        q2 = q_ref[:, h // HEADS_PER_VREG * LANES:(h // HEADS_PER_VREG + 1) * LANES]
        q2 = jnp.where(_head_mask(hh, 1), q2, row_pen[hh])
        bias = jnp.concatenate([
            jnp.concatenate([tb_ref[0, h, blk_idx[a][jp]] for jp in range(WIN_ROWS // 2)], axis=1)
            for a in range(Q_ROWS)], axis=0)
        s_w = lax.dot_general(q2, kwm_ref[h, window, :], nt, preferred_element_type=F32) + bias
        s_c = jnp.dot(q2, kcm_ref[h], preferred_element_type=F32)
        return s_w, s_c

    def finish(h, s):
        s_w, s_c = s
        mx = jnp.maximum(jnp.max(s_w, axis=-1, keepdims=True), jnp.max(s_c, axis=-1, keepdims=True))
        e_w = jnp.exp2(s_w - mx)
        e_c = jnp.exp2(s_c - mx)
        l = jnp.sum(e_w, axis=-1, keepdims=True) + jnp.sum(e_c, axis=-1, keepdims=True)
        return (jnp.dot(e_w.astype(BF16), vwm_ref[h, window, :], preferred_element_type=F32)
                + lax.dot_general(e_c.astype(BF16), vcm_ref[h], nt,
                                  preferred_element_type=F32)) * (1.0 / l)

    a_out = _attend_heads(logits, finish, router, depth=1)
    h2_scr[...] = _post_attention(a_out, sg_ref[...], x_ref[...], mod_ref[0, 0], wout_ref, n2w_ref,
                                  xm_ref)


def _nbr_attention(q, k, v, cache_k, cache_v, tb, sg, x, layer, mod, params, *, batch, seq):
    t_tokens = x.shape[0]
    rows = seq // GRID_W
    tq = Q_ROWS * GRID_W
    assert tq == SRC_TILE
    nt = seq // tq
    past = cache_k.shape[3]
    n_tiles = batch * nt
    out_specs, out_shape, h2_scratch = _post_attention_out(t_tokens)
    tile_of = lambda s: jnp.minimum(s, n_tiles - 1)
    seq_of = lambda s: tile_of(s) // nt
    tok = lambda s: (tile_of(s), 0)
    per_b = lambda s: (seq_of(s), 0)
    lyr3 = lambda s: (layer, 0, 0)
    full2 = lambda s: (0, 0)
    cache_spec = pl.BlockSpec((1, 1, ATTN_WIDTH, past), lambda s: (seq_of(s), layer, 0, 0))
    masked = lambda n: pltpu.VMEM((N_HEADS, n, LANES), BF16)
    masked_t = pltpu.VMEM((N_HEADS, LANES, past), BF16)
    return pl.pallas_call(
        functools.partial(_nbr_attn_kernel, rows=rows, n_tiles=n_tiles),
        grid=(n_tiles + 1,),
        in_specs=[
            pl.BlockSpec((tq, ATTN_WIDTH), tok),
            pl.BlockSpec((seq, ATTN_WIDTH), per_b),
            pl.BlockSpec((seq, ATTN_WIDTH), per_b),
            cache_spec,
            cache_spec,
            pl.BlockSpec((1, N_HEADS, 2 * WIN_H, GRID_W, LANES), lambda s: (layer, 0, 0, 0, 0)),
            pl.BlockSpec((tq, SGU_WIDTH), tok),
            pl.BlockSpec((tq, D_MODEL), tok),
            pl.BlockSpec((1, 1, 6, D_MODEL), lambda s: (layer, 1 + seq_of(s), 0, 0)),
            pl.BlockSpec((1, D_MODEL, D_MODEL), lyr3),
            pl.BlockSpec((1, 1, D_MODEL), lyr3),
            pl.BlockSpec((N_EXPERTS, D_MODEL), full2),
            pl.BlockSpec((N_EXPERTS, 1), full2),
            pl.BlockSpec((SRC_TILE, SRC_TILE), full2),
        ],
        out_specs=out_specs,
        out_shape=out_shape,
        scratch_shapes=[masked(seq), masked(seq), masked_t, masked_t, h2_scratch],
        compiler_params=_cparams("arbitrary"),
        name="nbr_attn",
    )(q, k, v, cache_k, cache_v, tb, sg, x, mod, params["wout"], params["n2w"], params["wr"],
      params["br"], params["tri"])


def _bias_tables(rpb):
    n_rel = 2 * WIN_W - 1
    cols = jnp.arange(GRID_W)
    col_start = jnp.clip(cols - WIN_W // 2, 0, GRID_W - WIN_W)
    col_ok = (cols[None, :] >= col_start[:, None]) & (cols[None, :] < col_start[:, None] + WIN_W)
    rel = cols[None, :] - cols[:, None] + WIN_W - 1
    pick = ((rel[None] == jnp.arange(n_rel)[:, None, None]) & col_ok[None]).astype(F32)
    zero = jnp.zeros_like(pick)
    halves = jnp.concatenate([jnp.concatenate([pick, zero], axis=2),
                              jnp.concatenate([zero, pick], axis=2)], axis=0)
    masked = jnp.where(jnp.concatenate([col_ok, col_ok], axis=1), 0.0, NEG)[None]
    select = jnp.concatenate([halves, masked], axis=0)

    by_offset = jnp.pad(rpb * LOG2E, ((0, 0), (0, 0), (1, 1), (0, 0)))
    feats = jnp.concatenate([by_offset[:, :, :-1], by_offset[:, :, 1:],
                             jnp.ones(by_offset.shape[:2] + (2 * WIN_H, 1), F32)], axis=-1)
    return jnp.einsum("lhdc,cqn->lhdqn", feats, select, precision=lax.Precision.HIGHEST)


def _max_moe_tiles(n_src):
    max_chunks = n_src * SRC_TILE // CHUNK_ROWS + n_src * N_EXPERT_GROUPS
    return -(-max_chunks // TILE_CHUNKS) + N_EXPERT_GROUPS


def _chunk_tables(chunks, n_tiles):
    n_src = chunks.shape[0]
    groups = jnp.arange(N_EXPERT_GROUPS)
    seg_end = jnp.cumsum(chunks, axis=1)
    seg_start = seg_end - chunks
    src_end = jnp.cumsum(chunks, axis=0)
    src_start = src_end - chunks
    total = src_end[-1]
    tiles = (total + TILE_CHUNKS - 1) // TILE_CHUNKS
    tile_end = jnp.cumsum(tiles)
    tile_start = tile_end - tiles
    n_used = tile_end[-1:]

    u = jnp.arange(n_tiles)
    gid = jnp.minimum(jnp.sum(u[:, None] >= tile_end[None, :], axis=1), N_EXPERT_GROUPS - 1)

    p = jnp.arange(n_tiles * TILE_CHUNKS)
    g_hot = jnp.repeat(gid, TILE_CHUNKS)[:, None] == groups[None, :]
    by_group = lambda vec: jnp.sum(jnp.where(g_hot, vec[None, :], 0), axis=1)
    q = p - TILE_CHUNKS * by_group(tile_start)
    live = q < by_group(total)
    src_end_p = jnp.sum(jnp.where(g_hot[:, None, :], src_end[None], 0), axis=2)
    t = jnp.minimum(jnp.sum(q[:, None] >= src_end_p, axis=1), n_src - 1)
    t_hot = t[:, None] == jnp.arange(n_src)[None, :]
    by_seg = lambda tab: jnp.sum(jnp.where(t_hot[:, :, None] & g_hot[:, None, :], tab[None], 0),
                                 axis=(1, 2))
    slab_chunk = by_seg(seg_start) + q - by_seg(src_start)
    src_row = jnp.where(live, t * SLAB_ROWS + CHUNK_ROWS * slab_chunk, 0)

    s = jnp.arange(SLAB_CHUNKS)
    gs = jnp.sum(s[None, :, None] >= seg_end[:, None, :], axis=2)
    used = gs < N_EXPERT_GROUPS
    s_hot = jnp.minimum(gs, N_EXPERT_GROUPS - 1)[:, :, None] == groups
    pick = lambda tab: jnp.sum(jnp.where(s_hot, tab, 0), axis=2)
    pos = (TILE_CHUNKS * pick(tile_start[None, None, :]) + pick(src_start[:, None, :])
           + s[None, :] - pick(seg_start[:, None, :]))
    out_row = jnp.where(used, CHUNK_ROWS * pos, 0).reshape(-1)
    i32 = lambda a: a.astype(jnp.int32)
    return i32(gid), i32(src_row), i32(out_row), i32(n_used)


def _chunk_copy(src_hbm, row, k, buf, sem, slot):
    return pltpu.make_async_copy(
        src_hbm.at[pl.ds(pl.multiple_of(row, CHUNK_ROWS), CHUNK_ROWS), :],
        buf.at[slot, pl.ds(k * CHUNK_ROWS, CHUNK_ROWS), :],
        sem.at[slot])


def _start_chunks(sources, row_ref, first, n_chunks, buf, sem, slot):
    bases = [sum(s.shape[0] for s in sources[:i]) for i in range(len(sources) + 1)]
    for k in range(n_chunks):
        row = row_ref[first + k]
        if len(sources) == 1:
            _chunk_copy(sources[0], row, k, buf, sem, slot).start()
        else:
            for src, lo, hi in zip(sources, bases[:-1], bases[1:]):
                @pl.when((row >= lo) & (row < hi))
                def _(src=src, lo=lo):
                    _chunk_copy(src, row - lo, k, buf, sem, slot).start()


def _gather_step(sources, row_ref, n_chunks, buf, sem):
    step = pl.program_id(0)
    slot = step % 2

    @pl.when(step == 0)
    def _():
        _start_chunks(sources, row_ref, 0, n_chunks, buf, sem, 0)

    @pl.when(step + 1 < pl.num_programs(0))
    def _():
        _start_chunks(sources, row_ref, (step + 1) * n_chunks, n_chunks, buf, sem, 1 - slot)

    for k in range(n_chunks):
        _chunk_copy(sources[0], 0, k, buf, sem, slot).wait()
    return slot


def _moe_kernel(gid_ref, src_ref, nused_ref, *refs, n_slabs):
    slabs = refs[:n_slabs]
    w1_ref, w3_ref, w2_ref, o_ref, buf, sem, w1b, w3b, w2b = refs[n_slabs:]
    u = pl.program_id(0)
    slot = _gather_step(slabs, src_ref, TILE_CHUNKS, buf, sem)

    @pl.when((u == 0) | (gid_ref[u] != gid_ref[jnp.maximum(u - 1, 0)]))
    def _():
        w1b[...] = w1_ref[0].astype(BF16)
        w3b[...] = w3_ref[0].astype(BF16)
        w2b[...] = w2_ref[0, 0].astype(BF16)

    @pl.when(u < nused_ref[0])
    def _():
        h = buf[slot, :, 0:D_MODEL]
        gates = buf[slot, :, D_MODEL:].astype(F32)
        hid = []
        for e in range(EXPERTS_PER_GROUP):
            h1 = jnp.dot(h, w1b[e], preferred_element_type=F32)
            h3 = jnp.dot(h, w3b[e], preferred_element_type=F32)
            act = (h1 * jax.nn.sigmoid(h1)) * h3
            gate = gates[:, e:e + 1] + gates[:, EXPERTS_PER_GROUP + e:EXPERTS_PER_GROUP + e + 1]
            hid.append((act * gate).astype(BF16))
        o_ref[...] = jnp.dot(jnp.concatenate(hid, axis=1), w2b[...],
                             preferred_element_type=F32).astype(o_ref.dtype)

    @pl.when(u >= nused_ref[0])
    def _():
        o_ref[...] = jnp.zeros_like(o_ref)


def _moe(slabs, tables, layer, params):
    gid, src_row, _, n_used = tables
    n_tiles = gid.shape[0]
    expert_w = pl.BlockSpec((1, EXPERTS_PER_GROUP, D_MODEL, D_EXPERT),
                            lambda u, gid, src, nu: (layer, gid[u], 0, 0))
    return pl.pallas_call(
        functools.partial(_moe_kernel, n_slabs=len(slabs)),
        grid_spec=pltpu.PrefetchScalarGridSpec(
            num_scalar_prefetch=3,
            grid=(n_tiles,),
            in_specs=[pl.BlockSpec(memory_space=pl.ANY)] * len(slabs) + [
                expert_w,
                expert_w,
                pl.BlockSpec((1, 1, GROUP_HIDDEN, D_MODEL), lambda u, gid, src, nu: (layer, gid[u], 0, 0)),
            ],
            out_specs=pl.BlockSpec((MOE_TILE, D_MODEL), lambda u, gid, src, nu: (u, 0)),
            scratch_shapes=[
                pltpu.VMEM((2, MOE_TILE, SLAB_WIDTH), BF16),
                pltpu.SemaphoreType.DMA((2,)),
                pltpu.VMEM((EXPERTS_PER_GROUP, D_MODEL, D_EXPERT), BF16),
                pltpu.VMEM((EXPERTS_PER_GROUP, D_MODEL, D_EXPERT), BF16),
                pltpu.VMEM((GROUP_HIDDEN, D_MODEL), BF16),
            ],
        ),
        out_shape=jax.ShapeDtypeStruct((n_tiles * MOE_TILE, D_MODEL), BF16),
        compiler_params=_cparams("arbitrary"),
        name="moe",
    )(gid, src_row, n_used, *slabs, params["w1"], params["w3"], params["w2"])


def _unsorted_moe(row_ref, dest_ref, y_hbm, buf, sem, n_slabs):
    slot = _gather_step([y_hbm], row_ref, n_slabs * SLAB_CHUNKS, buf, sem)
    tn = (((0,), (0,)), ((), ()))
    out = []
    for j in range(n_slabs):
        y = buf[slot, j * SLAB_ROWS:(j + 1) * SLAB_ROWS, :]
        out.append(lax.dot_general(_permutation(dest_ref[j]), y, tn, preferred_element_type=F32))
    return jnp.concatenate(out, axis=0)


def _unsort_kernel(row_ref, xm_ref, dest_ref, mod_ref, y_hbm, o_ref, buf, sem, *, n_slabs):
    moe = _unsorted_moe(row_ref, dest_ref, y_hbm, buf, sem, n_slabs)
    o_ref[...] = xm_ref[...] + mod_ref[0, 0][5:6] * moe


def _unsort(pending, layer, mod, row_of_tile, *, tm):
    xm, dest, out_row, y = pending
    t_tokens = xm.shape[0]
    n_slabs = tm // SRC_TILE
    return pl.pallas_call(
        functools.partial(_unsort_kernel, n_slabs=n_slabs),
        grid_spec=pltpu.PrefetchScalarGridSpec(
            num_scalar_prefetch=1,
            grid=(t_tokens // tm,),
            in_specs=[
                pl.BlockSpec((tm, D_MODEL), lambda t, rows: (t, 0)),
                pl.BlockSpec((n_slabs, 1, SRC_TILE), lambda t, rows: (t, 0, 0)),
                pl.BlockSpec((1, 1, 6, D_MODEL), lambda t, rows: (layer, row_of_tile(t), 0, 0)),
                pl.BlockSpec(memory_space=pl.ANY),
            ],
            out_specs=pl.BlockSpec((tm, D_MODEL), lambda t, rows: (t, 0)),
            scratch_shapes=[pltpu.VMEM((2, n_slabs * SLAB_ROWS, D_MODEL), y.dtype),
                            pltpu.SemaphoreType.DMA((2,))],
        ),
        out_shape=jax.ShapeDtypeStruct((t_tokens, D_MODEL), F32),
        compiler_params=_cparams("arbitrary"),
        name="unsort",
    )(out_row, xm, dest, mod, y)


def kernel(x_prompt, x_sample, cache_k, cache_v, c, c_ctx, w_ada, b_ada, norm1_w, norm2_w, w_in,
           q_norm_w, k_norm_w, rpb, w_sgu, b_sgu, sgu_ln_w, sgu_ln_b, w_out, w_router, b_router,
           w1, w3, w2):
    batch, seq, _ = x_prompt.shape
    dec_batch, dec_seq, _ = x_sample.shape
    past = cache_k.shape[2]
    assert 1 + dec_batch <= COND_ROWS and dec_seq % (Q_ROWS * GRID_W) == 0 and seq % CHUNK == 0

    cond = jnp.zeros((COND_ROWS, D_MODEL), F32).at[0].set(c_ctx).at[1:1 + dec_batch].set(c)
    mod = _adaln(cond, w_ada, b_ada).reshape(DEPTH, COND_ROWS, 6, D_MODEL)

    xp = x_prompt.reshape(batch * seq, D_MODEL)
    xs = x_sample.reshape(dec_batch * dec_seq, D_MODEL)
    from_cache_layout = lambda a: a.transpose(0, 1, 3, 4, 2).reshape(dec_batch, DEPTH, ATTN_WIDTH, past)
    ck = from_cache_layout(cache_k)
    cv = from_cache_layout(cache_v)

    head_of = jnp.arange(256) // HEAD_DIM
    bd = (head_of[:, None] == head_of[None, :]).astype(BF16)
    tok_id = jnp.arange(SRC_TILE)
    tri = (tok_id[:, None] <= tok_id[None, :]).astype(BF16)

    params = dict(
        n1w=norm1_w.reshape(DEPTH, 1, D_MODEL),
        n2w=norm2_w.reshape(DEPTH, 1, D_MODEL),
        win=w_in.astype(BF16),
        wout=w_out.astype(BF16),
        qw=jnp.tile(q_norm_w, (1, N_HEADS)).reshape(DEPTH, 1, ATTN_WIDTH),
        kw=jnp.tile(k_norm_w, (1, N_HEADS)).reshape(DEPTH, 1, ATTN_WIDTH),
        bd=jnp.concatenate([bd, bd], axis=0),
        ws=w_sgu.astype(BF16),
        bs=jnp.broadcast_to(b_sgu[..., None], (DEPTH, SGU_GROUPS, CHUNK, SGU_GROUP_DIM)),
        lnw=sgu_ln_w,
        lnb=sgu_ln_b,
        wr=w_router.T.astype(BF16),
        br=b_router.reshape(N_EXPERTS, 1),
        tri=tri,
        w1=w1,
        w3=w3,
        w2=w2.reshape(DEPTH, N_EXPERT_GROUPS, GROUP_HIDDEN, D_MODEL),
    )
    tb = _bias_tables(rpb)

    tm = 2 * SRC_TILE
    assert dec_seq % tm == 0 and tm % seq == 0
    lat_tiles_per_seq = dec_seq // tm
    ctx_row = lambda i: 0
    lat_row = lambda i: 1 + i // lat_tiles_per_seq

    def moe_block(attended, layer):
        chunks = jnp.concatenate([meta[:, :N_EXPERT_GROUPS, 0] for _, _, _, meta in attended], axis=0)
        tables = _chunk_tables(chunks, _max_moe_tiles(chunks.shape[0]))
        y = _moe([slab for _, slab, _, _ in attended], tables, layer, params)
        pending, first = [], 0
        for xm, _, dest, meta in attended:
            n_rows = meta.shape[0] * SLAB_CHUNKS
            pending.append((xm, dest, tables[2][first:first + n_rows], y))
            first += n_rows
        return pending

    kv_new = jax.ShapeDtypeStruct((batch, DEPTH, ATTN_WIDTH, seq), F32)

    for l in range(DEPTH):
        outs = _inproj(xp, l, mod, ctx_row, params, tm=tm,
                       kv_buffers=kv_new if l == 0 else (kbuf, vbuf))
        if l > 0:
            xp, outs = outs[0], outs[1:]
        q, kbuf, vbuf, v, sg = outs
        attended_ctx = _ctx_attention(q, kbuf, v, sg, xp, l, mod, params, seq=seq)

        outs = _inproj(xs, l, mod, lat_row, params, tm=tm)
        if l > 0:
            xs, outs = outs[0], outs[1:]
        q, k, v, sg = outs
        attended_lat = _nbr_attention(q, k, v, ck, cv, tb, sg, xs, l, mod, params,
                                      batch=dec_batch, seq=dec_seq)

        xp, xs = moe_block([attended_ctx, attended_lat], l)
    xp = _unsort(xp, DEPTH - 1, mod, ctx_row, tm=tm)
    xs = _unsort(xs, DEPTH - 1, mod, lat_row, tm=tm)

    to_cache_layout = lambda buf: buf.reshape(batch, DEPTH, N_HEADS, HEAD_DIM, seq).transpose(0, 1, 4, 2, 3)
    return (xp.reshape(batch, seq, D_MODEL), xs.reshape(dec_batch, dec_seq, D_MODEL),
            to_cache_layout(kbuf), to_cache_layout(vbuf))
```

```python
import functools

import jax
import jax.numpy as jnp
from jax import lax
from jax.experimental import pallas as pl
from jax.experimental.pallas import tpu as pltpu

F32 = jnp.float32
BF16 = jnp.bfloat16

D_MODEL = 1024
DEPTH = 4
N_HEADS = 8
HEAD_DIM = 64
ATTN_WIDTH = N_HEADS * HEAD_DIM
SGU_GROUPS = 4
SGU_GROUP_DIM = 128
SGU_WIDTH = SGU_GROUPS * SGU_GROUP_DIM
CHUNK = 128
IN_WIDTH = 3 * ATTN_WIDTH + 2 * SGU_WIDTH
GRID_W = 64
WIN_H = 8
WIN_W = 16
N_EXPERTS = 16
N_EXPERT_GROUPS = 4
EXPERTS_PER_GROUP = 4
D_EXPERT = 256
GROUP_HIDDEN = EXPERTS_PER_GROUP * D_EXPERT
EPS = 1e-6
NEG = -1e30
LOG2E = 1.4426950408889634

LANES = 128
HEADS_PER_VREG = LANES // HEAD_DIM
N_HEAD_PAIRS = N_HEADS // HEADS_PER_VREG
COND_ROWS = 16
Q_ROWS = 4
WIN_ROWS = Q_ROWS + WIN_H
VMEM_LIMIT = 48 * 1024 * 1024

SRC_TILE = 256
CHUNK_ROWS = 16
SLAB_ROWS = SRC_TILE + N_EXPERT_GROUPS * CHUNK_ROWS
SLAB_CHUNKS = SLAB_ROWS // CHUNK_ROWS
SLAB_WIDTH = D_MODEL + LANES
MOE_TILE = 512
TILE_CHUNKS = MOE_TILE // CHUNK_ROWS


def _cparams(*sem):
    return pltpu.CompilerParams(dimension_semantics=sem, vmem_limit_bytes=VMEM_LIMIT)


def _adaln_kernel(cond_ref, w_ref, b_ref, o_ref):
    c = cond_ref[...]
    a = (c * jax.nn.sigmoid(c)).astype(BF16)
    o_ref[0] = jnp.dot(a, w_ref[0].astype(BF16), preferred_element_type=F32) + b_ref[0]


def _adaln(cond, w_ada, b_ada):
    tn = 1536
    n = 6 * D_MODEL
    return pl.pallas_call(
        _adaln_kernel,
        grid=(DEPTH, n // tn),
        in_specs=[
            pl.BlockSpec((COND_ROWS, D_MODEL), lambda l, j: (0, 0)),
            pl.BlockSpec((1, D_MODEL, tn), lambda l, j: (l, 0, j)),
            pl.BlockSpec((1, 1, tn), lambda l, j: (l, 0, j)),
        ],
        out_specs=pl.BlockSpec((1, COND_ROWS, tn), lambda l, j: (l, 0, j)),
        out_shape=jax.ShapeDtypeStruct((DEPTH, COND_ROWS, n), F32),
        compiler_params=_cparams("arbitrary", "arbitrary"),
        name="adaln",
    )(cond, w_ada, b_ada.reshape(DEPTH, 1, n))


def _gelu_tanh(x):
    return 0.5 * x * (1.0 + jnp.tanh(0.7978845608028654 * (x + 0.044715 * (x * x * x))))


def _head_rms(t, bd, w):
    t2 = t * t
    hi = t2.astype(BF16)
    lo = (t2 - hi.astype(F32)).astype(BF16)
    outs = []
    for c in range(ATTN_WIDTH // 256):
        sl = slice(256 * c, 256 * c + 256)
        hl = jnp.concatenate([hi[:, sl], lo[:, sl]], axis=1)
        ss = jnp.dot(hl, bd, preferred_element_type=F32)
        outs.append(t[:, sl] * lax.rsqrt(ss * (1.0 / HEAD_DIM) + EPS))
    return jnp.concatenate(outs, axis=1) * w


def _inproj_kernel(*refs, tm, transposed_kv, new_kv_layer, fused_moe):
    refs = list(refs)
    if fused_moe:
        row_ref, xm_ref, dest_ref, prev_mod_ref, y_hbm = refs[:5]
        refs = refs[5:]
    else:
        x_ref = refs.pop(0)
    (mod_ref, n1w_ref, win_ref, qw_ref, kw_ref, bd_ref, ws_ref, bs_ref, lnw_ref, lnb_ref) = refs[:10]
    refs = refs[10:]
    if transposed_kv and new_kv_layer is None:
        refs = refs[2:]
    if fused_moe:
        x_out_ref = refs.pop(0)
    if transposed_kv:
        q_ref, kt_ref, vt_ref, v_ref, sg_ref = refs[:5]
        refs = refs[5:]
    else:
        q_ref, k_ref, v_ref, sg_ref = refs[:4]
        refs = refs[4:]
    if fused_moe:
        buf, sem = refs
        moe = _unsorted_moe(row_ref, dest_ref, y_hbm, buf, sem, tm // SRC_TILE)
        x = xm_ref[...] + prev_mod_ref[0, 0][5:6] * moe
        x_out_ref[...] = x
    else:
        x = x_ref[...]
    m = mod_ref[0, 0]
    sh1, sc1 = m[0:1], m[1:2]
    ms = jnp.mean(x * x, axis=-1, keepdims=True)
    h = x * lax.rsqrt(ms + EPS) * n1w_ref[0]
    h = (h * (1.0 + sc1) + sh1).astype(BF16)
    p = jnp.dot(h, win_ref[0], preferred_element_type=F32)

    bd = bd_ref[...]
    q = _head_rms(p[:, 0:ATTN_WIDTH], bd, qw_ref[0])
    k = _head_rms(p[:, ATTN_WIDTH:2 * ATTN_WIDTH], bd, kw_ref[0])
    v = p[:, 2 * ATTN_WIDTH:3 * ATTN_WIDTH]
    q_ref[...] = (q * (HEAD_DIM ** -0.5 * LOG2E)).astype(q_ref.dtype)
    if transposed_kv:
        seq = kt_ref.shape[3]
        if new_kv_layer is None:
            slot = 0
        else:
            slot = new_kv_layer
            kt_ref[...] = jnp.zeros_like(kt_ref)
            vt_ref[...] = jnp.zeros_like(vt_ref)
        for j in range(tm // seq):
            kt_ref[j, slot] = k[j * seq:(j + 1) * seq].T
            vt_ref[j, slot] = v[j * seq:(j + 1) * seq].T
    else:
        k_ref[...] = k.astype(k_ref.dtype)
    v_ref[...] = v.astype(v_ref.dtype)

    off_u = 3 * ATTN_WIDTH
    off_v = off_u + SGU_WIDTH
    for g in range(SGU_GROUPS):
        gl = slice(g * SGU_GROUP_DIM, (g + 1) * SGU_GROUP_DIM)
        u = _gelu_tanh(p[:, off_u + g * SGU_GROUP_DIM: off_u + (g + 1) * SGU_GROUP_DIM])
        t = _gelu_tanh(p[:, off_v + g * SGU_GROUP_DIM: off_v + (g + 1) * SGU_GROUP_DIM])
        mu = jnp.mean(t, axis=-1, keepdims=True)
        d = t - mu
        var = jnp.mean(d * d, axis=-1, keepdims=True)
        y = (d * lax.rsqrt(var + EPS) * lnw_ref[0, g:g + 1, :] + lnb_ref[0, g:g + 1, :]).astype(BF16)
        for c in range(tm // CHUNK):
            rows = slice(c * CHUNK, (c + 1) * CHUNK)
            sv = jnp.dot(ws_ref[0, g], y[rows], preferred_element_type=F32) + bs_ref[0, g]
            sg_ref[rows, gl] = (u[rows] * sv).astype(sg_ref.dtype)


def _inproj(x, layer, mod, row_of_tile, params, *, tm, kv_buffers=None):
    fused_moe = isinstance(x, tuple)
    tok = lambda i, *_: (i, 0)
    lyr3 = lambda i, *_: (layer, 0, 0)
    lyr4 = lambda i, *_: (layer, 0, 0, 0)
    mod_row = lambda lyr: pl.BlockSpec((1, 1, 6, D_MODEL), lambda i, *_: (lyr, row_of_tile(i), 0, 0))
    x_spec = pl.BlockSpec((tm, D_MODEL), tok)
    if fused_moe:
        n_slabs = tm // SRC_TILE
        xm, dest, out_row, y = x
        t_tokens = xm.shape[0]
        prefetch = [out_row]
        in_specs = [x_spec, pl.BlockSpec((n_slabs, 1, SRC_TILE), lambda i, *_: (i, 0, 0)),
                    mod_row(layer - 1), pl.BlockSpec(memory_space=pl.ANY)]
        args = [xm, dest, mod, y]
        scratch = [pltpu.VMEM((2, n_slabs * SLAB_ROWS, D_MODEL), y.dtype), pltpu.SemaphoreType.DMA((2,))]
    else:
        t_tokens = x.shape[0]
        prefetch, in_specs, args, scratch = [], [x_spec], [x], []
    act = pl.BlockSpec((tm, ATTN_WIDTH), tok)
    act_sds = jax.ShapeDtypeStruct((t_tokens, ATTN_WIDTH), BF16)
    in_specs += [
        mod_row(layer),
        pl.BlockSpec((1, 1, D_MODEL), lyr3),
        pl.BlockSpec((1, D_MODEL, IN_WIDTH), lyr3),
        pl.BlockSpec((1, 1, ATTN_WIDTH), lyr3),
        pl.BlockSpec((1, 1, ATTN_WIDTH), lyr3),
        pl.BlockSpec((512, 256), lambda i, *_: (0, 0)),
        pl.BlockSpec((1, SGU_GROUPS, CHUNK, CHUNK), lyr4),
        pl.BlockSpec((1, SGU_GROUPS, CHUNK, SGU_GROUP_DIM), lyr4),
        pl.BlockSpec((1, SGU_GROUPS, SGU_GROUP_DIM), lyr3),
        pl.BlockSpec((1, SGU_GROUPS, SGU_GROUP_DIM), lyr3),
    ]
    args += [mod, params["n1w"], params["win"], params["qw"], params["kw"], params["bd"],
             params["ws"], params["bs"], params["lnw"], params["lnb"]]
    out_specs = [x_spec] if fused_moe else []
    out_shape = [jax.ShapeDtypeStruct((t_tokens, D_MODEL), F32)] if fused_moe else []
    if kv_buffers is None:
        out_specs += [act] * 4
        out_shape += [act_sds] * 4
        aliases = {}
    elif isinstance(kv_buffers, jax.ShapeDtypeStruct):
        seq = kv_buffers.shape[3]
        kv_spec = pl.BlockSpec((tm // seq,) + kv_buffers.shape[1:], lambda i, *_: (i, 0, 0, 0))
        out_specs += [act, kv_spec, kv_spec, act, act]
        out_shape += [act_sds, kv_buffers, kv_buffers, act_sds, act_sds]
        aliases = {}
    else:
        kbuf, vbuf = kv_buffers
        seq = kbuf.shape[3]
        kv_spec = pl.BlockSpec((tm // seq, 1, ATTN_WIDTH, seq), lambda i, *_: (i, layer, 0, 0))
        first_alias_in = len(prefetch) + len(args)
        first_alias_out = len(out_specs) + 1
        in_specs += [pl.BlockSpec(memory_space=pl.ANY)] * 2
        args += [kbuf, vbuf]
        out_specs += [act, kv_spec, kv_spec, act, act]
        out_shape += [act_sds, jax.ShapeDtypeStruct(kbuf.shape, F32),
                      jax.ShapeDtypeStruct(vbuf.shape, F32), act_sds, act_sds]
        aliases = {first_alias_in: first_alias_out, first_alias_in + 1: first_alias_out + 1}
    return pl.pallas_call(
        functools.partial(_inproj_kernel, tm=tm, transposed_kv=kv_buffers is not None,
                          new_kv_layer=layer if isinstance(kv_buffers, jax.ShapeDtypeStruct) else None,
                          fused_moe=fused_moe),
        grid_spec=pltpu.PrefetchScalarGridSpec(
            num_scalar_prefetch=len(prefetch),
            grid=(t_tokens // tm,),
            in_specs=in_specs,
            out_specs=out_specs,
            scratch_shapes=scratch,
        ),
        out_shape=out_shape,
        input_output_aliases=aliases,
        compiler_params=_cparams("arbitrary"),
        name="inproj",
    )(*prefetch, *args)


def _route(lg_t, br):
    s = jax.nn.sigmoid(lg_t)
    sel = s + br
    row = lambda a, i: a[i:i + 1]
    g_score = []
    for g in range(N_EXPERT_GROUPS):
        v = [row(sel, 4 * g + i) for i in range(4)]
        best_pair = None
        for i in range(4):
            for j in range(i + 1, 4):
                pair = v[i] + v[j]
                best_pair = pair if best_pair is None else jnp.maximum(best_pair, pair)
        g_score.append(best_pair)
    best = jnp.zeros_like(g_score[0], dtype=jnp.int32)
    top = g_score[0]
    for g in range(1, N_EXPERT_GROUPS):
        upd = g_score[g] > top
        best = jnp.where(upd, g, best)
        top = jnp.where(upd, g_score[g], top)
    cand, aff = [], []
    for i in range(4):
        ci, si = row(sel, i), row(s, i)
        for g in range(1, N_EXPERT_GROUPS):
            ci = jnp.where(best == g, row(sel, 4 * g + i), ci)
            si = jnp.where(best == g, row(s, 4 * g + i), si)
        cand.append(ci)
        aff.append(si)

    def first_argmax(vals):
        idx = jnp.zeros_like(best)
        top_v = vals[0]
        for i in range(1, 4):
            upd = vals[i] > top_v
            idx = jnp.where(upd, i, idx)
            top_v = jnp.where(upd, vals[i], top_v)
        return idx

    i1 = first_argmax(cand)
    i2 = first_argmax([jnp.where(i1 == i, -jnp.inf, cand[i]) for i in range(4)])
    pick = lambda idx: sum(jnp.where(idx == i, aff[i], 0.0) for i in range(4))
    den = pick(i1) + pick(i2)
    gate = [jnp.where((i1 == i) | (i2 == i), aff[i] / den, 0.0) for i in range(4)]
    hi = [x.astype(BF16) for x in gate]
    lo = [(x - h.astype(F32)).astype(BF16) for x, h in zip(gate, hi)]
    return best, hi + lo


def _slab_positions(best, tri):
    n = best.shape[1]
    onehot = [jnp.where(best == g, 1.0, 0.0) for g in range(N_EXPERT_GROUPS)]
    pad = [jnp.zeros_like(onehot[0])] * (8 - N_EXPERT_GROUPS)
    oh = jnp.concatenate(onehot + pad, axis=0).astype(BF16)
    counts = jnp.dot(oh, tri, preferred_element_type=F32)
    dest = jnp.zeros((1, n), F32)
    seg_start = jnp.zeros((1, 1), F32)
    chunks = []
    for g in range(N_EXPERT_GROUPS):
        cg = counts[g:g + 1]
        n_g = jnp.max(cg, axis=1, keepdims=True)
        c_g = jnp.floor((n_g + (CHUNK_ROWS - 1)) * (1.0 / CHUNK_ROWS))
        dest = dest + onehot[g] * (seg_start + cg - 1.0)
        seg_start = seg_start + c_g * CHUNK_ROWS
        chunks.append(c_g)
    return dest.astype(jnp.int32), chunks


def _permutation(dest):
    n = dest.shape[1]
    hit = lax.broadcasted_iota(jnp.int32, (SLAB_ROWS, n), 0) == dest
    return jnp.where(hit, 1.0, 0.0).astype(BF16)


def _post_attention(a, sg, x, m, wout_ref, n2w_ref, xm_ref):
    g1, sh2, sc2 = m[2:3], m[3:4], m[4:5]
    cat = jnp.concatenate([a, sg], axis=1)
    y = jnp.dot(cat, wout_ref[0], preferred_element_type=F32)
    xm = x + g1 * y
    xm_ref[...] = xm
    ms = jnp.mean(xm * xm, axis=-1, keepdims=True)
    h2 = xm * lax.rsqrt(ms + EPS) * n2w_ref[0]
    return (h2 * (1.0 + sc2) + sh2).astype(BF16)


def _route_previous(h2_scr, wr_ref, br_ref, tri_ref, slab_ref, dest_ref, meta_ref):
    h2 = h2_scr[...]
    lg_t = lax.dot_general(wr_ref[...], h2, (((1,), (1,)), ((), ())), preferred_element_type=F32)
    yield
    best, gates = _route(lg_t, br_ref[...])
    dest, chunks = _slab_positions(best, tri_ref[...])
    yield
    perm = _permutation(dest)
    slab_ref[:, 0:D_MODEL] = jnp.dot(perm, h2, preferred_element_type=F32).astype(BF16)
    gmat = jnp.concatenate(gates + [jnp.zeros((LANES - len(gates), dest.shape[1]), BF16)], axis=0)
    slab_ref[:, D_MODEL:] = lax.dot_general(perm, gmat, (((1,), (1,)), ((), ())),
                                            preferred_element_type=F32).astype(BF16)
    dest_ref[0] = dest
    meta = [jnp.broadcast_to(c, (1, LANES)) for c in chunks]
    meta += [jnp.zeros((8 - len(chunks), LANES), F32)]
    meta_ref[0] = jnp.concatenate(meta, axis=0).astype(jnp.int32)


def _pair_pos(axis):
    return lax.broadcasted_iota(jnp.int32, (1, LANES) if axis == 1 else (LANES, 1), axis)


def _head_mask(hh, axis):
    pos = _pair_pos(axis)
    return (pos >= hh * HEAD_DIM) & (pos < (hh + 1) * HEAD_DIM)


def _only_head(x2, hh, axis):
    return jnp.where(_head_mask(hh, axis), x2, jnp.zeros_like(x2))


def _attend_heads(logits, finish, router, depth):
    outs, even = [], None
    ahead = [logits(h) for h in range(depth)]
    for h in range(N_HEADS):
        s = ahead.pop(0)
        if h + depth < N_HEADS:
            ahead.append(logits(h + depth))
        if h == N_HEADS // 2:
            next(router)
        o = finish(h, s)
        if h % HEADS_PER_VREG == 0:
            even = o
        else:
            outs.append((even + o).astype(BF16))
    for _ in router:
        pass
    return jnp.concatenate(outs, axis=1)


def _ctx_attn_kernel(q_ref, kt_ref, v_ref, sg_ref, x_ref, mod_ref, wout_ref, n2w_ref, wr_ref, br_ref,
                     tri_ref, xm_ref, slab_ref, dest_ref, meta_ref, h2_scr):
    step = pl.program_id(0)

    @pl.when(step == 0)
    def _():
        h2_scr[...] = jnp.zeros_like(h2_scr)

    def logits(h):
        cols = slice(h // HEADS_PER_VREG * LANES, (h // HEADS_PER_VREG + 1) * LANES)
        kt2 = kt_ref[0, 0, cols, :].astype(BF16)
        return jnp.dot(q_ref[:, cols], _only_head(kt2, h % HEADS_PER_VREG, 0),
                       preferred_element_type=F32)

    def finish(h, s):
        cols = slice(h // HEADS_PER_VREG * LANES, (h // HEADS_PER_VREG + 1) * LANES)
        e = jnp.exp2(s - jnp.max(s, axis=-1, keepdims=True))
        l = jnp.sum(e, axis=-1, keepdims=True)
        vm = _only_head(v_ref[:, cols], h % HEADS_PER_VREG, 1)
        return jnp.dot(e.astype(BF16), vm, preferred_element_type=F32) * (1.0 / l)

    new_router = lambda: _route_previous(h2_scr, wr_ref, br_ref, tri_ref, slab_ref, dest_ref, meta_ref)
    last = pl.num_programs(0) - 1

    @pl.when(step < last)
    def _():
        router = new_router()
        next(router)
        a = _attend_heads(logits, finish, router, depth=2)
        h2_scr[...] = _post_attention(a, sg_ref[...], x_ref[...], mod_ref[0, 0], wout_ref, n2w_ref,
                                      xm_ref)

    @pl.when(step == last)
    def _():
        for _ in new_router():
            pass


def _post_attention_out(t_tokens):
    n_src = t_tokens // SRC_TILE
    prev = lambda s: jnp.maximum(s - 1, 0)
    specs = [
        pl.BlockSpec((SRC_TILE, D_MODEL), lambda s: (jnp.minimum(s, n_src - 1), 0)),
        pl.BlockSpec((SLAB_ROWS, SLAB_WIDTH), lambda s: (prev(s), 0)),
        pl.BlockSpec((1, 1, SRC_TILE), lambda s: (prev(s), 0, 0)),
        pl.BlockSpec((1, 8, LANES), lambda s: (prev(s), 0, 0)),
    ]
    shapes = [
        jax.ShapeDtypeStruct((t_tokens, D_MODEL), F32),
        jax.ShapeDtypeStruct((n_src * SLAB_ROWS, SLAB_WIDTH), BF16),
        jax.ShapeDtypeStruct((n_src, 1, SRC_TILE), jnp.int32),
        jax.ShapeDtypeStruct((n_src, 8, LANES), jnp.int32),
    ]
    return specs, shapes, pltpu.VMEM((SRC_TILE, D_MODEL), BF16)


def _ctx_attention(q, kbuf, v, sg, x, layer, mod, params, *, seq):
    assert seq == SRC_TILE
    t_tokens = x.shape[0]
    n_seq = t_tokens // seq
    seq_of = lambda s: jnp.minimum(s, n_seq - 1)
    tok = lambda s: (seq_of(s), 0)
    lyr3 = lambda s: (layer, 0, 0)
    full2 = lambda s: (0, 0)
    out_specs, out_shape, h2_scratch = _post_attention_out(t_tokens)
    return pl.pallas_call(
        _ctx_attn_kernel,
        grid=(n_seq + 1,),
        in_specs=[
            pl.BlockSpec((seq, ATTN_WIDTH), tok),
            pl.BlockSpec((1, 1, ATTN_WIDTH, seq), lambda s: (seq_of(s), layer, 0, 0)),
            pl.BlockSpec((seq, ATTN_WIDTH), tok),
            pl.BlockSpec((seq, SGU_WIDTH), tok),
            pl.BlockSpec((seq, D_MODEL), tok),
            pl.BlockSpec((1, 1, 6, D_MODEL), lambda s: (layer, 0, 0, 0)),
            pl.BlockSpec((1, D_MODEL, D_MODEL), lyr3),
            pl.BlockSpec((1, 1, D_MODEL), lyr3),
            pl.BlockSpec((N_EXPERTS, D_MODEL), full2),
            pl.BlockSpec((N_EXPERTS, 1), full2),
            pl.BlockSpec((SRC_TILE, SRC_TILE), full2),
        ],
        out_specs=out_specs,
        out_shape=out_shape,
        scratch_shapes=[h2_scratch],
        compiler_params=_cparams("arbitrary"),
        name="ctx_attn",
    )(q, kbuf, v, sg, x, mod, params["wout"], params["n2w"], params["wr"], params["br"], params["tri"])


def _row_start(r, rows):
    return jnp.clip(r - WIN_H // 2, 0, rows - WIN_H)


def _nbr_attn_kernel(q_ref, k_ref, v_ref, ck_ref, cv_ref, tb_ref, sg_ref, x_ref, mod_ref, wout_ref,
                     n2w_ref, wr_ref, br_ref, tri_ref, xm_ref, slab_ref, dest_ref, meta_ref,
                     kwm_ref, vwm_ref, kcm_ref, vcm_ref, h2_scr, *, rows, n_tiles):
    step = pl.program_id(0)
    tiles_per_seq = rows // Q_ROWS
    t = jnp.minimum(step, n_tiles - 1) % tiles_per_seq

    @pl.when(step == 0)
    def _():
        h2_scr[...] = jnp.zeros_like(h2_scr)

    @pl.when((t == 0) & (step < n_tiles))
    def _():
        for j in range(N_HEAD_PAIRS):
            cols = slice(j * LANES, (j + 1) * LANES)
            k2, v2 = k_ref[:, cols], v_ref[:, cols]
            ck2 = ck_ref[0, 0, cols, :].astype(BF16)
            cv2 = cv_ref[0, 0, cols, :].astype(BF16)
            key_row = lax.broadcasted_iota(jnp.int32, (k2.shape[0], 1), 0) // GRID_W
            for hh in range(HEADS_PER_VREG):
                h = HEADS_PER_VREG * j + hh
                row_lane = _pair_pos(1) == (1 - hh) * HEAD_DIM + key_row
                kwm_ref[h] = jnp.where(_head_mask(hh, 1), k2, jnp.where(row_lane, 1.0, 0.0).astype(BF16))
                vwm_ref[h] = _only_head(v2, hh, 1)
                kcm_ref[h] = _only_head(ck2, hh, 0)
                vcm_ref[h] = _only_head(cv2, hh, 0)

    new_router = lambda: _route_previous(h2_scr, wr_ref, br_ref, tri_ref, slab_ref, dest_ref, meta_ref)

    @pl.when(step == n_tiles)
    def _():
        for _ in new_router():
            pass

    @pl.when(step < n_tiles)
    def _():
        _attend_tile(t, new_router(), q_ref, tb_ref, sg_ref, x_ref, mod_ref, wout_ref, n2w_ref, xm_ref,
                     kwm_ref, vwm_ref, kcm_ref, vcm_ref, h2_scr, rows)


def _attend_tile(t, router, q_ref, tb_ref, sg_ref, x_ref, mod_ref, wout_ref, n2w_ref, xm_ref,
                 kwm_ref, vwm_ref, kcm_ref, vcm_ref, h2_scr, rows):
    next(router)
    r0 = t * Q_ROWS
    ws = jnp.minimum(_row_start(r0, rows), rows - WIN_ROWS)
    tok0 = pl.multiple_of(ws * GRID_W, GRID_W)

    lane = _pair_pos(1)
    blk_idx = []
    row_pen = [[], []]
    for a in range(Q_ROWS):
        r = r0 + a
        rs = _row_start(r, rows)
        blk_idx.append([jnp.clip(ws + 2 * jp - r + WIN_H, 0, 2 * WIN_H - 1)
                        for jp in range(WIN_ROWS // 2)])
        for hh in range(HEADS_PER_VREG):
            kr = lane - (1 - hh) * HEAD_DIM
            in_band = (kr >= rs) & (kr < rs + WIN_H)
            pen = jnp.where(in_band | (kr < 0) | (kr >= rows), 0.0, NEG)
            row_pen[hh].append(jnp.broadcast_to(pen, (GRID_W, LANES)))
    row_pen = [jnp.concatenate(p, axis=0).astype(BF16) for p in row_pen]

    nt = (((1,), (1,)), ((), ()))
    window = pl.ds(tok0, WIN_ROWS * GRID_W)

    def logits(h):
        hh = h % HEADS_PER_VREG
        q2 = q_ref[:, h // HEADS_PER_VREG * LANES:(h // HEADS_PER_VREG + 1) * LANES]
        q2 = jnp.where(_head_mask(hh, 1), q2, row_pen[hh])
        bias = jnp.concatenate([
            jnp.concatenate([tb_ref[0, h, blk_idx[a][jp]] for jp in range(WIN_ROWS // 2)], axis=1)
            for a in range(Q_ROWS)], axis=0)
        s_w = lax.dot_general(q2, kwm_ref[h, window, :], nt, preferred_element_type=F32) + bias
        s_c = jnp.dot(q2, kcm_ref[h], preferred_element_type=F32)
        return s_w, s_c

    def finish(h, s):
        s_w, s_c = s
        mx = jnp.maximum(jnp.max(s_w, axis=-1, keepdims=True), jnp.max(s_c, axis=-1, keepdims=True))
        e_w = jnp.exp2(s_w - mx)
        e_c = jnp.exp2(s_c - mx)
        l = jnp.sum(e_w, axis=-1, keepdims=True) + jnp.sum(e_c, axis=-1, keepdims=True)
        return (jnp.dot(e_w.astype(BF16), vwm_ref[h, window, :], preferred_element_type=F32)
                + lax.dot_general(e_c.astype(BF16), vcm_ref[h], nt,
                                  preferred_element_type=F32)) * (1.0 / l)

    a_out = _attend_heads(logits, finish, router, depth=1)
    h2_scr[...] = _post_attention(a_out, sg_ref[...], x_ref[...], mod_ref[0, 0], wout_ref, n2w_ref,
                                  xm_ref)


def _nbr_attention(q, k, v, cache_k, cache_v, tb, sg, x, layer, mod, params, *, batch, seq):
    t_tokens = x.shape[0]
    rows = seq // GRID_W
    tq = Q_ROWS * GRID_W
    assert tq == SRC_TILE
    nt = seq // tq
    past = cache_k.shape[3]
    n_tiles = batch * nt
    out_specs, out_shape, h2_scratch = _post_attention_out(t_tokens)
    tile_of = lambda s: jnp.minimum(s, n_tiles - 1)
    seq_of = lambda s: tile_of(s) // nt
    tok = lambda s: (tile_of(s), 0)
    per_b = lambda s: (seq_of(s), 0)
    lyr3 = lambda s: (layer, 0, 0)
    full2 = lambda s: (0, 0)
    cache_spec = pl.BlockSpec((1, 1, ATTN_WIDTH, past), lambda s: (seq_of(s), layer, 0, 0))
    masked = lambda n: pltpu.VMEM((N_HEADS, n, LANES), BF16)
    masked_t = pltpu.VMEM((N_HEADS, LANES, past), BF16)
    return pl.pallas_call(
        functools.partial(_nbr_attn_kernel, rows=rows, n_tiles=n_tiles),
        grid=(n_tiles + 1,),
        in_specs=[
            pl.BlockSpec((tq, ATTN_WIDTH), tok),
            pl.BlockSpec((seq, ATTN_WIDTH), per_b),
            pl.BlockSpec((seq, ATTN_WIDTH), per_b),
            cache_spec,
            cache_spec,
            pl.BlockSpec((1, N_HEADS, 2 * WIN_H, GRID_W, LANES), lambda s: (layer, 0, 0, 0, 0)),
            pl.BlockSpec((tq, SGU_WIDTH), tok),
            pl.BlockSpec((tq, D_MODEL), tok),
            pl.BlockSpec((1, 1, 6, D_MODEL), lambda s: (layer, 1 + seq_of(s), 0, 0)),
            pl.BlockSpec((1, D_MODEL, D_MODEL), lyr3),
            pl.BlockSpec((1, 1, D_MODEL), lyr3),
            pl.BlockSpec((N_EXPERTS, D_MODEL), full2),
            pl.BlockSpec((N_EXPERTS, 1), full2),
            pl.BlockSpec((SRC_TILE, SRC_TILE), full2),
        ],
        out_specs=out_specs,
        out_shape=out_shape,
        scratch_shapes=[masked(seq), masked(seq), masked_t, masked_t, h2_scratch],
        compiler_params=_cparams("arbitrary"),
        name="nbr_attn",
    )(q, k, v, cache_k, cache_v, tb, sg, x, mod, params["wout"], params["n2w"], params["wr"],
      params["br"], params["tri"])


def _bias_tables(rpb):
    n_rel = 2 * WIN_W - 1
    cols = jnp.arange(GRID_W)
    col_start = jnp.clip(cols - WIN_W // 2, 0, GRID_W - WIN_W)
    col_ok = (cols[None, :] >= col_start[:, None]) & (cols[None, :] < col_start[:, None] + WIN_W)
    rel = cols[None, :] - cols[:, None] + WIN_W - 1
    pick = ((rel[None] == jnp.arange(n_rel)[:, None, None]) & col_ok[None]).astype(F32)
    zero = jnp.zeros_like(pick)
    halves = jnp.concatenate([jnp.concatenate([pick, zero], axis=2),
                              jnp.concatenate([zero, pick], axis=2)], axis=0)
    masked = jnp.where(jnp.concatenate([col_ok, col_ok], axis=1), 0.0, NEG)[None]
    select = jnp.concatenate([halves, masked], axis=0)

    by_offset = jnp.pad(rpb * LOG2E, ((0, 0), (0, 0), (1, 1), (0, 0)))
    feats = jnp.concatenate([by_offset[:, :, :-1], by_offset[:, :, 1:],
                             jnp.ones(by_offset.shape[:2] + (2 * WIN_H, 1), F32)], axis=-1)
    return jnp.einsum("lhdc,cqn->lhdqn", feats, select, precision=lax.Precision.HIGHEST)


def _max_moe_tiles(n_src):
    max_chunks = n_src * SRC_TILE // CHUNK_ROWS + n_src * N_EXPERT_GROUPS
    return -(-max_chunks // TILE_CHUNKS) + N_EXPERT_GROUPS


def _chunk_tables(chunks, n_tiles):
    n_src = chunks.shape[0]
    groups = jnp.arange(N_EXPERT_GROUPS)
    seg_end = jnp.cumsum(chunks, axis=1)
    seg_start = seg_end - chunks
    src_end = jnp.cumsum(chunks, axis=0)
    src_start = src_end - chunks
    total = src_end[-1]
    tiles = (total + TILE_CHUNKS - 1) // TILE_CHUNKS
    tile_end = jnp.cumsum(tiles)
    tile_start = tile_end - tiles
    n_used = tile_end[-1:]

    u = jnp.arange(n_tiles)
    gid = jnp.minimum(jnp.sum(u[:, None] >= tile_end[None, :], axis=1), N_EXPERT_GROUPS - 1)

    p = jnp.arange(n_tiles * TILE_CHUNKS)
    g_hot = jnp.repeat(gid, TILE_CHUNKS)[:, None] == groups[None, :]
    by_group = lambda vec: jnp.sum(jnp.where(g_hot, vec[None, :], 0), axis=1)
    q = p - TILE_CHUNKS * by_group(tile_start)
    live = q < by_group(total)
    src_end_p = jnp.sum(jnp.where(g_hot[:, None, :], src_end[None], 0), axis=2)
    t = jnp.minimum(jnp.sum(q[:, None] >= src_end_p, axis=1), n_src - 1)
    t_hot = t[:, None] == jnp.arange(n_src)[None, :]
    by_seg = lambda tab: jnp.sum(jnp.where(t_hot[:, :, None] & g_hot[:, None, :], tab[None], 0),
                                 axis=(1, 2))
    slab_chunk = by_seg(seg_start) + q - by_seg(src_start)
    src_row = jnp.where(live, t * SLAB_ROWS + CHUNK_ROWS * slab_chunk, 0)

    s = jnp.arange(SLAB_CHUNKS)
    gs = jnp.sum(s[None, :, None] >= seg_end[:, None, :], axis=2)
    used = gs < N_EXPERT_GROUPS
    s_hot = jnp.minimum(gs, N_EXPERT_GROUPS - 1)[:, :, None] == groups
    pick = lambda tab: jnp.sum(jnp.where(s_hot, tab, 0), axis=2)
    pos = (TILE_CHUNKS * pick(tile_start[None, None, :]) + pick(src_start[:, None, :])
           + s[None, :] - pick(seg_start[:, None, :]))
    out_row = jnp.where(used, CHUNK_ROWS * pos, 0).reshape(-1)
    i32 = lambda a: a.astype(jnp.int32)
    return i32(gid), i32(src_row), i32(out_row), i32(n_used)


def _chunk_copy(src_hbm, row, k, buf, sem, slot):
    return pltpu.make_async_copy(
        src_hbm.at[pl.ds(pl.multiple_of(row, CHUNK_ROWS), CHUNK_ROWS), :],
        buf.at[slot, pl.ds(k * CHUNK_ROWS, CHUNK_ROWS), :],
        sem.at[slot])


def _start_chunks(sources, row_ref, first, n_chunks, buf, sem, slot):
    bases = [sum(s.shape[0] for s in sources[:i]) for i in range(len(sources) + 1)]
    for k in range(n_chunks):
        row = row_ref[first + k]
        if len(sources) == 1:
            _chunk_copy(sources[0], row, k, buf, sem, slot).start(priority=k % 2)
        else:
            for src, lo, hi in zip(sources, bases[:-1], bases[1:]):
                @pl.when((row >= lo) & (row < hi))
                def _(src=src, lo=lo):
                    _chunk_copy(src, row - lo, k, buf, sem, slot).start(priority=k % 2)


def _gather_step(sources, row_ref, n_chunks, buf, sem):
    step = pl.program_id(0)
    slot = step % 2

    @pl.when(step == 0)
    def _():
        _start_chunks(sources, row_ref, 0, n_chunks, buf, sem, 0)

    @pl.when(step + 1 < pl.num_programs(0))
    def _():
        _start_chunks(sources, row_ref, (step + 1) * n_chunks, n_chunks, buf, sem, 1 - slot)

    for k in range(n_chunks):
        _chunk_copy(sources[0], 0, k, buf, sem, slot).wait()
    return slot


def _moe_kernel(gid_ref, src_ref, nused_ref, *refs, n_slabs):
    slabs = refs[:n_slabs]
    w1_ref, w3_ref, w2_ref, o_ref, buf, sem, w1b, w3b, w2b = refs[n_slabs:]
    u = pl.program_id(0)
    slot = _gather_step(slabs, src_ref, TILE_CHUNKS, buf, sem)

    @pl.when((u == 0) | (gid_ref[u] != gid_ref[jnp.maximum(u - 1, 0)]))
    def _():
        w1b[...] = w1_ref[0].astype(BF16)
        w3b[...] = w3_ref[0].astype(BF16)
        w2b[...] = w2_ref[0, 0].astype(BF16)

    @pl.when(u < nused_ref[0])
    def _():
        h = buf[slot, :, 0:D_MODEL]
        gates = buf[slot, :, D_MODEL:].astype(F32)
        hid = []
        for e in range(EXPERTS_PER_GROUP):
            h1 = jnp.dot(h, w1b[e], preferred_element_type=F32)
            h3 = jnp.dot(h, w3b[e], preferred_element_type=F32)
            act = (h1 * jax.nn.sigmoid(h1)) * h3
            gate = gates[:, e:e + 1] + gates[:, EXPERTS_PER_GROUP + e:EXPERTS_PER_GROUP + e + 1]
            hid.append((act * gate).astype(BF16))
        o_ref[...] = jnp.dot(jnp.concatenate(hid, axis=1), w2b[...],
                             preferred_element_type=F32).astype(o_ref.dtype)

    @pl.when(u >= nused_ref[0])
    def _():
        o_ref[...] = jnp.zeros_like(o_ref)


def _moe(slabs, tables, layer, params):
    gid, src_row, _, n_used = tables
    n_tiles = gid.shape[0]
    expert_w = pl.BlockSpec((1, EXPERTS_PER_GROUP, D_MODEL, D_EXPERT),
                            lambda u, gid, src, nu: (layer, gid[u], 0, 0))
    return pl.pallas_call(
        functools.partial(_moe_kernel, n_slabs=len(slabs)),
        grid_spec=pltpu.PrefetchScalarGridSpec(
            num_scalar_prefetch=3,
            grid=(n_tiles,),
            in_specs=[pl.BlockSpec(memory_space=pl.ANY)] * len(slabs) + [
                expert_w,
                expert_w,
                pl.BlockSpec((1, 1, GROUP_HIDDEN, D_MODEL), lambda u, gid, src, nu: (layer, gid[u], 0, 0)),
            ],
            out_specs=pl.BlockSpec((MOE_TILE, D_MODEL), lambda u, gid, src, nu: (u, 0)),
            scratch_shapes=[
                pltpu.VMEM((2, MOE_TILE, SLAB_WIDTH), BF16),
                pltpu.SemaphoreType.DMA((2,)),
                pltpu.VMEM((EXPERTS_PER_GROUP, D_MODEL, D_EXPERT), BF16),
                pltpu.VMEM((EXPERTS_PER_GROUP, D_MODEL, D_EXPERT), BF16),
                pltpu.VMEM((GROUP_HIDDEN, D_MODEL), BF16),
            ],
        ),
        out_shape=jax.ShapeDtypeStruct((n_tiles * MOE_TILE, D_MODEL), BF16),
        compiler_params=_cparams("arbitrary"),
        name="moe",
    )(gid, src_row, n_used, *slabs, params["w1"], params["w3"], params["w2"])


def _unsorted_moe(row_ref, dest_ref, y_hbm, buf, sem, n_slabs):
    slot = _gather_step([y_hbm], row_ref, n_slabs * SLAB_CHUNKS, buf, sem)
    tn = (((0,), (0,)), ((), ()))
    out = []
    for j in range(n_slabs):
        y = buf[slot, j * SLAB_ROWS:(j + 1) * SLAB_ROWS, :]
        out.append(lax.dot_general(_permutation(dest_ref[j]), y, tn, preferred_element_type=F32))
    return jnp.concatenate(out, axis=0)


def _unsort_kernel(row_ref, xm_ref, dest_ref, mod_ref, y_hbm, o_ref, buf, sem, *, n_slabs):
    moe = _unsorted_moe(row_ref, dest_ref, y_hbm, buf, sem, n_slabs)
    o_ref[...] = xm_ref[...] + mod_ref[0, 0][5:6] * moe


def _unsort(pending, layer, mod, row_of_tile, *, tm):
    xm, dest, out_row, y = pending
    t_tokens = xm.shape[0]
    n_slabs = tm // SRC_TILE
    return pl.pallas_call(
        functools.partial(_unsort_kernel, n_slabs=n_slabs),
        grid_spec=pltpu.PrefetchScalarGridSpec(
            num_scalar_prefetch=1,
            grid=(t_tokens // tm,),
            in_specs=[
                pl.BlockSpec((tm, D_MODEL), lambda t, rows: (t, 0)),
                pl.BlockSpec((n_slabs, 1, SRC_TILE), lambda t, rows: (t, 0, 0)),
                pl.BlockSpec((1, 1, 6, D_MODEL), lambda t, rows: (layer, row_of_tile(t), 0, 0)),
                pl.BlockSpec(memory_space=pl.ANY),
            ],
            out_specs=pl.BlockSpec((tm, D_MODEL), lambda t, rows: (t, 0)),
            scratch_shapes=[pltpu.VMEM((2, n_slabs * SLAB_ROWS, D_MODEL), y.dtype),
                            pltpu.SemaphoreType.DMA((2,))],
        ),
        out_shape=jax.ShapeDtypeStruct((t_tokens, D_MODEL), F32),
        compiler_params=_cparams("arbitrary"),
        name="unsort",
    )(out_row, xm, dest, mod, y)


def kernel(x_prompt, x_sample, cache_k, cache_v, c, c_ctx, w_ada, b_ada, norm1_w, norm2_w, w_in,
           q_norm_w, k_norm_w, rpb, w_sgu, b_sgu, sgu_ln_w, sgu_ln_b, w_out, w_router, b_router,
           w1, w3, w2):
    batch, seq, _ = x_prompt.shape
    dec_batch, dec_seq, _ = x_sample.shape
    past = cache_k.shape[2]
    assert 1 + dec_batch <= COND_ROWS and dec_seq % (Q_ROWS * GRID_W) == 0 and seq % CHUNK == 0

    cond = jnp.zeros((COND_ROWS, D_MODEL), F32).at[0].set(c_ctx).at[1:1 + dec_batch].set(c)
    mod = _adaln(cond, w_ada, b_ada).reshape(DEPTH, COND_ROWS, 6, D_MODEL)

    xp = x_prompt.reshape(batch * seq, D_MODEL)
    xs = x_sample.reshape(dec_batch * dec_seq, D_MODEL)
    from_cache_layout = lambda a: a.transpose(0, 1, 3, 4, 2).reshape(dec_batch, DEPTH, ATTN_WIDTH, past)
    ck = from_cache_layout(cache_k)
    cv = from_cache_layout(cache_v)

    head_of = jnp.arange(256) // HEAD_DIM
    bd = (head_of[:, None] == head_of[None, :]).astype(BF16)
    tok_id = jnp.arange(SRC_TILE)
    tri = (tok_id[:, None] <= tok_id[None, :]).astype(BF16)

    params = dict(
        n1w=norm1_w.reshape(DEPTH, 1, D_MODEL),
        n2w=norm2_w.reshape(DEPTH, 1, D_MODEL),
        win=w_in.astype(BF16),
        wout=w_out.astype(BF16),
        qw=jnp.tile(q_norm_w, (1, N_HEADS)).reshape(DEPTH, 1, ATTN_WIDTH),
        kw=jnp.tile(k_norm_w, (1, N_HEADS)).reshape(DEPTH, 1, ATTN_WIDTH),
        bd=jnp.concatenate([bd, bd], axis=0),
        ws=w_sgu.astype(BF16),
        bs=jnp.broadcast_to(b_sgu[..., None], (DEPTH, SGU_GROUPS, CHUNK, SGU_GROUP_DIM)),
        lnw=sgu_ln_w,
        lnb=sgu_ln_b,
        wr=w_router.T.astype(BF16),
        br=b_router.reshape(N_EXPERTS, 1),
        tri=tri,
        w1=w1,
        w3=w3,
        w2=w2.reshape(DEPTH, N_EXPERT_GROUPS, GROUP_HIDDEN, D_MODEL),
    )
    tb = _bias_tables(rpb)

    tm = 2 * SRC_TILE
    assert dec_seq % tm == 0 and tm % seq == 0
    lat_tiles_per_seq = dec_seq // tm
    ctx_row = lambda i: 0
    lat_row = lambda i: 1 + i // lat_tiles_per_seq

    def moe_block(attended, layer):
        chunks = jnp.concatenate([meta[:, :N_EXPERT_GROUPS, 0] for _, _, _, meta in attended], axis=0)
        tables = _chunk_tables(chunks, _max_moe_tiles(chunks.shape[0]))
        y = _moe([slab for _, slab, _, _ in attended], tables, layer, params)
        pending, first = [], 0
        for xm, _, dest, meta in attended:
            n_rows = meta.shape[0] * SLAB_CHUNKS
            pending.append((xm, dest, tables[2][first:first + n_rows], y))
            first += n_rows
        return pending

    kv_new = jax.ShapeDtypeStruct((batch, DEPTH, ATTN_WIDTH, seq), F32)

    for l in range(DEPTH):
        outs = _inproj(xp, l, mod, ctx_row, params, tm=tm,
                       kv_buffers=kv_new if l == 0 else (kbuf, vbuf))
        if l > 0:
            xp, outs = outs[0], outs[1:]
        q, kbuf, vbuf, v, sg = outs
        attended_ctx = _ctx_attention(q, kbuf, v, sg, xp, l, mod, params, seq=seq)

        outs = _inproj(xs, l, mod, lat_row, params, tm=tm)
        if l > 0:
            xs, outs = outs[0], outs[1:]
        q, k, v, sg = outs
        attended_lat = _nbr_attention(q, k, v, ck, cv, tb, sg, xs, l, mod, params,
                                      batch=dec_batch, seq=dec_seq)

        xp, xs = moe_block([attended_ctx, attended_lat], l)
    xp = _unsort(xp, DEPTH - 1, mod, ctx_row, tm=tm)
    xs = _unsort(xs, DEPTH - 1, mod, lat_row, tm=tm)

    to_cache_layout = lambda buf: buf.reshape(batch, DEPTH, N_HEADS, HEAD_DIM, seq).transpose(0, 1, 4, 2, 3)
    return (xp.reshape(batch, seq, D_MODEL), xs.reshape(dec_batch, dec_seq, D_MODEL),
            to_cache_layout(kbuf), to_cache_layout(vbuf))
```
